```python
import math
import jax, jax.numpy as jnp
from jax import lax
import numpy as np

D_MODEL = 1024
BATCH = 4
SEQ = 4096
DEPTH = 4
DEC_BATCH = 32
DEC_SEQ = 64
PAST_LEN = 4096

CHUNK = 64
EPS = 1e-6
NEG_INF = -1e30
N_HEADS = 8
N_KV_HEADS = 2
HEAD_DIM = 64
Q_PER_KV = N_HEADS // N_KV_HEADS
ATTN_WIDTH = N_HEADS * HEAD_DIM
KV_WIDTH = N_KV_HEADS * HEAD_DIM
WINDOW = 128
WINDOW_CHUNKS = WINDOW // CHUNK
ROPE_THETA = 10000.0
SSM_WIDTH = D_MODEL - ATTN_WIDTH
SSM_GROUP = 16
N_SSM_GROUPS = SSM_WIDTH // SSM_GROUP
SSM_STATE = 64
DT_MIN = 1e-3
DT_MAX = 1e-1
IN_WIDTH = ATTN_WIDTH + 2 * KV_WIDTH + SSM_WIDTH
N_MEM = 256
N_XHEADS = 4
XHEAD_DIM = D_MODEL // N_XHEADS
N_EXPERT_GROUPS = 4
EXPERTS_PER_GROUP = 8
N_EXPERTS = N_EXPERT_GROUPS * EXPERTS_PER_GROUP
TOP_K = 2
EXPERT_FF = 128

kernel_name = 'hymba_s5_stream_encoder_step'

F32 = jnp.float32


def rms_norm(x, g):
    xf = x.astype(F32)
    y = xf * lax.rsqrt(jnp.mean(xf * xf, axis=-1, keepdims=True) + EPS)
    return (y * g.astype(F32)).astype(x.dtype)


def rope(x, pos):
    half = HEAD_DIM // 2
    inv = ROPE_THETA ** (-jnp.arange(half, dtype=F32) / half)
    ang = pos.astype(F32)[:, None] * inv[None, :]
    cos = jnp.cos(ang)[:, None, :]
    sin = jnp.sin(ang)[:, None, :]
    xf = x.astype(F32)
    x1, x2 = xf[..., :half], xf[..., half:]
    return jnp.concatenate([x1 * cos - x2 * sin, x2 * cos + x1 * sin], axis=-1).astype(x.dtype)


def split_in_proj(z):
    b, t = z.shape[:2]
    q = z[..., :ATTN_WIDTH].reshape(b, t, N_HEADS, HEAD_DIM)
    k = z[..., ATTN_WIDTH:ATTN_WIDTH + KV_WIDTH].reshape(b, t, N_KV_HEADS, HEAD_DIM)
    v = z[..., ATTN_WIDTH + KV_WIDTH:ATTN_WIDTH + 2 * KV_WIDTH].reshape(b, t, N_KV_HEADS, HEAD_DIM)
    u = z[..., ATTN_WIDTH + 2 * KV_WIDTH:]
    return q, k, v, u


def sink_attend(s, sink, v, eq):
    m = jnp.maximum(s.max(axis=-1, keepdims=True), sink)
    e = jnp.exp(s - m)
    p = e / (e.sum(axis=-1, keepdims=True) + jnp.exp(sink - m))
    return jnp.einsum(eq, p.astype(v.dtype), v)


def window_attn_prompt(q, k, v, sink):
    b, s_len = q.shape[:2]
    nc = s_len // CHUNK
    band_len = (WINDOW_CHUNKS + 1) * CHUNK
    qb = q.reshape(b, nc, CHUNK, N_KV_HEADS, Q_PER_KV, HEAD_DIM)

    def band(t):
        tp = jnp.pad(t, ((0, 0), (WINDOW_CHUNKS * CHUNK, 0), (0, 0), (0, 0)))
        tp = tp.reshape(b, nc + WINDOW_CHUNKS, CHUNK, N_KV_HEADS, HEAD_DIM)
        return jnp.concatenate([tp[:, o:o + nc] for o in range(WINDOW_CHUNKS + 1)], axis=2)

    kb, vb = band(k), band(v)
    s = jnp.einsum('bcqkgd,bcjkd->bckgqj', qb, kb, preferred_element_type=F32) / math.sqrt(HEAD_DIM)
    key_pos = (jnp.arange(nc)[:, None] - WINDOW_CHUNKS) * CHUNK + jnp.arange(band_len)[None, :]
    s = jnp.where((key_pos >= 0)[None, :, None, None, None, :], s, NEG_INF)
    sk = sink.astype(F32).reshape(N_KV_HEADS, Q_PER_KV)[None, None, :, :, None, None]
    o = sink_attend(s, sk, vb, 'bckgqj,bcjkd->bcqkgd')
    return o.reshape(b, s_len, ATTN_WIDTH)


def window_attn_sample(q, k_all, v_all, sink):
    b, t = q.shape[:2]
    qg = q.reshape(b, t, N_KV_HEADS, Q_PER_KV, HEAD_DIM)
    s = jnp.einsum('btkgd,bjkd->bkgtj', qg, k_all, preferred_element_type=F32) / math.sqrt(HEAD_DIM)
    sk = sink.astype(F32).reshape(N_KV_HEADS, Q_PER_KV)[None, :, :, None, None]
    o = sink_attend(s, sk, v_all, 'bkgtj,bjkd->btkgd')
    return o.reshape(b, t, ATTN_WIDTH)


def ssm_discretize(lam_re, lam_im, log_dt, b_re, b_im):
    lam = lax.complex(lam_re.astype(F32), lam_im.astype(F32))
    dt = jnp.exp(log_dt.astype(F32))[:, None]
    lam_bar = jnp.exp(lam * dt)
    b_bar = ((lam_bar - 1.0) / lam)[..., None] * lax.complex(b_re.astype(F32), b_im.astype(F32))
    return lam_bar, b_bar


def _ssm_combine(e1, e2):
    a1, b1 = e1
    a2, b2 = e2
    return a1 * a2, a2 * b1 + b2


def ssm_scan(u, s0, lam_bar, b_bar, c_re, c_im, d):
    b, t = u.shape[:2]
    uf = u.astype(F32)
    ug = uf.reshape(b, t, N_SSM_GROUPS, SSM_GROUP).astype(jnp.complex64)
    bu = jnp.einsum('btgc,gpc->btgp', ug, b_bar)
    if s0 is not None:
        bu = bu.at[:, 0].add(lam_bar * s0)
    a = jnp.broadcast_to(lam_bar, bu.shape)
    _, states = lax.associative_scan(_ssm_combine, (a, bu), axis=1)
    c = lax.complex(c_re.astype(F32), c_im.astype(F32))
    y = jnp.einsum('btgp,gcp->btgc', states, c).real.reshape(b, t, SSM_WIDTH) + d.astype(F32) * uf
    return y.astype(u.dtype), states[:, -1]


def merge_head_groups(attn, ssm_y, w_glu, b_glu, g_a, g_s, w_out):
    g = jax.nn.gelu(ssm_y)
    glu = g * jax.nn.sigmoid(g @ w_glu + b_glu)
    cat = jnp.concatenate([rms_norm(attn, g_a), rms_norm(glu, g_s)], axis=-1)
    return cat @ w_out


def mem_kv(mem, g_mem, wk, wv):
    b = mem.shape[0]
    m = rms_norm(mem, g_mem)
    k = (m @ wk).reshape(b, N_MEM, N_XHEADS, XHEAD_DIM)
    v = (m @ wv).reshape(b, N_MEM, N_XHEADS, XHEAD_DIM)
    return k, v


def cross_attend(h, wq, wo, mk, mv):
    b, t = h.shape[:2]
    q = (h @ wq).reshape(b, t, N_XHEADS, XHEAD_DIM)
    s = jnp.einsum('bthd,bmhd->bhtm', q, mk, preferred_element_type=F32) / math.sqrt(XHEAD_DIM)
    p = jax.nn.softmax(s, axis=-1)
    o = jnp.einsum('bhtm,bmhd->bthd', p.astype(mv.dtype), mv).reshape(b, t, D_MODEL)
    return o @ wo


def hier_moe(h, w_group, b_group, w_router, b_router, w_gate, w_up, w_down):
    b, t = h.shape[:2]
    x = h.reshape(b * t, D_MODEL)
    g_logits = (x @ w_group).astype(F32) + b_group.astype(F32)
    g_w, g_idx = lax.top_k(jax.nn.softmax(g_logits, axis=-1), 1)
    e_logits = jnp.einsum('nd,gde->nge', x, w_router, preferred_element_type=F32) + b_router.astype(F32)
    g_sel = jax.nn.one_hot(g_idx[:, 0], N_EXPERT_GROUPS, dtype=F32)
    e_logits = jnp.einsum('nge,ng->ne', e_logits, g_sel)
    e_top, e_idx = lax.top_k(e_logits, TOP_K)
    e_w = jax.nn.softmax(e_top, axis=-1) * g_w
    eid = g_idx * EXPERTS_PER_GROUP + e_idx
    gates = jnp.einsum('nk,nke->ne', e_w, jax.nn.one_hot(eid, N_EXPERTS, dtype=F32))
    hid = jax.nn.silu(jnp.einsum('nd,edf->nef', x, w_gate)) * jnp.einsum('nd,edf->nef', x, w_up)
    y = jnp.einsum('nef,efd->nd', hid * gates[:, :, None].astype(hid.dtype), w_down)
    return y.reshape(b, t, D_MODEL)


def setup_inputs(seed: int = 0) -> dict:
    key = jax.random.key(seed)
    ks = jax.random.split(key, 40)
    win_rows = min(WINDOW, PAST_LEN)
    sd = D_MODEL ** -0.5
    nrm = lambda k, shape, scale=1.0: jax.random.normal(k, shape, F32) * scale
    gain = lambda k, shape: 1.0 + 0.01 * jax.random.normal(k, shape, F32)
    lam_im = jnp.pi * jnp.arange(SSM_STATE, dtype=F32)[None, None, :] + 0.01 * jax.random.normal(ks[11], (DEPTH, N_SSM_GROUPS, SSM_STATE), F32)
    return {
        'x_prompt': nrm(ks[0], (BATCH, SEQ, D_MODEL)),
        'x_sample': nrm(ks[1], (DEC_BATCH, DEC_SEQ, D_MODEL)),
        'cache_win_k': nrm(ks[2], (DEPTH, DEC_BATCH, win_rows, N_KV_HEADS, HEAD_DIM)),
        'cache_win_v': nrm(ks[3], (DEPTH, DEC_BATCH, win_rows, N_KV_HEADS, HEAD_DIM)),
        'state_ssm': nrm(ks[4], (DEPTH, DEC_BATCH, N_SSM_GROUPS, SSM_STATE, 2), 0.1),
        'cache_mem_k': nrm(ks[5], (DEPTH, DEC_BATCH, N_MEM, N_XHEADS, XHEAD_DIM)),
        'cache_mem_v': nrm(ks[6], (DEPTH, DEC_BATCH, N_MEM, N_XHEADS, XHEAD_DIM)),
        'mem_prompt': nrm(ks[7], (BATCH, N_MEM, D_MODEL)),
        'w_in': nrm(ks[8], (DEPTH, D_MODEL, IN_WIDTH), sd),
        'attn_sink': nrm(ks[9], (DEPTH, N_HEADS), 0.5),
        'lam_re': -0.5 + 0.01 * jax.random.normal(ks[10], (DEPTH, N_SSM_GROUPS, SSM_STATE), F32),
        'lam_im': lam_im,
        'log_dt': jax.random.uniform(ks[12], (DEPTH, N_SSM_GROUPS), F32, math.log(DT_MIN), math.log(DT_MAX)),
        'ssm_b_re': nrm(ks[13], (DEPTH, N_SSM_GROUPS, SSM_STATE, SSM_GROUP), (2 * SSM_GROUP) ** -0.5),
        'ssm_b_im': nrm(ks[14], (DEPTH, N_SSM_GROUPS, SSM_STATE, SSM_GROUP), (2 * SSM_GROUP) ** -0.5),
        'ssm_c_re': nrm(ks[15], (DEPTH, N_SSM_GROUPS, SSM_GROUP, SSM_STATE), SSM_STATE ** -0.5),
        'ssm_c_im': nrm(ks[16], (DEPTH, N_SSM_GROUPS, SSM_GROUP, SSM_STATE), SSM_STATE ** -0.5),
        'ssm_d': nrm(ks[17], (DEPTH, SSM_WIDTH)),
        'w_glu': nrm(ks[18], (DEPTH, SSM_WIDTH, SSM_WIDTH), SSM_WIDTH ** -0.5),
        'b_glu': nrm(ks[19], (DEPTH, SSM_WIDTH), 0.01),
        'g_attn_out': gain(ks[20], (DEPTH, ATTN_WIDTH)),
        'g_ssm_out': gain(ks[21], (DEPTH, SSM_WIDTH)),
        'w_out': nrm(ks[22], (DEPTH, D_MODEL, D_MODEL), sd),
        'g_mix': gain(ks[23], (DEPTH, D_MODEL)),
        'g_xattn': gain(ks[24], (DEPTH, D_MODEL)),
        'g_mem': gain(ks[25], (DEPTH, D_MODEL)),
        'wq_x': nrm(ks[26], (DEPTH, D_MODEL, D_MODEL), sd),
        'wk_x': nrm(ks[27], (DEPTH, D_MODEL, D_MODEL), sd),
        'wv_x': nrm(ks[28], (DEPTH, D_MODEL, D_MODEL), sd),
        'wo_x': nrm(ks[29], (DEPTH, D_MODEL, D_MODEL), sd),
        'g_ffn': gain(ks[30], (DEPTH, D_MODEL)),
        'w_group': nrm(ks[31], (DEPTH, D_MODEL, N_EXPERT_GROUPS), sd),
        'b_group': nrm(ks[32], (DEPTH, N_EXPERT_GROUPS), 0.01),
        'w_router': nrm(ks[33], (DEPTH, N_EXPERT_GROUPS, D_MODEL, EXPERTS_PER_GROUP), sd),
        'b_router': nrm(ks[34], (DEPTH, N_EXPERT_GROUPS, EXPERTS_PER_GROUP), 0.01),
        'w_gate': nrm(ks[35], (DEPTH, N_EXPERTS, D_MODEL, EXPERT_FF), sd),
        'w_up': nrm(ks[36], (DEPTH, N_EXPERTS, D_MODEL, EXPERT_FF), sd),
        'w_down': nrm(ks[37], (DEPTH, N_EXPERTS, EXPERT_FF, D_MODEL), EXPERT_FF ** -0.5),
        'g_final': gain(ks[38], (D_MODEL,)),
    }


def reference(x_prompt, x_sample, cache_win_k, cache_win_v, state_ssm, cache_mem_k, cache_mem_v, mem_prompt,
              w_in, attn_sink, lam_re, lam_im, log_dt, ssm_b_re, ssm_b_im, ssm_c_re, ssm_c_im, ssm_d,
              w_glu, b_glu, g_attn_out, g_ssm_out, w_out, g_mix, g_xattn, g_mem, wq_x, wk_x, wv_x, wo_x,
              g_ffn, w_group, b_group, w_router, b_router, w_gate, w_up, w_down, g_final):
    s_len = x_prompt.shape[1]
    t_len = x_sample.shape[1]
    win_rows = cache_win_k.shape[2]
    pos_p = jnp.arange(s_len)
    pos_s = PAST_LEN + jnp.arange(t_len)
    xp, xs = x_prompt, x_sample
    wk_p, wv_p, ssm_p, mk_p_l, mv_p_l, wk_s, wv_s, ssm_s = [], [], [], [], [], [], [], []
    for l in range(DEPTH):
        lam_bar, b_bar = ssm_discretize(lam_re[l], lam_im[l], log_dt[l], ssm_b_re[l], ssm_b_im[l])
        qp, kp, vp, up = split_in_proj(rms_norm(xp, g_mix[l]) @ w_in[l])
        qs, ks_, vs, us = split_in_proj(rms_norm(xs, g_mix[l]) @ w_in[l])
        qp, kp = rope(qp, pos_p), rope(kp, pos_p)
        qs, ks_ = rope(qs, pos_s), rope(ks_, pos_s)
        attn_p = window_attn_prompt(qp, kp, vp, attn_sink[l])
        k_all = jnp.concatenate([cache_win_k[l].astype(ks_.dtype), ks_], axis=1)
        v_all = jnp.concatenate([cache_win_v[l].astype(vs.dtype), vs], axis=1)
        attn_s = window_attn_sample(qs, k_all, v_all, attn_sink[l])
        wk_p.append(kp[:, -WINDOW:])
        wv_p.append(vp[:, -WINDOW:])
        wk_s.append(k_all[:, -win_rows:])
        wv_s.append(v_all[:, -win_rows:])
        y_ssm_p, fin_p = ssm_scan(up, None, lam_bar, b_bar, ssm_c_re[l], ssm_c_im[l], ssm_d[l])
        s0 = lax.complex(state_ssm[l, ..., 0].astype(F32), state_ssm[l, ..., 1].astype(F32))
        y_ssm_s, fin_s = ssm_scan(us, s0, lam_bar, b_bar, ssm_c_re[l], ssm_c_im[l], ssm_d[l])
        ssm_p.append(jnp.stack([fin_p.real, fin_p.imag], axis=-1))
        ssm_s.append(jnp.stack([fin_s.real, fin_s.imag], axis=-1))
        xp = xp + merge_head_groups(attn_p, y_ssm_p, w_glu[l], b_glu[l], g_attn_out[l], g_ssm_out[l], w_out[l])
        xs = xs + merge_head_groups(attn_s, y_ssm_s, w_glu[l], b_glu[l], g_attn_out[l], g_ssm_out[l], w_out[l])
        mk_p, mv_p = mem_kv(mem_prompt, g_mem[l], wk_x[l], wv_x[l])
        mk_p_l.append(mk_p)
        mv_p_l.append(mv_p)
        xp = xp + cross_attend(rms_norm(xp, g_xattn[l]), wq_x[l], wo_x[l], mk_p, mv_p)
        xs = xs + cross_attend(rms_norm(xs, g_xattn[l]), wq_x[l], wo_x[l],
                               cache_mem_k[l].astype(xs.dtype), cache_mem_v[l].astype(xs.dtype))
        xp = xp + hier_moe(rms_norm(xp, g_ffn[l]), w_group[l], b_group[l], w_router[l], b_router[l],
                           w_gate[l], w_up[l], w_down[l])
        xs = xs + hier_moe(rms_norm(xs, g_ffn[l]), w_group[l], b_group[l], w_router[l], b_router[l],
                           w_gate[l], w_up[l], w_down[l])
    y_prompt = rms_norm(xp, g_final)
    y_sample = rms_norm(xs, g_final)
    new_win_k_prompt = jnp.stack(wk_p, axis=0)
    new_win_v_prompt = jnp.stack(wv_p, axis=0)
    new_ssm_prompt = jnp.stack(ssm_p, axis=0)
    new_mem_k_prompt = jnp.stack(mk_p_l, axis=0)
    new_mem_v_prompt = jnp.stack(mv_p_l, axis=0)
    new_win_k_sample = jnp.stack(wk_s, axis=0)
    new_win_v_sample = jnp.stack(wv_s, axis=0)
    new_ssm_sample = jnp.stack(ssm_s, axis=0)
    return (y_prompt, y_sample, new_win_k_prompt, new_win_v_prompt, new_ssm_prompt, new_mem_k_prompt,
            new_mem_v_prompt, new_win_k_sample, new_win_v_sample, new_ssm_sample)
```

```python
import functools
import math

import jax
import jax.numpy as jnp
from jax import lax
from jax.experimental import pallas as pl
from jax.experimental.pallas import tpu as pltpu

F32 = jnp.float32
BF16 = jnp.bfloat16

D_MODEL = 1024
CHUNK = 64
EPS = 1e-6
NEG_INF = -1e30
N_HEADS = 8
N_KV_HEADS = 2
HEAD_DIM = 64
ATTN_WIDTH = N_HEADS * HEAD_DIM
KV_WIDTH = N_KV_HEADS * HEAD_DIM
WINDOW = 128
ROPE_THETA = 10000.0
SSM_WIDTH = D_MODEL - ATTN_WIDTH
SSM_GROUP = 16
N_SSM_GROUPS = SSM_WIDTH // SSM_GROUP
SSM_STATE = 64
N_STATE = N_SSM_GROUPS * SSM_STATE
IN_WIDTH = ATTN_WIDTH + 2 * KV_WIDTH + SSM_WIDTH
N_MEM = 256
N_XHEADS = 4
XHEAD_DIM = D_MODEL // N_XHEADS
N_EXPERT_GROUPS = 4
EXPERTS_PER_GROUP = 8
N_EXPERTS = N_EXPERT_GROUPS * EXPERTS_PER_GROUP
EXPERT_FF = 128
PAST_LEN = 4096

LANES = 128
SUBLANES = 8
VMEM_LIMIT = 56 * 1024 * 1024


def _cparams(*sem):
    return pltpu.CompilerParams(dimension_semantics=sem, vmem_limit_bytes=VMEM_LIMIT)


def _rms(x, g):
    return x * lax.rsqrt(jnp.mean(x * x, axis=-1, keepdims=True) + EPS) * g


def _rope_pairs(t, cos, sin_signed, first_half):
    swapped = jnp.where(first_half, pltpu.roll(t, LANES - HEAD_DIM // 2, 1), pltpu.roll(t, HEAD_DIM // 2, 1))
    return t * cos + swapped * sin_signed


def _in_proj_kernel(x_ref, g_ref, w_ref, cos_ref, sin_ref, q_ref, k_ref, v_ref, u_ref):
    xn = _rms(x_ref[...], g_ref[...])
    z = jnp.dot(xn.astype(BF16), w_ref[...], preferred_element_type=F32)
    cos = cos_ref[...]
    sin = sin_ref[...]
    lane = lax.broadcasted_iota(jnp.int32, cos.shape, 1)
    first_half = (lane % HEAD_DIM) < (HEAD_DIM // 2)
    scale = 1.0 / math.sqrt(HEAD_DIM)
    for j in range(ATTN_WIDTH // LANES):
        t = z[:, j * LANES:(j + 1) * LANES]
        q_ref[:, j * LANES:(j + 1) * LANES] = (_rope_pairs(t, cos, sin, first_half) * scale).astype(BF16)
    k_ref[...] = _rope_pairs(z[:, ATTN_WIDTH:ATTN_WIDTH + KV_WIDTH], cos, sin, first_half)
    v_ref[...] = z[:, ATTN_WIDTH + KV_WIDTH:ATTN_WIDTH + 2 * KV_WIDTH]
    u_ref[...] = z[:, ATTN_WIDTH + 2 * KV_WIDTH:]


def _in_proj(x, g, w_bf16, cos_tab, sin_tab, n_prompt, seq, tm):
    n = x.shape[0]
    n_prompt_tiles = n_prompt // tm
    tiles_per_seq = seq // tm

    def tab_map(i):
        return (jnp.where(i < n_prompt_tiles, i % tiles_per_seq, tiles_per_seq), 0)

    row = lambda w: pl.BlockSpec((tm, w), lambda i: (i, 0))
    return pl.pallas_call(
        _in_proj_kernel,
        grid=(n // tm,),
        in_specs=[row(D_MODEL),
                  pl.BlockSpec((1, D_MODEL), lambda i: (0, 0)),
                  pl.BlockSpec((D_MODEL, IN_WIDTH), lambda i: (0, 0)),
                  pl.BlockSpec((tm, LANES), tab_map),
                  pl.BlockSpec((tm, LANES), tab_map)],
        out_specs=[row(ATTN_WIDTH), row(KV_WIDTH), row(KV_WIDTH), row(SSM_WIDTH)],
        out_shape=[jax.ShapeDtypeStruct((n, ATTN_WIDTH), BF16),
                   jax.ShapeDtypeStruct((n, KV_WIDTH), F32),
                   jax.ShapeDtypeStruct((n, KV_WIDTH), F32),
                   jax.ShapeDtypeStruct((n, SSM_WIDTH), F32)],
        compiler_params=_cparams("parallel"),
        name="in_proj",
    )(x, g, w_bf16, cos_tab, sin_tab)


def _attend_pairs(q, keys, vals, mask, sink_ref, o_ref):
    tq = q.shape[0]
    lane = lax.broadcasted_iota(jnp.int32, keys.shape, 1)
    low = lane < HEAD_DIM
    k_sw = pltpu.roll(keys, HEAD_DIM, 1)
    v_sw = pltpu.roll(vals, HEAD_DIM, 1)
    kk = [jnp.where(low, keys, k_sw).astype(BF16), jnp.where(low, k_sw, keys).astype(BF16)]
    vv = [jnp.where(low, vals, v_sw).astype(BF16), jnp.where(low, v_sw, vals).astype(BF16)]
    qlane = lax.broadcasted_iota(jnp.int32, (tq, LANES), 1)
    qlow = qlane < HEAD_DIM
    row_top = lax.broadcasted_iota(jnp.int32, (2 * tq, 1), 0) < tq
    zero = jnp.zeros((), BF16)
    for pair in range(N_HEADS // 2):
        kv = pair // (N_HEADS // N_KV_HEADS // 2)
        qp = q[:, pair * LANES:(pair + 1) * LANES]
        qs = jnp.concatenate([jnp.where(qlow, qp, zero), jnp.where(qlow, zero, qp)], axis=0)
        s = lax.dot_general(qs, kk[kv], (((1,), (1,)), ((), ())), preferred_element_type=F32)
        if mask is not None:
            s = jnp.where(mask, s, NEG_INF)
        sink = jnp.where(row_top, sink_ref[2 * pair], sink_ref[2 * pair + 1])
        m = jnp.maximum(jnp.max(s, axis=-1, keepdims=True), sink)
        e = jnp.exp(s - m)
        p = e / (jnp.sum(e, axis=-1, keepdims=True) + jnp.exp(sink - m))
        o = jnp.dot(p.astype(BF16), vv[kv], preferred_element_type=F32)
        o_ref[:, pair * LANES:(pair + 1) * LANES] = jnp.where(qlow, o[:tq], o[tq:])


def _attn_prompt_kernel(sink_ref, q_ref, kp_ref, kc_ref, vp_ref, vc_ref, o_ref):
    i = pl.program_id(1)
    tq = q_ref.shape[0]
    keys = jnp.concatenate([kp_ref[...], kc_ref[...]], axis=0)
    vals = jnp.concatenate([vp_ref[...], vc_ref[...]], axis=0)
    nk = keys.shape[0]
    r = lax.broadcasted_iota(jnp.int32, (2 * tq, nk), 0) % tq
    c = lax.broadcasted_iota(jnp.int32, (2 * tq, nk), 1)
    qc = r // CHUNK
    kc = c // CHUNK
    mask = (kc >= qc) & (kc <= qc + WINDOW // CHUNK) & ((c >= WINDOW) | (i > 0))
    _attend_pairs(q_ref[...], keys, vals, mask, sink_ref, o_ref)


def _attn_prompt(q, k, v, sink, batch, seq):
    tq = WINDOW
    nt = seq // tq
    cur = lambda b, i: (b * nt + i, 0)
    prev = lambda b, i: (b * nt + jnp.maximum(i - 1, 0), 0)
    return pl.pallas_call(
        _attn_prompt_kernel,
        grid=(batch, nt),
        in_specs=[pl.BlockSpec(memory_space=pltpu.SMEM),
                  pl.BlockSpec((tq, ATTN_WIDTH), cur),
                  pl.BlockSpec((tq, KV_WIDTH), prev),
                  pl.BlockSpec((tq, KV_WIDTH), cur),
                  pl.BlockSpec((tq, KV_WIDTH), prev),
                  pl.BlockSpec((tq, KV_WIDTH), cur)],
        out_specs=pl.BlockSpec((tq, ATTN_WIDTH), cur),
        out_shape=jax.ShapeDtypeStruct((batch * seq, ATTN_WIDTH), F32),
        compiler_params=_cparams("parallel", "parallel"),
        name="attn_prompt",
    )(sink, q, k, k, v, v)


def _attn_sample_kernel(sink_ref, q_ref, ck_ref, cv_ref, k_ref, v_ref, o_ref):
    nb = q_ref.shape[0]
    for b in range(nb):
        keys = jnp.concatenate([ck_ref[b], k_ref[b]], axis=0)
        vals = jnp.concatenate([cv_ref[b], v_ref[b]], axis=0)
        _attend_pairs(q_ref[b], keys, vals, None, sink_ref, o_ref.at[b])


def _attn_sample(q, k, v, cache_k, cache_v, sink, nb):
    b, t, _ = q.shape
    w = cache_k.shape[1]
    blk = lambda rows, width: pl.BlockSpec((nb, rows, width), lambda i: (i, 0, 0))
    return pl.pallas_call(
        _attn_sample_kernel,
        grid=(b // nb,),
        in_specs=[pl.BlockSpec(memory_space=pltpu.SMEM),
                  blk(t, ATTN_WIDTH), blk(w, KV_WIDTH), blk(w, KV_WIDTH), blk(t, KV_WIDTH), blk(t, KV_WIDTH)],
        out_specs=blk(t, ATTN_WIDTH),
        out_shape=jax.ShapeDtypeStruct((b, t, ATTN_WIDTH), F32),
        compiler_params=_cparams("parallel"),
        name="attn_sample",
    )(sink, q, cache_k, cache_v, k, v)


def _cmul(ar, ai, br, bi):
    return ar * br - ai * bi, ar * bi + ai * br


def _ssm_disc_kernel(lre_ref, lim_ref, ldt_ref, bre_ref, bim_ref, pre_ref, pim_ref, bbre_ref, bbim_ref):
    lre = lre_ref[...]
    lim = lim_ref[...]
    dt = jnp.exp(ldt_ref[...])
    mag = jnp.exp(lre * dt)
    ang = lim * dt
    lbr = mag * jnp.cos(ang)
    lbi = mag * jnp.sin(ang)
    nr, ni = lbr - 1.0, lbi
    den = lre * lre + lim * lim
    cr = (nr * lre + ni * lim) / den
    ci = (ni * lre - nr * lim) / den
    for c in range(SSM_GROUP):
        br, bi = bre_ref[c], bim_ref[c]
        bbre_ref[c], bbim_ref[c] = _cmul(cr, ci, br, bi)
    pr, pi = lbr, lbi
    for k in range(SUBLANES):
        pre_ref[k] = pr
        pim_ref[k] = pi
        pr, pi = _cmul(pr, pi, lbr, lbi)


def _ssm_discretize(lam_re, lam_im, log_dt, b_re, b_im):
    g, p = lam_re.shape
    shp = lambda k: jax.ShapeDtypeStruct((k, g, p), F32)
    return pl.pallas_call(
        _ssm_disc_kernel,
        out_shape=[shp(SUBLANES), shp(SUBLANES), shp(SSM_GROUP), shp(SSM_GROUP)],
        name="ssm_discretize",
    )(lam_re, lam_im, log_dt, b_re, b_im)


GROUPS_PER_BLOCK = LANES // SSM_GROUP
N_CH_BLOCKS = SSM_WIDTH // LANES
STATE_PER_BLOCK = GROUPS_PER_BLOCK * SSM_STATE
N_LANE_BLOCKS = N_STATE // LANES


def _ssm_kernel(u_ref, s0_ref, bmat_ref, cmat_ref, d_ref, coef_ref, y_ref, fin_ref, st_ref):
    t_idx = pl.program_id(1)
    nb, tt, _ = u_ref.shape

    @pl.when(t_idx == 0)
    def _():
        for b in range(nb):
            st_ref[b, 0:SUBLANES, :] = jnp.broadcast_to(s0_ref[b], (SUBLANES, 2 * N_STATE))

    u = u_ref[...].reshape(nb * tt, SSM_WIDTH)
    ub = u.astype(BF16)
    for m in range(N_CH_BLOCKS):
        bu = jnp.dot(ub[:, m * LANES:(m + 1) * LANES], bmat_ref[m], preferred_element_type=F32)
        bu = bu.reshape(nb, tt, 2 * STATE_PER_BLOCK)
        st_ref[:, SUBLANES:, m * STATE_PER_BLOCK:(m + 1) * STATE_PER_BLOCK] = bu[:, :, :STATE_PER_BLOCK]
        st_ref[:, SUBLANES:, N_STATE + m * STATE_PER_BLOCK:N_STATE + (m + 1) * STATE_PER_BLOCK] = bu[:, :, STATE_PER_BLOCK:]

    def block(rb, carry):
        r0 = pl.multiple_of(SUBLANES + rb * SUBLANES, SUBLANES)
        for b in range(nb):
            for lb in range(N_LANE_BLOCKS):
                re_sl = pl.ds(lb * LANES, LANES)
                im_sl = pl.ds(N_STATE + lb * LANES, LANES)
                xr = st_ref[b, pl.ds(r0, SUBLANES), re_sl]
                xi = st_ref[b, pl.ds(r0, SUBLANES), im_sl]
                for step, sh in enumerate((1, 2, 4)):
                    cr = coef_ref[step, 0, :, re_sl]
                    ci = coef_ref[step, 1, :, re_sl]
                    sr = pltpu.roll(xr, sh, 0)
                    si = pltpu.roll(xi, sh, 0)
                    ar, ai = _cmul(cr, ci, sr, si)
                    xr = xr + ar
                    xi = xi + ai
                rp = pl.multiple_of(r0 - SUBLANES, SUBLANES)
                pr = jnp.broadcast_to(st_ref[b, pl.ds(rp, SUBLANES), re_sl][SUBLANES - 1:, :], (SUBLANES, LANES))
                pi = jnp.broadcast_to(st_ref[b, pl.ds(rp, SUBLANES), im_sl][SUBLANES - 1:, :], (SUBLANES, LANES))
                ar, ai = _cmul(coef_ref[3, 0, :, re_sl], coef_ref[3, 1, :, re_sl], pr, pi)
                st_ref[b, pl.ds(r0, SUBLANES), re_sl] = xr + ar
                st_ref[b, pl.ds(r0, SUBLANES), im_sl] = xi + ai
        return carry

    lax.fori_loop(0, tt // SUBLANES, block, 0)

    s = st_ref[:, SUBLANES:, :].reshape(nb * tt, 2 * N_STATE).astype(BF16)
    d = d_ref[...]
    for m in range(N_CH_BLOCKS):
        lhs = jnp.concatenate([s[:, m * STATE_PER_BLOCK:(m + 1) * STATE_PER_BLOCK],
                               s[:, N_STATE + m * STATE_PER_BLOCK:N_STATE + (m + 1) * STATE_PER_BLOCK]], axis=1)
        y = jnp.dot(lhs, cmat_ref[m], preferred_element_type=F32)
        y = y + d[:, m * LANES:(m + 1) * LANES] * u[:, m * LANES:(m + 1) * LANES]
        y_ref[:, :, m * LANES:(m + 1) * LANES] = y.reshape(nb, tt, LANES)

    for b in range(nb):
        tail = st_ref[b, tt:tt + SUBLANES, :]
        st_ref[b, 0:SUBLANES, :] = tail
        fin_ref[b] = tail[SUBLANES - 1:SUBLANES, :]


def _ssm(u, s0, bmat, cmat, d, coef, nb, tt):
    b, t, _ = u.shape
    full = lambda shape: pl.BlockSpec(shape, lambda i, j: (0,) * len(shape))
    return pl.pallas_call(
        _ssm_kernel,
        grid=(b // nb, t // tt),
        in_specs=[pl.BlockSpec((nb, tt, SSM_WIDTH), lambda i, j: (i, j, 0)),
                  pl.BlockSpec((nb, 1, 2 * N_STATE), lambda i, j: (i, 0, 0)),
                  full(bmat.shape), full(cmat.shape), full(d.shape), full(coef.shape)],
        out_specs=[pl.BlockSpec((nb, tt, SSM_WIDTH), lambda i, j: (i, j, 0)),
                   pl.BlockSpec((nb, 1, 2 * N_STATE), lambda i, j: (i, 0, 0))],
        out_shape=[jax.ShapeDtypeStruct((b, t, SSM_WIDTH), F32),
                   jax.ShapeDtypeStruct((b, 1, 2 * N_STATE), F32)],
        scratch_shapes=[pltpu.VMEM((nb, SUBLANES + tt, 2 * N_STATE), F32)],
        compiler_params=_cparams("parallel", "arbitrary"),
        name="ssm_scan",
    )(u, s0, bmat, cmat, d, coef)


def _ssm_operands(pre, pim, bbre, bbim, c_re, c_im):
    g, p, c = N_SSM_GROUPS, SSM_STATE, SSM_GROUP
    eye = jnp.eye(GROUPS_PER_BLOCK, dtype=F32)

    def bblock(bb):
        t = bb.transpose(1, 0, 2).reshape(N_CH_BLOCKS, GROUPS_PER_BLOCK, c, p)
        return jnp.einsum('mgcp,gh->mgchp', t, eye).reshape(N_CH_BLOCKS, LANES, STATE_PER_BLOCK)

    bmat = jnp.concatenate([bblock(bbre), bblock(bbim)], axis=-1).astype(BF16)

    def cblock(cc):
        t = cc.reshape(N_CH_BLOCKS, GROUPS_PER_BLOCK, c, p)
        return jnp.einsum('mgcp,gh->mgphc', t, eye).reshape(N_CH_BLOCKS, STATE_PER_BLOCK, LANES)

    cmat = jnp.concatenate([cblock(c_re), cblock(-c_im)], axis=1).astype(BF16)

    pr = pre.reshape(SUBLANES, g * p)
    pi = pim.reshape(SUBLANES, g * p)
    rows = jnp.arange(SUBLANES)[:, None]
    tabs = []
    for sh in (1, 2, 4):
        keep = rows >= sh
        tabs.append(jnp.stack([jnp.where(keep, pr[sh - 1][None], 0.0), jnp.where(keep, pi[sh - 1][None], 0.0)]))
    tabs.append(jnp.stack([pr, pi]))
    coef = jnp.stack(tabs)
    return bmat, cmat, coef


def _merge_kernel(x_ref, a_ref, y_ref, wglu_ref, bglu_ref, ga_ref, gs_ref, wout_ref, o_ref):
    g = jax.nn.gelu(y_ref[...])
    glu = g * jax.nn.sigmoid(jnp.dot(g.astype(BF16), wglu_ref[...], preferred_element_type=F32) + bglu_ref[...])
    na = _rms(a_ref[...], ga_ref[...]).astype(BF16)
    ns = _rms(glu, gs_ref[...]).astype(BF16)
    o = jnp.dot(na, wout_ref[0:ATTN_WIDTH, :], preferred_element_type=F32)
    o = o + jnp.dot(ns, wout_ref[ATTN_WIDTH:, :], preferred_element_type=F32)
    o_ref[...] = x_ref[...] + o


def _merge(x, attn, y, wglu, bglu, ga, gs, wout, tm):
    n = x.shape[0]
    row = lambda w: pl.BlockSpec((tm, w), lambda i: (i, 0))
    full = lambda a: pl.BlockSpec(a.shape, lambda i: (0, 0))
    return pl.pallas_call(
        _merge_kernel,
        grid=(n // tm,),
        in_specs=[row(D_MODEL), row(ATTN_WIDTH), row(SSM_WIDTH),
                  full(wglu), full(bglu), full(ga), full(gs), full(wout)],
        out_specs=row(D_MODEL),
        out_shape=jax.ShapeDtypeStruct((n, D_MODEL), F32),
        compiler_params=_cparams("parallel"),
        name="merge_heads",
    )(x, attn, y, wglu, bglu, ga, gs, wout)


def _mem_kv_kernel(m_ref, g_ref, wk_ref, wv_ref, k_ref, v_ref):
    mn = _rms(m_ref[...], g_ref[...]).astype(BF16)
    k_ref[...] = jnp.dot(mn, wk_ref[...], preferred_element_type=F32)
    v_ref[...] = jnp.dot(mn, wv_ref[...], preferred_element_type=F32)


def _mem_kv(mem, g, wk, wv, tm):
    n = mem.shape[0]
    row = pl.BlockSpec((tm, D_MODEL), lambda i: (i, 0))
    full = lambda a: pl.BlockSpec(a.shape, lambda i: (0, 0))
    return pl.pallas_call(
        _mem_kv_kernel,
        grid=(n // tm,),
        in_specs=[row, full(g), full(wk), full(wv)],
        out_specs=[row, row],
        out_shape=[jax.ShapeDtypeStruct((n, D_MODEL), F32)] * 2,
        compiler_params=_cparams("parallel"),
        name="mem_kv",
    )(mem, g, wk, wv)


def _xattn_kernel(x_ref, g_ref, wq_ref, wo_ref, mk_ref, mv_ref, o_ref, att_ref):
    nb, t, _ = x_ref.shape
    x = x_ref[...].reshape(nb * t, D_MODEL)
    hn = _rms(x, g_ref[...]).astype(BF16)
    q = jnp.dot(hn, wq_ref[...], preferred_element_type=F32) * (1.0 / math.sqrt(XHEAD_DIM))
    q = q.astype(BF16)
    for b in range(nb):
        mk = mk_ref[b].astype(BF16)
        mv = mv_ref[b].astype(BF16)
        for h in range(N_XHEADS):
            sl = slice(h * XHEAD_DIM, (h + 1) * XHEAD_DIM)
            s = lax.dot_general(q[b * t:(b + 1) * t, sl], mk[:, sl], (((1,), (1,)), ((), ())),
                                preferred_element_type=F32)
            m = jnp.max(s, axis=-1, keepdims=True)
            e = jnp.exp(s - m)
            p = e / jnp.sum(e, axis=-1, keepdims=True)
            att_ref[b * t:(b + 1) * t, sl] = jnp.dot(p.astype(BF16), mv[:, sl], preferred_element_type=F32)
    o = jnp.dot(att_ref[...].astype(BF16), wo_ref[...], preferred_element_type=F32)
    o_ref[...] = (x + o).reshape(nb, t, D_MODEL)


def _xattn(x, g, wq, wo, mk, mv, nb, t):
    b, t_total, _ = x.shape
    full = lambda a: pl.BlockSpec(a.shape, lambda i, j: (0, 0))
    return pl.pallas_call(
        _xattn_kernel,
        grid=(b // nb, t_total // t),
        in_specs=[pl.BlockSpec((nb, t, D_MODEL), lambda i, j: (i, j, 0)),
                  full(g), full(wq), full(wo),
                  pl.BlockSpec((nb, N_MEM, D_MODEL), lambda i, j: (i, 0, 0)),
                  pl.BlockSpec((nb, N_MEM, D_MODEL), lambda i, j: (i, 0, 0))],
        out_specs=pl.BlockSpec((nb, t, D_MODEL), lambda i, j: (i, j, 0)),
        out_shape=jax.ShapeDtypeStruct(x.shape, F32),
        scratch_shapes=[pltpu.VMEM((nb * t, D_MODEL), F32)],
        compiler_params=_cparams("parallel", "parallel"),
        name="cross_attn",
    )(x, g, wq, wo, mk, mv)


ROUTER_LANES = LANES
EXPERT_LANE0 = N_EXPERT_GROUPS
EXPERTS_PER_STEP = 4


def _route(logits):
    lane_i = lax.broadcasted_iota(jnp.int32, logits.shape, 1)
    lane = lane_i.astype(F32)
    neg = jnp.float32(-jnp.inf)
    is_g = lane_i < N_EXPERT_GROUPS
    gl = jnp.where(is_g, logits, neg)
    gmax = jnp.max(gl, axis=-1, keepdims=True)
    gidx = jnp.min(jnp.where(gl == gmax, lane, float(ROUTER_LANES)), axis=-1, keepdims=True)
    g_w = 1.0 / jnp.sum(jnp.where(is_g, jnp.exp(gl - gmax), 0.0), axis=-1, keepdims=True)
    first = EXPERT_LANE0 + gidx * EXPERTS_PER_GROUP
    sel = (lane >= first) & (lane < first + EXPERTS_PER_GROUP)
    el = jnp.where(sel, logits, neg)
    m1 = jnp.max(el, axis=-1, keepdims=True)
    i1 = jnp.min(jnp.where(el == m1, lane, float(ROUTER_LANES)), axis=-1, keepdims=True)
    el2 = jnp.where(lane == i1, neg, el)
    m2 = jnp.max(el2, axis=-1, keepdims=True)
    i2 = jnp.min(jnp.where(el2 == m2, lane, float(ROUTER_LANES)), axis=-1, keepdims=True)
    r = jnp.exp(m2 - m1)
    w1 = g_w / (1.0 + r)
    w2 = w1 * r
    return jnp.where(lane == i1, w1, jnp.where(lane == i2, w2, 0.0))


def _moe_kernel(x_ref, g_ref, wr_ref, br_ref, wgu_ref, wd_ref, o_ref, xn_ref, gate_ref):
    e_step = pl.program_id(1)

    @pl.when(e_step == 0)
    def _():
        x = x_ref[...]
        xn = _rms(x, g_ref[...])
        xn_ref[...] = xn.astype(BF16)
        logits = jnp.dot(xn, wr_ref[...], preferred_element_type=F32,
                         precision=lax.Precision.HIGHEST) + br_ref[...]
        gate_ref[...] = _route(logits)
        o_ref[...] = x

    xn = xn_ref[...]
    gates = gate_ref[...]
    lane = lax.broadcasted_iota(jnp.int32, gates.shape, 1)
    hids = []
    for j in range(EXPERTS_PER_STEP):
        h = jnp.dot(xn, wgu_ref[j], preferred_element_type=F32)
        ge = jnp.sum(jnp.where(lane == EXPERT_LANE0 + e_step * EXPERTS_PER_STEP + j, gates, 0.0),
                     axis=-1, keepdims=True)
        hid = jax.nn.silu(h[:, :EXPERT_FF]) * h[:, EXPERT_FF:] * ge
        hids.append(hid.astype(BF16))
    hid = jnp.concatenate(hids, axis=1)
    o_ref[...] += jnp.dot(hid, wd_ref[...], preferred_element_type=F32)


def _moe(x, g, wr, br, wgu, wd, tm):
    n = x.shape[0]
    es = EXPERTS_PER_STEP
    row = pl.BlockSpec((tm, D_MODEL), lambda i, e: (i, 0))
    full = lambda a: pl.BlockSpec(a.shape, lambda i, e: (0, 0))
    return pl.pallas_call(
        _moe_kernel,
        grid=(n // tm, N_EXPERTS // es),
        in_specs=[row, full(g), full(wr), full(br),
                  pl.BlockSpec((es, D_MODEL, 2 * EXPERT_FF), lambda i, e: (e, 0, 0)),
                  pl.BlockSpec((es * EXPERT_FF, D_MODEL), lambda i, e: (e, 0))],
        out_specs=row,
        out_shape=jax.ShapeDtypeStruct((n, D_MODEL), F32),
        scratch_shapes=[pltpu.VMEM((tm, D_MODEL), BF16), pltpu.VMEM((tm, ROUTER_LANES), F32)],
        compiler_params=_cparams("parallel", "arbitrary"),
        name="hier_moe",
    )(x, g, wr, br, wgu, wd)


def _final_norm_kernel(x_ref, g_ref, o_ref):
    o_ref[...] = _rms(x_ref[...], g_ref[...])


def _final_norm(x, g, tm):
    n = x.shape[0]
    row = pl.BlockSpec((tm, D_MODEL), lambda i: (i, 0))
    return pl.pallas_call(
        _final_norm_kernel,
        grid=(n // tm,),
        in_specs=[row, pl.BlockSpec((1, D_MODEL), lambda i: (0, 0))],
        out_specs=row,
        out_shape=jax.ShapeDtypeStruct((n, D_MODEL), F32),
        compiler_params=_cparams("parallel"),
        name="final_norm",
    )(x, g)


def _rope_tables(seq, t_len, tm):
    half = HEAD_DIM // 2
    inv = ROPE_THETA ** (-jnp.arange(half, dtype=F32) / half)
    pos_s = PAST_LEN + jnp.arange(t_len)
    pos = jnp.concatenate([jnp.arange(seq), jnp.tile(pos_s, tm // t_len)]).astype(F32)
    ang = pos[:, None] * inv[None, :]
    cos = jnp.tile(jnp.cos(ang), (1, LANES // half))
    sign = jnp.where((jnp.arange(LANES) % HEAD_DIM) < half, -1.0, 1.0).astype(F32)
    sin = jnp.tile(jnp.sin(ang), (1, LANES // half)) * sign[None, :]
    return cos, sin


def _state_to_lanes(s):
    b = s.shape[0]
    return jnp.concatenate([s[..., 0].reshape(b, 1, N_STATE), s[..., 1].reshape(b, 1, N_STATE)], axis=-1)


def _lanes_to_state(f):
    b = f.shape[0]
    return jnp.stack([f[:, 0, :N_STATE].reshape(b, N_SSM_GROUPS, SSM_STATE),
                      f[:, 0, N_STATE:].reshape(b, N_SSM_GROUPS, SSM_STATE)], axis=-1)


def kernel(x_prompt, x_sample, cache_win_k, cache_win_v, state_ssm, cache_mem_k, cache_mem_v, mem_prompt, w_in, attn_sink, lam_re, lam_im, log_dt, ssm_b_re, ssm_b_im, ssm_c_re, ssm_c_im, ssm_d, w_glu, b_glu, g_attn_out, g_ssm_out, w_out, g_mix, g_xattn, g_mem, wq_x, wk_x, wv_x, wo_x, g_ffn, w_group, b_group, w_router, b_router, w_gate, w_up, w_down, g_final):
    batch, seq, _ = x_prompt.shape
    dec_batch, t_len, _ = x_sample.shape
    depth = w_in.shape[0]
    win_rows = cache_win_k.shape[2]
    n_p = batch * seq
    n_s = dec_batch * t_len
    tm = 512
    tm_wide = 1024 if (n_p + n_s) % 1024 == 0 else 512

    x = jnp.concatenate([x_prompt.reshape(n_p, D_MODEL), x_sample.reshape(n_s, D_MODEL)], axis=0)
    cos_tab, sin_tab = _rope_tables(seq, t_len, tm)
    mem_flat = mem_prompt.reshape(batch * N_MEM, D_MODEL)
    zero_state = jnp.zeros((batch, 1, 2 * N_STATE), F32)
    row2 = lambda a: a.reshape(1, -1)

    outs = {k: [] for k in ("wk_p", "wv_p", "ssm_p", "mk_p", "mv_p", "wk_s", "wv_s", "ssm_s")}
    for l in range(depth):
        q, k, v, u = _in_proj(x, row2(g_mix[l]), w_in[l].astype(BF16), cos_tab, sin_tab, n_p, seq, tm)
        attn_p = _attn_prompt(q[:n_p], k[:n_p], v[:n_p], attn_sink[l], batch, seq)
        ks = k[n_p:].reshape(dec_batch, t_len, KV_WIDTH)
        vs = v[n_p:].reshape(dec_batch, t_len, KV_WIDTH)
        ck = cache_win_k[l].reshape(dec_batch, win_rows, KV_WIDTH)
        cv = cache_win_v[l].reshape(dec_batch, win_rows, KV_WIDTH)
        attn_s = _attn_sample(q[n_p:].reshape(dec_batch, t_len, ATTN_WIDTH), ks, vs, ck, cv, attn_sink[l], 4)
        kp = k[:n_p].reshape(batch, seq, N_KV_HEADS, HEAD_DIM)
        vp = v[:n_p].reshape(batch, seq, N_KV_HEADS, HEAD_DIM)
        outs["wk_p"].append(kp[:, -WINDOW:])
        outs["wv_p"].append(vp[:, -WINDOW:])
        k_all = jnp.concatenate([ck, ks], axis=1)[:, -win_rows:]
        v_all = jnp.concatenate([cv, vs], axis=1)[:, -win_rows:]
        outs["wk_s"].append(k_all.reshape(dec_batch, win_rows, N_KV_HEADS, HEAD_DIM))
        outs["wv_s"].append(v_all.reshape(dec_batch, win_rows, N_KV_HEADS, HEAD_DIM))

        pre, pim, bbre, bbim = _ssm_discretize(lam_re[l], lam_im[l], log_dt[l][:, None],
                                               ssm_b_re[l].transpose(2, 0, 1), ssm_b_im[l].transpose(2, 0, 1))
        bmat, cmat, coef = _ssm_operands(pre, pim, bbre, bbim, ssm_c_re[l], ssm_c_im[l])
        d = row2(ssm_d[l])
        y_p, fin_p = _ssm(u[:n_p].reshape(batch, seq, SSM_WIDTH), zero_state, bmat, cmat, d, coef, batch, 128)
        y_s, fin_s = _ssm(u[n_p:].reshape(dec_batch, t_len, SSM_WIDTH), _state_to_lanes(state_ssm[l]),
                          bmat, cmat, d, coef, 8, t_len)
        outs["ssm_p"].append(_lanes_to_state(fin_p))
        outs["ssm_s"].append(_lanes_to_state(fin_s))
        attn = jnp.concatenate([attn_p, attn_s.reshape(n_s, ATTN_WIDTH)], axis=0)
        y = jnp.concatenate([y_p.reshape(n_p, SSM_WIDTH), y_s.reshape(n_s, SSM_WIDTH)], axis=0)
        x = _merge(x, attn, y, w_glu[l].astype(BF16), row2(b_glu[l]), row2(g_attn_out[l]), row2(g_ssm_out[l]),
                   w_out[l].astype(BF16), tm)

        mk_p, mv_p = _mem_kv(mem_flat, row2(g_mem[l]), wk_x[l].astype(BF16), wv_x[l].astype(BF16), 512)
        outs["mk_p"].append(mk_p.reshape(batch, N_MEM, N_XHEADS, XHEAD_DIM))
        outs["mv_p"].append(mv_p.reshape(batch, N_MEM, N_XHEADS, XHEAD_DIM))
        wq = wq_x[l].astype(BF16)
        wo = wo_x[l].astype(BF16)
        gx = row2(g_xattn[l])
        xp = _xattn(x[:n_p].reshape(batch, seq, D_MODEL), gx, wq, wo,
                    mk_p.reshape(batch, N_MEM, D_MODEL), mv_p.reshape(batch, N_MEM, D_MODEL), 1, 512)
        xs = _xattn(x[n_p:].reshape(dec_batch, t_len, D_MODEL), gx, wq, wo,
                    cache_mem_k[l].reshape(dec_batch, N_MEM, D_MODEL),
                    cache_mem_v[l].reshape(dec_batch, N_MEM, D_MODEL), 8, t_len)
        x = jnp.concatenate([xp.reshape(n_p, D_MODEL), xs.reshape(n_s, D_MODEL)], axis=0)

        wr = jnp.concatenate([w_group[l], w_router[l].transpose(1, 0, 2).reshape(D_MODEL, N_EXPERTS)], axis=1)
        wr = jnp.pad(wr, ((0, 0), (0, ROUTER_LANES - wr.shape[1])))
        br = jnp.concatenate([b_group[l], b_router[l].reshape(N_EXPERTS)])
        br = jnp.pad(br, (0, ROUTER_LANES - br.shape[0])).reshape(1, ROUTER_LANES)
        wgu = jnp.concatenate([w_gate[l], w_up[l]], axis=-1).astype(BF16)
        wd = w_down[l].reshape(N_EXPERTS * EXPERT_FF, D_MODEL).astype(BF16)
        x = _moe(x, row2(g_ffn[l]), wr, br, wgu, wd, tm_wide)

    y = _final_norm(x, row2(g_final), tm_wide)
    st = lambda name: jnp.stack(outs[name], axis=0)
    return (y[:n_p].reshape(batch, seq, D_MODEL), y[n_p:].reshape(dec_batch, t_len, D_MODEL),
            st("wk_p"), st("wv_p"), st("ssm_p"), st("mk_p"), st("mv_p"), st("wk_s"), st("wv_s"), st("ssm_s"))
```

```python
import functools
import math

import jax
import jax.numpy as jnp
from jax import lax
from jax.experimental import pallas as pl
from jax.experimental.pallas import tpu as pltpu

F32 = jnp.float32
BF16 = jnp.bfloat16

D_MODEL = 1024
CHUNK = 64
EPS = 1e-6
NEG_INF = -1e30
N_HEADS = 8
N_KV_HEADS = 2
HEAD_DIM = 64
ATTN_WIDTH = N_HEADS * HEAD_DIM
KV_WIDTH = N_KV_HEADS * HEAD_DIM
WINDOW = 128
ROPE_THETA = 10000.0
SSM_WIDTH = D_MODEL - ATTN_WIDTH
SSM_GROUP = 16
N_SSM_GROUPS = SSM_WIDTH // SSM_GROUP
SSM_STATE = 64
N_STATE = N_SSM_GROUPS * SSM_STATE
IN_WIDTH = ATTN_WIDTH + 2 * KV_WIDTH + SSM_WIDTH
N_MEM = 256
N_XHEADS = 4
XHEAD_DIM = D_MODEL // N_XHEADS
N_EXPERT_GROUPS = 4
EXPERTS_PER_GROUP = 8
N_EXPERTS = N_EXPERT_GROUPS * EXPERTS_PER_GROUP
EXPERT_FF = 128
PAST_LEN = 4096

LANES = 128
SUBLANES = 8
VMEM_LIMIT = 56 * 1024 * 1024


def _cparams(*sem):
    return pltpu.CompilerParams(dimension_semantics=sem, vmem_limit_bytes=VMEM_LIMIT)


def _rms(x, g):
    return x * lax.rsqrt(jnp.mean(x * x, axis=-1, keepdims=True) + EPS) * g


def _layer_spec(arr, l):
    shape = arr.shape[1:]
    zeros = (0,) * len(shape)
    return pl.BlockSpec((None,) + shape, lambda *_: (l,) + zeros)


def _rope_pairs(t, cos, sin_signed, first_half):
    swapped = jnp.where(first_half, pltpu.roll(t, LANES - HEAD_DIM // 2, 1), pltpu.roll(t, HEAD_DIM // 2, 1))
    return t * cos + swapped * sin_signed


def _in_proj_kernel(x_ref, g_ref, w_ref, cos_ref, sin_ref, q_ref, k_ref, v_ref, u_ref):
    xn = _rms(x_ref[...], g_ref[...])
    z = jnp.dot(xn.astype(BF16), w_ref[...], preferred_element_type=F32)
    cos = cos_ref[...]
    sin = sin_ref[...]
    lane = lax.broadcasted_iota(jnp.int32, cos.shape, 1)
    first_half = (lane % HEAD_DIM) < (HEAD_DIM // 2)
    scale = 1.0 / math.sqrt(HEAD_DIM)
    for j in range(ATTN_WIDTH // LANES):
        t = z[:, j * LANES:(j + 1) * LANES]
        q_ref[:, j * LANES:(j + 1) * LANES] = (_rope_pairs(t, cos, sin, first_half) * scale).astype(BF16)
    k_ref[...] = _rope_pairs(z[:, ATTN_WIDTH:ATTN_WIDTH + KV_WIDTH], cos, sin, first_half)
    v_ref[...] = z[:, ATTN_WIDTH + KV_WIDTH:ATTN_WIDTH + 2 * KV_WIDTH]
    u_ref[...] = z[:, ATTN_WIDTH + 2 * KV_WIDTH:]


def _in_proj(x, g, w_bf16, l, cos_tab, sin_tab, n_prompt, seq, tm):
    n = x.shape[0]
    n_prompt_tiles = n_prompt // tm
    tiles_per_seq = seq // tm

    def tab_map(i):
        return (jnp.where(i < n_prompt_tiles, i % tiles_per_seq, tiles_per_seq), 0)

    row = lambda w: pl.BlockSpec((tm, w), lambda i: (i, 0))
    return pl.pallas_call(
        _in_proj_kernel,
        grid=(n // tm,),
        in_specs=[row(D_MODEL), _layer_spec(g, l), _layer_spec(w_bf16, l),
                  pl.BlockSpec((tm, LANES), tab_map),
                  pl.BlockSpec((tm, LANES), tab_map)],
        out_specs=[row(ATTN_WIDTH), row(KV_WIDTH), row(KV_WIDTH), row(SSM_WIDTH)],
        out_shape=[jax.ShapeDtypeStruct((n, ATTN_WIDTH), BF16),
                   jax.ShapeDtypeStruct((n, KV_WIDTH), F32),
                   jax.ShapeDtypeStruct((n, KV_WIDTH), F32),
                   jax.ShapeDtypeStruct((n, SSM_WIDTH), F32)],
        compiler_params=_cparams("parallel"),
        name="in_proj",
    )(x, g, w_bf16, cos_tab, sin_tab)


def _attend_pairs(q, keys, vals, mask, sink_ref, o_ref, row0):
    tq = q.shape[0]
    lane = lax.broadcasted_iota(jnp.int32, keys.shape, 1)
    low = lane < HEAD_DIM
    k_sw = pltpu.roll(keys, HEAD_DIM, 1)
    v_sw = pltpu.roll(vals, HEAD_DIM, 1)
    kk = [jnp.where(low, keys, k_sw).astype(BF16), jnp.where(low, k_sw, keys).astype(BF16)]
    vv = [jnp.where(low, vals, v_sw).astype(BF16), jnp.where(low, v_sw, vals).astype(BF16)]
    qlane = lax.broadcasted_iota(jnp.int32, (tq, LANES), 1)
    qlow = qlane < HEAD_DIM
    row_top = lax.broadcasted_iota(jnp.int32, (2 * tq, 1), 0) < tq
    zero = jnp.zeros((), BF16)
    for pair in range(N_HEADS // 2):
        kv = pair // (N_HEADS // N_KV_HEADS // 2)
        qp = q[:, pair * LANES:(pair + 1) * LANES]
        qs = jnp.concatenate([jnp.where(qlow, qp, zero), jnp.where(qlow, zero, qp)], axis=0)
        s = lax.dot_general(qs, kk[kv], (((1,), (1,)), ((), ())), preferred_element_type=F32)
        if mask is not None:
            s = jnp.where(mask, s, NEG_INF)
        sink = jnp.where(row_top, sink_ref[2 * pair], sink_ref[2 * pair + 1])
        m = jnp.maximum(jnp.max(s, axis=-1, keepdims=True), sink)
        e = jnp.exp(s - m)
        p = e / (jnp.sum(e, axis=-1, keepdims=True) + jnp.exp(sink - m))
        o = jnp.dot(p.astype(BF16), vv[kv], preferred_element_type=F32)
        o_ref[row0:row0 + tq, pair * LANES:(pair + 1) * LANES] = jnp.where(qlow, o[:tq], o[tq:])


def _attn_prompt_kernel(sink_ref, q_ref, kp_ref, kc_ref, vp_ref, vc_ref, o_ref):
    i = pl.program_id(1)
    tq = q_ref.shape[0]
    keys = jnp.concatenate([kp_ref[...], kc_ref[...]], axis=0)
    vals = jnp.concatenate([vp_ref[...], vc_ref[...]], axis=0)
    nk = keys.shape[0]
    r = lax.broadcasted_iota(jnp.int32, (2 * tq, nk), 0) % tq
    c = lax.broadcasted_iota(jnp.int32, (2 * tq, nk), 1)
    qc = r // CHUNK
    kc = c // CHUNK
    mask = (kc >= qc) & (kc <= qc + WINDOW // CHUNK) & ((c >= WINDOW) | (i > 0))
    _attend_pairs(q_ref[...], keys, vals, mask, sink_ref, o_ref, 0)


def _attn_prompt(q, k, v, sink, batch, seq):
    tq = WINDOW
    nt = seq // tq
    cur = lambda b, i: (b * nt + i, 0)
    prev = lambda b, i: (b * nt + jnp.maximum(i - 1, 0), 0)
    return pl.pallas_call(
        _attn_prompt_kernel,
        grid=(batch, nt),
        in_specs=[pl.BlockSpec(memory_space=pltpu.SMEM),
                  pl.BlockSpec((tq, ATTN_WIDTH), cur),
                  pl.BlockSpec((tq, KV_WIDTH), prev),
                  pl.BlockSpec((tq, KV_WIDTH), cur),
                  pl.BlockSpec((tq, KV_WIDTH), prev),
                  pl.BlockSpec((tq, KV_WIDTH), cur)],
        out_specs=pl.BlockSpec((tq, ATTN_WIDTH), cur),
        out_shape=jax.ShapeDtypeStruct((batch * seq, ATTN_WIDTH), F32),
        compiler_params=_cparams("parallel", "parallel"),
        name="attn_prompt",
    )(sink, q, k, k, v, v)


def _attn_sample_kernel(sink_ref, q_ref, ck_ref, cv_ref, k_ref, v_ref, o_ref):
    nb = ck_ref.shape[0]
    t = q_ref.shape[0] // nb
    for b in range(nb):
        rows = slice(b * t, (b + 1) * t)
        keys = jnp.concatenate([ck_ref[b], k_ref[rows, :]], axis=0)
        vals = jnp.concatenate([cv_ref[b], v_ref[rows, :]], axis=0)
        _attend_pairs(q_ref[rows, :], keys, vals, None, sink_ref, o_ref, b * t)


def _attn_sample(q, k, v, cache_k, cache_v, l, sink, n_prompt, dec_batch, t, nb):
    w = cache_k.shape[2]
    rows = nb * t
    base = n_prompt // rows
    tok = lambda width: pl.BlockSpec((rows, width), lambda i: (base + i, 0))
    cache = pl.BlockSpec((None, nb, w, KV_WIDTH), lambda i: (l, i, 0, 0))
    return pl.pallas_call(
        _attn_sample_kernel,
        grid=(dec_batch // nb,),
        in_specs=[pl.BlockSpec(memory_space=pltpu.SMEM),
                  tok(ATTN_WIDTH), cache, cache, tok(KV_WIDTH), tok(KV_WIDTH)],
        out_specs=pl.BlockSpec((rows, ATTN_WIDTH), lambda i: (i, 0)),
        out_shape=jax.ShapeDtypeStruct((dec_batch * t, ATTN_WIDTH), F32),
        compiler_params=_cparams("parallel"),
        name="attn_sample",
    )(sink, q, cache_k, cache_v, k, v)


def _cmul(ar, ai, br, bi):
    return ar * br - ai * bi, ar * bi + ai * br


def _ssm_disc_kernel(lre_ref, lim_ref, ldt_ref, bre_ref, bim_ref, pre_ref, pim_ref, bbre_ref, bbim_ref):
    lre = lre_ref[...]
    lim = lim_ref[...]
    dt = jnp.exp(ldt_ref[...])
    mag = jnp.exp(lre * dt)
    ang = lim * dt
    lbr = mag * jnp.cos(ang)
    lbi = mag * jnp.sin(ang)
    nr, ni = lbr - 1.0, lbi
    den = lre * lre + lim * lim
    cr = (nr * lre + ni * lim) / den
    ci = (ni * lre - nr * lim) / den
    for c in range(SSM_GROUP):
        br, bi = bre_ref[c], bim_ref[c]
        bbre_ref[c], bbim_ref[c] = _cmul(cr, ci, br, bi)
    pr, pi = lbr, lbi
    for k in range(SUBLANES):
        pre_ref[k] = pr
        pim_ref[k] = pi
        pr, pi = _cmul(pr, pi, lbr, lbi)


def _ssm_discretize(lam_re, lam_im, log_dt, b_re, b_im):
    g, p = lam_re.shape
    shp = lambda k: jax.ShapeDtypeStruct((k, g, p), F32)
    return pl.pallas_call(
        _ssm_disc_kernel,
        out_shape=[shp(SUBLANES), shp(SUBLANES), shp(SSM_GROUP), shp(SSM_GROUP)],
        name="ssm_discretize",
    )(lam_re, lam_im, log_dt, b_re, b_im)


GROUPS_PER_BLOCK = LANES // SSM_GROUP
N_CH_BLOCKS = SSM_WIDTH // LANES
STATE_PER_BLOCK = GROUPS_PER_BLOCK * SSM_STATE
N_LANE_BLOCKS = N_STATE // LANES


def _ssm_kernel(u_ref, s0_ref, bmat_ref, cmat_ref, d_ref, coef_ref, y_ref, fin_ref, st_ref):
    t_idx = pl.program_id(1)
    nb = st_ref.shape[0]
    tt = st_ref.shape[1] - SUBLANES

    @pl.when(t_idx == 0)
    def _():
        for b in range(nb):
            st_ref[b, 0:SUBLANES, :] = jnp.broadcast_to(s0_ref[b], (SUBLANES, 2 * N_STATE))

    u = u_ref[...]
    ub = u.astype(BF16)
    for m in range(N_CH_BLOCKS):
        bu = jnp.dot(ub[:, m * LANES:(m + 1) * LANES], bmat_ref[m], preferred_element_type=F32)
        bu = bu.reshape(nb, tt, 2 * STATE_PER_BLOCK)
        st_ref[:, SUBLANES:, m * STATE_PER_BLOCK:(m + 1) * STATE_PER_BLOCK] = bu[:, :, :STATE_PER_BLOCK]
        st_ref[:, SUBLANES:, N_STATE + m * STATE_PER_BLOCK:N_STATE + (m + 1) * STATE_PER_BLOCK] = bu[:, :, STATE_PER_BLOCK:]

    def block(rb, carry):
        r0 = pl.multiple_of(SUBLANES + rb * SUBLANES, SUBLANES)
        for b in range(nb):
            for lb in range(N_LANE_BLOCKS):
                re_sl = pl.ds(lb * LANES, LANES)
                im_sl = pl.ds(N_STATE + lb * LANES, LANES)
                xr = st_ref[b, pl.ds(r0, SUBLANES), re_sl]
                xi = st_ref[b, pl.ds(r0, SUBLANES), im_sl]
                for step, sh in enumerate((1, 2, 4)):
                    cr = coef_ref[step, 0, :, re_sl]
                    ci = coef_ref[step, 1, :, re_sl]
                    sr = pltpu.roll(xr, sh, 0)
                    si = pltpu.roll(xi, sh, 0)
                    ar, ai = _cmul(cr, ci, sr, si)
                    xr = xr + ar
                    xi = xi + ai
                rp = pl.multiple_of(r0 - SUBLANES, SUBLANES)
                pr = jnp.broadcast_to(st_ref[b, pl.ds(rp, SUBLANES), re_sl][SUBLANES - 1:, :], (SUBLANES, LANES))
                pi = jnp.broadcast_to(st_ref[b, pl.ds(rp, SUBLANES), im_sl][SUBLANES - 1:, :], (SUBLANES, LANES))
                ar, ai = _cmul(coef_ref[3, 0, :, re_sl], coef_ref[3, 1, :, re_sl], pr, pi)
                st_ref[b, pl.ds(r0, SUBLANES), re_sl] = xr + ar
                st_ref[b, pl.ds(r0, SUBLANES), im_sl] = xi + ai
        return carry

    lax.fori_loop(0, tt // SUBLANES, block, 0)

    s = st_ref[:, SUBLANES:, :].reshape(nb * tt, 2 * N_STATE).astype(BF16)
    d = d_ref[...]
    for m in range(N_CH_BLOCKS):
        lhs = jnp.concatenate([s[:, m * STATE_PER_BLOCK:(m + 1) * STATE_PER_BLOCK],
                               s[:, N_STATE + m * STATE_PER_BLOCK:N_STATE + (m + 1) * STATE_PER_BLOCK]], axis=1)
        y = jnp.dot(lhs, cmat_ref[m], preferred_element_type=F32)
        y_ref[:, m * LANES:(m + 1) * LANES] = y + d[:, m * LANES:(m + 1) * LANES] * u[:, m * LANES:(m + 1) * LANES]

    for b in range(nb):
        tail = st_ref[b, tt:tt + SUBLANES, :]
        st_ref[b, 0:SUBLANES, :] = tail
        fin_ref[b] = tail[SUBLANES - 1:SUBLANES, :]


def _ssm(u, s0, bmat, cmat, d, l, coef, row_base, n_seq, seq_len, nb, tt):
    rows = nb * tt
    nt = seq_len // tt
    base = row_base // rows
    full = lambda a: pl.BlockSpec(a.shape, lambda i, j: (0,) * a.ndim)
    return pl.pallas_call(
        _ssm_kernel,
        grid=(n_seq // nb, nt),
        in_specs=[pl.BlockSpec((rows, SSM_WIDTH), lambda i, j: (base + i * nt + j, 0)),
                  pl.BlockSpec((nb, 1, 2 * N_STATE), lambda i, j: (i, 0, 0)),
                  full(bmat), full(cmat), _layer_spec(d, l), full(coef)],
        out_specs=[pl.BlockSpec((rows, SSM_WIDTH), lambda i, j: (i * nt + j, 0)),
                   pl.BlockSpec((nb, 1, 2 * N_STATE), lambda i, j: (i, 0, 0))],
        out_shape=[jax.ShapeDtypeStruct((n_seq * seq_len, SSM_WIDTH), F32),
                   jax.ShapeDtypeStruct((n_seq, 1, 2 * N_STATE), F32)],
        scratch_shapes=[pltpu.VMEM((nb, SUBLANES + tt, 2 * N_STATE), F32)],
        compiler_params=_cparams("parallel", "arbitrary"),
        name="ssm_scan",
    )(u, s0, bmat, cmat, d, coef)


def _ssm_operands(pre, pim, bbre, bbim, c_re, c_im):
    g, p, c = N_SSM_GROUPS, SSM_STATE, SSM_GROUP
    eye = jnp.eye(GROUPS_PER_BLOCK, dtype=F32)

    def bblock(bb):
        t = bb.transpose(1, 0, 2).reshape(N_CH_BLOCKS, GROUPS_PER_BLOCK, c, p)
        return jnp.einsum('mgcp,gh->mgchp', t, eye).reshape(N_CH_BLOCKS, LANES, STATE_PER_BLOCK)

    bmat = jnp.concatenate([bblock(bbre), bblock(bbim)], axis=-1).astype(BF16)

    def cblock(cc):
        t = cc.reshape(N_CH_BLOCKS, GROUPS_PER_BLOCK, c, p)
        return jnp.einsum('mgcp,gh->mgphc', t, eye).reshape(N_CH_BLOCKS, STATE_PER_BLOCK, LANES)

    cmat = jnp.concatenate([cblock(c_re), cblock(-c_im)], axis=1).astype(BF16)

    pr = pre.reshape(SUBLANES, g * p)
    pi = pim.reshape(SUBLANES, g * p)
    rows = jnp.arange(SUBLANES)[:, None]
    tabs = []
    for sh in (1, 2, 4):
        keep = rows >= sh
        tabs.append(jnp.stack([jnp.where(keep, pr[sh - 1][None], 0.0), jnp.where(keep, pi[sh - 1][None], 0.0)]))
    tabs.append(jnp.stack([pr, pi]))
    coef = jnp.stack(tabs)
    return bmat, cmat, coef


def _merge_kernel(n_prompt_tiles, x_ref, ap_ref, as_ref, yp_ref, ys_ref, wglu_ref, bglu_ref, ga_ref, gs_ref,
                  wout_ref, o_ref):
    is_prompt = pl.program_id(0) < n_prompt_tiles
    attn = jnp.where(is_prompt, ap_ref[...], as_ref[...])
    g = jax.nn.gelu(jnp.where(is_prompt, yp_ref[...], ys_ref[...]))
    glu = g * jax.nn.sigmoid(jnp.dot(g.astype(BF16), wglu_ref[...], preferred_element_type=F32) + bglu_ref[...])
    na = _rms(attn, ga_ref[...]).astype(BF16)
    ns = _rms(glu, gs_ref[...]).astype(BF16)
    o = jnp.dot(na, wout_ref[0:ATTN_WIDTH, :], preferred_element_type=F32)
    o = o + jnp.dot(ns, wout_ref[ATTN_WIDTH:, :], preferred_element_type=F32)
    o_ref[...] = x_ref[...] + o


def _merge(x, attn_p, attn_s, y_p, y_s, wglu, bglu, ga, gs, wout, l, tm):
    n = x.shape[0]
    npt = attn_p.shape[0] // tm
    nst = attn_s.shape[0] // tm
    row = lambda w: pl.BlockSpec((tm, w), lambda i: (i, 0))
    prow = lambda w: pl.BlockSpec((tm, w), lambda i: (jnp.minimum(i, npt - 1), 0))
    srow = lambda w: pl.BlockSpec((tm, w), lambda i: (jnp.clip(i - npt, 0, nst - 1), 0))
    return pl.pallas_call(
        functools.partial(_merge_kernel, npt),
        grid=(n // tm,),
        in_specs=[row(D_MODEL), prow(ATTN_WIDTH), srow(ATTN_WIDTH), prow(SSM_WIDTH), srow(SSM_WIDTH),
                  _layer_spec(wglu, l), _layer_spec(bglu, l), _layer_spec(ga, l), _layer_spec(gs, l),
                  _layer_spec(wout, l)],
        out_specs=row(D_MODEL),
        out_shape=jax.ShapeDtypeStruct((n, D_MODEL), F32),
        compiler_params=_cparams("parallel"),
        name="merge_heads",
    )(x, attn_p, attn_s, y_p, y_s, wglu, bglu, ga, gs, wout)


def _mem_kv_kernel(m_ref, g_ref, wk_ref, wv_ref, k_ref, v_ref):
    mn = _rms(m_ref[...], g_ref[...]).astype(BF16)
    k_ref[...] = jnp.dot(mn, wk_ref[...], preferred_element_type=F32)
    v_ref[...] = jnp.dot(mn, wv_ref[...], preferred_element_type=F32)


def _mem_kv(mem, g, wk, wv, l, tm):
    n = mem.shape[0]
    row = pl.BlockSpec((tm, D_MODEL), lambda i: (i, 0))
    return pl.pallas_call(
        _mem_kv_kernel,
        grid=(n // tm,),
        in_specs=[row, _layer_spec(g, l), _layer_spec(wk, l), _layer_spec(wv, l)],
        out_specs=[row, row],
        out_shape=[jax.ShapeDtypeStruct((n, D_MODEL), F32)] * 2,
        compiler_params=_cparams("parallel"),
        name="mem_kv",
    )(mem, g, wk, wv)


def _xattn_kernel(nb, x_ref, g_ref, wq_ref, wo_ref, mk_ref, mv_ref, o_ref, att_ref):
    t = x_ref.shape[0] // nb
    x = x_ref[...]
    hn = _rms(x, g_ref[...]).astype(BF16)
    q = jnp.dot(hn, wq_ref[...], preferred_element_type=F32) * (1.0 / math.sqrt(XHEAD_DIM))
    q = q.astype(BF16)
    for b in range(nb):
        mk = mk_ref[b * N_MEM:(b + 1) * N_MEM, :].astype(BF16)
        mv = mv_ref[b * N_MEM:(b + 1) * N_MEM, :].astype(BF16)
        for h in range(N_XHEADS):
            sl = slice(h * XHEAD_DIM, (h + 1) * XHEAD_DIM)
            s = lax.dot_general(q[b * t:(b + 1) * t, sl], mk[:, sl], (((1,), (1,)), ((), ())),
                                preferred_element_type=F32)
            m = jnp.max(s, axis=-1, keepdims=True)
            e = jnp.exp(s - m)
            p = e / jnp.sum(e, axis=-1, keepdims=True)
            att_ref[b * t:(b + 1) * t, sl] = jnp.dot(p.astype(BF16), mv[:, sl], preferred_element_type=F32)
    o = jnp.dot(att_ref[...].astype(BF16), wo_ref[...], preferred_element_type=F32)
    o_ref[...] = x + o


def _xattn(x, g, wq, wo, l, mk, mv, mem_spec, row_base, n_rows, nb, tm):
    base = row_base // tm
    xspec = pl.BlockSpec((tm, D_MODEL), lambda i: (base + i, 0))
    return pl.pallas_call(
        functools.partial(_xattn_kernel, nb),
        grid=(n_rows // tm,),
        in_specs=[xspec, _layer_spec(g, l), _layer_spec(wq, l), _layer_spec(wo, l), mem_spec, mem_spec],
        out_specs=xspec,
        out_shape=jax.ShapeDtypeStruct(x.shape, F32),
        scratch_shapes=[pltpu.VMEM((tm, D_MODEL), F32)],
        input_output_aliases={0: 0},
        compiler_params=_cparams("parallel"),
        name="cross_attn",
    )(x, g, wq, wo, mk, mv)


ROUTER_LANES = LANES
EXPERT_LANE0 = N_EXPERT_GROUPS
EXPERTS_PER_STEP = 4


def _route(logits):
    lane_i = lax.broadcasted_iota(jnp.int32, logits.shape, 1)
    lane = lane_i.astype(F32)
    neg = jnp.float32(-jnp.inf)
    is_g = lane_i < N_EXPERT_GROUPS
    gl = jnp.where(is_g, logits, neg)
    gmax = jnp.max(gl, axis=-1, keepdims=True)
    gidx = jnp.min(jnp.where(gl == gmax, lane, float(ROUTER_LANES)), axis=-1, keepdims=True)
    g_w = 1.0 / jnp.sum(jnp.where(is_g, jnp.exp(gl - gmax), 0.0), axis=-1, keepdims=True)
    first = EXPERT_LANE0 + gidx * EXPERTS_PER_GROUP
    sel = (lane >= first) & (lane < first + EXPERTS_PER_GROUP)
    el = jnp.where(sel, logits, neg)
    m1 = jnp.max(el, axis=-1, keepdims=True)
    i1 = jnp.min(jnp.where(el == m1, lane, float(ROUTER_LANES)), axis=-1, keepdims=True)
    el2 = jnp.where(lane == i1, neg, el)
    m2 = jnp.max(el2, axis=-1, keepdims=True)
    i2 = jnp.min(jnp.where(el2 == m2, lane, float(ROUTER_LANES)), axis=-1, keepdims=True)
    r = jnp.exp(m2 - m1)
    w1 = g_w / (1.0 + r)
    w2 = w1 * r
    return jnp.where(lane == i1, w1, jnp.where(lane == i2, w2, 0.0))


def _moe_kernel(x_ref, g_ref, wr_ref, br_ref, wgu_ref, wd_ref, o_ref, xn_ref, gate_ref):
    e_step = pl.program_id(1)

    @pl.when(e_step == 0)
    def _():
        x = x_ref[...]
        xn = _rms(x, g_ref[...])
        xn_ref[...] = xn.astype(BF16)
        logits = jnp.dot(xn, wr_ref[...], preferred_element_type=F32,
                         precision=lax.Precision.HIGHEST) + br_ref[...]
        gate_ref[...] = _route(logits)
        o_ref[...] = x

    xn = xn_ref[...]
    gates = gate_ref[...]
    lane = lax.broadcasted_iota(jnp.int32, gates.shape, 1)
    hids = []
    for j in range(EXPERTS_PER_STEP):
        h = jnp.dot(xn, wgu_ref[j], preferred_element_type=F32)
        ge = jnp.sum(jnp.where(lane == EXPERT_LANE0 + e_step * EXPERTS_PER_STEP + j, gates, 0.0),
                     axis=-1, keepdims=True)
        hid = jax.nn.silu(h[:, :EXPERT_FF]) * h[:, EXPERT_FF:] * ge
        hids.append(hid.astype(BF16))
    hid = jnp.concatenate(hids, axis=1)
    o_ref[...] += jnp.dot(hid, wd_ref[...], preferred_element_type=F32)


def _moe(x, g, wr, br, wgu, wd, l, tm):
    n = x.shape[0]
    es = EXPERTS_PER_STEP
    row = pl.BlockSpec((tm, D_MODEL), lambda i, e: (i, 0))
    return pl.pallas_call(
        _moe_kernel,
        grid=(n // tm, N_EXPERTS // es),
        in_specs=[row, _layer_spec(g, l), _layer_spec(wr, l), _layer_spec(br, l),
                  pl.BlockSpec((None, es, D_MODEL, 2 * EXPERT_FF), lambda i, e: (l, e, 0, 0)),
                  pl.BlockSpec((None, es * EXPERT_FF, D_MODEL), lambda i, e: (l, e, 0))],
        out_specs=row,
        out_shape=jax.ShapeDtypeStruct((n, D_MODEL), F32),
        scratch_shapes=[pltpu.VMEM((tm, D_MODEL), BF16), pltpu.VMEM((tm, ROUTER_LANES), F32)],
        compiler_params=_cparams("parallel", "arbitrary"),
        name="hier_moe",
    )(x, g, wr, br, wgu, wd)


def _final_norm_kernel(x_ref, g_ref, o_ref):
    o_ref[...] = _rms(x_ref[...], g_ref[...])


def _final_norm(x, g, row_base, n_rows, tm):
    base = row_base // tm
    return pl.pallas_call(
        _final_norm_kernel,
        grid=(n_rows // tm,),
        in_specs=[pl.BlockSpec((tm, D_MODEL), lambda i: (base + i, 0)),
                  pl.BlockSpec((1, D_MODEL), lambda i: (0, 0))],
        out_specs=pl.BlockSpec((tm, D_MODEL), lambda i: (i, 0)),
        out_shape=jax.ShapeDtypeStruct((n_rows, D_MODEL), F32),
        compiler_params=_cparams("parallel"),
        name="final_norm",
    )(x, g)


def _rope_tables(seq, t_len, tm):
    half = HEAD_DIM // 2
    inv = ROPE_THETA ** (-jnp.arange(half, dtype=F32) / half)
    pos_s = PAST_LEN + jnp.arange(t_len)
    pos = jnp.concatenate([jnp.arange(seq), jnp.tile(pos_s, tm // t_len)]).astype(F32)
    ang = pos[:, None] * inv[None, :]
    cos = jnp.tile(jnp.cos(ang), (1, LANES // half))
    sign = jnp.where((jnp.arange(LANES) % HEAD_DIM) < half, -1.0, 1.0).astype(F32)
    sin = jnp.tile(jnp.sin(ang), (1, LANES // half)) * sign[None, :]
    return cos, sin


def _lanes_to_state(f):
    b = f.shape[0]
    return jnp.stack([f[:, 0, :N_STATE].reshape(b, N_SSM_GROUPS, SSM_STATE),
                      f[:, 0, N_STATE:].reshape(b, N_SSM_GROUPS, SSM_STATE)], axis=-1)


def kernel(x_prompt, x_sample, cache_win_k, cache_win_v, state_ssm, cache_mem_k, cache_mem_v, mem_prompt, w_in, attn_sink, lam_re, lam_im, log_dt, ssm_b_re, ssm_b_im, ssm_c_re, ssm_c_im, ssm_d, w_glu, b_glu, g_attn_out, g_ssm_out, w_out, g_mix, g_xattn, g_mem, wq_x, wk_x, wv_x, wo_x, g_ffn, w_group, b_group, w_router, b_router, w_gate, w_up, w_down, g_final):
    batch, seq, _ = x_prompt.shape
    dec_batch, t_len, _ = x_sample.shape
    depth = w_in.shape[0]
    win_rows = cache_win_k.shape[2]
    n_p = batch * seq
    n_s = dec_batch * t_len
    tm = 512
    tm_wide = 1024 if (n_p + n_s) % 1024 == 0 else 512
    sample_nb = tm // t_len

    x = jnp.concatenate([x_prompt.reshape(n_p, D_MODEL), x_sample.reshape(n_s, D_MODEL)], axis=0)
    cos_tab, sin_tab = _rope_tables(seq, t_len, tm)
    mem_flat = mem_prompt.reshape(batch * N_MEM, D_MODEL)
    zero_state = jnp.zeros((batch, 1, 2 * N_STATE), F32)
    vec = lambda a: a.reshape(depth, 1, a.shape[-1])

    w_in_b, w_glu_b, w_out_b = w_in.astype(BF16), w_glu.astype(BF16), w_out.astype(BF16)
    wq_b, wk_b, wv_b, wo_b = (w.astype(BF16) for w in (wq_x, wk_x, wv_x, wo_x))
    wgu_b = jnp.concatenate([w_gate, w_up], axis=-1).astype(BF16)
    wd_b = w_down.reshape(depth, N_EXPERTS * EXPERT_FF, D_MODEL).astype(BF16)
    wr = jnp.concatenate([w_group, w_router.transpose(0, 2, 1, 3).reshape(depth, D_MODEL, N_EXPERTS)], axis=-1)
    wr = jnp.pad(wr, ((0, 0), (0, 0), (0, ROUTER_LANES - wr.shape[-1])))
    br = jnp.concatenate([b_group, b_router.reshape(depth, N_EXPERTS)], axis=-1)
    br = jnp.pad(br, ((0, 0), (0, ROUTER_LANES - br.shape[-1]))).reshape(depth, 1, ROUTER_LANES)
    g_mix_r, g_xattn_r, g_mem_r, g_ffn_r = vec(g_mix), vec(g_xattn), vec(g_mem), vec(g_ffn)
    g_a_r, g_s_r, b_glu_r, ssm_d_r = vec(g_attn_out), vec(g_ssm_out), vec(b_glu), vec(ssm_d)
    cache_k = cache_win_k.reshape(depth, dec_batch, win_rows, KV_WIDTH)
    cache_v = cache_win_v.reshape(depth, dec_batch, win_rows, KV_WIDTH)
    cmem_k = cache_mem_k.reshape(depth, dec_batch * N_MEM, D_MODEL)
    cmem_v = cache_mem_v.reshape(depth, dec_batch * N_MEM, D_MODEL)
    state_in = jnp.concatenate([state_ssm[..., 0].reshape(depth, dec_batch, 1, N_STATE),
                                state_ssm[..., 1].reshape(depth, dec_batch, 1, N_STATE)], axis=-1)

    outs = {k: [] for k in ("wk_p", "wv_p", "ssm_p", "mk_p", "mv_p", "wk_s", "wv_s", "ssm_s")}
    for l in range(depth):
        q, k, v, u = _in_proj(x, g_mix_r, w_in_b, l, cos_tab, sin_tab, n_p, seq, tm)
        attn_p = _attn_prompt(q, k, v, attn_sink[l], batch, seq)
        attn_s = _attn_sample(q, k, v, cache_k, cache_v, l, attn_sink[l], n_p, dec_batch, t_len, 4)
        kp = k[:n_p].reshape(batch, seq, N_KV_HEADS, HEAD_DIM)
        vp = v[:n_p].reshape(batch, seq, N_KV_HEADS, HEAD_DIM)
        outs["wk_p"].append(kp[:, -WINDOW:])
        outs["wv_p"].append(vp[:, -WINDOW:])
        ks = k[n_p:].reshape(dec_batch, t_len, KV_WIDTH)
        vs = v[n_p:].reshape(dec_batch, t_len, KV_WIDTH)
        k_all = jnp.concatenate([cache_k[l], ks], axis=1)[:, -win_rows:]
        v_all = jnp.concatenate([cache_v[l], vs], axis=1)[:, -win_rows:]
        outs["wk_s"].append(k_all.reshape(dec_batch, win_rows, N_KV_HEADS, HEAD_DIM))
        outs["wv_s"].append(v_all.reshape(dec_batch, win_rows, N_KV_HEADS, HEAD_DIM))

        pre, pim, bbre, bbim = _ssm_discretize(lam_re[l], lam_im[l], log_dt[l][:, None],
                                               ssm_b_re[l].transpose(2, 0, 1), ssm_b_im[l].transpose(2, 0, 1))
        bmat, cmat, coef = _ssm_operands(pre, pim, bbre, bbim, ssm_c_re[l], ssm_c_im[l])
        y_p, fin_p = _ssm(u, zero_state, bmat, cmat, ssm_d_r, l, coef, 0, batch, seq, 1, 256)
        y_s, fin_s = _ssm(u, state_in[l], bmat, cmat, ssm_d_r, l, coef, n_p, dec_batch, t_len, 8, t_len)
        outs["ssm_p"].append(_lanes_to_state(fin_p))
        outs["ssm_s"].append(_lanes_to_state(fin_s))
        x = _merge(x, attn_p, attn_s, y_p, y_s, w_glu_b, b_glu_r, g_a_r, g_s_r, w_out_b, l, tm)

        mk_p, mv_p = _mem_kv(mem_flat, g_mem_r, wk_b, wv_b, l, 512)
        outs["mk_p"].append(mk_p.reshape(batch, N_MEM, N_XHEADS, XHEAD_DIM))
        outs["mv_p"].append(mv_p.reshape(batch, N_MEM, N_XHEADS, XHEAD_DIM))
        tiles_per_seq = seq // tm
        x = _xattn(x, g_xattn_r, wq_b, wo_b, l, mk_p, mv_p,
                   pl.BlockSpec((N_MEM, D_MODEL), lambda i: (i // tiles_per_seq, 0)), 0, n_p, 1, tm)
        x = _xattn(x, g_xattn_r, wq_b, wo_b, l, cmem_k, cmem_v,
                   pl.BlockSpec((None, sample_nb * N_MEM, D_MODEL), lambda i: (l, i, 0)), n_p, n_s, sample_nb, tm)

        x = _moe(x, g_ffn_r, wr, br, wgu_b, wd_b, l, tm_wide)

    gf = g_final.reshape(1, D_MODEL)
    y_p = _final_norm(x, gf, 0, n_p, tm)
    y_s = _final_norm(x, gf, n_p, n_s, tm)
    st = lambda name: jnp.stack(outs[name], axis=0)
    return (y_p.reshape(batch, seq, D_MODEL), y_s.reshape(dec_batch, t_len, D_MODEL),
            st("wk_p"), st("wv_p"), st("ssm_p"), st("mk_p"), st("mv_p"), st("wk_s"), st("wv_s"), st("ssm_s"))
```

```python
import functools
import math

import jax
import jax.numpy as jnp
from jax import lax
from jax.experimental import pallas as pl
from jax.experimental.pallas import tpu as pltpu

F32 = jnp.float32
BF16 = jnp.bfloat16

D_MODEL = 1024
CHUNK = 64
EPS = 1e-6
NEG_INF = -1e30
N_HEADS = 8
N_KV_HEADS = 2
HEAD_DIM = 64
ATTN_WIDTH = N_HEADS * HEAD_DIM
KV_WIDTH = N_KV_HEADS * HEAD_DIM
WINDOW = 128
ROPE_THETA = 10000.0
SSM_WIDTH = D_MODEL - ATTN_WIDTH
SSM_GROUP = 16
N_SSM_GROUPS = SSM_WIDTH // SSM_GROUP
SSM_STATE = 64
N_STATE = N_SSM_GROUPS * SSM_STATE
IN_WIDTH = ATTN_WIDTH + 2 * KV_WIDTH + SSM_WIDTH
N_MEM = 256
N_XHEADS = 4
XHEAD_DIM = D_MODEL // N_XHEADS
N_EXPERT_GROUPS = 4
EXPERTS_PER_GROUP = 8
N_EXPERTS = N_EXPERT_GROUPS * EXPERTS_PER_GROUP
EXPERT_FF = 128
PAST_LEN = 4096

LANES = 128
SUBLANES = 8
VMEM_LIMIT = 56 * 1024 * 1024


def _cparams(*sem):
    return pltpu.CompilerParams(dimension_semantics=sem, vmem_limit_bytes=VMEM_LIMIT)


def _rms(x, g):
    return x * lax.rsqrt(jnp.mean(x * x, axis=-1, keepdims=True) + EPS) * g


def _layer_spec(arr, l):
    shape = arr.shape[1:]
    zeros = (0,) * len(shape)
    return pl.BlockSpec((None,) + shape, lambda *_: (l,) + zeros)


def _rope_pairs(t, cos, sin_signed, first_half):
    swapped = jnp.where(first_half, pltpu.roll(t, LANES - HEAD_DIM // 2, 1), pltpu.roll(t, HEAD_DIM // 2, 1))
    return t * cos + swapped * sin_signed


def _in_proj_kernel(x_ref, g_ref, w_ref, cos_ref, sin_ref, q_ref, k_ref, v_ref, u_ref):
    xn = _rms(x_ref[...], g_ref[...])
    z = jnp.dot(xn.astype(BF16), w_ref[...], preferred_element_type=F32)
    cos = cos_ref[...]
    sin = sin_ref[...]
    lane = lax.broadcasted_iota(jnp.int32, cos.shape, 1)
    first_half = (lane % HEAD_DIM) < (HEAD_DIM // 2)
    scale = 1.0 / math.sqrt(HEAD_DIM)
    for j in range(ATTN_WIDTH // LANES):
        t = z[:, j * LANES:(j + 1) * LANES]
        q_ref[:, j * LANES:(j + 1) * LANES] = (_rope_pairs(t, cos, sin, first_half) * scale).astype(BF16)
    k_ref[...] = _rope_pairs(z[:, ATTN_WIDTH:ATTN_WIDTH + KV_WIDTH], cos, sin, first_half)
    v_ref[...] = z[:, ATTN_WIDTH + KV_WIDTH:ATTN_WIDTH + 2 * KV_WIDTH]
    u_ref[...] = z[:, ATTN_WIDTH + 2 * KV_WIDTH:]


def _in_proj(x, g, w_bf16, l, cos_tab, sin_tab, n_prompt, seq, tm):
    n = x.shape[0]
    n_prompt_tiles = n_prompt // tm
    tiles_per_seq = seq // tm

    def tab_map(i):
        return (jnp.where(i < n_prompt_tiles, i % tiles_per_seq, tiles_per_seq), 0)

    row = lambda w: pl.BlockSpec((tm, w), lambda i: (i, 0))
    return pl.pallas_call(
        _in_proj_kernel,
        grid=(n // tm,),
        in_specs=[row(D_MODEL), _layer_spec(g, l), _layer_spec(w_bf16, l),
                  pl.BlockSpec((tm, LANES), tab_map),
                  pl.BlockSpec((tm, LANES), tab_map)],
        out_specs=[row(ATTN_WIDTH), row(KV_WIDTH), row(KV_WIDTH), row(SSM_WIDTH)],
        out_shape=[jax.ShapeDtypeStruct((n, ATTN_WIDTH), BF16),
                   jax.ShapeDtypeStruct((n, KV_WIDTH), F32),
                   jax.ShapeDtypeStruct((n, KV_WIDTH), F32),
                   jax.ShapeDtypeStruct((n, SSM_WIDTH), F32)],
        compiler_params=_cparams("parallel"),
        name="in_proj",
    )(x, g, w_bf16, cos_tab, sin_tab)


def _attend_pairs(q, keys, vals, mask, sink_ref, o_ref, row0):
    tq = q.shape[0]
    lane = lax.broadcasted_iota(jnp.int32, keys.shape, 1)
    low = lane < HEAD_DIM
    k_sw = pltpu.roll(keys, HEAD_DIM, 1)
    v_sw = pltpu.roll(vals, HEAD_DIM, 1)
    kk = [jnp.where(low, keys, k_sw).astype(BF16), jnp.where(low, k_sw, keys).astype(BF16)]
    vv = [jnp.where(low, vals, v_sw).astype(BF16), jnp.where(low, v_sw, vals).astype(BF16)]
    qlane = lax.broadcasted_iota(jnp.int32, (tq, LANES), 1)
    qlow = qlane < HEAD_DIM
    row_top = lax.broadcasted_iota(jnp.int32, (2 * tq, 1), 0) < tq
    zero = jnp.zeros((), BF16)
    for pair in range(N_HEADS // 2):
        kv = pair // (N_HEADS // N_KV_HEADS // 2)
        qp = q[:, pair * LANES:(pair + 1) * LANES]
        qs = jnp.concatenate([jnp.where(qlow, qp, zero), jnp.where(qlow, zero, qp)], axis=0)
        s = lax.dot_general(qs, kk[kv], (((1,), (1,)), ((), ())), preferred_element_type=F32)
        if mask is not None:
            s = jnp.where(mask, s, NEG_INF)
        sink = jnp.where(row_top, sink_ref[2 * pair], sink_ref[2 * pair + 1])
        m = jnp.maximum(jnp.max(s, axis=-1, keepdims=True), sink)
        e = jnp.exp(s - m)
        p = e / (jnp.sum(e, axis=-1, keepdims=True) + jnp.exp(sink - m))
        o = jnp.dot(p.astype(BF16), vv[kv], preferred_element_type=F32)
        o_ref[row0:row0 + tq, pair * LANES:(pair + 1) * LANES] = jnp.where(qlow, o[:tq], o[tq:])


def _attn_prompt_kernel(sink_ref, q_ref, kp_ref, kc_ref, vp_ref, vc_ref, o_ref):
    i = pl.program_id(1)
    tq = q_ref.shape[0]
    keys = jnp.concatenate([kp_ref[...], kc_ref[...]], axis=0)
    vals = jnp.concatenate([vp_ref[...], vc_ref[...]], axis=0)
    nk = keys.shape[0]
    r = lax.broadcasted_iota(jnp.int32, (2 * tq, nk), 0) % tq
    c = lax.broadcasted_iota(jnp.int32, (2 * tq, nk), 1)
    qc = r // CHUNK
    kc = c // CHUNK
    mask = (kc >= qc) & (kc <= qc + WINDOW // CHUNK) & ((c >= WINDOW) | (i > 0))
    _attend_pairs(q_ref[...], keys, vals, mask, sink_ref, o_ref, 0)


def _attn_prompt(q, k, v, sink, batch, seq):
    tq = WINDOW
    nt = seq // tq
    cur = lambda b, i: (b * nt + i, 0)
    prev = lambda b, i: (b * nt + jnp.maximum(i - 1, 0), 0)
    return pl.pallas_call(
        _attn_prompt_kernel,
        grid=(batch, nt),
        in_specs=[pl.BlockSpec(memory_space=pltpu.SMEM),
                  pl.BlockSpec((tq, ATTN_WIDTH), cur),
                  pl.BlockSpec((tq, KV_WIDTH), prev),
                  pl.BlockSpec((tq, KV_WIDTH), cur),
                  pl.BlockSpec((tq, KV_WIDTH), prev),
                  pl.BlockSpec((tq, KV_WIDTH), cur)],
        out_specs=pl.BlockSpec((tq, ATTN_WIDTH), cur),
        out_shape=jax.ShapeDtypeStruct((batch * seq, ATTN_WIDTH), F32),
        compiler_params=_cparams("parallel", "parallel"),
        name="attn_prompt",
    )(sink, q, k, k, v, v)


def _attn_sample_kernel(sink_ref, q_ref, ck_ref, cv_ref, k_ref, v_ref, o_ref):
    nb = ck_ref.shape[0]
    t = q_ref.shape[0] // nb
    for b in range(nb):
        rows = slice(b * t, (b + 1) * t)
        keys = jnp.concatenate([ck_ref[b], k_ref[rows, :]], axis=0)
        vals = jnp.concatenate([cv_ref[b], v_ref[rows, :]], axis=0)
        _attend_pairs(q_ref[rows, :], keys, vals, None, sink_ref, o_ref, b * t)


def _attn_sample(q, k, v, cache_k, cache_v, l, sink, n_prompt, dec_batch, t, nb):
    w = cache_k.shape[2]
    rows = nb * t
    base = n_prompt // rows
    tok = lambda width: pl.BlockSpec((rows, width), lambda i: (base + i, 0))
    cache = pl.BlockSpec((None, nb, w, KV_WIDTH), lambda i: (l, i, 0, 0))
    return pl.pallas_call(
        _attn_sample_kernel,
        grid=(dec_batch // nb,),
        in_specs=[pl.BlockSpec(memory_space=pltpu.SMEM),
                  tok(ATTN_WIDTH), cache, cache, tok(KV_WIDTH), tok(KV_WIDTH)],
        out_specs=pl.BlockSpec((rows, ATTN_WIDTH), lambda i: (i, 0)),
        out_shape=jax.ShapeDtypeStruct((dec_batch * t, ATTN_WIDTH), F32),
        compiler_params=_cparams("parallel"),
        name="attn_sample",
    )(sink, q, cache_k, cache_v, k, v)


SSM_BLOCK = SUBLANES


def _cmul(ar, ai, br, bi):
    return ar * br - ai * bi, ar * bi + ai * br


def _ssm_disc_kernel(lre_ref, lim_ref, ldt_ref, bre_ref, bim_ref, cre_ref, cim_ref,
                     l8re_ref, l8im_ref, were_ref, weim_ref, cvre_ref, cvim_ref, kt_ref):
    lre = lre_ref[...]
    lim = lim_ref[...]
    dt = jnp.exp(ldt_ref[...])
    mag = jnp.exp(lre * dt)
    ang = lim * dt
    lbr = mag * jnp.cos(ang)
    lbi = mag * jnp.sin(ang)
    nr, ni = lbr - 1.0, lbi
    den = lre * lre + lim * lim
    fr = (nr * lre + ni * lim) / den
    fi = (ni * lre - nr * lim) / den
    bbar = [_cmul(fr, fi, bre_ref[c], bim_ref[c]) for c in range(SSM_GROUP)]
    pw = [(jnp.ones_like(lbr), jnp.zeros_like(lbr))]
    for _ in range(SSM_BLOCK):
        pw.append(_cmul(pw[-1][0], pw[-1][1], lbr, lbi))
    qr, qi = pw[SSM_BLOCK]
    for k in range(SUBLANES):
        l8re_ref[k] = qr
        l8im_ref[k] = qi
        qr, qi = _cmul(qr, qi, pw[SSM_BLOCK][0], pw[SSM_BLOCK][1])
    lane = lax.broadcasted_iota(jnp.int32, (lre.shape[0], LANES), 1)
    for j in range(SSM_BLOCK):
        for c in range(SSM_GROUP):
            were_ref[j, c], weim_ref[j, c] = _cmul(pw[SSM_BLOCK - 1 - j][0], pw[SSM_BLOCK - 1 - j][1], bbar[c][0], bbar[c][1])
    for t in range(SSM_BLOCK):
        for c in range(SSM_GROUP):
            clr, cli = _cmul(cre_ref[c], cim_ref[c], pw[t + 1][0], pw[t + 1][1])
            cvre_ref[t, c] = clr
            cvim_ref[t, c] = -cli
    for d in range(SSM_BLOCK):
        for c in range(SSM_GROUP):
            clr, cli = _cmul(cre_ref[c], cim_ref[c], pw[d][0], pw[d][1])
            row = jnp.zeros((lre.shape[0], LANES), F32)
            for c2 in range(SSM_GROUP):
                val = jnp.sum(clr * bbar[c2][0] - cli * bbar[c2][1], axis=-1, keepdims=True)
                row = jnp.where(lane == c2, val, row)
            kt_ref[d, c] = row


def _ssm_discretize(lam_re, lam_im, log_dt, b_re, b_im, c_re, c_im):
    g, p = lam_re.shape
    blk = jax.ShapeDtypeStruct((SSM_BLOCK, SSM_GROUP, g, p), F32)
    return pl.pallas_call(
        _ssm_disc_kernel,
        out_shape=[jax.ShapeDtypeStruct((SUBLANES, g, p), F32)] * 2 + [blk] * 4
                  + [jax.ShapeDtypeStruct((SSM_BLOCK, SSM_GROUP, g, LANES), F32)],
        compiler_params=pltpu.CompilerParams(vmem_limit_bytes=VMEM_LIMIT),
        name="ssm_discretize",
    )(lam_re, lam_im, log_dt, b_re, b_im, c_re, c_im)


GROUPS_PER_CH_BLOCK = LANES // SSM_GROUP
N_CH_BLOCKS = SSM_WIDTH // LANES
STATE_LANES = 2 * GROUPS_PER_CH_BLOCK * SSM_STATE
STATE_TILES = STATE_LANES // (2 * LANES)
BLOCK_LANES = SSM_BLOCK * LANES


def _ssm_kernel(nb, u_ref, s0_ref, we_ref, tv_ref, d_ref, coef_ref, y_ref, fin_ref, st_ref, sprev_ref, ucat_ref):
    t_idx = pl.program_id(2)
    r = st_ref.shape[1] - SUBLANES
    rows = nb * r

    @pl.when(t_idx == 0)
    def _():
        for b in range(nb):
            st_ref[b, 0:SUBLANES, :] = jnp.broadcast_to(s0_ref[b], (SUBLANES, STATE_LANES))

    for j in range(SSM_BLOCK):
        ucat_ref[:, j * LANES:(j + 1) * LANES] = u_ref[pl.ds(j, rows, stride=SSM_BLOCK), :].astype(BF16)
    e = jnp.dot(ucat_ref[...], we_ref[...], preferred_element_type=F32)
    st_ref[:, SUBLANES:, :] = e.reshape(nb, r, STATE_LANES)

    first_row = lax.broadcasted_iota(jnp.int32, (SUBLANES, LANES), 0) == 0

    def group(rg, carry):
        r0 = pl.multiple_of(SUBLANES + rg * SUBLANES, SUBLANES)
        rp = pl.multiple_of(rg * SUBLANES, SUBLANES)
        for b in range(nb):
            for q in range(STATE_TILES):
                re_sl = pl.ds(q * 2 * LANES, LANES)
                im_sl = pl.ds(q * 2 * LANES + LANES, LANES)
                co = pl.ds(q * LANES, LANES)
                xr = st_ref[b, pl.ds(r0, SUBLANES), re_sl]
                xi = st_ref[b, pl.ds(r0, SUBLANES), im_sl]
                for step, sh in enumerate((1, 2, 4)):
                    ar, ai = _cmul(coef_ref[step, 0, :, co], coef_ref[step, 1, :, co],
                                   pltpu.roll(xr, sh, 0), pltpu.roll(xi, sh, 0))
                    xr = xr + ar
                    xi = xi + ai
                pr = jnp.broadcast_to(st_ref[b, pl.ds(rp, SUBLANES), re_sl][SUBLANES - 1:, :], (SUBLANES, LANES))
                pi = jnp.broadcast_to(st_ref[b, pl.ds(rp, SUBLANES), im_sl][SUBLANES - 1:, :], (SUBLANES, LANES))
                ar, ai = _cmul(coef_ref[3, 0, :, co], coef_ref[3, 1, :, co], pr, pi)
                xr = xr + ar
                xi = xi + ai
                st_ref[b, pl.ds(r0, SUBLANES), re_sl] = xr
                st_ref[b, pl.ds(r0, SUBLANES), im_sl] = xi
                out_rows = pl.ds(pl.multiple_of(b * r + rg * SUBLANES, SUBLANES), SUBLANES)
                sprev_ref[out_rows, re_sl] = jnp.where(first_row, pr, pltpu.roll(xr, 1, 0))
                sprev_ref[out_rows, im_sl] = jnp.where(first_row, pi, pltpu.roll(xi, 1, 0))
        return carry

    lax.fori_loop(0, r // SUBLANES, group, 0)

    lhs = jnp.concatenate([ucat_ref[...], sprev_ref[...].astype(BF16)], axis=1)
    ycat = jnp.dot(lhs, tv_ref[...], preferred_element_type=F32)
    d = d_ref[...]
    for t in range(SSM_BLOCK):
        tok = pl.ds(t, rows, stride=SSM_BLOCK)
        y_ref[tok, :] = ycat[:, t * LANES:(t + 1) * LANES] + d * u_ref[tok, :]

    for b in range(nb):
        tail = st_ref[b, r:r + SUBLANES, :]
        st_ref[b, 0:SUBLANES, :] = tail
        fin_ref[b] = tail[SUBLANES - 1:SUBLANES, :]


def _ssm(u, s0, we, tv, d, l, coef, row_base, n_seq, seq_len, nb, tt):
    rows = nb * tt
    r = tt // SSM_BLOCK
    nt = seq_len // tt
    base = row_base // rows
    return pl.pallas_call(
        functools.partial(_ssm_kernel, nb),
        grid=(N_CH_BLOCKS, n_seq // nb, nt),
        in_specs=[pl.BlockSpec((rows, LANES), lambda m, i, j: (base + i * nt + j, m)),
                  pl.BlockSpec((nb, 1, STATE_LANES), lambda m, i, j: (i, 0, m)),
                  pl.BlockSpec((None, BLOCK_LANES, STATE_LANES), lambda m, i, j: (m, 0, 0)),
                  pl.BlockSpec((None, BLOCK_LANES + STATE_LANES, BLOCK_LANES), lambda m, i, j: (m, 0, 0)),
                  pl.BlockSpec((None, 1, LANES), lambda m, i, j: (l, 0, m)),
                  pl.BlockSpec((None, 4, 2, SUBLANES, STATE_LANES // 2), lambda m, i, j: (m, 0, 0, 0, 0))],
        out_specs=[pl.BlockSpec((rows, LANES), lambda m, i, j: (i * nt + j, m)),
                   pl.BlockSpec((nb, 1, STATE_LANES), lambda m, i, j: (i, 0, m))],
        out_shape=[jax.ShapeDtypeStruct((n_seq * seq_len, SSM_WIDTH), F32),
                   jax.ShapeDtypeStruct((n_seq, 1, N_CH_BLOCKS * STATE_LANES), F32)],
        scratch_shapes=[pltpu.VMEM((nb, SUBLANES + r, STATE_LANES), F32),
                        pltpu.VMEM((nb * r, STATE_LANES), F32),
                        pltpu.VMEM((nb * r, BLOCK_LANES), BF16)],
        compiler_params=_cparams("parallel", "parallel", "arbitrary"),
        name="ssm_scan",
    )(u, s0, we, tv, d, coef)


def _ssm_operands(l8re, l8im, were, weim, cvre, cvim, kt):
    m, qn, hn, p, c, nblk = N_CH_BLOCKS, STATE_TILES, 2, SSM_STATE, SSM_GROUP, SSM_BLOCK
    eq = jnp.eye(qn, dtype=F32)
    eh = jnp.eye(hn, dtype=F32)
    split = lambda a: a.reshape(a.shape[:-2] + (m, qn, hn, a.shape[-1]))
    w1 = jnp.stack([split(were), split(weim)])
    we = jnp.einsum('ajcmqhp,qQ,hH->mjqhcQaHp', w1, eq, eh).reshape(m, BLOCK_LANES, STATE_LANES)
    v1 = jnp.stack([split(cvre), split(cvim)])
    v = jnp.einsum('atcmqhp,qQ,hH->mQaHptqhc', v1, eq, eh).reshape(m, STATE_LANES, BLOCK_LANES)
    jj = jnp.arange(nblk)[:, None]
    tt = jnp.arange(nblk)[None, :]
    lag = jnp.clip(tt - jj, 0, nblk - 1)
    k1 = jnp.where((tt >= jj)[:, :, None, None, None], kt[..., :c][lag], 0.0)
    k1 = k1.reshape(nblk, nblk, c, m, GROUPS_PER_CH_BLOCK, c)
    tmat = jnp.einsum('jtcmgd,gG->mjgdtGc', k1, jnp.eye(GROUPS_PER_CH_BLOCK, dtype=F32))
    tmat = tmat.reshape(m, BLOCK_LANES, BLOCK_LANES)
    tv = jnp.concatenate([tmat, v], axis=1).astype(BF16)
    lay = lambda a: a.reshape(SUBLANES, m, STATE_LANES // 2).transpose(1, 0, 2)
    pr, pi = lay(l8re), lay(l8im)
    rows = jnp.arange(SUBLANES)[None, :, None]
    tabs = []
    for sh in (1, 2, 4):
        keep = rows >= sh
        tabs.append(jnp.stack([jnp.where(keep, pr[:, sh - 1:sh], 0.0), jnp.where(keep, pi[:, sh - 1:sh], 0.0)], axis=1))
    tabs.append(jnp.stack([pr, pi], axis=1))
    coef = jnp.stack(tabs, axis=1)
    return we.astype(BF16), tv, coef


def _state_to_tiles(s):
    lead = s.shape[:-3]
    t = s.reshape(lead + (N_CH_BLOCKS, STATE_TILES, 2, SSM_STATE, 2))
    t = jnp.moveaxis(t, -1, -3)
    return t.reshape(lead + (1, N_CH_BLOCKS * STATE_LANES))


def _tiles_to_state(f):
    b = f.shape[0]
    t = f.reshape(b, N_CH_BLOCKS, STATE_TILES, 2, 2, SSM_STATE)
    t = jnp.moveaxis(t, 3, -1)
    return t.reshape(b, N_SSM_GROUPS, SSM_STATE, 2)


def _merge_kernel(n_prompt_tiles, x_ref, ap_ref, as_ref, yp_ref, ys_ref, wglu_ref, bglu_ref, ga_ref, gs_ref,
                  wout_ref, o_ref):
    is_prompt = pl.program_id(0) < n_prompt_tiles
    attn = jnp.where(is_prompt, ap_ref[...], as_ref[...])
    g = jax.nn.gelu(jnp.where(is_prompt, yp_ref[...], ys_ref[...]))
    glu = g * jax.nn.sigmoid(jnp.dot(g.astype(BF16), wglu_ref[...], preferred_element_type=F32) + bglu_ref[...])
    na = _rms(attn, ga_ref[...]).astype(BF16)
    ns = _rms(glu, gs_ref[...]).astype(BF16)
    o = jnp.dot(na, wout_ref[0:ATTN_WIDTH, :], preferred_element_type=F32)
    o = o + jnp.dot(ns, wout_ref[ATTN_WIDTH:, :], preferred_element_type=F32)
    o_ref[...] = x_ref[...] + o


def _merge(x, attn_p, attn_s, y_p, y_s, wglu, bglu, ga, gs, wout, l, tm):
    n = x.shape[0]
    npt = attn_p.shape[0] // tm
    nst = attn_s.shape[0] // tm
    row = lambda w: pl.BlockSpec((tm, w), lambda i: (i, 0))
    prow = lambda w: pl.BlockSpec((tm, w), lambda i: (jnp.minimum(i, npt - 1), 0))
    srow = lambda w: pl.BlockSpec((tm, w), lambda i: (jnp.clip(i - npt, 0, nst - 1), 0))
    return pl.pallas_call(
        functools.partial(_merge_kernel, npt),
        grid=(n // tm,),
        in_specs=[row(D_MODEL), prow(ATTN_WIDTH), srow(ATTN_WIDTH), prow(SSM_WIDTH), srow(SSM_WIDTH),
                  _layer_spec(wglu, l), _layer_spec(bglu, l), _layer_spec(ga, l), _layer_spec(gs, l),
                  _layer_spec(wout, l)],
        out_specs=row(D_MODEL),
        out_shape=jax.ShapeDtypeStruct((n, D_MODEL), F32),
        compiler_params=_cparams("parallel"),
        name="merge_heads",
    )(x, attn_p, attn_s, y_p, y_s, wglu, bglu, ga, gs, wout)


def _mem_kv_kernel(m_ref, g_ref, wk_ref, wv_ref, k_ref, v_ref):
    mn = _rms(m_ref[...], g_ref[...]).astype(BF16)
    k_ref[...] = jnp.dot(mn, wk_ref[...], preferred_element_type=F32)
    v_ref[...] = jnp.dot(mn, wv_ref[...], preferred_element_type=F32)


def _mem_kv(mem, g, wk, wv, l, tm):
    n = mem.shape[0]
    row = pl.BlockSpec((tm, D_MODEL), lambda i: (i, 0))
    return pl.pallas_call(
        _mem_kv_kernel,
        grid=(n // tm,),
        in_specs=[row, _layer_spec(g, l), _layer_spec(wk, l), _layer_spec(wv, l)],
        out_specs=[row, row],
        out_shape=[jax.ShapeDtypeStruct((n, D_MODEL), F32)] * 2,
        compiler_params=_cparams("parallel"),
        name="mem_kv",
    )(mem, g, wk, wv)


def _xattn_kernel(nb, x_ref, g_ref, wq_ref, wo_ref, mk_ref, mv_ref, o_ref, att_ref):
    t = x_ref.shape[0] // nb
    x = x_ref[...]
    hn = _rms(x, g_ref[...]).astype(BF16)
    q = jnp.dot(hn, wq_ref[...], preferred_element_type=F32) * (1.0 / math.sqrt(XHEAD_DIM))
    q = q.astype(BF16)
    for b in range(nb):
        mk = mk_ref[b * N_MEM:(b + 1) * N_MEM, :].astype(BF16)
        mv = mv_ref[b * N_MEM:(b + 1) * N_MEM, :].astype(BF16)
        for h in range(N_XHEADS):
            sl = slice(h * XHEAD_DIM, (h + 1) * XHEAD_DIM)
            s = lax.dot_general(q[b * t:(b + 1) * t, sl], mk[:, sl], (((1,), (1,)), ((), ())),
                                preferred_element_type=F32)
            m = jnp.max(s, axis=-1, keepdims=True)
            e = jnp.exp(s - m)
            p = e / jnp.sum(e, axis=-1, keepdims=True)
            att_ref[b * t:(b + 1) * t, sl] = jnp.dot(p.astype(BF16), mv[:, sl], preferred_element_type=F32)
    o = jnp.dot(att_ref[...].astype(BF16), wo_ref[...], preferred_element_type=F32)
    o_ref[...] = x + o


def _xattn(x, g, wq, wo, l, mk, mv, mem_spec, row_base, n_rows, nb, tm):
    base = row_base // tm
    xspec = pl.BlockSpec((tm, D_MODEL), lambda i: (base + i, 0))
    return pl.pallas_call(
        functools.partial(_xattn_kernel, nb),
        grid=(n_rows // tm,),
        in_specs=[xspec, _layer_spec(g, l), _layer_spec(wq, l), _layer_spec(wo, l), mem_spec, mem_spec],
        out_specs=xspec,
        out_shape=jax.ShapeDtypeStruct(x.shape, F32),
        scratch_shapes=[pltpu.VMEM((tm, D_MODEL), F32)],
        input_output_aliases={0: 0},
        compiler_params=_cparams("parallel"),
        name="cross_attn",
    )(x, g, wq, wo, mk, mv)


ROUTER_LANES = LANES
EXPERT_LANE0 = N_EXPERT_GROUPS
EXPERTS_PER_STEP = 4


def _route(logits):
    lane_i = lax.broadcasted_iota(jnp.int32, logits.shape, 1)
    lane = lane_i.astype(F32)
    neg = jnp.float32(-jnp.inf)
    is_g = lane_i < N_EXPERT_GROUPS
    gl = jnp.where(is_g, logits, neg)
    gmax = jnp.max(gl, axis=-1, keepdims=True)
    gidx = jnp.min(jnp.where(gl == gmax, lane, float(ROUTER_LANES)), axis=-1, keepdims=True)
    g_w = 1.0 / jnp.sum(jnp.where(is_g, jnp.exp(gl - gmax), 0.0), axis=-1, keepdims=True)
    first = EXPERT_LANE0 + gidx * EXPERTS_PER_GROUP
    sel = (lane >= first) & (lane < first + EXPERTS_PER_GROUP)
    el = jnp.where(sel, logits, neg)
    m1 = jnp.max(el, axis=-1, keepdims=True)
    i1 = jnp.min(jnp.where(el == m1, lane, float(ROUTER_LANES)), axis=-1, keepdims=True)
    el2 = jnp.where(lane == i1, neg, el)
    m2 = jnp.max(el2, axis=-1, keepdims=True)
    i2 = jnp.min(jnp.where(el2 == m2, lane, float(ROUTER_LANES)), axis=-1, keepdims=True)
    r = jnp.exp(m2 - m1)
    w1 = g_w / (1.0 + r)
    w2 = w1 * r
    return jnp.where(lane == i1, w1, jnp.where(lane == i2, w2, 0.0))


def _moe_kernel(x_ref, g_ref, wr_ref, br_ref, wgu_ref, wd_ref, o_ref, xn_ref, gate_ref):
    e_step = pl.program_id(1)

    @pl.when(e_step == 0)
    def _():
        x = x_ref[...]
        xn = _rms(x, g_ref[...])
        xn_ref[...] = xn.astype(BF16)
        logits = jnp.dot(xn, wr_ref[...], preferred_element_type=F32,
                         precision=lax.Precision.HIGHEST) + br_ref[...]
        gate_ref[...] = _route(logits)
        o_ref[...] = x

    xn = xn_ref[...]
    gates = gate_ref[...]
    lane = lax.broadcasted_iota(jnp.int32, gates.shape, 1)
    hids = []
    for j in range(EXPERTS_PER_STEP):
        h = jnp.dot(xn, wgu_ref[j], preferred_element_type=F32)
        ge = jnp.sum(jnp.where(lane == EXPERT_LANE0 + e_step * EXPERTS_PER_STEP + j, gates, 0.0),
                     axis=-1, keepdims=True)
        hid = jax.nn.silu(h[:, :EXPERT_FF]) * h[:, EXPERT_FF:] * ge
        hids.append(hid.astype(BF16))
    hid = jnp.concatenate(hids, axis=1)
    o_ref[...] += jnp.dot(hid, wd_ref[...], preferred_element_type=F32)


def _moe(x, g, wr, br, wgu, wd, l, tm):
    n = x.shape[0]
    es = EXPERTS_PER_STEP
    row = pl.BlockSpec((tm, D_MODEL), lambda i, e: (i, 0))
    return pl.pallas_call(
        _moe_kernel,
        grid=(n // tm, N_EXPERTS // es),
        in_specs=[row, _layer_spec(g, l), _layer_spec(wr, l), _layer_spec(br, l),
                  pl.BlockSpec((None, es, D_MODEL, 2 * EXPERT_FF), lambda i, e: (l, e, 0, 0)),
                  pl.BlockSpec((None, es * EXPERT_FF, D_MODEL), lambda i, e: (l, e, 0))],
        out_specs=row,
        out_shape=jax.ShapeDtypeStruct((n, D_MODEL), F32),
        scratch_shapes=[pltpu.VMEM((tm, D_MODEL), BF16), pltpu.VMEM((tm, ROUTER_LANES), F32)],
        compiler_params=_cparams("parallel", "arbitrary"),
        name="hier_moe",
    )(x, g, wr, br, wgu, wd)


def _final_norm_kernel(x_ref, g_ref, o_ref):
    o_ref[...] = _rms(x_ref[...], g_ref[...])


def _final_norm(x, g, row_base, n_rows, tm):
    base = row_base // tm
    return pl.pallas_call(
        _final_norm_kernel,
        grid=(n_rows // tm,),
        in_specs=[pl.BlockSpec((tm, D_MODEL), lambda i: (base + i, 0)),
                  pl.BlockSpec((1, D_MODEL), lambda i: (0, 0))],
        out_specs=pl.BlockSpec((tm, D_MODEL), lambda i: (i, 0)),
        out_shape=jax.ShapeDtypeStruct((n_rows, D_MODEL), F32),
        compiler_params=_cparams("parallel"),
        name="final_norm",
    )(x, g)


def _rope_tables(seq, t_len, tm):
    half = HEAD_DIM // 2
    inv = ROPE_THETA ** (-jnp.arange(half, dtype=F32) / half)
    pos_s = PAST_LEN + jnp.arange(t_len)
    pos = jnp.concatenate([jnp.arange(seq), jnp.tile(pos_s, tm // t_len)]).astype(F32)
    ang = pos[:, None] * inv[None, :]
    cos = jnp.tile(jnp.cos(ang), (1, LANES // half))
    sign = jnp.where((jnp.arange(LANES) % HEAD_DIM) < half, -1.0, 1.0).astype(F32)
    sin = jnp.tile(jnp.sin(ang), (1, LANES // half)) * sign[None, :]
    return cos, sin


def kernel(x_prompt, x_sample, cache_win_k, cache_win_v, state_ssm, cache_mem_k, cache_mem_v, mem_prompt, w_in, attn_sink, lam_re, lam_im, log_dt, ssm_b_re, ssm_b_im, ssm_c_re, ssm_c_im, ssm_d, w_glu, b_glu, g_attn_out, g_ssm_out, w_out, g_mix, g_xattn, g_mem, wq_x, wk_x, wv_x, wo_x, g_ffn, w_group, b_group, w_router, b_router, w_gate, w_up, w_down, g_final):
    batch, seq, _ = x_prompt.shape
    dec_batch, t_len, _ = x_sample.shape
    depth = w_in.shape[0]
    win_rows = cache_win_k.shape[2]
    n_p = batch * seq
    n_s = dec_batch * t_len
    tm = 512
    tm_wide = 1024 if (n_p + n_s) % 1024 == 0 else 512
    sample_nb = tm // t_len

    x = jnp.concatenate([x_prompt.reshape(n_p, D_MODEL), x_sample.reshape(n_s, D_MODEL)], axis=0)
    cos_tab, sin_tab = _rope_tables(seq, t_len, tm)
    mem_flat = mem_prompt.reshape(batch * N_MEM, D_MODEL)
    zero_state = jnp.zeros((batch, 1, 2 * N_STATE), F32)
    vec = lambda a: a.reshape(depth, 1, a.shape[-1])

    w_in_b, w_glu_b, w_out_b = w_in.astype(BF16), w_glu.astype(BF16), w_out.astype(BF16)
    wq_b, wk_b, wv_b, wo_b = (w.astype(BF16) for w in (wq_x, wk_x, wv_x, wo_x))
    wgu_b = jnp.concatenate([w_gate, w_up], axis=-1).astype(BF16)
    wd_b = w_down.reshape(depth, N_EXPERTS * EXPERT_FF, D_MODEL).astype(BF16)
    wr = jnp.concatenate([w_group, w_router.transpose(0, 2, 1, 3).reshape(depth, D_MODEL, N_EXPERTS)], axis=-1)
    wr = jnp.pad(wr, ((0, 0), (0, 0), (0, ROUTER_LANES - wr.shape[-1])))
    br = jnp.concatenate([b_group, b_router.reshape(depth, N_EXPERTS)], axis=-1)
    br = jnp.pad(br, ((0, 0), (0, ROUTER_LANES - br.shape[-1]))).reshape(depth, 1, ROUTER_LANES)
    g_mix_r, g_xattn_r, g_mem_r, g_ffn_r = vec(g_mix), vec(g_xattn), vec(g_mem), vec(g_ffn)
    g_a_r, g_s_r, b_glu_r, ssm_d_r = vec(g_attn_out), vec(g_ssm_out), vec(b_glu), vec(ssm_d)
    cache_k = cache_win_k.reshape(depth, dec_batch, win_rows, KV_WIDTH)
    cache_v = cache_win_v.reshape(depth, dec_batch, win_rows, KV_WIDTH)
    cmem_k = cache_mem_k.reshape(depth, dec_batch * N_MEM, D_MODEL)
    cmem_v = cache_mem_v.reshape(depth, dec_batch * N_MEM, D_MODEL)
    state_in = _state_to_tiles(state_ssm)

    outs = {k: [] for k in ("wk_p", "wv_p", "ssm_p", "mk_p", "mv_p", "wk_s", "wv_s", "ssm_s")}
    for l in range(depth):
        q, k, v, u = _in_proj(x, g_mix_r, w_in_b, l, cos_tab, sin_tab, n_p, seq, tm)
        attn_p = _attn_prompt(q, k, v, attn_sink[l], batch, seq)
        attn_s = _attn_sample(q, k, v, cache_k, cache_v, l, attn_sink[l], n_p, dec_batch, t_len, 4)
        kp = k[:n_p].reshape(batch, seq, N_KV_HEADS, HEAD_DIM)
        vp = v[:n_p].reshape(batch, seq, N_KV_HEADS, HEAD_DIM)
        outs["wk_p"].append(kp[:, -WINDOW:])
        outs["wv_p"].append(vp[:, -WINDOW:])
        ks = k[n_p:].reshape(dec_batch, t_len, KV_WIDTH)
        vs = v[n_p:].reshape(dec_batch, t_len, KV_WIDTH)
        k_all = jnp.concatenate([cache_k[l], ks], axis=1)[:, -win_rows:]
        v_all = jnp.concatenate([cache_v[l], vs], axis=1)[:, -win_rows:]
        outs["wk_s"].append(k_all.reshape(dec_batch, win_rows, N_KV_HEADS, HEAD_DIM))
        outs["wv_s"].append(v_all.reshape(dec_batch, win_rows, N_KV_HEADS, HEAD_DIM))

        tabs = _ssm_discretize(lam_re[l], lam_im[l], log_dt[l][:, None],
                               ssm_b_re[l].transpose(2, 0, 1), ssm_b_im[l].transpose(2, 0, 1),
                               ssm_c_re[l].transpose(1, 0, 2), ssm_c_im[l].transpose(1, 0, 2))
        we, tv, coef = _ssm_operands(*tabs)
        y_p, fin_p = _ssm(u, zero_state, we, tv, ssm_d_r, l, coef, 0, batch, seq, 1, seq)
        y_s, fin_s = _ssm(u, state_in[l], we, tv, ssm_d_r, l, coef, n_p, dec_batch, t_len, dec_batch, t_len)
        outs["ssm_p"].append(_tiles_to_state(fin_p))
        outs["ssm_s"].append(_tiles_to_state(fin_s))
        x = _merge(x, attn_p, attn_s, y_p, y_s, w_glu_b, b_glu_r, g_a_r, g_s_r, w_out_b, l, tm)

        mk_p, mv_p = _mem_kv(mem_flat, g_mem_r, wk_b, wv_b, l, 512)
        outs["mk_p"].append(mk_p.reshape(batch, N_MEM, N_XHEADS, XHEAD_DIM))
        outs["mv_p"].append(mv_p.reshape(batch, N_MEM, N_XHEADS, XHEAD_DIM))
        tiles_per_seq = seq // tm
        x = _xattn(x, g_xattn_r, wq_b, wo_b, l, mk_p, mv_p,
                   pl.BlockSpec((N_MEM, D_MODEL), lambda i: (i // tiles_per_seq, 0)), 0, n_p, 1, tm)
        x = _xattn(x, g_xattn_r, wq_b, wo_b, l, cmem_k, cmem_v,
                   pl.BlockSpec((None, sample_nb * N_MEM, D_MODEL), lambda i: (l, i, 0)), n_p, n_s, sample_nb, tm)

        x = _moe(x, g_ffn_r, wr, br, wgu_b, wd_b, l, tm_wide)

    gf = g_final.reshape(1, D_MODEL)
    y_p = _final_norm(x, gf, 0, n_p, tm)
    y_s = _final_norm(x, gf, n_p, n_s, tm)
    st = lambda name: jnp.stack(outs[name], axis=0)
    return (y_p.reshape(batch, seq, D_MODEL), y_s.reshape(dec_batch, t_len, D_MODEL),
            st("wk_p"), st("wv_p"), st("ssm_p"), st("mk_p"), st("mv_p"), st("wk_s"), st("wv_s"), st("ssm_s"))
```

```python
import functools
import math

import jax
import jax.numpy as jnp
from jax import lax
from jax.experimental import pallas as pl
from jax.experimental.pallas import tpu as pltpu

F32 = jnp.float32
BF16 = jnp.bfloat16

D_MODEL = 1024
CHUNK = 64
EPS = 1e-6
NEG_INF = -1e30
N_HEADS = 8
N_KV_HEADS = 2
HEAD_DIM = 64
ATTN_WIDTH = N_HEADS * HEAD_DIM
KV_WIDTH = N_KV_HEADS * HEAD_DIM
WINDOW = 128
ROPE_THETA = 10000.0
SSM_WIDTH = D_MODEL - ATTN_WIDTH
SSM_GROUP = 16
N_SSM_GROUPS = SSM_WIDTH // SSM_GROUP
SSM_STATE = 64
N_STATE = N_SSM_GROUPS * SSM_STATE
IN_WIDTH = ATTN_WIDTH + 2 * KV_WIDTH + SSM_WIDTH
N_MEM = 256
N_XHEADS = 4
XHEAD_DIM = D_MODEL // N_XHEADS
N_EXPERT_GROUPS = 4
EXPERTS_PER_GROUP = 8
N_EXPERTS = N_EXPERT_GROUPS * EXPERTS_PER_GROUP
EXPERT_FF = 128
PAST_LEN = 4096

LANES = 128
SUBLANES = 8
VMEM_LIMIT = 56 * 1024 * 1024


def _cparams(*sem):
    return pltpu.CompilerParams(dimension_semantics=sem, vmem_limit_bytes=VMEM_LIMIT)


def _rms(x, g):
    return x * lax.rsqrt(jnp.mean(x * x, axis=-1, keepdims=True) + EPS) * g


def _layer_spec(arr, l):
    shape = arr.shape[1:]
    zeros = (0,) * len(shape)
    return pl.BlockSpec((None,) + shape, lambda *_: (l,) + zeros)


def _rope_pairs(t, cos, sin_signed, first_half):
    swapped = jnp.where(first_half, pltpu.roll(t, LANES - HEAD_DIM // 2, 1), pltpu.roll(t, HEAD_DIM // 2, 1))
    return t * cos + swapped * sin_signed


def _in_proj_kernel(x_ref, g_ref, w_ref, cos_ref, sin_ref, q_ref, k_ref, v_ref, u_ref):
    xn = _rms(x_ref[...], g_ref[...])
    z = jnp.dot(xn.astype(BF16), w_ref[...], preferred_element_type=F32)
    cos = cos_ref[...]
    sin = sin_ref[...]
    lane = lax.broadcasted_iota(jnp.int32, cos.shape, 1)
    first_half = (lane % HEAD_DIM) < (HEAD_DIM // 2)
    scale = 1.0 / math.sqrt(HEAD_DIM)
    for j in range(ATTN_WIDTH // LANES):
        t = z[:, j * LANES:(j + 1) * LANES]
        q_ref[:, j * LANES:(j + 1) * LANES] = (_rope_pairs(t, cos, sin, first_half) * scale).astype(BF16)
    k_ref[...] = _rope_pairs(z[:, ATTN_WIDTH:ATTN_WIDTH + KV_WIDTH], cos, sin, first_half)
    v_ref[...] = z[:, ATTN_WIDTH + KV_WIDTH:ATTN_WIDTH + 2 * KV_WIDTH]
    u_ref[...] = z[:, ATTN_WIDTH + 2 * KV_WIDTH:]


def _in_proj(x, g, w_bf16, l, cos_tab, sin_tab, n_prompt, seq, tm):
    n = x.shape[0]
    n_prompt_tiles = n_prompt // tm
    tiles_per_seq = seq // tm

    def tab_map(i):
        return (jnp.where(i < n_prompt_tiles, i % tiles_per_seq, tiles_per_seq), 0)

    row = lambda w: pl.BlockSpec((tm, w), lambda i: (i, 0))
    return pl.pallas_call(
        _in_proj_kernel,
        grid=(n // tm,),
        in_specs=[row(D_MODEL), _layer_spec(g, l), _layer_spec(w_bf16, l),
                  pl.BlockSpec((tm, LANES), tab_map),
                  pl.BlockSpec((tm, LANES), tab_map)],
        out_specs=[row(ATTN_WIDTH), row(KV_WIDTH), row(KV_WIDTH), row(SSM_WIDTH)],
        out_shape=[jax.ShapeDtypeStruct((n, ATTN_WIDTH), BF16),
                   jax.ShapeDtypeStruct((n, KV_WIDTH), F32),
                   jax.ShapeDtypeStruct((n, KV_WIDTH), F32),
                   jax.ShapeDtypeStruct((n, SSM_WIDTH), F32)],
        compiler_params=_cparams("parallel"),
        name="in_proj",
    )(x, g, w_bf16, cos_tab, sin_tab)


def _attend_pairs(q, keys, vals, mask, sink_ref, o_ref, row0):
    tq = q.shape[0]
    lane = lax.broadcasted_iota(jnp.int32, keys.shape, 1)
    low = lane < HEAD_DIM
    k_sw = pltpu.roll(keys, HEAD_DIM, 1)
    v_sw = pltpu.roll(vals, HEAD_DIM, 1)
    kk = [jnp.where(low, keys, k_sw).astype(BF16), jnp.where(low, k_sw, keys).astype(BF16)]
    vv = [jnp.where(low, vals, v_sw).astype(BF16), jnp.where(low, v_sw, vals).astype(BF16)]
    qlane = lax.broadcasted_iota(jnp.int32, (tq, LANES), 1)
    qlow = qlane < HEAD_DIM
    row_top = lax.broadcasted_iota(jnp.int32, (2 * tq, 1), 0) < tq
    zero = jnp.zeros((), BF16)
    for pair in range(N_HEADS // 2):
        kv = pair // (N_HEADS // N_KV_HEADS // 2)
        qp = q[:, pair * LANES:(pair + 1) * LANES]
        qs = jnp.concatenate([jnp.where(qlow, qp, zero), jnp.where(qlow, zero, qp)], axis=0)
        s = lax.dot_general(qs, kk[kv], (((1,), (1,)), ((), ())), preferred_element_type=F32)
        if mask is not None:
            s = jnp.where(mask, s, NEG_INF)
        sink = jnp.where(row_top, sink_ref[2 * pair], sink_ref[2 * pair + 1])
        m = jnp.maximum(jnp.max(s, axis=-1, keepdims=True), sink)
        e = jnp.exp(s - m)
        p = e / (jnp.sum(e, axis=-1, keepdims=True) + jnp.exp(sink - m))
        o = jnp.dot(p.astype(BF16), vv[kv], preferred_element_type=F32)
        o_ref[row0:row0 + tq, pair * LANES:(pair + 1) * LANES] = jnp.where(qlow, o[:tq], o[tq:])


def _attn_prompt_kernel(sink_ref, q_ref, kp_ref, kc_ref, vp_ref, vc_ref, o_ref):
    i = pl.program_id(1)
    tq = q_ref.shape[0]
    keys = jnp.concatenate([kp_ref[...], kc_ref[...]], axis=0)
    vals = jnp.concatenate([vp_ref[...], vc_ref[...]], axis=0)
    nk = keys.shape[0]
    r = lax.broadcasted_iota(jnp.int32, (2 * tq, nk), 0) % tq
    c = lax.broadcasted_iota(jnp.int32, (2 * tq, nk), 1)
    qc = r // CHUNK
    kc = c // CHUNK
    mask = (kc >= qc) & (kc <= qc + WINDOW // CHUNK) & ((c >= WINDOW) | (i > 0))
    _attend_pairs(q_ref[...], keys, vals, mask, sink_ref, o_ref, 0)


def _attn_prompt(q, k, v, sink, batch, seq):
    tq = WINDOW
    nt = seq // tq
    cur = lambda b, i: (b * nt + i, 0)
    prev = lambda b, i: (b * nt + jnp.maximum(i - 1, 0), 0)
    return pl.pallas_call(
        _attn_prompt_kernel,
        grid=(batch, nt),
        in_specs=[pl.BlockSpec(memory_space=pltpu.SMEM),
                  pl.BlockSpec((tq, ATTN_WIDTH), cur),
                  pl.BlockSpec((tq, KV_WIDTH), prev),
                  pl.BlockSpec((tq, KV_WIDTH), cur),
                  pl.BlockSpec((tq, KV_WIDTH), prev),
                  pl.BlockSpec((tq, KV_WIDTH), cur)],
        out_specs=pl.BlockSpec((tq, ATTN_WIDTH), cur),
        out_shape=jax.ShapeDtypeStruct((batch * seq, ATTN_WIDTH), F32),
        compiler_params=_cparams("parallel", "parallel"),
        name="attn_prompt",
    )(sink, q, k, k, v, v)


def _attn_sample_kernel(sink_ref, q_ref, ck_ref, cv_ref, k_ref, v_ref, o_ref):
    nb = ck_ref.shape[0]
    t = q_ref.shape[0] // nb
    for b in range(nb):
        rows = slice(b * t, (b + 1) * t)
        keys = jnp.concatenate([ck_ref[b], k_ref[rows, :]], axis=0)
        vals = jnp.concatenate([cv_ref[b], v_ref[rows, :]], axis=0)
        _attend_pairs(q_ref[rows, :], keys, vals, None, sink_ref, o_ref, b * t)


def _attn_sample(q, k, v, cache_k, cache_v, l, sink, n_prompt, dec_batch, t, nb):
    w = cache_k.shape[2]
    rows = nb * t
    base = n_prompt // rows
    tok = lambda width: pl.BlockSpec((rows, width), lambda i: (base + i, 0))
    cache = pl.BlockSpec((None, nb, w, KV_WIDTH), lambda i: (l, i, 0, 0))
    return pl.pallas_call(
        _attn_sample_kernel,
        grid=(dec_batch // nb,),
        in_specs=[pl.BlockSpec(memory_space=pltpu.SMEM),
                  tok(ATTN_WIDTH), cache, cache, tok(KV_WIDTH), tok(KV_WIDTH)],
        out_specs=pl.BlockSpec((rows, ATTN_WIDTH), lambda i: (i, 0)),
        out_shape=jax.ShapeDtypeStruct((dec_batch * t, ATTN_WIDTH), F32),
        compiler_params=_cparams("parallel"),
        name="attn_sample",
    )(sink, q, cache_k, cache_v, k, v)


SSM_BLOCK = SUBLANES


def _cmul(ar, ai, br, bi):
    return ar * br - ai * bi, ar * bi + ai * br


def _ssm_disc_kernel(lre_ref, lim_ref, dt_ref, bre_ref, bim_ref, cre_ref, cim_ref,
                     we_ref, tv_ref, coef_ref, vt_ref, wb_ref):
    we_ref[...] = jnp.zeros(we_ref.shape, we_ref.dtype)
    vt_ref[...] = jnp.zeros(vt_ref.shape, vt_ref.dtype)
    wb_ref[...] = jnp.zeros(wb_ref.shape, wb_ref.dtype)
    lane = lax.broadcasted_iota(jnp.int32, (SSM_GROUP, LANES), 1)
    half = [lane < SSM_STATE, lane >= SSM_STATE]
    row8 = lax.broadcasted_iota(jnp.int32, (SUBLANES, LANES), 0)
    for q in range(STATE_TILES):
        lre = lre_ref[q]
        lim = lim_ref[q]
        dt = dt_ref[q]
        mag = jnp.exp(lre * dt)
        ang = lim * dt
        lbr = mag * jnp.cos(ang)
        lbi = mag * jnp.sin(ang)
        nr, ni = lbr - 1.0, lbi
        den = lre * lre + lim * lim
        fr = (nr * lre + ni * lim) / den
        fi = (ni * lre - nr * lim) / den
        bbr, bbi = _cmul(fr, fi, bre_ref[q], bim_ref[q])
        cr, ci = cre_ref[q], cim_ref[q]
        pw = [(jnp.ones_like(lbr), jnp.zeros_like(lbr))]
        for _ in range(SSM_BLOCK):
            pw.append(_cmul(pw[-1][0], pw[-1][1], lbr, lbi))
        re_l = slice(q * 2 * LANES, q * 2 * LANES + LANES)
        im_l = slice(q * 2 * LANES + LANES, (q + 1) * 2 * LANES)
        for h in range(2):
            g = 2 * q + h
            grow = lambda blk: slice(blk * LANES + g * SSM_GROUP, blk * LANES + (g + 1) * SSM_GROUP)
            for j in range(SSM_BLOCK):
                wr, wi = _cmul(pw[SSM_BLOCK - 1 - j][0], pw[SSM_BLOCK - 1 - j][1], bbr, bbi)
                we_ref[grow(j), re_l] = jnp.where(half[h], wr, 0.0).astype(we_ref.dtype)
                we_ref[grow(j), im_l] = jnp.where(half[h], wi, 0.0).astype(we_ref.dtype)
            for d in range(SSM_BLOCK + 1):
                xr, xi = _cmul(cr, ci, pw[d][0], pw[d][1])
                vt_ref[grow(d), re_l] = jnp.where(half[h], xr, 0.0)
                vt_ref[grow(d), im_l] = jnp.where(half[h], -xi, 0.0)
            wb_ref[grow(0), re_l] = jnp.where(half[h], bbr, 0.0)
            wb_ref[grow(0), im_l] = jnp.where(half[h], bbi, 0.0)
        l8 = [pw[SSM_BLOCK]]
        for _ in range(SUBLANES - 1):
            l8.append(_cmul(l8[-1][0], l8[-1][1], pw[SSM_BLOCK][0], pw[SSM_BLOCK][1]))
        co = slice(q * LANES, (q + 1) * LANES)
        for kind, sh in enumerate((1, 2, 4)):
            for a in range(2):
                coef_ref[kind, a, :, co] = jnp.where(row8 >= sh, l8[sh - 1][a], 0.0)
        for a in range(2):
            tab = jnp.zeros((SUBLANES, LANES), F32)
            for k in range(SUBLANES):
                tab = jnp.where(row8 == k, l8[k][a], tab)
            coef_ref[3, a, :, co] = tab
    t0 = lax.dot_general(wb_ref[...], vt_ref[0:BLOCK_LANES, :], (((1,), (1,)), ((), ())),
                         preferred_element_type=F32, precision=lax.Precision.HIGHEST)
    for j in range(SSM_BLOCK):
        if j:
            tv_ref[j * LANES:(j + 1) * LANES, 0:j * LANES] = jnp.zeros((LANES, j * LANES), tv_ref.dtype)
        tv_ref[j * LANES:(j + 1) * LANES, j * LANES:] = t0[:, 0:BLOCK_LANES - j * LANES].astype(tv_ref.dtype)
    tv_ref[BLOCK_LANES:, :] = jnp.transpose(vt_ref[LANES:, :]).astype(tv_ref.dtype)


def _ssm_discretize(lam_re, lam_im, log_dt, b_re, b_im, c_re, c_im):
    g, p = lam_re.shape
    npair = g // 2
    pair = lambda a: a.reshape(npair, 1, 2 * p)
    rows = lambda a: a.reshape(npair, 2, SSM_GROUP, p).transpose(0, 2, 1, 3).reshape(npair, SSM_GROUP, 2 * p)
    dt = jnp.repeat(jnp.exp(log_dt), p).reshape(npair, 1, 2 * p)
    vec = pl.BlockSpec((STATE_TILES, 1, LANES), lambda m: (m, 0, 0))
    mat = pl.BlockSpec((STATE_TILES, SSM_GROUP, LANES), lambda m: (m, 0, 0))
    return pl.pallas_call(
        _ssm_disc_kernel,
        grid=(N_CH_BLOCKS,),
        in_specs=[vec, vec, vec, mat, mat, mat, mat],
        out_specs=[pl.BlockSpec((None, BLOCK_LANES, STATE_LANES), lambda m: (m, 0, 0)),
                   pl.BlockSpec((None, BLOCK_LANES + STATE_LANES, BLOCK_LANES), lambda m: (m, 0, 0)),
                   pl.BlockSpec((None, 4, 2, SUBLANES, STATE_LANES // 2), lambda m: (m, 0, 0, 0, 0))],
        out_shape=[jax.ShapeDtypeStruct((N_CH_BLOCKS, BLOCK_LANES, STATE_LANES), BF16),
                   jax.ShapeDtypeStruct((N_CH_BLOCKS, BLOCK_LANES + STATE_LANES, BLOCK_LANES), BF16),
                   jax.ShapeDtypeStruct((N_CH_BLOCKS, 4, 2, SUBLANES, STATE_LANES // 2), F32)],
        scratch_shapes=[pltpu.VMEM(((SSM_BLOCK + 1) * LANES, STATE_LANES), F32),
                        pltpu.VMEM((LANES, STATE_LANES), F32)],
        compiler_params=_cparams("parallel"),
        name="ssm_discretize",
    )(pair(lam_re), pair(lam_im), dt, rows(b_re.transpose(0, 2, 1)), rows(b_im.transpose(0, 2, 1)),
      rows(c_re), rows(c_im))


GROUPS_PER_CH_BLOCK = LANES // SSM_GROUP
N_CH_BLOCKS = SSM_WIDTH // LANES
STATE_LANES = 2 * GROUPS_PER_CH_BLOCK * SSM_STATE
STATE_TILES = STATE_LANES // (2 * LANES)
BLOCK_LANES = SSM_BLOCK * LANES


def _ssm_kernel(nb, u_ref, s0_ref, we_ref, tv_ref, d_ref, coef_ref, y_ref, fin_ref, st_ref, sprev_ref, ucat_ref):
    t_idx = pl.program_id(2)
    r = st_ref.shape[1] - SUBLANES
    rows = nb * r

    @pl.when(t_idx == 0)
    def _():
        for b in range(nb):
            st_ref[b, 0:SUBLANES, :] = jnp.broadcast_to(s0_ref[b], (SUBLANES, STATE_LANES))

    for j in range(SSM_BLOCK):
        ucat_ref[:, j * LANES:(j + 1) * LANES] = u_ref[pl.ds(j, rows, stride=SSM_BLOCK), :].astype(BF16)
    e = jnp.dot(ucat_ref[...], we_ref[...], preferred_element_type=F32)
    st_ref[:, SUBLANES:, :] = e.reshape(nb, r, STATE_LANES)

    first_row = lax.broadcasted_iota(jnp.int32, (SUBLANES, LANES), 0) == 0

    def group(rg, carry):
        r0 = pl.multiple_of(SUBLANES + rg * SUBLANES, SUBLANES)
        rp = pl.multiple_of(rg * SUBLANES, SUBLANES)
        for b in range(nb):
            for q in range(STATE_TILES):
                re_sl = pl.ds(q * 2 * LANES, LANES)
                im_sl = pl.ds(q * 2 * LANES + LANES, LANES)
                co = pl.ds(q * LANES, LANES)
                xr = st_ref[b, pl.ds(r0, SUBLANES), re_sl]
                xi = st_ref[b, pl.ds(r0, SUBLANES), im_sl]
                for step, sh in enumerate((1, 2, 4)):
                    ar, ai = _cmul(coef_ref[step, 0, :, co], coef_ref[step, 1, :, co],
                                   pltpu.roll(xr, sh, 0), pltpu.roll(xi, sh, 0))
                    xr = xr + ar
                    xi = xi + ai
                pr = jnp.broadcast_to(st_ref[b, pl.ds(rp, SUBLANES), re_sl][SUBLANES - 1:, :], (SUBLANES, LANES))
                pi = jnp.broadcast_to(st_ref[b, pl.ds(rp, SUBLANES), im_sl][SUBLANES - 1:, :], (SUBLANES, LANES))
                ar, ai = _cmul(coef_ref[3, 0, :, co], coef_ref[3, 1, :, co], pr, pi)
                xr = xr + ar
                xi = xi + ai
                st_ref[b, pl.ds(r0, SUBLANES), re_sl] = xr
                st_ref[b, pl.ds(r0, SUBLANES), im_sl] = xi
                out_rows = pl.ds(pl.multiple_of(b * r + rg * SUBLANES, SUBLANES), SUBLANES)
                sprev_ref[out_rows, re_sl] = jnp.where(first_row, pr, pltpu.roll(xr, 1, 0))
                sprev_ref[out_rows, im_sl] = jnp.where(first_row, pi, pltpu.roll(xi, 1, 0))
        return carry

    lax.fori_loop(0, r // SUBLANES, group, 0)

    lhs = jnp.concatenate([ucat_ref[...], sprev_ref[...].astype(BF16)], axis=1)
    ycat = jnp.dot(lhs, tv_ref[...], preferred_element_type=F32)
    d = d_ref[...]
    for t in range(SSM_BLOCK):
        tok = pl.ds(t, rows, stride=SSM_BLOCK)
        y_ref[tok, :] = ycat[:, t * LANES:(t + 1) * LANES] + d * u_ref[tok, :]

    for b in range(nb):
        tail = st_ref[b, r:r + SUBLANES, :]
        st_ref[b, 0:SUBLANES, :] = tail
        fin_ref[b] = tail[SUBLANES - 1:SUBLANES, :]


def _ssm(u, s0, we, tv, d, l, coef, row_base, n_seq, seq_len, nb, tt):
    rows = nb * tt
    r = tt // SSM_BLOCK
    nt = seq_len // tt
    base = row_base // rows
    return pl.pallas_call(
        functools.partial(_ssm_kernel, nb),
        grid=(N_CH_BLOCKS, n_seq // nb, nt),
        in_specs=[pl.BlockSpec((rows, LANES), lambda m, i, j: (base + i * nt + j, m)),
                  pl.BlockSpec((nb, 1, STATE_LANES), lambda m, i, j: (i, 0, m)),
                  pl.BlockSpec((None, BLOCK_LANES, STATE_LANES), lambda m, i, j: (m, 0, 0)),
                  pl.BlockSpec((None, BLOCK_LANES + STATE_LANES, BLOCK_LANES), lambda m, i, j: (m, 0, 0)),
                  pl.BlockSpec((None, 1, LANES), lambda m, i, j: (l, 0, m)),
                  pl.BlockSpec((None, 4, 2, SUBLANES, STATE_LANES // 2), lambda m, i, j: (m, 0, 0, 0, 0))],
        out_specs=[pl.BlockSpec((rows, LANES), lambda m, i, j: (i * nt + j, m)),
                   pl.BlockSpec((nb, 1, STATE_LANES), lambda m, i, j: (i, 0, m))],
        out_shape=[jax.ShapeDtypeStruct((n_seq * seq_len, SSM_WIDTH), F32),
                   jax.ShapeDtypeStruct((n_seq, 1, N_CH_BLOCKS * STATE_LANES), F32)],
        scratch_shapes=[pltpu.VMEM((nb, SUBLANES + r, STATE_LANES), F32),
                        pltpu.VMEM((nb * r, STATE_LANES), F32),
                        pltpu.VMEM((nb * r, BLOCK_LANES), BF16)],
        compiler_params=_cparams("parallel", "parallel", "arbitrary"),
        name="ssm_scan",
    )(u, s0, we, tv, d, coef)


def _state_to_tiles(s):
    lead = s.shape[:-3]
    t = s.reshape(lead + (N_CH_BLOCKS, STATE_TILES, 2, SSM_STATE, 2))
    t = jnp.moveaxis(t, -1, -3)
    return t.reshape(lead + (1, N_CH_BLOCKS * STATE_LANES))


def _tiles_to_state(f):
    b = f.shape[0]
    t = f.reshape(b, N_CH_BLOCKS, STATE_TILES, 2, 2, SSM_STATE)
    t = jnp.moveaxis(t, 3, -1)
    return t.reshape(b, N_SSM_GROUPS, SSM_STATE, 2)


def _merge_kernel(n_prompt_tiles, x_ref, ap_ref, as_ref, yp_ref, ys_ref, wglu_ref, bglu_ref, ga_ref, gs_ref,
                  wout_ref, o_ref):
    is_prompt = pl.program_id(0) < n_prompt_tiles
    attn = jnp.where(is_prompt, ap_ref[...], as_ref[...])
    g = jax.nn.gelu(jnp.where(is_prompt, yp_ref[...], ys_ref[...]))
    glu = g * jax.nn.sigmoid(jnp.dot(g.astype(BF16), wglu_ref[...], preferred_element_type=F32) + bglu_ref[...])
    na = _rms(attn, ga_ref[...]).astype(BF16)
    ns = _rms(glu, gs_ref[...]).astype(BF16)
    o = jnp.dot(na, wout_ref[0:ATTN_WIDTH, :], preferred_element_type=F32)
    o = o + jnp.dot(ns, wout_ref[ATTN_WIDTH:, :], preferred_element_type=F32)
    o_ref[...] = x_ref[...] + o


def _merge(x, attn_p, attn_s, y_p, y_s, wglu, bglu, ga, gs, wout, l, tm):
    n = x.shape[0]
    npt = attn_p.shape[0] // tm
    nst = attn_s.shape[0] // tm
    row = lambda w: pl.BlockSpec((tm, w), lambda i: (i, 0))
    prow = lambda w: pl.BlockSpec((tm, w), lambda i: (jnp.minimum(i, npt - 1), 0))
    srow = lambda w: pl.BlockSpec((tm, w), lambda i: (jnp.clip(i - npt, 0, nst - 1), 0))
    return pl.pallas_call(
        functools.partial(_merge_kernel, npt),
        grid=(n // tm,),
        in_specs=[row(D_MODEL), prow(ATTN_WIDTH), srow(ATTN_WIDTH), prow(SSM_WIDTH), srow(SSM_WIDTH),
                  _layer_spec(wglu, l), _layer_spec(bglu, l), _layer_spec(ga, l), _layer_spec(gs, l),
                  _layer_spec(wout, l)],
        out_specs=row(D_MODEL),
        out_shape=jax.ShapeDtypeStruct((n, D_MODEL), F32),
        compiler_params=_cparams("parallel"),
        name="merge_heads",
    )(x, attn_p, attn_s, y_p, y_s, wglu, bglu, ga, gs, wout)


def _mem_kv_kernel(m_ref, g_ref, wk_ref, wv_ref, k_ref, v_ref):
    mn = _rms(m_ref[...], g_ref[...]).astype(BF16)
    k_ref[...] = jnp.dot(mn, wk_ref[...], preferred_element_type=F32)
    v_ref[...] = jnp.dot(mn, wv_ref[...], preferred_element_type=F32)


def _mem_kv(mem, g, wk, wv, l, tm):
    n = mem.shape[0]
    row = pl.BlockSpec((tm, D_MODEL), lambda i: (i, 0))
    return pl.pallas_call(
        _mem_kv_kernel,
        grid=(n // tm,),
        in_specs=[row, _layer_spec(g, l), _layer_spec(wk, l), _layer_spec(wv, l)],
        out_specs=[row, row],
        out_shape=[jax.ShapeDtypeStruct((n, D_MODEL), F32)] * 2,
        compiler_params=_cparams("parallel"),
        name="mem_kv",
    )(mem, g, wk, wv)


def _xattn_kernel(nb, x_ref, g_ref, wq_ref, wo_ref, mk_ref, mv_ref, o_ref, att_ref):
    t = x_ref.shape[0] // nb
    x = x_ref[...]
    hn = _rms(x, g_ref[...]).astype(BF16)
    q = jnp.dot(hn, wq_ref[...], preferred_element_type=F32) * (1.0 / math.sqrt(XHEAD_DIM))
    q = q.astype(BF16)
    for b in range(nb):
        mk = mk_ref[b * N_MEM:(b + 1) * N_MEM, :].astype(BF16)
        mv = mv_ref[b * N_MEM:(b + 1) * N_MEM, :].astype(BF16)
        for h in range(N_XHEADS):
            sl = slice(h * XHEAD_DIM, (h + 1) * XHEAD_DIM)
            s = lax.dot_general(q[b * t:(b + 1) * t, sl], mk[:, sl], (((1,), (1,)), ((), ())),
                                preferred_element_type=F32)
            m = jnp.max(s, axis=-1, keepdims=True)
            e = jnp.exp(s - m)
            p = e / jnp.sum(e, axis=-1, keepdims=True)
            att_ref[b * t:(b + 1) * t, sl] = jnp.dot(p.astype(BF16), mv[:, sl], preferred_element_type=F32)
    o = jnp.dot(att_ref[...].astype(BF16), wo_ref[...], preferred_element_type=F32)
    o_ref[...] = x + o


def _xattn(x, g, wq, wo, l, mk, mv, mem_spec, row_base, n_rows, nb, tm):
    base = row_base // tm
    xspec = pl.BlockSpec((tm, D_MODEL), lambda i: (base + i, 0))
    return pl.pallas_call(
        functools.partial(_xattn_kernel, nb),
        grid=(n_rows // tm,),
        in_specs=[xspec, _layer_spec(g, l), _layer_spec(wq, l), _layer_spec(wo, l), mem_spec, mem_spec],
        out_specs=xspec,
        out_shape=jax.ShapeDtypeStruct(x.shape, F32),
        scratch_shapes=[pltpu.VMEM((tm, D_MODEL), F32)],
        input_output_aliases={0: 0},
        compiler_params=_cparams("parallel"),
        name="cross_attn",
    )(x, g, wq, wo, mk, mv)


ROUTER_LANES = LANES
EXPERT_LANE0 = N_EXPERT_GROUPS
EXPERTS_PER_STEP = 4


def _route(logits):
    lane_i = lax.broadcasted_iota(jnp.int32, logits.shape, 1)
    lane = lane_i.astype(F32)
    neg = jnp.float32(-jnp.inf)
    is_g = lane_i < N_EXPERT_GROUPS
    gl = jnp.where(is_g, logits, neg)
    gmax = jnp.max(gl, axis=-1, keepdims=True)
    gidx = jnp.min(jnp.where(gl == gmax, lane, float(ROUTER_LANES)), axis=-1, keepdims=True)
    g_w = 1.0 / jnp.sum(jnp.where(is_g, jnp.exp(gl - gmax), 0.0), axis=-1, keepdims=True)
    first = EXPERT_LANE0 + gidx * EXPERTS_PER_GROUP
    sel = (lane >= first) & (lane < first + EXPERTS_PER_GROUP)
    el = jnp.where(sel, logits, neg)
    m1 = jnp.max(el, axis=-1, keepdims=True)
    i1 = jnp.min(jnp.where(el == m1, lane, float(ROUTER_LANES)), axis=-1, keepdims=True)
    el2 = jnp.where(lane == i1, neg, el)
    m2 = jnp.max(el2, axis=-1, keepdims=True)
    i2 = jnp.min(jnp.where(el2 == m2, lane, float(ROUTER_LANES)), axis=-1, keepdims=True)
    r = jnp.exp(m2 - m1)
    w1 = g_w / (1.0 + r)
    w2 = w1 * r
    return jnp.where(lane == i1, w1, jnp.where(lane == i2, w2, 0.0))


def _moe_kernel(x_ref, g_ref, wr_ref, br_ref, wgu_ref, wd_ref, o_ref, xn_ref, gate_ref):
    e_step = pl.program_id(1)

    @pl.when(e_step == 0)
    def _():
        x = x_ref[...]
        xn = _rms(x, g_ref[...])
        xn_ref[...] = xn.astype(BF16)
        logits = jnp.dot(xn, wr_ref[...], preferred_element_type=F32,
                         precision=lax.Precision.HIGHEST) + br_ref[...]
        gate_ref[...] = _route(logits)
        o_ref[...] = x

    xn = xn_ref[...]
    gates = gate_ref[...]
    lane = lax.broadcasted_iota(jnp.int32, gates.shape, 1)
    hids = []
    for j in range(EXPERTS_PER_STEP):
        h = jnp.dot(xn, wgu_ref[j], preferred_element_type=F32)
        ge = jnp.sum(jnp.where(lane == EXPERT_LANE0 + e_step * EXPERTS_PER_STEP + j, gates, 0.0),
                     axis=-1, keepdims=True)
        hid = jax.nn.silu(h[:, :EXPERT_FF]) * h[:, EXPERT_FF:] * ge
        hids.append(hid.astype(BF16))
    hid = jnp.concatenate(hids, axis=1)
    o_ref[...] += jnp.dot(hid, wd_ref[...], preferred_element_type=F32)


def _moe(x, g, wr, br, wgu, wd, l, tm):
    n = x.shape[0]
    es = EXPERTS_PER_STEP
    row = pl.BlockSpec((tm, D_MODEL), lambda i, e: (i, 0))
    return pl.pallas_call(
        _moe_kernel,
        grid=(n // tm, N_EXPERTS // es),
        in_specs=[row, _layer_spec(g, l), _layer_spec(wr, l), _layer_spec(br, l),
                  pl.BlockSpec((None, es, D_MODEL, 2 * EXPERT_FF), lambda i, e: (l, e, 0, 0)),
                  pl.BlockSpec((None, es * EXPERT_FF, D_MODEL), lambda i, e: (l, e, 0))],
        out_specs=row,
        out_shape=jax.ShapeDtypeStruct((n, D_MODEL), F32),
        scratch_shapes=[pltpu.VMEM((tm, D_MODEL), BF16), pltpu.VMEM((tm, ROUTER_LANES), F32)],
        compiler_params=_cparams("parallel", "arbitrary"),
        name="hier_moe",
    )(x, g, wr, br, wgu, wd)


def _final_norm_kernel(x_ref, g_ref, o_ref):
    o_ref[...] = _rms(x_ref[...], g_ref[...])


def _final_norm(x, g, row_base, n_rows, tm):
    base = row_base // tm
    return pl.pallas_call(
        _final_norm_kernel,
        grid=(n_rows // tm,),
        in_specs=[pl.BlockSpec((tm, D_MODEL), lambda i: (base + i, 0)),
                  pl.BlockSpec((1, D_MODEL), lambda i: (0, 0))],
        out_specs=pl.BlockSpec((tm, D_MODEL), lambda i: (i, 0)),
        out_shape=jax.ShapeDtypeStruct((n_rows, D_MODEL), F32),
        compiler_params=_cparams("parallel"),
        name="final_norm",
    )(x, g)


def _rope_tables(seq, t_len, tm):
    half = HEAD_DIM // 2
    inv = ROPE_THETA ** (-jnp.arange(half, dtype=F32) / half)
    pos_s = PAST_LEN + jnp.arange(t_len)
    pos = jnp.concatenate([jnp.arange(seq), jnp.tile(pos_s, tm // t_len)]).astype(F32)
    ang = pos[:, None] * inv[None, :]
    cos = jnp.tile(jnp.cos(ang), (1, LANES // half))
    sign = jnp.where((jnp.arange(LANES) % HEAD_DIM) < half, -1.0, 1.0).astype(F32)
    sin = jnp.tile(jnp.sin(ang), (1, LANES // half)) * sign[None, :]
    return cos, sin


def kernel(x_prompt, x_sample, cache_win_k, cache_win_v, state_ssm, cache_mem_k, cache_mem_v, mem_prompt, w_in, attn_sink, lam_re, lam_im, log_dt, ssm_b_re, ssm_b_im, ssm_c_re, ssm_c_im, ssm_d, w_glu, b_glu, g_attn_out, g_ssm_out, w_out, g_mix, g_xattn, g_mem, wq_x, wk_x, wv_x, wo_x, g_ffn, w_group, b_group, w_router, b_router, w_gate, w_up, w_down, g_final):
    batch, seq, _ = x_prompt.shape
    dec_batch, t_len, _ = x_sample.shape
    depth = w_in.shape[0]
    win_rows = cache_win_k.shape[2]
    n_p = batch * seq
    n_s = dec_batch * t_len
    tm = 512
    tm_wide = 1024 if (n_p + n_s) % 1024 == 0 else 512
    sample_nb = tm // t_len

    x = jnp.concatenate([x_prompt.reshape(n_p, D_MODEL), x_sample.reshape(n_s, D_MODEL)], axis=0)
    cos_tab, sin_tab = _rope_tables(seq, t_len, tm)
    mem_flat = mem_prompt.reshape(batch * N_MEM, D_MODEL)
    zero_state = jnp.zeros((batch, 1, 2 * N_STATE), F32)
    vec = lambda a: a.reshape(depth, 1, a.shape[-1])

    w_in_b, w_glu_b, w_out_b = w_in.astype(BF16), w_glu.astype(BF16), w_out.astype(BF16)
    wq_b, wk_b, wv_b, wo_b = (w.astype(BF16) for w in (wq_x, wk_x, wv_x, wo_x))
    wgu_b = jnp.concatenate([w_gate, w_up], axis=-1).astype(BF16)
    wd_b = w_down.reshape(depth, N_EXPERTS * EXPERT_FF, D_MODEL).astype(BF16)
    wr = jnp.concatenate([w_group, w_router.transpose(0, 2, 1, 3).reshape(depth, D_MODEL, N_EXPERTS)], axis=-1)
    wr = jnp.pad(wr, ((0, 0), (0, 0), (0, ROUTER_LANES - wr.shape[-1])))
    br = jnp.concatenate([b_group, b_router.reshape(depth, N_EXPERTS)], axis=-1)
    br = jnp.pad(br, ((0, 0), (0, ROUTER_LANES - br.shape[-1]))).reshape(depth, 1, ROUTER_LANES)
    g_mix_r, g_xattn_r, g_mem_r, g_ffn_r = vec(g_mix), vec(g_xattn), vec(g_mem), vec(g_ffn)
    g_a_r, g_s_r, b_glu_r, ssm_d_r = vec(g_attn_out), vec(g_ssm_out), vec(b_glu), vec(ssm_d)
    cache_k = cache_win_k.reshape(depth, dec_batch, win_rows, KV_WIDTH)
    cache_v = cache_win_v.reshape(depth, dec_batch, win_rows, KV_WIDTH)
    cmem_k = cache_mem_k.reshape(depth, dec_batch * N_MEM, D_MODEL)
    cmem_v = cache_mem_v.reshape(depth, dec_batch * N_MEM, D_MODEL)
    state_in = _state_to_tiles(state_ssm)

    outs = {k: [] for k in ("wk_p", "wv_p", "ssm_p", "mk_p", "mv_p", "wk_s", "wv_s", "ssm_s")}
    for l in range(depth):
        q, k, v, u = _in_proj(x, g_mix_r, w_in_b, l, cos_tab, sin_tab, n_p, seq, tm)
        attn_p = _attn_prompt(q, k, v, attn_sink[l], batch, seq)
        attn_s = _attn_sample(q, k, v, cache_k, cache_v, l, attn_sink[l], n_p, dec_batch, t_len, 4)
        kp = k[:n_p].reshape(batch, seq, N_KV_HEADS, HEAD_DIM)
        vp = v[:n_p].reshape(batch, seq, N_KV_HEADS, HEAD_DIM)
        outs["wk_p"].append(kp[:, -WINDOW:])
        outs["wv_p"].append(vp[:, -WINDOW:])
        ks = k[n_p:].reshape(dec_batch, t_len, KV_WIDTH)
        vs = v[n_p:].reshape(dec_batch, t_len, KV_WIDTH)
        k_all = jnp.concatenate([cache_k[l], ks], axis=1)[:, -win_rows:]
        v_all = jnp.concatenate([cache_v[l], vs], axis=1)[:, -win_rows:]
        outs["wk_s"].append(k_all.reshape(dec_batch, win_rows, N_KV_HEADS, HEAD_DIM))
        outs["wv_s"].append(v_all.reshape(dec_batch, win_rows, N_KV_HEADS, HEAD_DIM))

        we, tv, coef = _ssm_discretize(lam_re[l], lam_im[l], log_dt[l], ssm_b_re[l], ssm_b_im[l],
                                       ssm_c_re[l], ssm_c_im[l])
        y_p, fin_p = _ssm(u, zero_state, we, tv, ssm_d_r, l, coef, 0, batch, seq, 1, seq)
        y_s, fin_s = _ssm(u, state_in[l], we, tv, ssm_d_r, l, coef, n_p, dec_batch, t_len, dec_batch, t_len)
        outs["ssm_p"].append(_tiles_to_state(fin_p))
        outs["ssm_s"].append(_tiles_to_state(fin_s))
        x = _merge(x, attn_p, attn_s, y_p, y_s, w_glu_b, b_glu_r, g_a_r, g_s_r, w_out_b, l, tm)

        mk_p, mv_p = _mem_kv(mem_flat, g_mem_r, wk_b, wv_b, l, 512)
        outs["mk_p"].append(mk_p.reshape(batch, N_MEM, N_XHEADS, XHEAD_DIM))
        outs["mv_p"].append(mv_p.reshape(batch, N_MEM, N_XHEADS, XHEAD_DIM))
        tiles_per_seq = seq // tm
        x = _xattn(x, g_xattn_r, wq_b, wo_b, l, mk_p, mv_p,
                   pl.BlockSpec((N_MEM, D_MODEL), lambda i: (i // tiles_per_seq, 0)), 0, n_p, 1, tm)
        x = _xattn(x, g_xattn_r, wq_b, wo_b, l, cmem_k, cmem_v,
                   pl.BlockSpec((None, sample_nb * N_MEM, D_MODEL), lambda i: (l, i, 0)), n_p, n_s, sample_nb, tm)

        x = _moe(x, g_ffn_r, wr, br, wgu_b, wd_b, l, tm_wide)

    gf = g_final.reshape(1, D_MODEL)
    y_p = _final_norm(x, gf, 0, n_p, tm)
    y_s = _final_norm(x, gf, n_p, n_s, tm)
    st = lambda name: jnp.stack(outs[name], axis=0)
    return (y_p.reshape(batch, seq, D_MODEL), y_s.reshape(dec_batch, t_len, D_MODEL),
            st("wk_p"), st("wv_p"), st("ssm_p"), st("mk_p"), st("mv_p"), st("wk_s"), st("wv_s"), st("ssm_s"))
```

```python
import functools
import math

import jax
import jax.numpy as jnp
from jax import lax
from jax.experimental import pallas as pl
from jax.experimental.pallas import tpu as pltpu

F32 = jnp.float32
BF16 = jnp.bfloat16

D_MODEL = 1024
CHUNK = 64
EPS = 1e-6
NEG_INF = -1e30
N_HEADS = 8
N_KV_HEADS = 2
HEAD_DIM = 64
ATTN_WIDTH = N_HEADS * HEAD_DIM
KV_WIDTH = N_KV_HEADS * HEAD_DIM
WINDOW = 128
ROPE_THETA = 10000.0
SSM_WIDTH = D_MODEL - ATTN_WIDTH
SSM_GROUP = 16
N_SSM_GROUPS = SSM_WIDTH // SSM_GROUP
SSM_STATE = 64
N_STATE = N_SSM_GROUPS * SSM_STATE
IN_WIDTH = ATTN_WIDTH + 2 * KV_WIDTH + SSM_WIDTH
N_MEM = 256
N_XHEADS = 4
XHEAD_DIM = D_MODEL // N_XHEADS
N_EXPERT_GROUPS = 4
EXPERTS_PER_GROUP = 8
N_EXPERTS = N_EXPERT_GROUPS * EXPERTS_PER_GROUP
EXPERT_FF = 128
PAST_LEN = 4096

LANES = 128
SUBLANES = 8
VMEM_LIMIT = 56 * 1024 * 1024


def _cparams(*sem):
    return pltpu.CompilerParams(dimension_semantics=sem, vmem_limit_bytes=VMEM_LIMIT)


def _rms(x, g):
    return x * lax.rsqrt(jnp.mean(x * x, axis=-1, keepdims=True) + EPS) * g


def _layer_spec(arr, l):
    shape = arr.shape[1:]
    zeros = (0,) * len(shape)
    return pl.BlockSpec((None,) + shape, lambda *_: (l,) + zeros)


def _rope_pairs(t, cos, sin_signed, first_half):
    swapped = jnp.where(first_half, pltpu.roll(t, LANES - HEAD_DIM // 2, 1), pltpu.roll(t, HEAD_DIM // 2, 1))
    return t * cos + swapped * sin_signed


def _two_source_specs(xp, xs, n_prompt, tm, width):
    npt = n_prompt // tm
    s_off = 0 if xs is xp else npt
    s_last = xs.shape[0] // tm - 1
    pspec = pl.BlockSpec((tm, width), lambda i, *_: (jnp.minimum(i, npt - 1), 0))
    sspec = pl.BlockSpec((tm, width), lambda i, *_: (jnp.clip(i - s_off, npt - s_off, s_last), 0))
    return pspec, sspec


def _in_proj_kernel(n_prompt_tiles, xp_ref, xs_ref, g_ref, w_ref, cos_ref, sin_ref, q_ref, k_ref, v_ref, u_ref):
    x = jnp.where(pl.program_id(0) < n_prompt_tiles, xp_ref[...], xs_ref[...])
    xn = _rms(x, g_ref[...])
    z = jnp.dot(xn.astype(BF16), w_ref[...], preferred_element_type=F32)
    cos = cos_ref[...]
    sin = sin_ref[...]
    lane = lax.broadcasted_iota(jnp.int32, cos.shape, 1)
    first_half = (lane % HEAD_DIM) < (HEAD_DIM // 2)
    scale = 1.0 / math.sqrt(HEAD_DIM)
    for j in range(ATTN_WIDTH // LANES):
        t = z[:, j * LANES:(j + 1) * LANES]
        q_ref[:, j * LANES:(j + 1) * LANES] = (_rope_pairs(t, cos, sin, first_half) * scale).astype(BF16)
    k_ref[...] = _rope_pairs(z[:, ATTN_WIDTH:ATTN_WIDTH + KV_WIDTH], cos, sin, first_half)
    v_ref[...] = z[:, ATTN_WIDTH + KV_WIDTH:ATTN_WIDTH + 2 * KV_WIDTH]
    u_ref[...] = z[:, ATTN_WIDTH + 2 * KV_WIDTH:]


def _in_proj(xp, xs, n, g, w_bf16, l, cos_tab, sin_tab, n_prompt, seq, tm):
    n_prompt_tiles = n_prompt // tm
    tiles_per_seq = seq // tm

    def tab_map(i):
        return (jnp.where(i < n_prompt_tiles, i % tiles_per_seq, tiles_per_seq), 0)

    row = lambda w: pl.BlockSpec((tm, w), lambda i: (i, 0))
    return pl.pallas_call(
        functools.partial(_in_proj_kernel, n_prompt_tiles),
        grid=(n // tm,),
        in_specs=[*_two_source_specs(xp, xs, n_prompt, tm, D_MODEL), _layer_spec(g, l), _layer_spec(w_bf16, l),
                  pl.BlockSpec((tm, LANES), tab_map),
                  pl.BlockSpec((tm, LANES), tab_map)],
        out_specs=[row(ATTN_WIDTH), row(KV_WIDTH), row(KV_WIDTH), row(SSM_WIDTH)],
        out_shape=[jax.ShapeDtypeStruct((n, ATTN_WIDTH), BF16),
                   jax.ShapeDtypeStruct((n, KV_WIDTH), F32),
                   jax.ShapeDtypeStruct((n, KV_WIDTH), F32),
                   jax.ShapeDtypeStruct((n, SSM_WIDTH), F32)],
        compiler_params=_cparams("parallel"),
        name="in_proj",
    )(xp, xs, g, w_bf16, cos_tab, sin_tab)


def _kv_pairs(keys, vals):
    lane = lax.broadcasted_iota(jnp.int32, keys.shape, 1)
    low = lane < HEAD_DIM
    k_sw = pltpu.roll(keys, HEAD_DIM, 1)
    v_sw = pltpu.roll(vals, HEAD_DIM, 1)
    kk = [jnp.where(low, keys, k_sw).astype(BF16), jnp.where(low, k_sw, keys).astype(BF16)]
    vv = [jnp.where(low, vals, v_sw).astype(BF16), jnp.where(low, v_sw, vals).astype(BF16)]
    return kk, vv


def _attend_pairs(q, kk, vv, key_rows, mask_add, sink_ref, o_ref, row0):
    tq = q.shape[0]
    qlane = lax.broadcasted_iota(jnp.int32, (tq, LANES), 1)
    qlow = qlane < HEAD_DIM
    row_top = lax.broadcasted_iota(jnp.int32, (2 * tq, 1), 0) < tq
    zero = jnp.zeros((), BF16)
    for pair in range(N_HEADS // 2):
        kv = pair // (N_HEADS // N_KV_HEADS // 2)
        qp = q[:, pair * LANES:(pair + 1) * LANES]
        qs = jnp.concatenate([jnp.where(qlow, qp, zero), jnp.where(qlow, zero, qp)], axis=0)
        s = lax.dot_general(qs, kk[kv][key_rows, :], (((1,), (1,)), ((), ())), preferred_element_type=F32)
        if mask_add is not None:
            s = s + mask_add
        sink = jnp.where(row_top, sink_ref[2 * pair], sink_ref[2 * pair + 1])
        m = jnp.maximum(jnp.max(s, axis=-1, keepdims=True), sink)
        e = jnp.exp(s - m)
        p = e / (jnp.sum(e, axis=-1, keepdims=True) + jnp.exp(sink - m))
        o = jnp.dot(p.astype(BF16), vv[kv][key_rows, :], preferred_element_type=F32)
        o_ref[row0:row0 + tq, pair * LANES:(pair + 1) * LANES] = jnp.where(qlow, o[:tq], o[tq:])


ATTN_SUB = WINDOW
ATTN_TILE = 2 * ATTN_SUB


def _attn_prompt_kernel(sink_ref, ma_ref, mb_ref, q_ref, kp_ref, kc_ref, vp_ref, vc_ref, o_ref):
    kk, vv = _kv_pairs(jnp.concatenate([kp_ref[...], kc_ref[...]], axis=0),
                       jnp.concatenate([vp_ref[...], vc_ref[...]], axis=0))
    for s, m_ref in enumerate((ma_ref, mb_ref)):
        rows = slice(s * ATTN_SUB, s * ATTN_SUB + 2 * WINDOW)
        _attend_pairs(q_ref[s * ATTN_SUB:(s + 1) * ATTN_SUB, :], kk, vv, rows, m_ref[...], sink_ref, o_ref, s * ATTN_SUB)


def _band_masks():
    r = (jnp.arange(2 * ATTN_SUB) % ATTN_SUB)[:, None] // CHUNK
    c = jnp.arange(2 * WINDOW)[None, :]
    band = (c // CHUNK >= r) & (c // CHUNK <= r + WINDOW // CHUNK)
    masks = jnp.stack([band, band & (c >= WINDOW)])
    return jnp.where(masks, 0.0, NEG_INF).astype(F32)


def _attn_prompt(q, k, v, sink, batch, seq):
    nt = seq // ATTN_TILE
    per_seq = seq // ATTN_SUB
    cur = lambda b, i: (b * nt + i, 0)
    prev = lambda b, i: (b * per_seq + jnp.maximum(2 * i - 1, 0), 0)
    masks = _band_masks()
    return pl.pallas_call(
        _attn_prompt_kernel,
        grid=(batch, nt),
        in_specs=[pl.BlockSpec(memory_space=pltpu.SMEM),
                  pl.BlockSpec((None, 2 * ATTN_SUB, 2 * WINDOW), lambda b, i: (jnp.where(i == 0, 1, 0), 0, 0)),
                  pl.BlockSpec((None, 2 * ATTN_SUB, 2 * WINDOW), lambda b, i: (0, 0, 0)),
                  pl.BlockSpec((ATTN_TILE, ATTN_WIDTH), cur),
                  pl.BlockSpec((ATTN_SUB, KV_WIDTH), prev),
                  pl.BlockSpec((ATTN_TILE, KV_WIDTH), cur),
                  pl.BlockSpec((ATTN_SUB, KV_WIDTH), prev),
                  pl.BlockSpec((ATTN_TILE, KV_WIDTH), cur)],
        out_specs=pl.BlockSpec((ATTN_TILE, ATTN_WIDTH), cur),
        out_shape=jax.ShapeDtypeStruct((batch * seq, ATTN_WIDTH), F32),
        compiler_params=_cparams("parallel", "parallel"),
        name="attn_prompt",
    )(sink, masks, masks, q, k, k, v, v)


def _attn_sample_kernel(sink_ref, q_ref, ck_ref, cv_ref, k_ref, v_ref, o_ref):
    nb = ck_ref.shape[0]
    t = q_ref.shape[0] // nb
    for b in range(nb):
        rows = slice(b * t, (b + 1) * t)
        kk, vv = _kv_pairs(jnp.concatenate([ck_ref[b], k_ref[rows, :]], axis=0),
                           jnp.concatenate([cv_ref[b], v_ref[rows, :]], axis=0))
        _attend_pairs(q_ref[rows, :], kk, vv, slice(None), None, sink_ref, o_ref, b * t)


def _attn_sample(q, k, v, cache_k, cache_v, l, sink, n_prompt, dec_batch, t, nb):
    w = cache_k.shape[2]
    rows = nb * t
    base = n_prompt // rows
    tok = lambda width: pl.BlockSpec((rows, width), lambda i: (base + i, 0))
    cache = pl.BlockSpec((None, nb, w, KV_WIDTH), lambda i: (l, i, 0, 0))
    return pl.pallas_call(
        _attn_sample_kernel,
        grid=(dec_batch // nb,),
        in_specs=[pl.BlockSpec(memory_space=pltpu.SMEM),
                  tok(ATTN_WIDTH), cache, cache, tok(KV_WIDTH), tok(KV_WIDTH)],
        out_specs=pl.BlockSpec((rows, ATTN_WIDTH), lambda i: (i, 0)),
        out_shape=jax.ShapeDtypeStruct((dec_batch * t, ATTN_WIDTH), F32),
        compiler_params=_cparams("parallel"),
        name="attn_sample",
    )(sink, q, cache_k, cache_v, k, v)


SSM_BLOCK = SUBLANES


def _cmul(ar, ai, br, bi):
    return ar * br - ai * bi, ar * bi + ai * br


def _ssm_disc_kernel(lre_ref, lim_ref, dt_ref, bre_ref, bim_ref, cre_ref, cim_ref,
                     we_ref, tv_ref, coef_ref, vt_ref, wb_ref):
    we_ref[...] = jnp.zeros(we_ref.shape, we_ref.dtype)
    vt_ref[...] = jnp.zeros(vt_ref.shape, vt_ref.dtype)
    wb_ref[...] = jnp.zeros(wb_ref.shape, wb_ref.dtype)
    lane = lax.broadcasted_iota(jnp.int32, (SSM_GROUP, LANES), 1)
    half = [lane < SSM_STATE, lane >= SSM_STATE]
    row8 = lax.broadcasted_iota(jnp.int32, (SUBLANES, LANES), 0)
    for q in range(STATE_TILES):
        lre = lre_ref[q]
        lim = lim_ref[q]
        dt = dt_ref[q]
        mag = jnp.exp(lre * dt)
        ang = lim * dt
        lbr = mag * jnp.cos(ang)
        lbi = mag * jnp.sin(ang)
        nr, ni = lbr - 1.0, lbi
        den = lre * lre + lim * lim
        fr = (nr * lre + ni * lim) / den
        fi = (ni * lre - nr * lim) / den
        bbr, bbi = _cmul(fr, fi, bre_ref[q], bim_ref[q])
        cr, ci = cre_ref[q], cim_ref[q]
        pw = [(jnp.ones_like(lbr), jnp.zeros_like(lbr))]
        for _ in range(SSM_BLOCK):
            pw.append(_cmul(pw[-1][0], pw[-1][1], lbr, lbi))
        re_l = slice(q * 2 * LANES, q * 2 * LANES + LANES)
        im_l = slice(q * 2 * LANES + LANES, (q + 1) * 2 * LANES)
        for h in range(2):
            g = 2 * q + h
            grow = lambda blk: slice(blk * LANES + g * SSM_GROUP, blk * LANES + (g + 1) * SSM_GROUP)
            for j in range(SSM_BLOCK):
                wr, wi = _cmul(pw[SSM_BLOCK - 1 - j][0], pw[SSM_BLOCK - 1 - j][1], bbr, bbi)
                we_ref[grow(j), re_l] = jnp.where(half[h], wr, 0.0).astype(we_ref.dtype)
                we_ref[grow(j), im_l] = jnp.where(half[h], wi, 0.0).astype(we_ref.dtype)
            for d in range(SSM_BLOCK + 1):
                xr, xi = _cmul(cr, ci, pw[d][0], pw[d][1])
                vt_ref[grow(d), re_l] = jnp.where(half[h], xr, 0.0)
                vt_ref[grow(d), im_l] = jnp.where(half[h], -xi, 0.0)
            wb_ref[grow(0), re_l] = jnp.where(half[h], bbr, 0.0)
            wb_ref[grow(0), im_l] = jnp.where(half[h], bbi, 0.0)
        l8 = [pw[SSM_BLOCK]]
        for _ in range(SUBLANES - 1):
            l8.append(_cmul(l8[-1][0], l8[-1][1], pw[SSM_BLOCK][0], pw[SSM_BLOCK][1]))
        co = slice(q * LANES, (q + 1) * LANES)
        for kind, sh in enumerate((1, 2, 4)):
            for a in range(2):
                coef_ref[kind, a, :, co] = jnp.where(row8 >= sh, l8[sh - 1][a], 0.0)
        for a in range(2):
            tab = jnp.zeros((SUBLANES, LANES), F32)
            for k in range(SUBLANES):
                tab = jnp.where(row8 == k, l8[k][a], tab)
            coef_ref[3, a, :, co] = tab
    t0 = lax.dot_general(wb_ref[...], vt_ref[0:BLOCK_LANES, :], (((1,), (1,)), ((), ())),
                         preferred_element_type=F32, precision=lax.Precision.HIGHEST)
    for j in range(SSM_BLOCK):
        if j:
            tv_ref[j * LANES:(j + 1) * LANES, 0:j * LANES] = jnp.zeros((LANES, j * LANES), tv_ref.dtype)
        tv_ref[j * LANES:(j + 1) * LANES, j * LANES:] = t0[:, 0:BLOCK_LANES - j * LANES].astype(tv_ref.dtype)
    tv_ref[BLOCK_LANES:, :] = jnp.transpose(vt_ref[LANES:, :]).astype(tv_ref.dtype)


def _ssm_discretize(lam_re, lam_im, log_dt, b_re, b_im, c_re, c_im):
    g, p = lam_re.shape
    npair = g // 2
    pair = lambda a: a.reshape(npair, 1, 2 * p)
    rows = lambda a: a.reshape(npair, 2, SSM_GROUP, p).transpose(0, 2, 1, 3).reshape(npair, SSM_GROUP, 2 * p)
    dt = jnp.repeat(jnp.exp(log_dt), p).reshape(npair, 1, 2 * p)
    vec = pl.BlockSpec((STATE_TILES, 1, LANES), lambda m: (m, 0, 0))
    mat = pl.BlockSpec((STATE_TILES, SSM_GROUP, LANES), lambda m: (m, 0, 0))
    return pl.pallas_call(
        _ssm_disc_kernel,
        grid=(N_CH_BLOCKS,),
        in_specs=[vec, vec, vec, mat, mat, mat, mat],
        out_specs=[pl.BlockSpec((None, BLOCK_LANES, STATE_LANES), lambda m: (m, 0, 0)),
                   pl.BlockSpec((None, BLOCK_LANES + STATE_LANES, BLOCK_LANES), lambda m: (m, 0, 0)),
                   pl.BlockSpec((None, 4, 2, SUBLANES, STATE_LANES // 2), lambda m: (m, 0, 0, 0, 0))],
        out_shape=[jax.ShapeDtypeStruct((N_CH_BLOCKS, BLOCK_LANES, STATE_LANES), BF16),
                   jax.ShapeDtypeStruct((N_CH_BLOCKS, BLOCK_LANES + STATE_LANES, BLOCK_LANES), BF16),
                   jax.ShapeDtypeStruct((N_CH_BLOCKS, 4, 2, SUBLANES, STATE_LANES // 2), F32)],
        scratch_shapes=[pltpu.VMEM(((SSM_BLOCK + 1) * LANES, STATE_LANES), F32),
                        pltpu.VMEM((LANES, STATE_LANES), F32)],
        compiler_params=_cparams("parallel"),
        name="ssm_discretize",
    )(pair(lam_re), pair(lam_im), dt, rows(b_re.transpose(0, 2, 1)), rows(b_im.transpose(0, 2, 1)),
      rows(c_re), rows(c_im))


GROUPS_PER_CH_BLOCK = LANES // SSM_GROUP
N_CH_BLOCKS = SSM_WIDTH // LANES
STATE_LANES = 2 * GROUPS_PER_CH_BLOCK * SSM_STATE
STATE_TILES = STATE_LANES // (2 * LANES)
BLOCK_LANES = SSM_BLOCK * LANES


def _ssm_kernel(nb, u_ref, s0_ref, we_ref, tv_ref, d_ref, coef_ref, y_ref, fin_ref, st_ref, sprev_ref, ucat_ref):
    t_idx = pl.program_id(2)
    r = st_ref.shape[1] - SUBLANES
    rows = nb * r

    @pl.when(t_idx == 0)
    def _():
        for b in range(nb):
            st_ref[b, 0:SUBLANES, :] = jnp.broadcast_to(s0_ref[b], (SUBLANES, STATE_LANES))

    for j in range(SSM_BLOCK):
        ucat_ref[:, j * LANES:(j + 1) * LANES] = u_ref[pl.ds(j, rows, stride=SSM_BLOCK), :].astype(BF16)
    e = jnp.dot(ucat_ref[...], we_ref[...], preferred_element_type=F32)
    st_ref[:, SUBLANES:, :] = e.reshape(nb, r, STATE_LANES)

    first_row = lax.broadcasted_iota(jnp.int32, (SUBLANES, LANES), 0) == 0

    def group(rg, carry):
        r0 = pl.multiple_of(SUBLANES + rg * SUBLANES, SUBLANES)
        rp = pl.multiple_of(rg * SUBLANES, SUBLANES)
        for b in range(nb):
            for q in range(STATE_TILES):
                re_sl = pl.ds(q * 2 * LANES, LANES)
                im_sl = pl.ds(q * 2 * LANES + LANES, LANES)
                co = pl.ds(q * LANES, LANES)
                xr = st_ref[b, pl.ds(r0, SUBLANES), re_sl]
                xi = st_ref[b, pl.ds(r0, SUBLANES), im_sl]
                for step, sh in enumerate((1, 2, 4)):
                    ar, ai = _cmul(coef_ref[step, 0, :, co], coef_ref[step, 1, :, co],
                                   pltpu.roll(xr, sh, 0), pltpu.roll(xi, sh, 0))
                    xr = xr + ar
                    xi = xi + ai
                pr = jnp.broadcast_to(st_ref[b, pl.ds(rp, SUBLANES), re_sl][SUBLANES - 1:, :], (SUBLANES, LANES))
                pi = jnp.broadcast_to(st_ref[b, pl.ds(rp, SUBLANES), im_sl][SUBLANES - 1:, :], (SUBLANES, LANES))
                ar, ai = _cmul(coef_ref[3, 0, :, co], coef_ref[3, 1, :, co], pr, pi)
                xr = xr + ar
                xi = xi + ai
                st_ref[b, pl.ds(r0, SUBLANES), re_sl] = xr
                st_ref[b, pl.ds(r0, SUBLANES), im_sl] = xi
                out_rows = pl.ds(pl.multiple_of(b * r + rg * SUBLANES, SUBLANES), SUBLANES)
                sprev_ref[out_rows, re_sl] = jnp.where(first_row, pr, pltpu.roll(xr, 1, 0))
                sprev_ref[out_rows, im_sl] = jnp.where(first_row, pi, pltpu.roll(xi, 1, 0))
        return carry

    lax.fori_loop(0, r // SUBLANES, group, 0)

    lhs = jnp.concatenate([ucat_ref[...], sprev_ref[...].astype(BF16)], axis=1)
    ycat = jnp.dot(lhs, tv_ref[...], preferred_element_type=F32)
    d = d_ref[...]
    for t in range(SSM_BLOCK):
        tok = pl.ds(t, rows, stride=SSM_BLOCK)
        y_ref[tok, :] = ycat[:, t * LANES:(t + 1) * LANES] + d * u_ref[tok, :]

    for b in range(nb):
        tail = st_ref[b, r:r + SUBLANES, :]
        st_ref[b, 0:SUBLANES, :] = tail
        fin_ref[b] = tail[SUBLANES - 1:SUBLANES, :]


def _ssm(u, s0, we, tv, d, l, coef, row_base, n_seq, seq_len, nb, tt):
    rows = nb * tt
    r = tt // SSM_BLOCK
    nt = seq_len // tt
    base = row_base // rows
    return pl.pallas_call(
        functools.partial(_ssm_kernel, nb),
        grid=(N_CH_BLOCKS, n_seq // nb, nt),
        in_specs=[pl.BlockSpec((rows, LANES), lambda m, i, j: (base + i * nt + j, m)),
                  pl.BlockSpec((nb, 1, STATE_LANES), lambda m, i, j: (i, 0, m)),
                  pl.BlockSpec((None, BLOCK_LANES, STATE_LANES), lambda m, i, j: (m, 0, 0)),
                  pl.BlockSpec((None, BLOCK_LANES + STATE_LANES, BLOCK_LANES), lambda m, i, j: (m, 0, 0)),
                  pl.BlockSpec((None, 1, LANES), lambda m, i, j: (l, 0, m)),
                  pl.BlockSpec((None, 4, 2, SUBLANES, STATE_LANES // 2), lambda m, i, j: (m, 0, 0, 0, 0))],
        out_specs=[pl.BlockSpec((rows, LANES), lambda m, i, j: (i * nt + j, m)),
                   pl.BlockSpec((nb, 1, STATE_LANES), lambda m, i, j: (i, 0, m))],
        out_shape=[jax.ShapeDtypeStruct((n_seq * seq_len, SSM_WIDTH), F32),
                   jax.ShapeDtypeStruct((n_seq, 1, N_CH_BLOCKS * STATE_LANES), F32)],
        scratch_shapes=[pltpu.VMEM((nb, SUBLANES + r, STATE_LANES), F32),
                        pltpu.VMEM((nb * r, STATE_LANES), F32),
                        pltpu.VMEM((nb * r, BLOCK_LANES), BF16)],
        compiler_params=_cparams("parallel", "parallel", "arbitrary"),
        name="ssm_scan",
    )(u, s0, we, tv, d, coef)


def _state_to_tiles(s):
    lead = s.shape[:-3]
    t = s.reshape(lead + (N_CH_BLOCKS, STATE_TILES, 2, SSM_STATE, 2))
    t = jnp.moveaxis(t, -1, -3)
    return t.reshape(lead + (1, N_CH_BLOCKS * STATE_LANES))


def _tiles_to_state(f):
    b = f.shape[0]
    t = f.reshape(b, N_CH_BLOCKS, STATE_TILES, 2, 2, SSM_STATE)
    t = jnp.moveaxis(t, 3, -1)
    return t.reshape(b, N_SSM_GROUPS, SSM_STATE, 2)


def _merge_kernel(n_prompt_tiles, xp_ref, xs_ref, ap_ref, as_ref, yp_ref, ys_ref, wglu_ref, bglu_ref, ga_ref, gs_ref,
                  wout_ref, o_ref):
    is_prompt = pl.program_id(0) < n_prompt_tiles
    attn = jnp.where(is_prompt, ap_ref[...], as_ref[...])
    g = jax.nn.gelu(jnp.where(is_prompt, yp_ref[...], ys_ref[...]))
    glu = g * jax.nn.sigmoid(jnp.dot(g.astype(BF16), wglu_ref[...], preferred_element_type=F32) + bglu_ref[...])
    na = _rms(attn, ga_ref[...]).astype(BF16)
    ns = _rms(glu, gs_ref[...]).astype(BF16)
    o = jnp.dot(na, wout_ref[0:ATTN_WIDTH, :], preferred_element_type=F32)
    o = o + jnp.dot(ns, wout_ref[ATTN_WIDTH:, :], preferred_element_type=F32)
    o_ref[...] = jnp.where(is_prompt, xp_ref[...], xs_ref[...]) + o


def _merge(xp, xs, n, attn_p, attn_s, y_p, y_s, wglu, bglu, ga, gs, wout, l, tm):
    npt = attn_p.shape[0] // tm
    nst = attn_s.shape[0] // tm
    row = lambda w: pl.BlockSpec((tm, w), lambda i: (i, 0))
    prow = lambda w: pl.BlockSpec((tm, w), lambda i: (jnp.minimum(i, npt - 1), 0))
    srow = lambda w: pl.BlockSpec((tm, w), lambda i: (jnp.clip(i - npt, 0, nst - 1), 0))
    return pl.pallas_call(
        functools.partial(_merge_kernel, npt),
        grid=(n // tm,),
        in_specs=[*_two_source_specs(xp, xs, attn_p.shape[0], tm, D_MODEL),
                  prow(ATTN_WIDTH), srow(ATTN_WIDTH), prow(SSM_WIDTH), srow(SSM_WIDTH),
                  _layer_spec(wglu, l), _layer_spec(bglu, l), _layer_spec(ga, l), _layer_spec(gs, l),
                  _layer_spec(wout, l)],
        out_specs=row(D_MODEL),
        out_shape=jax.ShapeDtypeStruct((n, D_MODEL), F32),
        compiler_params=_cparams("parallel"),
        name="merge_heads",
    )(xp, xs, attn_p, attn_s, y_p, y_s, wglu, bglu, ga, gs, wout)


def _mem_kv_kernel(m_ref, g_ref, wk_ref, wv_ref, k_ref, v_ref):
    mn = _rms(m_ref[...], g_ref[...]).astype(BF16)
    k_ref[...] = jnp.dot(mn, wk_ref[...], preferred_element_type=F32)
    v_ref[...] = jnp.dot(mn, wv_ref[...], preferred_element_type=F32)


def _mem_kv(mem, g, wk, wv, l, tm):
    n = mem.shape[0]
    row = pl.BlockSpec((tm, D_MODEL), lambda i: (i, 0))
    return pl.pallas_call(
        _mem_kv_kernel,
        grid=(n // tm,),
        in_specs=[row, _layer_spec(g, l), _layer_spec(wk, l), _layer_spec(wv, l)],
        out_specs=[row, row],
        out_shape=[jax.ShapeDtypeStruct((n, D_MODEL), F32)] * 2,
        compiler_params=_cparams("parallel"),
        name="mem_kv",
    )(mem, g, wk, wv)


def _xattn_kernel(nb, x_ref, g_ref, wq_ref, wo_ref, mk_ref, mv_ref, o_ref, att_ref):
    t = x_ref.shape[0] // nb
    x = x_ref[...]
    hn = _rms(x, g_ref[...]).astype(BF16)
    q = jnp.dot(hn, wq_ref[...], preferred_element_type=F32) * (1.0 / math.sqrt(XHEAD_DIM))
    q = q.astype(BF16)
    for b in range(nb):
        mk = mk_ref[b * N_MEM:(b + 1) * N_MEM, :].astype(BF16)
        mv = mv_ref[b * N_MEM:(b + 1) * N_MEM, :].astype(BF16)
        for h in range(N_XHEADS):
            sl = slice(h * XHEAD_DIM, (h + 1) * XHEAD_DIM)
            s = lax.dot_general(q[b * t:(b + 1) * t, sl], mk[:, sl], (((1,), (1,)), ((), ())),
                                preferred_element_type=F32)
            m = jnp.max(s, axis=-1, keepdims=True)
            e = jnp.exp(s - m)
            p = e / jnp.sum(e, axis=-1, keepdims=True)
            att_ref[b * t:(b + 1) * t, sl] = jnp.dot(p.astype(BF16), mv[:, sl], preferred_element_type=F32)
    o = jnp.dot(att_ref[...].astype(BF16), wo_ref[...], preferred_element_type=F32)
    o_ref[...] = x + o


def _xattn(x, g, wq, wo, l, mk, mv, mem_spec, row_base, n_rows, nb, tm):
    base = row_base // tm
    xspec = pl.BlockSpec((tm, D_MODEL), lambda i: (base + i, 0))
    return pl.pallas_call(
        functools.partial(_xattn_kernel, nb),
        grid=(n_rows // tm,),
        in_specs=[xspec, _layer_spec(g, l), _layer_spec(wq, l), _layer_spec(wo, l), mem_spec, mem_spec],
        out_specs=xspec,
        out_shape=jax.ShapeDtypeStruct(x.shape, F32),
        scratch_shapes=[pltpu.VMEM((tm, D_MODEL), F32)],
        input_output_aliases={0: 0},
        compiler_params=_cparams("parallel"),
        name="cross_attn",
    )(x, g, wq, wo, mk, mv)


ROUTER_LANES = LANES
EXPERT_LANE0 = N_EXPERT_GROUPS
EXPERTS_PER_STEP = 4


def _dot_f32_3pass(x, w):
    xh = x.astype(BF16)
    xl = (x - xh.astype(F32)).astype(BF16)
    wh = w.astype(BF16)
    wl = (w - wh.astype(F32)).astype(BF16)
    dot = lambda a, b: jnp.dot(a, b, preferred_element_type=F32)
    return dot(xh, wh) + (dot(xl, wh) + dot(xh, wl))


def _route(logits):
    lane_i = lax.broadcasted_iota(jnp.int32, logits.shape, 1)
    lane = lane_i.astype(F32)
    neg = jnp.float32(-jnp.inf)
    is_g = lane_i < N_EXPERT_GROUPS
    gl = jnp.where(is_g, logits, neg)
    gmax = jnp.max(gl, axis=-1, keepdims=True)
    gidx = jnp.min(jnp.where(gl == gmax, lane, float(ROUTER_LANES)), axis=-1, keepdims=True)
    g_w = 1.0 / jnp.sum(jnp.where(is_g, jnp.exp(gl - gmax), 0.0), axis=-1, keepdims=True)
    first = EXPERT_LANE0 + gidx * EXPERTS_PER_GROUP
    sel = (lane >= first) & (lane < first + EXPERTS_PER_GROUP)
    el = jnp.where(sel, logits, neg)
    m1 = jnp.max(el, axis=-1, keepdims=True)
    i1 = jnp.min(jnp.where(el == m1, lane, float(ROUTER_LANES)), axis=-1, keepdims=True)
    el2 = jnp.where(lane == i1, neg, el)
    m2 = jnp.max(el2, axis=-1, keepdims=True)
    i2 = jnp.min(jnp.where(el2 == m2, lane, float(ROUTER_LANES)), axis=-1, keepdims=True)
    r = jnp.exp(m2 - m1)
    w1 = g_w / (1.0 + r)
    w2 = w1 * r
    return jnp.where(lane == i1, w1, jnp.where(lane == i2, w2, 0.0))


def _moe_kernel(x_ref, g_ref, wr_ref, br_ref, wgu_ref, wd_ref, o_ref, xn_ref, gate_ref):
    e_step = pl.program_id(1)

    @pl.when(e_step == 0)
    def _():
        x = x_ref[...]
        xn = _rms(x, g_ref[...])
        xn_ref[...] = xn.astype(BF16)
        logits = _dot_f32_3pass(xn, wr_ref[...]) + br_ref[...]
        gate_ref[...] = _route(logits)
        o_ref[...] = x

    xn = xn_ref[...]
    gates = gate_ref[...]
    lane = lax.broadcasted_iota(jnp.int32, gates.shape, 1)
    hids = []
    for j in range(EXPERTS_PER_STEP):
        h = jnp.dot(xn, wgu_ref[j], preferred_element_type=F32)
        ge = jnp.sum(jnp.where(lane == EXPERT_LANE0 + e_step * EXPERTS_PER_STEP + j, gates, 0.0),
                     axis=-1, keepdims=True)
        hid = jax.nn.silu(h[:, :EXPERT_FF]) * h[:, EXPERT_FF:] * ge
        hids.append(hid.astype(BF16))
    hid = jnp.concatenate(hids, axis=1)
    o_ref[...] += jnp.dot(hid, wd_ref[...], preferred_element_type=F32)


def _moe(x, g, wr, br, wgu, wd, l, tm):
    n = x.shape[0]
    es = EXPERTS_PER_STEP
    row = pl.BlockSpec((tm, D_MODEL), lambda i, e: (i, 0))
    return pl.pallas_call(
        _moe_kernel,
        grid=(n // tm, N_EXPERTS // es),
        in_specs=[row, _layer_spec(g, l), _layer_spec(wr, l), _layer_spec(br, l),
                  pl.BlockSpec((None, es, D_MODEL, 2 * EXPERT_FF), lambda i, e: (l, e, 0, 0)),
                  pl.BlockSpec((None, es * EXPERT_FF, D_MODEL), lambda i, e: (l, e, 0))],
        out_specs=row,
        out_shape=jax.ShapeDtypeStruct((n, D_MODEL), F32),
        scratch_shapes=[pltpu.VMEM((tm, D_MODEL), BF16), pltpu.VMEM((tm, ROUTER_LANES), F32)],
        compiler_params=_cparams("parallel", "arbitrary"),
        name="hier_moe",
    )(x, g, wr, br, wgu, wd)


def _final_norm_kernel(x_ref, g_ref, o_ref):
    o_ref[...] = _rms(x_ref[...], g_ref[...])


def _final_norm(x, g, row_base, n_rows, tm):
    base = row_base // tm
    return pl.pallas_call(
        _final_norm_kernel,
        grid=(n_rows // tm,),
        in_specs=[pl.BlockSpec((tm, D_MODEL), lambda i: (base + i, 0)),
                  pl.BlockSpec((1, D_MODEL), lambda i: (0, 0))],
        out_specs=pl.BlockSpec((tm, D_MODEL), lambda i: (i, 0)),
        out_shape=jax.ShapeDtypeStruct((n_rows, D_MODEL), F32),
        compiler_params=_cparams("parallel"),
        name="final_norm",
    )(x, g)


def _rope_tables(seq, t_len, tm):
    half = HEAD_DIM // 2
    inv = ROPE_THETA ** (-jnp.arange(half, dtype=F32) / half)
    pos_s = PAST_LEN + jnp.arange(t_len)
    pos = jnp.concatenate([jnp.arange(seq), jnp.tile(pos_s, tm // t_len)]).astype(F32)
    ang = pos[:, None] * inv[None, :]
    cos = jnp.tile(jnp.cos(ang), (1, LANES // half))
    sign = jnp.where((jnp.arange(LANES) % HEAD_DIM) < half, -1.0, 1.0).astype(F32)
    sin = jnp.tile(jnp.sin(ang), (1, LANES // half)) * sign[None, :]
    return cos, sin


def kernel(x_prompt, x_sample, cache_win_k, cache_win_v, state_ssm, cache_mem_k, cache_mem_v, mem_prompt, w_in, attn_sink, lam_re, lam_im, log_dt, ssm_b_re, ssm_b_im, ssm_c_re, ssm_c_im, ssm_d, w_glu, b_glu, g_attn_out, g_ssm_out, w_out, g_mix, g_xattn, g_mem, wq_x, wk_x, wv_x, wo_x, g_ffn, w_group, b_group, w_router, b_router, w_gate, w_up, w_down, g_final):
    batch, seq, _ = x_prompt.shape
    dec_batch, t_len, _ = x_sample.shape
    depth = w_in.shape[0]
    win_rows = cache_win_k.shape[2]
    n_p = batch * seq
    n_s = dec_batch * t_len
    tm = 512
    tm_wide = 1024 if (n_p + n_s) % 1024 == 0 else 512
    sample_nb = tm // t_len

    n = n_p + n_s
    xp = x_prompt.reshape(n_p, D_MODEL)
    xs = x_sample.reshape(n_s, D_MODEL)
    cos_tab, sin_tab = _rope_tables(seq, t_len, tm)
    mem_flat = mem_prompt.reshape(batch * N_MEM, D_MODEL)
    zero_state = jnp.zeros((batch, 1, 2 * N_STATE), F32)
    vec = lambda a: a.reshape(depth, 1, a.shape[-1])

    w_in_b, w_glu_b, w_out_b = w_in.astype(BF16), w_glu.astype(BF16), w_out.astype(BF16)
    wq_b, wk_b, wv_b, wo_b = (w.astype(BF16) for w in (wq_x, wk_x, wv_x, wo_x))
    wgu_b = jnp.concatenate([w_gate, w_up], axis=-1).astype(BF16)
    wd_b = w_down.reshape(depth, N_EXPERTS * EXPERT_FF, D_MODEL).astype(BF16)
    wr = jnp.concatenate([w_group, w_router.transpose(0, 2, 1, 3).reshape(depth, D_MODEL, N_EXPERTS)], axis=-1)
    wr = jnp.pad(wr, ((0, 0), (0, 0), (0, ROUTER_LANES - wr.shape[-1])))
    br = jnp.concatenate([b_group, b_router.reshape(depth, N_EXPERTS)], axis=-1)
    br = jnp.pad(br, ((0, 0), (0, ROUTER_LANES - br.shape[-1]))).reshape(depth, 1, ROUTER_LANES)
    g_mix_r, g_xattn_r, g_mem_r, g_ffn_r = vec(g_mix), vec(g_xattn), vec(g_mem), vec(g_ffn)
    g_a_r, g_s_r, b_glu_r, ssm_d_r = vec(g_attn_out), vec(g_ssm_out), vec(b_glu), vec(ssm_d)
    cache_k = cache_win_k.reshape(depth, dec_batch, win_rows, KV_WIDTH)
    cache_v = cache_win_v.reshape(depth, dec_batch, win_rows, KV_WIDTH)
    cmem_k = cache_mem_k.reshape(depth, dec_batch * N_MEM, D_MODEL)
    cmem_v = cache_mem_v.reshape(depth, dec_batch * N_MEM, D_MODEL)
    state_in = _state_to_tiles(state_ssm)

    outs = {k: [] for k in ("wk_p", "wv_p", "ssm_p", "mk_p", "mv_p", "wk_s", "wv_s", "ssm_s")}
    for l in range(depth):
        q, k, v, u = _in_proj(xp, xs, n, g_mix_r, w_in_b, l, cos_tab, sin_tab, n_p, seq, tm)
        attn_p = _attn_prompt(q, k, v, attn_sink[l], batch, seq)
        attn_s = _attn_sample(q, k, v, cache_k, cache_v, l, attn_sink[l], n_p, dec_batch, t_len, 4)
        tail = lambda a: jnp.stack([a[(b + 1) * seq - WINDOW:(b + 1) * seq] for b in range(batch)])
        outs["wk_p"].append(tail(k).reshape(batch, WINDOW, N_KV_HEADS, HEAD_DIM))
        outs["wv_p"].append(tail(v).reshape(batch, WINDOW, N_KV_HEADS, HEAD_DIM))
        ks = k[n_p:].reshape(dec_batch, t_len, KV_WIDTH)
        vs = v[n_p:].reshape(dec_batch, t_len, KV_WIDTH)
        k_all = jnp.concatenate([cache_k[l], ks], axis=1)[:, -win_rows:]
        v_all = jnp.concatenate([cache_v[l], vs], axis=1)[:, -win_rows:]
        outs["wk_s"].append(k_all.reshape(dec_batch, win_rows, N_KV_HEADS, HEAD_DIM))
        outs["wv_s"].append(v_all.reshape(dec_batch, win_rows, N_KV_HEADS, HEAD_DIM))

        we, tv, coef = _ssm_discretize(lam_re[l], lam_im[l], log_dt[l], ssm_b_re[l], ssm_b_im[l],
                                       ssm_c_re[l], ssm_c_im[l])
        y_p, fin_p = _ssm(u, zero_state, we, tv, ssm_d_r, l, coef, 0, batch, seq, 1, seq)
        y_s, fin_s = _ssm(u, state_in[l], we, tv, ssm_d_r, l, coef, n_p, dec_batch, t_len, dec_batch, t_len)
        outs["ssm_p"].append(_tiles_to_state(fin_p))
        outs["ssm_s"].append(_tiles_to_state(fin_s))
        x = _merge(xp, xs, n, attn_p, attn_s, y_p, y_s, w_glu_b, b_glu_r, g_a_r, g_s_r, w_out_b, l, tm)

        mk_p, mv_p = _mem_kv(mem_flat, g_mem_r, wk_b, wv_b, l, 512)
        outs["mk_p"].append(mk_p.reshape(batch, N_MEM, N_XHEADS, XHEAD_DIM))
        outs["mv_p"].append(mv_p.reshape(batch, N_MEM, N_XHEADS, XHEAD_DIM))
        tiles_per_seq = seq // tm
        x = _xattn(x, g_xattn_r, wq_b, wo_b, l, mk_p, mv_p,
                   pl.BlockSpec((N_MEM, D_MODEL), lambda i: (i // tiles_per_seq, 0)), 0, n_p, 1, tm)
        x = _xattn(x, g_xattn_r, wq_b, wo_b, l, cmem_k, cmem_v,
                   pl.BlockSpec((None, sample_nb * N_MEM, D_MODEL), lambda i: (l, i, 0)), n_p, n_s, sample_nb, tm)

        x = _moe(x, g_ffn_r, wr, br, wgu_b, wd_b, l, tm_wide)
        xp = xs = x

    gf = g_final.reshape(1, D_MODEL)
    y_p = _final_norm(x, gf, 0, n_p, tm)
    y_s = _final_norm(x, gf, n_p, n_s, tm)
    st = lambda name: jnp.stack(outs[name], axis=0)
    return (y_p.reshape(batch, seq, D_MODEL), y_s.reshape(dec_batch, t_len, D_MODEL),
            st("wk_p"), st("wv_p"), st("ssm_p"), st("mk_p"), st("mv_p"), st("wk_s"), st("wv_s"), st("ssm_s"))
```

```python
import functools
import math

import jax
import jax.numpy as jnp
from jax import lax
from jax.experimental import pallas as pl
from jax.experimental.pallas import tpu as pltpu

F32 = jnp.float32
BF16 = jnp.bfloat16

D_MODEL = 1024
CHUNK = 64
EPS = 1e-6
NEG_INF = -1e30
N_HEADS = 8
N_KV_HEADS = 2
HEAD_DIM = 64
ATTN_WIDTH = N_HEADS * HEAD_DIM
KV_WIDTH = N_KV_HEADS * HEAD_DIM
WINDOW = 128
ROPE_THETA = 10000.0
SSM_WIDTH = D_MODEL - ATTN_WIDTH
SSM_GROUP = 16
N_SSM_GROUPS = SSM_WIDTH // SSM_GROUP
SSM_STATE = 64
N_STATE = N_SSM_GROUPS * SSM_STATE
IN_WIDTH = ATTN_WIDTH + 2 * KV_WIDTH + SSM_WIDTH
N_MEM = 256
N_XHEADS = 4
XHEAD_DIM = D_MODEL // N_XHEADS
N_EXPERT_GROUPS = 4
EXPERTS_PER_GROUP = 8
N_EXPERTS = N_EXPERT_GROUPS * EXPERTS_PER_GROUP
EXPERT_FF = 128
PAST_LEN = 4096

LANES = 128
SUBLANES = 8
VMEM_LIMIT = 56 * 1024 * 1024


def _cparams(*sem):
    return pltpu.CompilerParams(dimension_semantics=sem, vmem_limit_bytes=VMEM_LIMIT)


def _rms(x, g):
    return x * lax.rsqrt(jnp.mean(x * x, axis=-1, keepdims=True) + EPS) * g


def _layer_spec(arr, l):
    shape = arr.shape[1:]
    zeros = (0,) * len(shape)
    return pl.BlockSpec((None,) + shape, lambda *_: (l,) + zeros, pipeline_mode=pl.Buffered(1))


def _rope_pairs(t, cos, sin_signed, first_half):
    swapped = jnp.where(first_half, pltpu.roll(t, LANES - HEAD_DIM // 2, 1), pltpu.roll(t, HEAD_DIM // 2, 1))
    return t * cos + swapped * sin_signed


def _two_source_specs(xp, xs, n_prompt, tm, width):
    npt = n_prompt // tm
    s_off = 0 if xs is xp else npt
    s_last = xs.shape[0] // tm - 1
    pspec = pl.BlockSpec((tm, width), lambda i, *_: (jnp.minimum(i, npt - 1), 0))
    sspec = pl.BlockSpec((tm, width), lambda i, *_: (jnp.clip(i - s_off, npt - s_off, s_last), 0))
    return pspec, sspec


def _in_proj_kernel(n_prompt_tiles, xp_ref, xs_ref, g_ref, w_ref, cos_ref, sin_ref, q_ref, k_ref, v_ref, u_ref):
    x = jnp.where(pl.program_id(0) < n_prompt_tiles, xp_ref[...], xs_ref[...])
    xn = _rms(x, g_ref[...])
    z = jnp.dot(xn.astype(BF16), w_ref[...], preferred_element_type=F32)
    cos = cos_ref[...]
    sin = sin_ref[...]
    lane = lax.broadcasted_iota(jnp.int32, cos.shape, 1)
    first_half = (lane % HEAD_DIM) < (HEAD_DIM // 2)
    scale = 1.0 / math.sqrt(HEAD_DIM)
    for j in range(ATTN_WIDTH // LANES):
        t = z[:, j * LANES:(j + 1) * LANES]
        q_ref[:, j * LANES:(j + 1) * LANES] = (_rope_pairs(t, cos, sin, first_half) * scale).astype(BF16)
    k_ref[...] = _rope_pairs(z[:, ATTN_WIDTH:ATTN_WIDTH + KV_WIDTH], cos, sin, first_half)
    v_ref[...] = z[:, ATTN_WIDTH + KV_WIDTH:ATTN_WIDTH + 2 * KV_WIDTH]
    u_ref[...] = z[:, ATTN_WIDTH + 2 * KV_WIDTH:]


def _in_proj(xp, xs, n, g, w_bf16, l, cos_tab, sin_tab, n_prompt, seq, tm):
    n_prompt_tiles = n_prompt // tm
    tiles_per_seq = seq // tm

    def tab_map(i):
        return (jnp.where(i < n_prompt_tiles, i % tiles_per_seq, tiles_per_seq), 0)

    row = lambda w: pl.BlockSpec((tm, w), lambda i: (i, 0))
    return pl.pallas_call(
        functools.partial(_in_proj_kernel, n_prompt_tiles),
        grid=(n // tm,),
        in_specs=[*_two_source_specs(xp, xs, n_prompt, tm, D_MODEL), _layer_spec(g, l), _layer_spec(w_bf16, l),
                  pl.BlockSpec((tm, LANES), tab_map),
                  pl.BlockSpec((tm, LANES), tab_map)],
        out_specs=[row(ATTN_WIDTH), row(KV_WIDTH), row(KV_WIDTH), row(SSM_WIDTH)],
        out_shape=[jax.ShapeDtypeStruct((n, ATTN_WIDTH), BF16),
                   jax.ShapeDtypeStruct((n, KV_WIDTH), F32),
                   jax.ShapeDtypeStruct((n, KV_WIDTH), F32),
                   jax.ShapeDtypeStruct((n, SSM_WIDTH), F32)],
        compiler_params=_cparams("parallel"),
        name="in_proj",
    )(xp, xs, g, w_bf16, cos_tab, sin_tab)


def _kv_pairs(keys, vals):
    lane = lax.broadcasted_iota(jnp.int32, keys.shape, 1)
    low = lane < HEAD_DIM
    k_sw = pltpu.roll(keys, HEAD_DIM, 1)
    v_sw = pltpu.roll(vals, HEAD_DIM, 1)
    kk = [jnp.where(low, keys, k_sw).astype(BF16), jnp.where(low, k_sw, keys).astype(BF16)]
    vv = [jnp.where(low, vals, v_sw).astype(BF16), jnp.where(low, v_sw, vals).astype(BF16)]
    return kk, vv


def _attend_pairs(q, kk, vv, key_rows, mask_add, sink_ref, o_ref, row0):
    tq = q.shape[0]
    qlane = lax.broadcasted_iota(jnp.int32, (tq, LANES), 1)
    qlow = qlane < HEAD_DIM
    row_top = lax.broadcasted_iota(jnp.int32, (2 * tq, 1), 0) < tq
    zero = jnp.zeros((), BF16)
    for pair in range(N_HEADS // 2):
        kv = pair // (N_HEADS // N_KV_HEADS // 2)
        qp = q[:, pair * LANES:(pair + 1) * LANES]
        qs = jnp.concatenate([jnp.where(qlow, qp, zero), jnp.where(qlow, zero, qp)], axis=0)
        s = lax.dot_general(qs, kk[kv][key_rows, :], (((1,), (1,)), ((), ())), preferred_element_type=F32)
        if mask_add is not None:
            s = s + mask_add
        sink = jnp.where(row_top, sink_ref[2 * pair], sink_ref[2 * pair + 1])
        m = jnp.maximum(jnp.max(s, axis=-1, keepdims=True), sink)
        e = jnp.exp(s - m)
        p = e / (jnp.sum(e, axis=-1, keepdims=True) + jnp.exp(sink - m))
        o = jnp.dot(p.astype(BF16), vv[kv][key_rows, :], preferred_element_type=F32)
        o_ref[row0:row0 + tq, pair * LANES:(pair + 1) * LANES] = jnp.where(qlow, o[:tq], o[tq:])


ATTN_SUB = WINDOW
ATTN_TILE = 2 * ATTN_SUB


def _attn_prompt_kernel(sink_ref, ma_ref, mb_ref, q_ref, kp_ref, kc_ref, vp_ref, vc_ref, o_ref):
    kk, vv = _kv_pairs(jnp.concatenate([kp_ref[...], kc_ref[...]], axis=0),
                       jnp.concatenate([vp_ref[...], vc_ref[...]], axis=0))
    for s, m_ref in enumerate((ma_ref, mb_ref)):
        rows = slice(s * ATTN_SUB, s * ATTN_SUB + 2 * WINDOW)
        _attend_pairs(q_ref[s * ATTN_SUB:(s + 1) * ATTN_SUB, :], kk, vv, rows, m_ref[...], sink_ref, o_ref, s * ATTN_SUB)


def _band_masks():
    r = (jnp.arange(2 * ATTN_SUB) % ATTN_SUB)[:, None] // CHUNK
    c = jnp.arange(2 * WINDOW)[None, :]
    band = (c // CHUNK >= r) & (c // CHUNK <= r + WINDOW // CHUNK)
    masks = jnp.stack([band, band & (c >= WINDOW)])
    return jnp.where(masks, 0.0, NEG_INF).astype(F32)


def _attn_prompt(q, k, v, sink, batch, seq):
    nt = seq // ATTN_TILE
    per_seq = seq // ATTN_SUB
    cur = lambda b, i: (b * nt + i, 0)
    prev = lambda b, i: (b * per_seq + jnp.maximum(2 * i - 1, 0), 0)
    masks = _band_masks()
    return pl.pallas_call(
        _attn_prompt_kernel,
        grid=(batch, nt),
        in_specs=[pl.BlockSpec(memory_space=pltpu.SMEM),
                  pl.BlockSpec((None, 2 * ATTN_SUB, 2 * WINDOW), lambda b, i: (jnp.where(i == 0, 1, 0), 0, 0)),
                  pl.BlockSpec((None, 2 * ATTN_SUB, 2 * WINDOW), lambda b, i: (0, 0, 0)),
                  pl.BlockSpec((ATTN_TILE, ATTN_WIDTH), cur),
                  pl.BlockSpec((ATTN_SUB, KV_WIDTH), prev),
                  pl.BlockSpec((ATTN_TILE, KV_WIDTH), cur),
                  pl.BlockSpec((ATTN_SUB, KV_WIDTH), prev),
                  pl.BlockSpec((ATTN_TILE, KV_WIDTH), cur)],
        out_specs=pl.BlockSpec((ATTN_TILE, ATTN_WIDTH), cur),
        out_shape=jax.ShapeDtypeStruct((batch * seq, ATTN_WIDTH), F32),
        compiler_params=_cparams("parallel", "parallel"),
        name="attn_prompt",
    )(sink, masks, masks, q, k, k, v, v)


def _attn_sample_kernel(sink_ref, q_ref, ck_ref, cv_ref, k_ref, v_ref, o_ref):
    nb = ck_ref.shape[0]
    t = q_ref.shape[0] // nb
    for b in range(nb):
        rows = slice(b * t, (b + 1) * t)
        kk, vv = _kv_pairs(jnp.concatenate([ck_ref[b], k_ref[rows, :]], axis=0),
                           jnp.concatenate([cv_ref[b], v_ref[rows, :]], axis=0))
        _attend_pairs(q_ref[rows, :], kk, vv, slice(None), None, sink_ref, o_ref, b * t)


def _attn_sample(q, k, v, cache_k, cache_v, l, sink, n_prompt, dec_batch, t, nb):
    w = cache_k.shape[2]
    rows = nb * t
    base = n_prompt // rows
    tok = lambda width: pl.BlockSpec((rows, width), lambda i: (base + i, 0))
    cache = pl.BlockSpec((None, nb, w, KV_WIDTH), lambda i: (l, i, 0, 0))
    return pl.pallas_call(
        _attn_sample_kernel,
        grid=(dec_batch // nb,),
        in_specs=[pl.BlockSpec(memory_space=pltpu.SMEM),
                  tok(ATTN_WIDTH), cache, cache, tok(KV_WIDTH), tok(KV_WIDTH)],
        out_specs=pl.BlockSpec((rows, ATTN_WIDTH), lambda i: (i, 0)),
        out_shape=jax.ShapeDtypeStruct((dec_batch * t, ATTN_WIDTH), F32),
        compiler_params=_cparams("parallel"),
        name="attn_sample",
    )(sink, q, cache_k, cache_v, k, v)


SSM_BLOCK = SUBLANES


def _cmul(ar, ai, br, bi):
    return ar * br - ai * bi, ar * bi + ai * br


def _ssm_disc_kernel(lre_ref, lim_ref, dt_ref, bre_ref, bim_ref, cre_ref, cim_ref,
                     we_ref, tv_ref, coef_ref, vt_ref, wb_ref):
    we_ref[...] = jnp.zeros(we_ref.shape, we_ref.dtype)
    vt_ref[...] = jnp.zeros(vt_ref.shape, vt_ref.dtype)
    wb_ref[...] = jnp.zeros(wb_ref.shape, wb_ref.dtype)
    lane = lax.broadcasted_iota(jnp.int32, (SSM_GROUP, LANES), 1)
    half = [lane < SSM_STATE, lane >= SSM_STATE]
    row8 = lax.broadcasted_iota(jnp.int32, (SUBLANES, LANES), 0)
    for q in range(STATE_TILES):
        lre = lre_ref[q]
        lim = lim_ref[q]
        dt = dt_ref[q]
        mag = jnp.exp(lre * dt)
        ang = lim * dt
        lbr = mag * jnp.cos(ang)
        lbi = mag * jnp.sin(ang)
        nr, ni = lbr - 1.0, lbi
        den = lre * lre + lim * lim
        fr = (nr * lre + ni * lim) / den
        fi = (ni * lre - nr * lim) / den
        bbr, bbi = _cmul(fr, fi, bre_ref[q], bim_ref[q])
        cr, ci = cre_ref[q], cim_ref[q]
        pw = [(jnp.ones_like(lbr), jnp.zeros_like(lbr))]
        for _ in range(SSM_BLOCK):
            pw.append(_cmul(pw[-1][0], pw[-1][1], lbr, lbi))
        re_l = slice(q * 2 * LANES, q * 2 * LANES + LANES)
        im_l = slice(q * 2 * LANES + LANES, (q + 1) * 2 * LANES)
        for h in range(2):
            g = 2 * q + h
            grow = lambda blk: slice(blk * LANES + g * SSM_GROUP, blk * LANES + (g + 1) * SSM_GROUP)
            for j in range(SSM_BLOCK):
                wr, wi = _cmul(pw[SSM_BLOCK - 1 - j][0], pw[SSM_BLOCK - 1 - j][1], bbr, bbi)
                we_ref[grow(j), re_l] = jnp.where(half[h], wr, 0.0).astype(we_ref.dtype)
                we_ref[grow(j), im_l] = jnp.where(half[h], wi, 0.0).astype(we_ref.dtype)
            for d in range(SSM_BLOCK + 1):
                xr, xi = _cmul(cr, ci, pw[d][0], pw[d][1])
                vt_ref[grow(d), re_l] = jnp.where(half[h], xr, 0.0)
                vt_ref[grow(d), im_l] = jnp.where(half[h], -xi, 0.0)
            wb_ref[grow(0), re_l] = jnp.where(half[h], bbr, 0.0)
            wb_ref[grow(0), im_l] = jnp.where(half[h], bbi, 0.0)
        l8 = [pw[SSM_BLOCK]]
        for _ in range(SUBLANES - 1):
            l8.append(_cmul(l8[-1][0], l8[-1][1], pw[SSM_BLOCK][0], pw[SSM_BLOCK][1]))
        co = slice(q * LANES, (q + 1) * LANES)
        for kind, sh in enumerate((1, 2, 4)):
            for a in range(2):
                coef_ref[kind, a, :, co] = jnp.where(row8 >= sh, l8[sh - 1][a], 0.0)
        for a in range(2):
            tab = jnp.zeros((SUBLANES, LANES), F32)
            for k in range(SUBLANES):
                tab = jnp.where(row8 == k, l8[k][a], tab)
            coef_ref[3, a, :, co] = tab
    t0 = lax.dot_general(wb_ref[...], vt_ref[0:BLOCK_LANES, :], (((1,), (1,)), ((), ())),
                         preferred_element_type=F32, precision=lax.Precision.HIGHEST)
    for j in range(SSM_BLOCK):
        if j:
            tv_ref[j * LANES:(j + 1) * LANES, 0:j * LANES] = jnp.zeros((LANES, j * LANES), tv_ref.dtype)
        tv_ref[j * LANES:(j + 1) * LANES, j * LANES:] = t0[:, 0:BLOCK_LANES - j * LANES].astype(tv_ref.dtype)
    tv_ref[BLOCK_LANES:, :] = jnp.transpose(vt_ref[LANES:, :]).astype(tv_ref.dtype)


def _ssm_discretize(lam_re, lam_im, log_dt, b_re, b_im, c_re, c_im):
    g, p = lam_re.shape
    npair = g // 2
    pair = lambda a: a.reshape(npair, 1, 2 * p)
    rows = lambda a: a.reshape(npair, 2, SSM_GROUP, p).transpose(0, 2, 1, 3).reshape(npair, SSM_GROUP, 2 * p)
    dt = jnp.repeat(jnp.exp(log_dt), p).reshape(npair, 1, 2 * p)
    vec = pl.BlockSpec((STATE_TILES, 1, LANES), lambda m: (m, 0, 0))
    mat = pl.BlockSpec((STATE_TILES, SSM_GROUP, LANES), lambda m: (m, 0, 0))
    return pl.pallas_call(
        _ssm_disc_kernel,
        grid=(N_CH_BLOCKS,),
        in_specs=[vec, vec, vec, mat, mat, mat, mat],
        out_specs=[pl.BlockSpec((None, BLOCK_LANES, STATE_LANES), lambda m: (m, 0, 0)),
                   pl.BlockSpec((None, BLOCK_LANES + STATE_LANES, BLOCK_LANES), lambda m: (m, 0, 0)),
                   pl.BlockSpec((None, 4, 2, SUBLANES, STATE_LANES // 2), lambda m: (m, 0, 0, 0, 0))],
        out_shape=[jax.ShapeDtypeStruct((N_CH_BLOCKS, BLOCK_LANES, STATE_LANES), BF16),
                   jax.ShapeDtypeStruct((N_CH_BLOCKS, BLOCK_LANES + STATE_LANES, BLOCK_LANES), BF16),
                   jax.ShapeDtypeStruct((N_CH_BLOCKS, 4, 2, SUBLANES, STATE_LANES // 2), F32)],
        scratch_shapes=[pltpu.VMEM(((SSM_BLOCK + 1) * LANES, STATE_LANES), F32),
                        pltpu.VMEM((LANES, STATE_LANES), F32)],
        compiler_params=_cparams("parallel"),
        name="ssm_discretize",
    )(pair(lam_re), pair(lam_im), dt, rows(b_re.transpose(0, 2, 1)), rows(b_im.transpose(0, 2, 1)),
      rows(c_re), rows(c_im))


GROUPS_PER_CH_BLOCK = LANES // SSM_GROUP
N_CH_BLOCKS = SSM_WIDTH // LANES
STATE_LANES = 2 * GROUPS_PER_CH_BLOCK * SSM_STATE
STATE_TILES = STATE_LANES // (2 * LANES)
BLOCK_LANES = SSM_BLOCK * LANES


def _ssm_kernel(nb, u_ref, s0_ref, we_ref, tv_ref, d_ref, coef_ref, y_ref, fin_ref, st_ref, sprev_ref, ucat_ref):
    t_idx = pl.program_id(2)
    r = st_ref.shape[1] - SUBLANES
    rows = nb * r

    @pl.when(t_idx == 0)
    def _():
        for b in range(nb):
            st_ref[b, 0:SUBLANES, :] = jnp.broadcast_to(s0_ref[b], (SUBLANES, STATE_LANES))

    for j in range(SSM_BLOCK):
        ucat_ref[:, j * LANES:(j + 1) * LANES] = u_ref[pl.ds(j, rows, stride=SSM_BLOCK), :].astype(BF16)
    e = jnp.dot(ucat_ref[...], we_ref[...], preferred_element_type=F32)
    st_ref[:, SUBLANES:, :] = e.reshape(nb, r, STATE_LANES)

    first_row = lax.broadcasted_iota(jnp.int32, (SUBLANES, LANES), 0) == 0

    def group(rg, carry):
        r0 = pl.multiple_of(SUBLANES + rg * SUBLANES, SUBLANES)
        rp = pl.multiple_of(rg * SUBLANES, SUBLANES)
        for b in range(nb):
            for q in range(STATE_TILES):
                re_sl = pl.ds(q * 2 * LANES, LANES)
                im_sl = pl.ds(q * 2 * LANES + LANES, LANES)
                co = pl.ds(q * LANES, LANES)
                xr = st_ref[b, pl.ds(r0, SUBLANES), re_sl]
                xi = st_ref[b, pl.ds(r0, SUBLANES), im_sl]
                for step, sh in enumerate((1, 2, 4)):
                    ar, ai = _cmul(coef_ref[step, 0, :, co], coef_ref[step, 1, :, co],
                                   pltpu.roll(xr, sh, 0), pltpu.roll(xi, sh, 0))
                    xr = xr + ar
                    xi = xi + ai
                pr = jnp.broadcast_to(st_ref[b, pl.ds(rp, SUBLANES), re_sl][SUBLANES - 1:, :], (SUBLANES, LANES))
                pi = jnp.broadcast_to(st_ref[b, pl.ds(rp, SUBLANES), im_sl][SUBLANES - 1:, :], (SUBLANES, LANES))
                ar, ai = _cmul(coef_ref[3, 0, :, co], coef_ref[3, 1, :, co], pr, pi)
                xr = xr + ar
                xi = xi + ai
                st_ref[b, pl.ds(r0, SUBLANES), re_sl] = xr
                st_ref[b, pl.ds(r0, SUBLANES), im_sl] = xi
                out_rows = pl.ds(pl.multiple_of(b * r + rg * SUBLANES, SUBLANES), SUBLANES)
                sprev_ref[out_rows, re_sl] = jnp.where(first_row, pr, pltpu.roll(xr, 1, 0))
                sprev_ref[out_rows, im_sl] = jnp.where(first_row, pi, pltpu.roll(xi, 1, 0))
        return carry

    lax.fori_loop(0, r // SUBLANES, group, 0)

    lhs = jnp.concatenate([ucat_ref[...], sprev_ref[...].astype(BF16)], axis=1)
    ycat = jnp.dot(lhs, tv_ref[...], preferred_element_type=F32)
    d = d_ref[...]
    for t in range(SSM_BLOCK):
        tok = pl.ds(t, rows, stride=SSM_BLOCK)
        y_ref[tok, :] = ycat[:, t * LANES:(t + 1) * LANES] + d * u_ref[tok, :]

    for b in range(nb):
        tail = st_ref[b, r:r + SUBLANES, :]
        st_ref[b, 0:SUBLANES, :] = tail
        fin_ref[b] = tail[SUBLANES - 1:SUBLANES, :]


def _ssm(u, s0, we, tv, d, l, coef, row_base, n_seq, seq_len, nb, tt):
    rows = nb * tt
    r = tt // SSM_BLOCK
    nt = seq_len // tt
    base = row_base // rows
    return pl.pallas_call(
        functools.partial(_ssm_kernel, nb),
        grid=(N_CH_BLOCKS, n_seq // nb, nt),
        in_specs=[pl.BlockSpec((rows, LANES), lambda m, i, j: (base + i * nt + j, m)),
                  pl.BlockSpec((nb, 1, STATE_LANES), lambda m, i, j: (i, 0, m)),
                  pl.BlockSpec((None, BLOCK_LANES, STATE_LANES), lambda m, i, j: (m, 0, 0)),
                  pl.BlockSpec((None, BLOCK_LANES + STATE_LANES, BLOCK_LANES), lambda m, i, j: (m, 0, 0)),
                  pl.BlockSpec((None, 1, LANES), lambda m, i, j: (l, 0, m)),
                  pl.BlockSpec((None, 4, 2, SUBLANES, STATE_LANES // 2), lambda m, i, j: (m, 0, 0, 0, 0))],
        out_specs=[pl.BlockSpec((rows, LANES), lambda m, i, j: (i * nt + j, m)),
                   pl.BlockSpec((nb, 1, STATE_LANES), lambda m, i, j: (i, 0, m))],
        out_shape=[jax.ShapeDtypeStruct((n_seq * seq_len, SSM_WIDTH), F32),
                   jax.ShapeDtypeStruct((n_seq, 1, N_CH_BLOCKS * STATE_LANES), F32)],
        scratch_shapes=[pltpu.VMEM((nb, SUBLANES + r, STATE_LANES), F32),
                        pltpu.VMEM((nb * r, STATE_LANES), F32),
                        pltpu.VMEM((nb * r, BLOCK_LANES), BF16)],
        compiler_params=_cparams("parallel", "parallel", "arbitrary"),
        name="ssm_scan",
    )(u, s0, we, tv, d, coef)


def _state_to_tiles(s):
    lead = s.shape[:-3]
    t = s.reshape(lead + (N_CH_BLOCKS, STATE_TILES, 2, SSM_STATE, 2))
    t = jnp.moveaxis(t, -1, -3)
    return t.reshape(lead + (1, N_CH_BLOCKS * STATE_LANES))


def _tiles_to_state(f):
    b = f.shape[0]
    t = f.reshape(b, N_CH_BLOCKS, STATE_TILES, 2, 2, SSM_STATE)
    t = jnp.moveaxis(t, 3, -1)
    return t.reshape(b, N_SSM_GROUPS, SSM_STATE, 2)


def _merge_kernel(n_prompt_tiles, xp_ref, xs_ref, ap_ref, as_ref, yp_ref, ys_ref, wglu_ref, bglu_ref, ga_ref, gs_ref,
                  wout_ref, o_ref):
    is_prompt = pl.program_id(0) < n_prompt_tiles
    attn = jnp.where(is_prompt, ap_ref[...], as_ref[...])
    g = jax.nn.gelu(jnp.where(is_prompt, yp_ref[...], ys_ref[...]))
    glu = g * jax.nn.sigmoid(jnp.dot(g.astype(BF16), wglu_ref[...], preferred_element_type=F32) + bglu_ref[...])
    na = _rms(attn, ga_ref[...]).astype(BF16)
    ns = _rms(glu, gs_ref[...]).astype(BF16)
    o = jnp.dot(na, wout_ref[0:ATTN_WIDTH, :], preferred_element_type=F32)
    o = o + jnp.dot(ns, wout_ref[ATTN_WIDTH:, :], preferred_element_type=F32)
    o_ref[...] = jnp.where(is_prompt, xp_ref[...], xs_ref[...]) + o


def _merge(xp, xs, n, attn_p, attn_s, y_p, y_s, wglu, bglu, ga, gs, wout, l, tm):
    npt = attn_p.shape[0] // tm
    nst = attn_s.shape[0] // tm
    row = lambda w: pl.BlockSpec((tm, w), lambda i: (i, 0))
    prow = lambda w: pl.BlockSpec((tm, w), lambda i: (jnp.minimum(i, npt - 1), 0))
    srow = lambda w: pl.BlockSpec((tm, w), lambda i: (jnp.clip(i - npt, 0, nst - 1), 0))
    return pl.pallas_call(
        functools.partial(_merge_kernel, npt),
        grid=(n // tm,),
        in_specs=[*_two_source_specs(xp, xs, attn_p.shape[0], tm, D_MODEL),
                  prow(ATTN_WIDTH), srow(ATTN_WIDTH), prow(SSM_WIDTH), srow(SSM_WIDTH),
                  _layer_spec(wglu, l), _layer_spec(bglu, l), _layer_spec(ga, l), _layer_spec(gs, l),
                  _layer_spec(wout, l)],
        out_specs=row(D_MODEL),
        out_shape=jax.ShapeDtypeStruct((n, D_MODEL), F32),
        compiler_params=_cparams("parallel"),
        name="merge_heads",
    )(xp, xs, attn_p, attn_s, y_p, y_s, wglu, bglu, ga, gs, wout)


def _mem_kv_kernel(m_ref, g_ref, wk_ref, wv_ref, k_ref, v_ref):
    mn = _rms(m_ref[...], g_ref[...]).astype(BF16)
    k_ref[...] = jnp.dot(mn, wk_ref[...], preferred_element_type=F32)
    v_ref[...] = jnp.dot(mn, wv_ref[...], preferred_element_type=F32)


def _mem_kv(mem, g, wk, wv, l, tm):
    n = mem.shape[0]
    row = pl.BlockSpec((tm, D_MODEL), lambda i: (i, 0))
    return pl.pallas_call(
        _mem_kv_kernel,
        grid=(n // tm,),
        in_specs=[row, _layer_spec(g, l), _layer_spec(wk, l), _layer_spec(wv, l)],
        out_specs=[row, row],
        out_shape=[jax.ShapeDtypeStruct((n, D_MODEL), F32)] * 2,
        compiler_params=_cparams("parallel"),
        name="mem_kv",
    )(mem, g, wk, wv)


def _xattn_kernel(nb, x_ref, g_ref, wq_ref, wo_ref, mk_ref, mv_ref, o_ref, att_ref):
    t = x_ref.shape[0] // nb
    x = x_ref[...]
    hn = _rms(x, g_ref[...]).astype(BF16)
    q = jnp.dot(hn, wq_ref[...], preferred_element_type=F32) * (1.0 / math.sqrt(XHEAD_DIM))
    q = q.astype(BF16)
    for b in range(nb):
        mk = mk_ref[b * N_MEM:(b + 1) * N_MEM, :].astype(BF16)
        mv = mv_ref[b * N_MEM:(b + 1) * N_MEM, :].astype(BF16)
        for h in range(N_XHEADS):
            sl = slice(h * XHEAD_DIM, (h + 1) * XHEAD_DIM)
            s = lax.dot_general(q[b * t:(b + 1) * t, sl], mk[:, sl], (((1,), (1,)), ((), ())),
                                preferred_element_type=F32)
            m = jnp.max(s, axis=-1, keepdims=True)
            e = jnp.exp(s - m)
            p = e / jnp.sum(e, axis=-1, keepdims=True)
            att_ref[b * t:(b + 1) * t, sl] = jnp.dot(p.astype(BF16), mv[:, sl], preferred_element_type=F32)
    o = jnp.dot(att_ref[...].astype(BF16), wo_ref[...], preferred_element_type=F32)
    o_ref[...] = x + o


def _xattn(x, g, wq, wo, l, mk, mv, mem_spec, row_base, n_rows, nb, tm):
    base = row_base // tm
    xspec = pl.BlockSpec((tm, D_MODEL), lambda i: (base + i, 0))
    return pl.pallas_call(
        functools.partial(_xattn_kernel, nb),
        grid=(n_rows // tm,),
        in_specs=[xspec, _layer_spec(g, l), _layer_spec(wq, l), _layer_spec(wo, l), mem_spec, mem_spec],
        out_specs=xspec,
        out_shape=jax.ShapeDtypeStruct(x.shape, F32),
        scratch_shapes=[pltpu.VMEM((tm, D_MODEL), F32)],
        input_output_aliases={0: 0},
        compiler_params=_cparams("parallel"),
        name="cross_attn",
    )(x, g, wq, wo, mk, mv)


ROUTER_LANES = LANES
EXPERT_LANE0 = N_EXPERT_GROUPS
EXPERTS_PER_STEP = 4


def _dot_f32_3pass(x, w):
    xh = x.astype(BF16)
    xl = (x - xh.astype(F32)).astype(BF16)
    wh = w.astype(BF16)
    wl = (w - wh.astype(F32)).astype(BF16)
    dot = lambda a, b: jnp.dot(a, b, preferred_element_type=F32)
    return dot(xh, wh) + (dot(xl, wh) + dot(xh, wl))


def _route(logits):
    lane_i = lax.broadcasted_iota(jnp.int32, logits.shape, 1)
    lane = lane_i.astype(F32)
    neg = jnp.float32(-jnp.inf)
    is_g = lane_i < N_EXPERT_GROUPS
    gl = jnp.where(is_g, logits, neg)
    gmax = jnp.max(gl, axis=-1, keepdims=True)
    gidx = jnp.min(jnp.where(gl == gmax, lane, float(ROUTER_LANES)), axis=-1, keepdims=True)
    g_w = 1.0 / jnp.sum(jnp.where(is_g, jnp.exp(gl - gmax), 0.0), axis=-1, keepdims=True)
    first = EXPERT_LANE0 + gidx * EXPERTS_PER_GROUP
    sel = (lane >= first) & (lane < first + EXPERTS_PER_GROUP)
    el = jnp.where(sel, logits, neg)
    m1 = jnp.max(el, axis=-1, keepdims=True)
    i1 = jnp.min(jnp.where(el == m1, lane, float(ROUTER_LANES)), axis=-1, keepdims=True)
    el2 = jnp.where(lane == i1, neg, el)
    m2 = jnp.max(el2, axis=-1, keepdims=True)
    i2 = jnp.min(jnp.where(el2 == m2, lane, float(ROUTER_LANES)), axis=-1, keepdims=True)
    r = jnp.exp(m2 - m1)
    w1 = g_w / (1.0 + r)
    w2 = w1 * r
    return jnp.where(lane == i1, w1, jnp.where(lane == i2, w2, 0.0))


def _moe_kernel(x_ref, g_ref, wr_ref, br_ref, wgu_ref, wd_ref, o_ref, xn_ref, gate_ref):
    e_step = pl.program_id(1)

    @pl.when(e_step == 0)
    def _():
        x = x_ref[...]
        xn = _rms(x, g_ref[...])
        xn_ref[...] = xn.astype(BF16)
        logits = _dot_f32_3pass(xn, wr_ref[...]) + br_ref[...]
        gate_ref[...] = _route(logits)
        o_ref[...] = x

    xn = xn_ref[...]
    gates = gate_ref[...]
    lane = lax.broadcasted_iota(jnp.int32, gates.shape, 1)
    hids = []
    for j in range(EXPERTS_PER_STEP):
        h = jnp.dot(xn, wgu_ref[j], preferred_element_type=F32)
        ge = jnp.sum(jnp.where(lane == EXPERT_LANE0 + e_step * EXPERTS_PER_STEP + j, gates, 0.0),
                     axis=-1, keepdims=True)
        hid = jax.nn.silu(h[:, :EXPERT_FF]) * h[:, EXPERT_FF:] * ge
        hids.append(hid.astype(BF16))
    hid = jnp.concatenate(hids, axis=1)
    o_ref[...] += jnp.dot(hid, wd_ref[...], preferred_element_type=F32)


def _moe(x, g, wr, br, wgu, wd, l, tm):
    n = x.shape[0]
    es = EXPERTS_PER_STEP
    row = pl.BlockSpec((tm, D_MODEL), lambda i, e: (i, 0))
    return pl.pallas_call(
        _moe_kernel,
        grid=(n // tm, N_EXPERTS // es),
        in_specs=[row, _layer_spec(g, l), _layer_spec(wr, l), _layer_spec(br, l),
                  pl.BlockSpec((None, es, D_MODEL, 2 * EXPERT_FF), lambda i, e: (l, e, 0, 0)),
                  pl.BlockSpec((None, es * EXPERT_FF, D_MODEL), lambda i, e: (l, e, 0))],
        out_specs=row,
        out_shape=jax.ShapeDtypeStruct((n, D_MODEL), F32),
        scratch_shapes=[pltpu.VMEM((tm, D_MODEL), BF16), pltpu.VMEM((tm, ROUTER_LANES), F32)],
        compiler_params=_cparams("parallel", "arbitrary"),
        name="hier_moe",
    )(x, g, wr, br, wgu, wd)


def _final_norm_kernel(x_ref, g_ref, o_ref):
    o_ref[...] = _rms(x_ref[...], g_ref[...])


def _final_norm(x, g, row_base, n_rows, tm):
    base = row_base // tm
    return pl.pallas_call(
        _final_norm_kernel,
        grid=(n_rows // tm,),
        in_specs=[pl.BlockSpec((tm, D_MODEL), lambda i: (base + i, 0)),
                  pl.BlockSpec((1, D_MODEL), lambda i: (0, 0))],
        out_specs=pl.BlockSpec((tm, D_MODEL), lambda i: (i, 0)),
        out_shape=jax.ShapeDtypeStruct((n_rows, D_MODEL), F32),
        compiler_params=_cparams("parallel"),
        name="final_norm",
    )(x, g)


def _rope_tables(seq, t_len, tm):
    half = HEAD_DIM // 2
    inv = ROPE_THETA ** (-jnp.arange(half, dtype=F32) / half)
    pos_s = PAST_LEN + jnp.arange(t_len)
    pos = jnp.concatenate([jnp.arange(seq), jnp.tile(pos_s, tm // t_len)]).astype(F32)
    ang = pos[:, None] * inv[None, :]
    cos = jnp.tile(jnp.cos(ang), (1, LANES // half))
    sign = jnp.where((jnp.arange(LANES) % HEAD_DIM) < half, -1.0, 1.0).astype(F32)
    sin = jnp.tile(jnp.sin(ang), (1, LANES // half)) * sign[None, :]
    return cos, sin


def kernel(x_prompt, x_sample, cache_win_k, cache_win_v, state_ssm, cache_mem_k, cache_mem_v, mem_prompt, w_in, attn_sink, lam_re, lam_im, log_dt, ssm_b_re, ssm_b_im, ssm_c_re, ssm_c_im, ssm_d, w_glu, b_glu, g_attn_out, g_ssm_out, w_out, g_mix, g_xattn, g_mem, wq_x, wk_x, wv_x, wo_x, g_ffn, w_group, b_group, w_router, b_router, w_gate, w_up, w_down, g_final):
    batch, seq, _ = x_prompt.shape
    dec_batch, t_len, _ = x_sample.shape
    depth = w_in.shape[0]
    win_rows = cache_win_k.shape[2]
    n_p = batch * seq
    n_s = dec_batch * t_len
    tm_wide = 1024 if (n_p + n_s) % 1024 == 0 else 512
    tm = tm_wide
    tm_x = 512
    sample_nb = tm_x // t_len

    n = n_p + n_s
    xp = x_prompt.reshape(n_p, D_MODEL)
    xs = x_sample.reshape(n_s, D_MODEL)
    cos_tab, sin_tab = _rope_tables(seq, t_len, tm)
    mem_flat = mem_prompt.reshape(batch * N_MEM, D_MODEL)
    zero_state = jnp.zeros((batch, 1, 2 * N_STATE), F32)
    vec = lambda a: a.reshape(depth, 1, a.shape[-1])

    w_in_b, w_glu_b, w_out_b = w_in.astype(BF16), w_glu.astype(BF16), w_out.astype(BF16)
    wq_b, wk_b, wv_b, wo_b = (w.astype(BF16) for w in (wq_x, wk_x, wv_x, wo_x))
    wgu_b = jnp.concatenate([w_gate, w_up], axis=-1).astype(BF16)
    wd_b = w_down.reshape(depth, N_EXPERTS * EXPERT_FF, D_MODEL).astype(BF16)
    wr = jnp.concatenate([w_group, w_router.transpose(0, 2, 1, 3).reshape(depth, D_MODEL, N_EXPERTS)], axis=-1)
    wr = jnp.pad(wr, ((0, 0), (0, 0), (0, ROUTER_LANES - wr.shape[-1])))
    br = jnp.concatenate([b_group, b_router.reshape(depth, N_EXPERTS)], axis=-1)
    br = jnp.pad(br, ((0, 0), (0, ROUTER_LANES - br.shape[-1]))).reshape(depth, 1, ROUTER_LANES)
    g_mix_r, g_xattn_r, g_mem_r, g_ffn_r = vec(g_mix), vec(g_xattn), vec(g_mem), vec(g_ffn)
    g_a_r, g_s_r, b_glu_r, ssm_d_r = vec(g_attn_out), vec(g_ssm_out), vec(b_glu), vec(ssm_d)
    cache_k = cache_win_k.reshape(depth, dec_batch, win_rows, KV_WIDTH)
    cache_v = cache_win_v.reshape(depth, dec_batch, win_rows, KV_WIDTH)
    cmem_k = cache_mem_k.reshape(depth, dec_batch * N_MEM, D_MODEL)
    cmem_v = cache_mem_v.reshape(depth, dec_batch * N_MEM, D_MODEL)
    state_in = _state_to_tiles(state_ssm)

    outs = {k: [] for k in ("wk_p", "wv_p", "ssm_p", "mk_p", "mv_p", "wk_s", "wv_s", "ssm_s")}
    for l in range(depth):
        q, k, v, u = _in_proj(xp, xs, n, g_mix_r, w_in_b, l, cos_tab, sin_tab, n_p, seq, tm)
        attn_p = _attn_prompt(q, k, v, attn_sink[l], batch, seq)
        attn_s = _attn_sample(q, k, v, cache_k, cache_v, l, attn_sink[l], n_p, dec_batch, t_len, 4)
        tail = lambda a: jnp.stack([a[(b + 1) * seq - WINDOW:(b + 1) * seq] for b in range(batch)])
        outs["wk_p"].append(tail(k).reshape(batch, WINDOW, N_KV_HEADS, HEAD_DIM))
        outs["wv_p"].append(tail(v).reshape(batch, WINDOW, N_KV_HEADS, HEAD_DIM))
        ks = k[n_p:].reshape(dec_batch, t_len, KV_WIDTH)
        vs = v[n_p:].reshape(dec_batch, t_len, KV_WIDTH)
        k_all = jnp.concatenate([cache_k[l], ks], axis=1)[:, -win_rows:]
        v_all = jnp.concatenate([cache_v[l], vs], axis=1)[:, -win_rows:]
        outs["wk_s"].append(k_all.reshape(dec_batch, win_rows, N_KV_HEADS, HEAD_DIM))
        outs["wv_s"].append(v_all.reshape(dec_batch, win_rows, N_KV_HEADS, HEAD_DIM))

        we, tv, coef = _ssm_discretize(lam_re[l], lam_im[l], log_dt[l], ssm_b_re[l], ssm_b_im[l],
                                       ssm_c_re[l], ssm_c_im[l])
        y_p, fin_p = _ssm(u, zero_state, we, tv, ssm_d_r, l, coef, 0, batch, seq, 1, seq)
        y_s, fin_s = _ssm(u, state_in[l], we, tv, ssm_d_r, l, coef, n_p, dec_batch, t_len, dec_batch, t_len)
        outs["ssm_p"].append(_tiles_to_state(fin_p))
        outs["ssm_s"].append(_tiles_to_state(fin_s))
        x = _merge(xp, xs, n, attn_p, attn_s, y_p, y_s, w_glu_b, b_glu_r, g_a_r, g_s_r, w_out_b, l, tm)

        mk_p, mv_p = _mem_kv(mem_flat, g_mem_r, wk_b, wv_b, l, 512)
        outs["mk_p"].append(mk_p.reshape(batch, N_MEM, N_XHEADS, XHEAD_DIM))
        outs["mv_p"].append(mv_p.reshape(batch, N_MEM, N_XHEADS, XHEAD_DIM))
        tiles_per_seq = seq // tm
        x = _xattn(x, g_xattn_r, wq_b, wo_b, l, mk_p, mv_p,
                   pl.BlockSpec((N_MEM, D_MODEL), lambda i: (i // tiles_per_seq, 0)), 0, n_p, 1, tm)
        x = _xattn(x, g_xattn_r, wq_b, wo_b, l, cmem_k, cmem_v,
                   pl.BlockSpec((None, sample_nb * N_MEM, D_MODEL), lambda i: (l, i, 0)), n_p, n_s, sample_nb, tm_x)

        x = _moe(x, g_ffn_r, wr, br, wgu_b, wd_b, l, tm_wide)
        xp = xs = x

    gf = g_final.reshape(1, D_MODEL)
    y_p = _final_norm(x, gf, 0, n_p, tm)
    y_s = _final_norm(x, gf, n_p, n_s, tm)
    st = lambda name: jnp.stack(outs[name], axis=0)
    return (y_p.reshape(batch, seq, D_MODEL), y_s.reshape(dec_batch, t_len, D_MODEL),
            st("wk_p"), st("wv_p"), st("ssm_p"), st("mk_p"), st("mv_p"), st("wk_s"), st("wv_s"), st("ssm_s"))
```

```python
import functools
import math

import jax
import jax.numpy as jnp
from jax import lax
from jax.experimental import pallas as pl
from jax.experimental.pallas import tpu as pltpu

F32 = jnp.float32
BF16 = jnp.bfloat16

D_MODEL = 1024
CHUNK = 64
EPS = 1e-6
NEG_INF = -1e30
N_HEADS = 8
N_KV_HEADS = 2
HEAD_DIM = 64
ATTN_WIDTH = N_HEADS * HEAD_DIM
KV_WIDTH = N_KV_HEADS * HEAD_DIM
WINDOW = 128
ROPE_THETA = 10000.0
SSM_WIDTH = D_MODEL - ATTN_WIDTH
SSM_GROUP = 16
N_SSM_GROUPS = SSM_WIDTH // SSM_GROUP
SSM_STATE = 64
N_STATE = N_SSM_GROUPS * SSM_STATE
IN_WIDTH = ATTN_WIDTH + 2 * KV_WIDTH + SSM_WIDTH
N_MEM = 256
N_XHEADS = 4
XHEAD_DIM = D_MODEL // N_XHEADS
N_EXPERT_GROUPS = 4
EXPERTS_PER_GROUP = 8
N_EXPERTS = N_EXPERT_GROUPS * EXPERTS_PER_GROUP
EXPERT_FF = 128
PAST_LEN = 4096

LANES = 128
SUBLANES = 8
VMEM_LIMIT = 56 * 1024 * 1024


def _cparams(*sem):
    return pltpu.CompilerParams(dimension_semantics=sem, vmem_limit_bytes=VMEM_LIMIT)


def _rms(x, g):
    return x * lax.rsqrt(jnp.mean(x * x, axis=-1, keepdims=True) + EPS) * g


def _layer_spec(arr, l):
    shape = arr.shape[1:]
    zeros = (0,) * len(shape)
    return pl.BlockSpec((None,) + shape, lambda *_: (l,) + zeros, pipeline_mode=pl.Buffered(1))


def _rope_pairs(t, cos, sin_signed, first_half):
    swapped = jnp.where(first_half, pltpu.roll(t, LANES - HEAD_DIM // 2, 1), pltpu.roll(t, HEAD_DIM // 2, 1))
    return t * cos + swapped * sin_signed


def _two_source_specs(xp, xs, n_prompt, tm, width):
    npt = n_prompt // tm
    s_off = 0 if xs is xp else npt
    s_last = xs.shape[0] // tm - 1
    pspec = pl.BlockSpec((tm, width), lambda i, *_: (jnp.minimum(i, npt - 1), 0))
    sspec = pl.BlockSpec((tm, width), lambda i, *_: (jnp.clip(i - s_off, npt - s_off, s_last), 0))
    return pspec, sspec


def _in_proj_kernel(n_prompt_tiles, xp_ref, xs_ref, g_ref, w_ref, cos_ref, sin_ref, q_ref, k_ref, v_ref, u_ref):
    x = jnp.where(pl.program_id(0) < n_prompt_tiles, xp_ref[...], xs_ref[...])
    xn = _rms(x, g_ref[...])
    z = jnp.dot(xn.astype(BF16), w_ref[...], preferred_element_type=F32)
    cos = cos_ref[...]
    sin = sin_ref[...]
    lane = lax.broadcasted_iota(jnp.int32, cos.shape, 1)
    first_half = (lane % HEAD_DIM) < (HEAD_DIM // 2)
    scale = 1.0 / math.sqrt(HEAD_DIM)
    for j in range(ATTN_WIDTH // LANES):
        t = z[:, j * LANES:(j + 1) * LANES]
        q_ref[:, j * LANES:(j + 1) * LANES] = (_rope_pairs(t, cos, sin, first_half) * scale).astype(BF16)
    k_ref[...] = _rope_pairs(z[:, ATTN_WIDTH:ATTN_WIDTH + KV_WIDTH], cos, sin, first_half)
    v_ref[...] = z[:, ATTN_WIDTH + KV_WIDTH:ATTN_WIDTH + 2 * KV_WIDTH]
    u_ref[...] = z[:, ATTN_WIDTH + 2 * KV_WIDTH:]


def _in_proj(xp, xs, n, g, w_bf16, l, cos_tab, sin_tab, n_prompt, seq, tm):
    n_prompt_tiles = n_prompt // tm
    tiles_per_seq = seq // tm

    def tab_map(i):
        return (jnp.where(i < n_prompt_tiles, i % tiles_per_seq, tiles_per_seq), 0)

    row = lambda w: pl.BlockSpec((tm, w), lambda i: (i, 0))
    return pl.pallas_call(
        functools.partial(_in_proj_kernel, n_prompt_tiles),
        grid=(n // tm,),
        in_specs=[*_two_source_specs(xp, xs, n_prompt, tm, D_MODEL), _layer_spec(g, l), _layer_spec(w_bf16, l),
                  pl.BlockSpec((tm, LANES), tab_map),
                  pl.BlockSpec((tm, LANES), tab_map)],
        out_specs=[row(ATTN_WIDTH), row(KV_WIDTH), row(KV_WIDTH), row(SSM_WIDTH)],
        out_shape=[jax.ShapeDtypeStruct((n, ATTN_WIDTH), BF16),
                   jax.ShapeDtypeStruct((n, KV_WIDTH), F32),
                   jax.ShapeDtypeStruct((n, KV_WIDTH), F32),
                   jax.ShapeDtypeStruct((n, SSM_WIDTH), F32)],
        compiler_params=_cparams("parallel"),
        name="in_proj",
    )(xp, xs, g, w_bf16, cos_tab, sin_tab)


def _kv_pairs(keys, vals):
    lane = lax.broadcasted_iota(jnp.int32, keys.shape, 1)
    low = lane < HEAD_DIM
    k_sw = pltpu.roll(keys, HEAD_DIM, 1)
    v_sw = pltpu.roll(vals, HEAD_DIM, 1)
    kk = [jnp.where(low, keys, k_sw).astype(BF16), jnp.where(low, k_sw, keys).astype(BF16)]
    vv = [jnp.where(low, vals, v_sw).astype(BF16), jnp.where(low, v_sw, vals).astype(BF16)]
    return kk, vv


def _attend_pairs(q, kk, vv, key_rows, mask_add, sink_ref, o_ref, row0):
    tq = q.shape[0]
    qlane = lax.broadcasted_iota(jnp.int32, (tq, LANES), 1)
    qlow = qlane < HEAD_DIM
    row_top = lax.broadcasted_iota(jnp.int32, (2 * tq, 1), 0) < tq
    zero = jnp.zeros((), BF16)
    for pair in range(N_HEADS // 2):
        kv = pair // (N_HEADS // N_KV_HEADS // 2)
        qp = q[:, pair * LANES:(pair + 1) * LANES]
        qs = jnp.concatenate([jnp.where(qlow, qp, zero), jnp.where(qlow, zero, qp)], axis=0)
        s = lax.dot_general(qs, kk[kv][key_rows, :], (((1,), (1,)), ((), ())), preferred_element_type=F32)
        if mask_add is not None:
            s = s + mask_add
        sink = jnp.where(row_top, sink_ref[2 * pair], sink_ref[2 * pair + 1])
        m = jnp.maximum(jnp.max(s, axis=-1, keepdims=True), sink)
        e = jnp.exp(s - m)
        p = e / (jnp.sum(e, axis=-1, keepdims=True) + jnp.exp(sink - m))
        o = jnp.dot(p.astype(BF16), vv[kv][key_rows, :], preferred_element_type=F32)
        o_ref[row0:row0 + tq, pair * LANES:(pair + 1) * LANES] = jnp.where(qlow, o[:tq], o[tq:])


def _attend_blocks(blocks, sink_ref, o_ref):
    tq = blocks[0][0].shape[0]
    qlane = lax.broadcasted_iota(jnp.int32, (tq, LANES), 1)
    qlow = qlane < HEAD_DIM
    row_top = lax.broadcasted_iota(jnp.int32, (2 * tq, 1), 0) < tq
    zero = jnp.zeros((), BF16)
    scores, sinks = [], []
    for q, kk, vv, key_rows, mask_add, row0 in blocks:
        for pair in range(N_HEADS // 2):
            kv = pair // (N_HEADS // N_KV_HEADS // 2)
            qp = q[:, pair * LANES:(pair + 1) * LANES]
            qs = jnp.concatenate([jnp.where(qlow, qp, zero), jnp.where(qlow, zero, qp)], axis=0)
            s = lax.dot_general(qs, kk[kv][key_rows, :], (((1,), (1,)), ((), ())), preferred_element_type=F32)
            scores.append(s if mask_add is None else s + mask_add)
            sinks.append(jnp.where(row_top, sink_ref[2 * pair], sink_ref[2 * pair + 1]))
    s = jnp.concatenate(scores, axis=0)
    sink = jnp.concatenate(sinks, axis=0)
    m = jnp.maximum(jnp.max(s, axis=-1, keepdims=True), sink)
    e = jnp.exp(s - m)
    p = (e / (jnp.sum(e, axis=-1, keepdims=True) + jnp.exp(sink - m))).astype(BF16)
    piece = 0
    for q, kk, vv, key_rows, mask_add, row0 in blocks:
        for pair in range(N_HEADS // 2):
            kv = pair // (N_HEADS // N_KV_HEADS // 2)
            o = jnp.dot(p[piece * 2 * tq:(piece + 1) * 2 * tq, :], vv[kv][key_rows, :], preferred_element_type=F32)
            o_ref[row0:row0 + tq, pair * LANES:(pair + 1) * LANES] = jnp.where(qlow, o[:tq], o[tq:])
            piece += 1


ATTN_SUB = WINDOW
ATTN_TILE = 2 * ATTN_SUB


def _attn_prompt_kernel(sink_ref, ma_ref, mb_ref, q_ref, kp_ref, kc_ref, vp_ref, vc_ref, o_ref):
    kk, vv = _kv_pairs(jnp.concatenate([kp_ref[...], kc_ref[...]], axis=0),
                       jnp.concatenate([vp_ref[...], vc_ref[...]], axis=0))
    for s, m_ref in enumerate((ma_ref, mb_ref)):
        rows = slice(s * ATTN_SUB, s * ATTN_SUB + 2 * WINDOW)
        _attend_pairs(q_ref[s * ATTN_SUB:(s + 1) * ATTN_SUB, :], kk, vv, rows, m_ref[...], sink_ref, o_ref, s * ATTN_SUB)


def _band_masks():
    r = (jnp.arange(2 * ATTN_SUB) % ATTN_SUB)[:, None] // CHUNK
    c = jnp.arange(2 * WINDOW)[None, :]
    band = (c // CHUNK >= r) & (c // CHUNK <= r + WINDOW // CHUNK)
    masks = jnp.stack([band, band & (c >= WINDOW)])
    return jnp.where(masks, 0.0, NEG_INF).astype(F32)


def _attn_prompt(q, k, v, sink, batch, seq):
    nt = seq // ATTN_TILE
    per_seq = seq // ATTN_SUB
    cur = lambda b, i: (b * nt + i, 0)
    prev = lambda b, i: (b * per_seq + jnp.maximum(2 * i - 1, 0), 0)
    masks = _band_masks()
    return pl.pallas_call(
        _attn_prompt_kernel,
        grid=(batch, nt),
        in_specs=[pl.BlockSpec(memory_space=pltpu.SMEM),
                  pl.BlockSpec((None, 2 * ATTN_SUB, 2 * WINDOW), lambda b, i: (jnp.where(i == 0, 1, 0), 0, 0)),
                  pl.BlockSpec((None, 2 * ATTN_SUB, 2 * WINDOW), lambda b, i: (0, 0, 0)),
                  pl.BlockSpec((ATTN_TILE, ATTN_WIDTH), cur),
                  pl.BlockSpec((ATTN_SUB, KV_WIDTH), prev),
                  pl.BlockSpec((ATTN_TILE, KV_WIDTH), cur),
                  pl.BlockSpec((ATTN_SUB, KV_WIDTH), prev),
                  pl.BlockSpec((ATTN_TILE, KV_WIDTH), cur)],
        out_specs=pl.BlockSpec((ATTN_TILE, ATTN_WIDTH), cur),
        out_shape=jax.ShapeDtypeStruct((batch * seq, ATTN_WIDTH), F32),
        compiler_params=_cparams("parallel", "parallel"),
        name="attn_prompt",
    )(sink, masks, masks, q, k, k, v, v)


def _attn_sample_kernel(sink_ref, q_ref, ck_ref, cv_ref, k_ref, v_ref, o_ref):
    nb = ck_ref.shape[0]
    t = q_ref.shape[0] // nb
    blocks = []
    for b in range(nb):
        rows = slice(b * t, (b + 1) * t)
        kk, vv = _kv_pairs(jnp.concatenate([ck_ref[b], k_ref[rows, :]], axis=0),
                           jnp.concatenate([cv_ref[b], v_ref[rows, :]], axis=0))
        blocks.append((q_ref[rows, :], kk, vv, slice(None), None, b * t))
    _attend_blocks(blocks, sink_ref, o_ref)


def _attn_sample(q, k, v, cache_k, cache_v, l, sink, n_prompt, dec_batch, t, nb):
    w = cache_k.shape[2]
    rows = nb * t
    base = n_prompt // rows
    tok = lambda width: pl.BlockSpec((rows, width), lambda i: (base + i, 0))
    cache = pl.BlockSpec((None, nb, w, KV_WIDTH), lambda i: (l, i, 0, 0))
    return pl.pallas_call(
        _attn_sample_kernel,
        grid=(dec_batch // nb,),
        in_specs=[pl.BlockSpec(memory_space=pltpu.SMEM),
                  tok(ATTN_WIDTH), cache, cache, tok(KV_WIDTH), tok(KV_WIDTH)],
        out_specs=pl.BlockSpec((rows, ATTN_WIDTH), lambda i: (i, 0)),
        out_shape=jax.ShapeDtypeStruct((dec_batch * t, ATTN_WIDTH), F32),
        compiler_params=_cparams("parallel"),
        name="attn_sample",
    )(sink, q, cache_k, cache_v, k, v)


SSM_BLOCK = SUBLANES


def _cmul(ar, ai, br, bi):
    return ar * br - ai * bi, ar * bi + ai * br


def _ssm_disc_kernel(lre_ref, lim_ref, dt_ref, bre_ref, bim_ref, cre_ref, cim_ref,
                     we_ref, tv_ref, coef_ref, vt_ref, wb_ref):
    we_ref[...] = jnp.zeros(we_ref.shape, we_ref.dtype)
    vt_ref[...] = jnp.zeros(vt_ref.shape, vt_ref.dtype)
    wb_ref[...] = jnp.zeros(wb_ref.shape, wb_ref.dtype)
    lane = lax.broadcasted_iota(jnp.int32, (SSM_GROUP, LANES), 1)
    half = [lane < SSM_STATE, lane >= SSM_STATE]
    row8 = lax.broadcasted_iota(jnp.int32, (SUBLANES, LANES), 0)
    for q in range(STATE_TILES):
        lre = lre_ref[q]
        lim = lim_ref[q]
        dt = dt_ref[q]
        mag = jnp.exp(lre * dt)
        ang = lim * dt
        lbr = mag * jnp.cos(ang)
        lbi = mag * jnp.sin(ang)
        nr, ni = lbr - 1.0, lbi
        den = lre * lre + lim * lim
        fr = (nr * lre + ni * lim) / den
        fi = (ni * lre - nr * lim) / den
        bbr, bbi = _cmul(fr, fi, bre_ref[q], bim_ref[q])
        cr, ci = cre_ref[q], cim_ref[q]
        pw = [(jnp.ones_like(lbr), jnp.zeros_like(lbr))]
        for _ in range(SSM_BLOCK):
            pw.append(_cmul(pw[-1][0], pw[-1][1], lbr, lbi))
        re_l = slice(q * 2 * LANES, q * 2 * LANES + LANES)
        im_l = slice(q * 2 * LANES + LANES, (q + 1) * 2 * LANES)
        for h in range(2):
            g = 2 * q + h
            grow = lambda blk: slice(blk * LANES + g * SSM_GROUP, blk * LANES + (g + 1) * SSM_GROUP)
            for j in range(SSM_BLOCK):
                wr, wi = _cmul(pw[SSM_BLOCK - 1 - j][0], pw[SSM_BLOCK - 1 - j][1], bbr, bbi)
                we_ref[grow(j), re_l] = jnp.where(half[h], wr, 0.0).astype(we_ref.dtype)
                we_ref[grow(j), im_l] = jnp.where(half[h], wi, 0.0).astype(we_ref.dtype)
            for d in range(SSM_BLOCK + 1):
                xr, xi = _cmul(cr, ci, pw[d][0], pw[d][1])
                vt_ref[grow(d), re_l] = jnp.where(half[h], xr, 0.0)
                vt_ref[grow(d), im_l] = jnp.where(half[h], -xi, 0.0)
            wb_ref[grow(0), re_l] = jnp.where(half[h], bbr, 0.0)
            wb_ref[grow(0), im_l] = jnp.where(half[h], bbi, 0.0)
        l8 = [pw[SSM_BLOCK]]
        for _ in range(SUBLANES - 1):
            l8.append(_cmul(l8[-1][0], l8[-1][1], pw[SSM_BLOCK][0], pw[SSM_BLOCK][1]))
        co = slice(q * LANES, (q + 1) * LANES)
        for kind, sh in enumerate((1, 2, 4)):
            for a in range(2):
                coef_ref[kind, a, :, co] = jnp.where(row8 >= sh, l8[sh - 1][a], 0.0)
        for a in range(2):
            tab = jnp.zeros((SUBLANES, LANES), F32)
            for k in range(SUBLANES):
                tab = jnp.where(row8 == k, l8[k][a], tab)
            coef_ref[3, a, :, co] = tab
    t0 = lax.dot_general(wb_ref[...], vt_ref[0:BLOCK_LANES, :], (((1,), (1,)), ((), ())),
                         preferred_element_type=F32, precision=lax.Precision.HIGHEST)
    for j in range(SSM_BLOCK):
        if j:
            tv_ref[j * LANES:(j + 1) * LANES, 0:j * LANES] = jnp.zeros((LANES, j * LANES), tv_ref.dtype)
        tv_ref[j * LANES:(j + 1) * LANES, j * LANES:] = t0[:, 0:BLOCK_LANES - j * LANES].astype(tv_ref.dtype)
    tv_ref[BLOCK_LANES:, :] = jnp.transpose(vt_ref[LANES:, :]).astype(tv_ref.dtype)


def _ssm_discretize(lam_re, lam_im, log_dt, b_re, b_im, c_re, c_im):
    g, p = lam_re.shape
    npair = g // 2
    pair = lambda a: a.reshape(npair, 1, 2 * p)
    rows = lambda a: a.reshape(npair, 2, SSM_GROUP, p).transpose(0, 2, 1, 3).reshape(npair, SSM_GROUP, 2 * p)
    dt = jnp.repeat(jnp.exp(log_dt), p).reshape(npair, 1, 2 * p)
    vec = pl.BlockSpec((STATE_TILES, 1, LANES), lambda m: (m, 0, 0))
    mat = pl.BlockSpec((STATE_TILES, SSM_GROUP, LANES), lambda m: (m, 0, 0))
    return pl.pallas_call(
        _ssm_disc_kernel,
        grid=(N_CH_BLOCKS,),
        in_specs=[vec, vec, vec, mat, mat, mat, mat],
        out_specs=[pl.BlockSpec((None, BLOCK_LANES, STATE_LANES), lambda m: (m, 0, 0)),
                   pl.BlockSpec((None, BLOCK_LANES + STATE_LANES, BLOCK_LANES), lambda m: (m, 0, 0)),
                   pl.BlockSpec((None, 4, 2, SUBLANES, STATE_LANES // 2), lambda m: (m, 0, 0, 0, 0))],
        out_shape=[jax.ShapeDtypeStruct((N_CH_BLOCKS, BLOCK_LANES, STATE_LANES), BF16),
                   jax.ShapeDtypeStruct((N_CH_BLOCKS, BLOCK_LANES + STATE_LANES, BLOCK_LANES), BF16),
                   jax.ShapeDtypeStruct((N_CH_BLOCKS, 4, 2, SUBLANES, STATE_LANES // 2), F32)],
        scratch_shapes=[pltpu.VMEM(((SSM_BLOCK + 1) * LANES, STATE_LANES), F32),
                        pltpu.VMEM((LANES, STATE_LANES), F32)],
        compiler_params=_cparams("parallel"),
        name="ssm_discretize",
    )(pair(lam_re), pair(lam_im), dt, rows(b_re.transpose(0, 2, 1)), rows(b_im.transpose(0, 2, 1)),
      rows(c_re), rows(c_im))


GROUPS_PER_CH_BLOCK = LANES // SSM_GROUP
N_CH_BLOCKS = SSM_WIDTH // LANES
STATE_LANES = 2 * GROUPS_PER_CH_BLOCK * SSM_STATE
STATE_TILES = STATE_LANES // (2 * LANES)
BLOCK_LANES = SSM_BLOCK * LANES


def _ssm_kernel(nb, u_ref, s0_ref, we_ref, tv_ref, d_ref, coef_ref, y_ref, fin_ref, st_ref, sprev_ref, ucat_ref):
    t_idx = pl.program_id(2)
    r = st_ref.shape[1] - SUBLANES
    rows = nb * r

    @pl.when(t_idx == 0)
    def _():
        for b in range(nb):
            st_ref[b, 0:SUBLANES, :] = jnp.broadcast_to(s0_ref[b], (SUBLANES, STATE_LANES))

    for j in range(SSM_BLOCK):
        ucat_ref[:, j * LANES:(j + 1) * LANES] = u_ref[pl.ds(j, rows, stride=SSM_BLOCK), :].astype(BF16)
    e = jnp.dot(ucat_ref[...], we_ref[...], preferred_element_type=F32)
    st_ref[:, SUBLANES:, :] = e.reshape(nb, r, STATE_LANES)

    first_row = lax.broadcasted_iota(jnp.int32, (SUBLANES, LANES), 0) == 0

    def group(rg, carry):
        r0 = pl.multiple_of(SUBLANES + rg * SUBLANES, SUBLANES)
        rp = pl.multiple_of(rg * SUBLANES, SUBLANES)
        for b in range(nb):
            for q in range(STATE_TILES):
                re_sl = pl.ds(q * 2 * LANES, LANES)
                im_sl = pl.ds(q * 2 * LANES + LANES, LANES)
                co = pl.ds(q * LANES, LANES)
                xr = st_ref[b, pl.ds(r0, SUBLANES), re_sl]
                xi = st_ref[b, pl.ds(r0, SUBLANES), im_sl]
                for step, sh in enumerate((1, 2, 4)):
                    ar, ai = _cmul(coef_ref[step, 0, :, co], coef_ref[step, 1, :, co],
                                   pltpu.roll(xr, sh, 0), pltpu.roll(xi, sh, 0))
                    xr = xr + ar
                    xi = xi + ai
                pr = jnp.broadcast_to(st_ref[b, pl.ds(rp, SUBLANES), re_sl][SUBLANES - 1:, :], (SUBLANES, LANES))
                pi = jnp.broadcast_to(st_ref[b, pl.ds(rp, SUBLANES), im_sl][SUBLANES - 1:, :], (SUBLANES, LANES))
                ar, ai = _cmul(coef_ref[3, 0, :, co], coef_ref[3, 1, :, co], pr, pi)
                xr = xr + ar
                xi = xi + ai
                st_ref[b, pl.ds(r0, SUBLANES), re_sl] = xr
                st_ref[b, pl.ds(r0, SUBLANES), im_sl] = xi
                out_rows = pl.ds(pl.multiple_of(b * r + rg * SUBLANES, SUBLANES), SUBLANES)
                sprev_ref[out_rows, re_sl] = jnp.where(first_row, pr, pltpu.roll(xr, 1, 0))
                sprev_ref[out_rows, im_sl] = jnp.where(first_row, pi, pltpu.roll(xi, 1, 0))
        return carry

    lax.fori_loop(0, r // SUBLANES, group, 0)

    lhs = jnp.concatenate([ucat_ref[...], sprev_ref[...].astype(BF16)], axis=1)
    ycat = jnp.dot(lhs, tv_ref[...], preferred_element_type=F32)
    d = d_ref[...]
    for t in range(SSM_BLOCK):
        tok = pl.ds(t, rows, stride=SSM_BLOCK)
        y_ref[tok, :] = ycat[:, t * LANES:(t + 1) * LANES] + d * u_ref[tok, :]

    for b in range(nb):
        tail = st_ref[b, r:r + SUBLANES, :]
        st_ref[b, 0:SUBLANES, :] = tail
        fin_ref[b] = tail[SUBLANES - 1:SUBLANES, :]


def _ssm(u, s0, we, tv, d, l, coef, row_base, n_seq, seq_len, nb, tt):
    rows = nb * tt
    r = tt // SSM_BLOCK
    nt = seq_len // tt
    base = row_base // rows
    return pl.pallas_call(
        functools.partial(_ssm_kernel, nb),
        grid=(N_CH_BLOCKS, n_seq // nb, nt),
        in_specs=[pl.BlockSpec((rows, LANES), lambda m, i, j: (base + i * nt + j, m)),
                  pl.BlockSpec((nb, 1, STATE_LANES), lambda m, i, j: (i, 0, m)),
                  pl.BlockSpec((None, BLOCK_LANES, STATE_LANES), lambda m, i, j: (m, 0, 0)),
                  pl.BlockSpec((None, BLOCK_LANES + STATE_LANES, BLOCK_LANES), lambda m, i, j: (m, 0, 0)),
                  pl.BlockSpec((None, 1, LANES), lambda m, i, j: (l, 0, m)),
                  pl.BlockSpec((None, 4, 2, SUBLANES, STATE_LANES // 2), lambda m, i, j: (m, 0, 0, 0, 0))],
        out_specs=[pl.BlockSpec((rows, LANES), lambda m, i, j: (i * nt + j, m)),
                   pl.BlockSpec((nb, 1, STATE_LANES), lambda m, i, j: (i, 0, m))],
        out_shape=[jax.ShapeDtypeStruct((n_seq * seq_len, SSM_WIDTH), F32),
                   jax.ShapeDtypeStruct((n_seq, 1, N_CH_BLOCKS * STATE_LANES), F32)],
        scratch_shapes=[pltpu.VMEM((nb, SUBLANES + r, STATE_LANES), F32),
                        pltpu.VMEM((nb * r, STATE_LANES), F32),
                        pltpu.VMEM((nb * r, BLOCK_LANES), BF16)],
        compiler_params=_cparams("parallel", "parallel", "arbitrary"),
        name="ssm_scan",
    )(u, s0, we, tv, d, coef)


def _state_to_tiles(s):
    lead = s.shape[:-3]
    t = s.reshape(lead + (N_CH_BLOCKS, STATE_TILES, 2, SSM_STATE, 2))
    t = jnp.moveaxis(t, -1, -3)
    return t.reshape(lead + (1, N_CH_BLOCKS * STATE_LANES))


def _tiles_to_state(f):
    b = f.shape[0]
    t = f.reshape(b, N_CH_BLOCKS, STATE_TILES, 2, 2, SSM_STATE)
    t = jnp.moveaxis(t, 3, -1)
    return t.reshape(b, N_SSM_GROUPS, SSM_STATE, 2)


def _merge_kernel(n_prompt_tiles, xp_ref, xs_ref, ap_ref, as_ref, yp_ref, ys_ref, wglu_ref, bglu_ref, ga_ref, gs_ref,
                  wout_ref, o_ref):
    is_prompt = pl.program_id(0) < n_prompt_tiles
    attn = jnp.where(is_prompt, ap_ref[...], as_ref[...])
    g = jax.nn.gelu(jnp.where(is_prompt, yp_ref[...], ys_ref[...]))
    glu = g * jax.nn.sigmoid(jnp.dot(g.astype(BF16), wglu_ref[...], preferred_element_type=F32) + bglu_ref[...])
    na = _rms(attn, ga_ref[...]).astype(BF16)
    ns = _rms(glu, gs_ref[...]).astype(BF16)
    o = jnp.dot(na, wout_ref[0:ATTN_WIDTH, :], preferred_element_type=F32)
    o = o + jnp.dot(ns, wout_ref[ATTN_WIDTH:, :], preferred_element_type=F32)
    o_ref[...] = jnp.where(is_prompt, xp_ref[...], xs_ref[...]) + o


def _merge(xp, xs, n, attn_p, attn_s, y_p, y_s, wglu, bglu, ga, gs, wout, l, tm):
    npt = attn_p.shape[0] // tm
    nst = attn_s.shape[0] // tm
    row = lambda w: pl.BlockSpec((tm, w), lambda i: (i, 0))
    prow = lambda w: pl.BlockSpec((tm, w), lambda i: (jnp.minimum(i, npt - 1), 0))
    srow = lambda w: pl.BlockSpec((tm, w), lambda i: (jnp.clip(i - npt, 0, nst - 1), 0))
    return pl.pallas_call(
        functools.partial(_merge_kernel, npt),
        grid=(n // tm,),
        in_specs=[*_two_source_specs(xp, xs, attn_p.shape[0], tm, D_MODEL),
                  prow(ATTN_WIDTH), srow(ATTN_WIDTH), prow(SSM_WIDTH), srow(SSM_WIDTH),
                  _layer_spec(wglu, l), _layer_spec(bglu, l), _layer_spec(ga, l), _layer_spec(gs, l),
                  _layer_spec(wout, l)],
        out_specs=row(D_MODEL),
        out_shape=jax.ShapeDtypeStruct((n, D_MODEL), F32),
        compiler_params=_cparams("parallel"),
        name="merge_heads",
    )(xp, xs, attn_p, attn_s, y_p, y_s, wglu, bglu, ga, gs, wout)


def _mem_kv_kernel(m_ref, g_ref, wk_ref, wv_ref, k_ref, v_ref):
    mn = _rms(m_ref[...], g_ref[...]).astype(BF16)
    k_ref[...] = jnp.dot(mn, wk_ref[...], preferred_element_type=F32)
    v_ref[...] = jnp.dot(mn, wv_ref[...], preferred_element_type=F32)


def _mem_kv(mem, g, wk, wv, l, tm):
    n = mem.shape[0]
    row = pl.BlockSpec((tm, D_MODEL), lambda i: (i, 0))
    return pl.pallas_call(
        _mem_kv_kernel,
        grid=(n // tm,),
        in_specs=[row, _layer_spec(g, l), _layer_spec(wk, l), _layer_spec(wv, l)],
        out_specs=[row, row],
        out_shape=[jax.ShapeDtypeStruct((n, D_MODEL), F32)] * 2,
        compiler_params=_cparams("parallel"),
        name="mem_kv",
    )(mem, g, wk, wv)


def _xattn_kernel(nb, x_ref, g_ref, wq_ref, wo_ref, mk_ref, mv_ref, o_ref, att_ref):
    t = x_ref.shape[0] // nb
    x = x_ref[...]
    hn = _rms(x, g_ref[...]).astype(BF16)
    q = jnp.dot(hn, wq_ref[...], preferred_element_type=F32) * (1.0 / math.sqrt(XHEAD_DIM))
    q = q.astype(BF16)
    heads = [slice(h * XHEAD_DIM, (h + 1) * XHEAD_DIM) for h in range(N_XHEADS)]
    scores = []
    for b in range(nb):
        mk = mk_ref[b * N_MEM:(b + 1) * N_MEM, :].astype(BF16)
        for sl in heads:
            scores.append(lax.dot_general(q[b * t:(b + 1) * t, sl], mk[:, sl], (((1,), (1,)), ((), ())),
                                          preferred_element_type=F32))
    s = jnp.concatenate(scores, axis=0)
    e = jnp.exp(s - jnp.max(s, axis=-1, keepdims=True))
    p = (e / jnp.sum(e, axis=-1, keepdims=True)).astype(BF16)
    for b in range(nb):
        mv = mv_ref[b * N_MEM:(b + 1) * N_MEM, :].astype(BF16)
        for h, sl in enumerate(heads):
            r0 = (b * N_XHEADS + h) * t
            att_ref[b * t:(b + 1) * t, sl] = jnp.dot(p[r0:r0 + t, :], mv[:, sl], preferred_element_type=F32)
    o = jnp.dot(att_ref[...].astype(BF16), wo_ref[...], preferred_element_type=F32)
    o_ref[...] = x + o


def _xattn(x, g, wq, wo, l, mk, mv, mem_spec, row_base, n_rows, nb, tm):
    base = row_base // tm
    xspec = pl.BlockSpec((tm, D_MODEL), lambda i: (base + i, 0))
    return pl.pallas_call(
        functools.partial(_xattn_kernel, nb),
        grid=(n_rows // tm,),
        in_specs=[xspec, _layer_spec(g, l), _layer_spec(wq, l), _layer_spec(wo, l), mem_spec, mem_spec],
        out_specs=xspec,
        out_shape=jax.ShapeDtypeStruct(x.shape, F32),
        scratch_shapes=[pltpu.VMEM((tm, D_MODEL), F32)],
        input_output_aliases={0: 0},
        compiler_params=_cparams("parallel"),
        name="cross_attn",
    )(x, g, wq, wo, mk, mv)


ROUTER_LANES = LANES
EXPERT_LANE0 = N_EXPERT_GROUPS
EXPERTS_PER_STEP = 4


def _dot_f32_3pass(x, w):
    xh = x.astype(BF16)
    xl = (x - xh.astype(F32)).astype(BF16)
    wh = w.astype(BF16)
    wl = (w - wh.astype(F32)).astype(BF16)
    dot = lambda a, b: jnp.dot(a, b, preferred_element_type=F32)
    return dot(xh, wh) + (dot(xl, wh) + dot(xh, wl))


def _route(logits):
    lane_i = lax.broadcasted_iota(jnp.int32, logits.shape, 1)
    lane = lane_i.astype(F32)
    neg = jnp.float32(-jnp.inf)
    is_g = lane_i < N_EXPERT_GROUPS
    gl = jnp.where(is_g, logits, neg)
    gmax = jnp.max(gl, axis=-1, keepdims=True)
    gidx = jnp.min(jnp.where(gl == gmax, lane, float(ROUTER_LANES)), axis=-1, keepdims=True)
    g_w = 1.0 / jnp.sum(jnp.where(is_g, jnp.exp(gl - gmax), 0.0), axis=-1, keepdims=True)
    first = EXPERT_LANE0 + gidx * EXPERTS_PER_GROUP
    sel = (lane >= first) & (lane < first + EXPERTS_PER_GROUP)
    el = jnp.where(sel, logits, neg)
    m1 = jnp.max(el, axis=-1, keepdims=True)
    i1 = jnp.min(jnp.where(el == m1, lane, float(ROUTER_LANES)), axis=-1, keepdims=True)
    el2 = jnp.where(lane == i1, neg, el)
    m2 = jnp.max(el2, axis=-1, keepdims=True)
    i2 = jnp.min(jnp.where(el2 == m2, lane, float(ROUTER_LANES)), axis=-1, keepdims=True)
    r = jnp.exp(m2 - m1)
    w1 = g_w / (1.0 + r)
    w2 = w1 * r
    return jnp.where(lane == i1, w1, jnp.where(lane == i2, w2, 0.0))


def _moe_kernel(x_ref, g_ref, wr_ref, br_ref, wgu_ref, wd_ref, o_ref, xn_ref, gate_ref):
    e_step = pl.program_id(1)

    @pl.when(e_step == 0)
    def _():
        x = x_ref[...]
        xn = _rms(x, g_ref[...])
        xn_ref[...] = xn.astype(BF16)
        logits = _dot_f32_3pass(xn, wr_ref[...]) + br_ref[...]
        gate_ref[...] = _route(logits)
        o_ref[...] = x

    xn = xn_ref[...]
    gates = gate_ref[...]
    lane = lax.broadcasted_iota(jnp.int32, gates.shape, 1)
    hids = []
    for j in range(EXPERTS_PER_STEP):
        h = jnp.dot(xn, wgu_ref[j], preferred_element_type=F32)
        ge = jnp.sum(jnp.where(lane == EXPERT_LANE0 + e_step * EXPERTS_PER_STEP + j, gates, 0.0),
                     axis=-1, keepdims=True)
        hid = jax.nn.silu(h[:, :EXPERT_FF]) * h[:, EXPERT_FF:] * ge
        hids.append(hid.astype(BF16))
    hid = jnp.concatenate(hids, axis=1)
    o_ref[...] += jnp.dot(hid, wd_ref[...], preferred_element_type=F32)


def _moe(x, g, wr, br, wgu, wd, l, tm):
    n = x.shape[0]
    es = EXPERTS_PER_STEP
    row = pl.BlockSpec((tm, D_MODEL), lambda i, e: (i, 0))
    return pl.pallas_call(
        _moe_kernel,
        grid=(n // tm, N_EXPERTS // es),
        in_specs=[row, _layer_spec(g, l), _layer_spec(wr, l), _layer_spec(br, l),
                  pl.BlockSpec((None, es, D_MODEL, 2 * EXPERT_FF), lambda i, e: (l, e, 0, 0)),
                  pl.BlockSpec((None, es * EXPERT_FF, D_MODEL), lambda i, e: (l, e, 0))],
        out_specs=row,
        out_shape=jax.ShapeDtypeStruct((n, D_MODEL), F32),
        scratch_shapes=[pltpu.VMEM((tm, D_MODEL), BF16), pltpu.VMEM((tm, ROUTER_LANES), F32)],
        compiler_params=_cparams("parallel", "arbitrary"),
        name="hier_moe",
    )(x, g, wr, br, wgu, wd)


def _final_norm_kernel(x_ref, g_ref, o_ref):
    o_ref[...] = _rms(x_ref[...], g_ref[...])


def _final_norm(x, g, row_base, n_rows, tm):
    base = row_base // tm
    return pl.pallas_call(
        _final_norm_kernel,
        grid=(n_rows // tm,),
        in_specs=[pl.BlockSpec((tm, D_MODEL), lambda i: (base + i, 0)),
                  pl.BlockSpec((1, D_MODEL), lambda i: (0, 0))],
        out_specs=pl.BlockSpec((tm, D_MODEL), lambda i: (i, 0)),
        out_shape=jax.ShapeDtypeStruct((n_rows, D_MODEL), F32),
        compiler_params=_cparams("parallel"),
        name="final_norm",
    )(x, g)


def _rope_tables(seq, t_len, tm):
    half = HEAD_DIM // 2
    inv = ROPE_THETA ** (-jnp.arange(half, dtype=F32) / half)
    pos_s = PAST_LEN + jnp.arange(t_len)
    pos = jnp.concatenate([jnp.arange(seq), jnp.tile(pos_s, tm // t_len)]).astype(F32)
    ang = pos[:, None] * inv[None, :]
    cos = jnp.tile(jnp.cos(ang), (1, LANES // half))
    sign = jnp.where((jnp.arange(LANES) % HEAD_DIM) < half, -1.0, 1.0).astype(F32)
    sin = jnp.tile(jnp.sin(ang), (1, LANES // half)) * sign[None, :]
    return cos, sin


def kernel(x_prompt, x_sample, cache_win_k, cache_win_v, state_ssm, cache_mem_k, cache_mem_v, mem_prompt, w_in, attn_sink, lam_re, lam_im, log_dt, ssm_b_re, ssm_b_im, ssm_c_re, ssm_c_im, ssm_d, w_glu, b_glu, g_attn_out, g_ssm_out, w_out, g_mix, g_xattn, g_mem, wq_x, wk_x, wv_x, wo_x, g_ffn, w_group, b_group, w_router, b_router, w_gate, w_up, w_down, g_final):
    batch, seq, _ = x_prompt.shape
    dec_batch, t_len, _ = x_sample.shape
    depth = w_in.shape[0]
    win_rows = cache_win_k.shape[2]
    n_p = batch * seq
    n_s = dec_batch * t_len
    tm_wide = 1024 if (n_p + n_s) % 1024 == 0 else 512
    tm = tm_wide
    tm_x = 512
    sample_nb = tm_x // t_len

    n = n_p + n_s
    xp = x_prompt.reshape(n_p, D_MODEL)
    xs = x_sample.reshape(n_s, D_MODEL)
    cos_tab, sin_tab = _rope_tables(seq, t_len, tm)
    mem_flat = mem_prompt.reshape(batch * N_MEM, D_MODEL)
    zero_state = jnp.zeros((batch, 1, 2 * N_STATE), F32)
    vec = lambda a: a.reshape(depth, 1, a.shape[-1])

    w_in_b, w_glu_b, w_out_b = w_in.astype(BF16), w_glu.astype(BF16), w_out.astype(BF16)
    wq_b, wk_b, wv_b, wo_b = (w.astype(BF16) for w in (wq_x, wk_x, wv_x, wo_x))
    wgu_b = jnp.concatenate([w_gate, w_up], axis=-1).astype(BF16)
    wd_b = w_down.reshape(depth, N_EXPERTS * EXPERT_FF, D_MODEL).astype(BF16)
    wr = jnp.concatenate([w_group, w_router.transpose(0, 2, 1, 3).reshape(depth, D_MODEL, N_EXPERTS)], axis=-1)
    wr = jnp.pad(wr, ((0, 0), (0, 0), (0, ROUTER_LANES - wr.shape[-1])))
    br = jnp.concatenate([b_group, b_router.reshape(depth, N_EXPERTS)], axis=-1)
    br = jnp.pad(br, ((0, 0), (0, ROUTER_LANES - br.shape[-1]))).reshape(depth, 1, ROUTER_LANES)
    g_mix_r, g_xattn_r, g_mem_r, g_ffn_r = vec(g_mix), vec(g_xattn), vec(g_mem), vec(g_ffn)
    g_a_r, g_s_r, b_glu_r, ssm_d_r = vec(g_attn_out), vec(g_ssm_out), vec(b_glu), vec(ssm_d)
    cache_k = cache_win_k.reshape(depth, dec_batch, win_rows, KV_WIDTH)
    cache_v = cache_win_v.reshape(depth, dec_batch, win_rows, KV_WIDTH)
    cmem_k = cache_mem_k.reshape(depth, dec_batch * N_MEM, D_MODEL)
    cmem_v = cache_mem_v.reshape(depth, dec_batch * N_MEM, D_MODEL)
    state_in = _state_to_tiles(state_ssm)

    outs = {k: [] for k in ("wk_p", "wv_p", "ssm_p", "mk_p", "mv_p", "wk_s", "wv_s", "ssm_s")}
    for l in range(depth):
        q, k, v, u = _in_proj(xp, xs, n, g_mix_r, w_in_b, l, cos_tab, sin_tab, n_p, seq, tm)
        attn_p = _attn_prompt(q, k, v, attn_sink[l], batch, seq)
        attn_s = _attn_sample(q, k, v, cache_k, cache_v, l, attn_sink[l], n_p, dec_batch, t_len, 4)
        tail = lambda a: jnp.stack([a[(b + 1) * seq - WINDOW:(b + 1) * seq] for b in range(batch)])
        outs["wk_p"].append(tail(k).reshape(batch, WINDOW, N_KV_HEADS, HEAD_DIM))
        outs["wv_p"].append(tail(v).reshape(batch, WINDOW, N_KV_HEADS, HEAD_DIM))
        ks = k[n_p:].reshape(dec_batch, t_len, KV_WIDTH)
        vs = v[n_p:].reshape(dec_batch, t_len, KV_WIDTH)
        k_all = jnp.concatenate([cache_k[l], ks], axis=1)[:, -win_rows:]
        v_all = jnp.concatenate([cache_v[l], vs], axis=1)[:, -win_rows:]
        outs["wk_s"].append(k_all.reshape(dec_batch, win_rows, N_KV_HEADS, HEAD_DIM))
        outs["wv_s"].append(v_all.reshape(dec_batch, win_rows, N_KV_HEADS, HEAD_DIM))

        we, tv, coef = _ssm_discretize(lam_re[l], lam_im[l], log_dt[l], ssm_b_re[l], ssm_b_im[l],
                                       ssm_c_re[l], ssm_c_im[l])
        y_p, fin_p = _ssm(u, zero_state, we, tv, ssm_d_r, l, coef, 0, batch, seq, 1, seq)
        y_s, fin_s = _ssm(u, state_in[l], we, tv, ssm_d_r, l, coef, n_p, dec_batch, t_len, dec_batch, t_len)
        outs["ssm_p"].append(_tiles_to_state(fin_p))
        outs["ssm_s"].append(_tiles_to_state(fin_s))
        x = _merge(xp, xs, n, attn_p, attn_s, y_p, y_s, w_glu_b, b_glu_r, g_a_r, g_s_r, w_out_b, l, tm)

        mk_p, mv_p = _mem_kv(mem_flat, g_mem_r, wk_b, wv_b, l, 512)
        outs["mk_p"].append(mk_p.reshape(batch, N_MEM, N_XHEADS, XHEAD_DIM))
        outs["mv_p"].append(mv_p.reshape(batch, N_MEM, N_XHEADS, XHEAD_DIM))
        tiles_per_seq = seq // tm
        x = _xattn(x, g_xattn_r, wq_b, wo_b, l, mk_p, mv_p,
                   pl.BlockSpec((N_MEM, D_MODEL), lambda i: (i // tiles_per_seq, 0)), 0, n_p, 1, tm)
        x = _xattn(x, g_xattn_r, wq_b, wo_b, l, cmem_k, cmem_v,
                   pl.BlockSpec((None, sample_nb * N_MEM, D_MODEL), lambda i: (l, i, 0)), n_p, n_s, sample_nb, tm_x)

        x = _moe(x, g_ffn_r, wr, br, wgu_b, wd_b, l, tm_wide)
        xp = xs = x

    gf = g_final.reshape(1, D_MODEL)
    y_p = _final_norm(x, gf, 0, n_p, tm)
    y_s = _final_norm(x, gf, n_p, n_s, tm)
    st = lambda name: jnp.stack(outs[name], axis=0)
    return (y_p.reshape(batch, seq, D_MODEL), y_s.reshape(dec_batch, t_len, D_MODEL),
            st("wk_p"), st("wv_p"), st("ssm_p"), st("mk_p"), st("mv_p"), st("wk_s"), st("wv_s"), st("ssm_s"))
```

```python
import functools
import math

import jax
import jax.numpy as jnp
from jax import lax
from jax.experimental import pallas as pl
from jax.experimental.pallas import tpu as pltpu

F32 = jnp.float32
BF16 = jnp.bfloat16

D_MODEL = 1024
CHUNK = 64
EPS = 1e-6
NEG_INF = -1e30
N_HEADS = 8
N_KV_HEADS = 2
HEAD_DIM = 64
ATTN_WIDTH = N_HEADS * HEAD_DIM
KV_WIDTH = N_KV_HEADS * HEAD_DIM
WINDOW = 128
ROPE_THETA = 10000.0
SSM_WIDTH = D_MODEL - ATTN_WIDTH
SSM_GROUP = 16
N_SSM_GROUPS = SSM_WIDTH // SSM_GROUP
SSM_STATE = 64
N_STATE = N_SSM_GROUPS * SSM_STATE
IN_WIDTH = ATTN_WIDTH + 2 * KV_WIDTH + SSM_WIDTH
N_MEM = 256
N_XHEADS = 4
XHEAD_DIM = D_MODEL // N_XHEADS
N_EXPERT_GROUPS = 4
EXPERTS_PER_GROUP = 8
N_EXPERTS = N_EXPERT_GROUPS * EXPERTS_PER_GROUP
EXPERT_FF = 128
PAST_LEN = 4096

LANES = 128
SUBLANES = 8
VMEM_LIMIT = 56 * 1024 * 1024


def _cparams(*sem):
    return pltpu.CompilerParams(dimension_semantics=sem, vmem_limit_bytes=VMEM_LIMIT)


def _rms(x, g):
    return x * lax.rsqrt(jnp.mean(x * x, axis=-1, keepdims=True) + EPS) * g


def _layer_spec(arr, l):
    shape = arr.shape[1:]
    zeros = (0,) * len(shape)
    return pl.BlockSpec((None,) + shape, lambda *_: (l,) + zeros, pipeline_mode=pl.Buffered(1))


def _rope_pairs(t, cos, sin_signed, first_half):
    swapped = jnp.where(first_half, pltpu.roll(t, LANES - HEAD_DIM // 2, 1), pltpu.roll(t, HEAD_DIM // 2, 1))
    return t * cos + swapped * sin_signed


def _two_source_specs(xp, xs, n_prompt, tm, width):
    npt = n_prompt // tm
    s_off = 0 if xs is xp else npt
    s_last = xs.shape[0] // tm - 1
    pspec = pl.BlockSpec((tm, width), lambda i, *_: (jnp.minimum(i, npt - 1), 0))
    sspec = pl.BlockSpec((tm, width), lambda i, *_: (jnp.clip(i - s_off, npt - s_off, s_last), 0))
    return pspec, sspec


def _in_proj_kernel(n_prompt_tiles, xp_ref, xs_ref, g_ref, w_ref, cos_ref, sin_ref, q_ref, k_ref, v_ref, u_ref):
    x = jnp.where(pl.program_id(0) < n_prompt_tiles, xp_ref[...], xs_ref[...])
    xn = _rms(x, g_ref[...])
    z = jnp.dot(xn.astype(BF16), w_ref[...], preferred_element_type=F32)
    cos = cos_ref[...]
    sin = sin_ref[...]
    lane = lax.broadcasted_iota(jnp.int32, cos.shape, 1)
    first_half = (lane % HEAD_DIM) < (HEAD_DIM // 2)
    scale = 1.0 / math.sqrt(HEAD_DIM)
    for j in range(ATTN_WIDTH // LANES):
        t = z[:, j * LANES:(j + 1) * LANES]
        q_ref[:, j * LANES:(j + 1) * LANES] = (_rope_pairs(t, cos, sin, first_half) * scale).astype(BF16)
    k_ref[...] = _rope_pairs(z[:, ATTN_WIDTH:ATTN_WIDTH + KV_WIDTH], cos, sin, first_half)
    v_ref[...] = z[:, ATTN_WIDTH + KV_WIDTH:ATTN_WIDTH + 2 * KV_WIDTH]
    u_ref[...] = z[:, ATTN_WIDTH + 2 * KV_WIDTH:]


def _in_proj(xp, xs, n, g, w_bf16, l, cos_tab, sin_tab, n_prompt, seq, tm):
    n_prompt_tiles = n_prompt // tm
    tiles_per_seq = seq // tm

    def tab_map(i):
        return (jnp.where(i < n_prompt_tiles, i % tiles_per_seq, tiles_per_seq), 0)

    row = lambda w: pl.BlockSpec((tm, w), lambda i: (i, 0))
    return pl.pallas_call(
        functools.partial(_in_proj_kernel, n_prompt_tiles),
        grid=(n // tm,),
        in_specs=[*_two_source_specs(xp, xs, n_prompt, tm, D_MODEL), _layer_spec(g, l), _layer_spec(w_bf16, l),
                  pl.BlockSpec((tm, LANES), tab_map),
                  pl.BlockSpec((tm, LANES), tab_map)],
        out_specs=[row(ATTN_WIDTH), row(KV_WIDTH), row(KV_WIDTH), row(SSM_WIDTH)],
        out_shape=[jax.ShapeDtypeStruct((n, ATTN_WIDTH), BF16),
                   jax.ShapeDtypeStruct((n, KV_WIDTH), F32),
                   jax.ShapeDtypeStruct((n, KV_WIDTH), F32),
                   jax.ShapeDtypeStruct((n, SSM_WIDTH), F32)],
        compiler_params=_cparams("parallel"),
        name="in_proj",
    )(xp, xs, g, w_bf16, cos_tab, sin_tab)


def _kv_pairs(keys, vals):
    lane = lax.broadcasted_iota(jnp.int32, keys.shape, 1)
    low = lane < HEAD_DIM
    k_sw = pltpu.roll(keys, HEAD_DIM, 1)
    v_sw = pltpu.roll(vals, HEAD_DIM, 1)
    kk = [jnp.where(low, keys, k_sw).astype(BF16), jnp.where(low, k_sw, keys).astype(BF16)]
    vv = [jnp.where(low, vals, v_sw).astype(BF16), jnp.where(low, v_sw, vals).astype(BF16)]
    return kk, vv


def _attend_pairs(q, kk, vv, key_rows, mask_add, sink_ref, o_ref, row0):
    tq = q.shape[0]
    qlane = lax.broadcasted_iota(jnp.int32, (tq, LANES), 1)
    qlow = qlane < HEAD_DIM
    row_top = lax.broadcasted_iota(jnp.int32, (2 * tq, 1), 0) < tq
    zero = jnp.zeros((), BF16)
    for pair in range(N_HEADS // 2):
        kv = pair // (N_HEADS // N_KV_HEADS // 2)
        qp = q[:, pair * LANES:(pair + 1) * LANES]
        qs = jnp.concatenate([jnp.where(qlow, qp, zero), jnp.where(qlow, zero, qp)], axis=0)
        s = lax.dot_general(qs, kk[kv][key_rows, :], (((1,), (1,)), ((), ())), preferred_element_type=F32)
        if mask_add is not None:
            s = s + mask_add
        sink = jnp.where(row_top, sink_ref[2 * pair], sink_ref[2 * pair + 1])
        m = jnp.maximum(jnp.max(s, axis=-1, keepdims=True), sink)
        e = jnp.exp(s - m)
        p = e / (jnp.sum(e, axis=-1, keepdims=True) + jnp.exp(sink - m))
        o = jnp.dot(p.astype(BF16), vv[kv][key_rows, :], preferred_element_type=F32)
        o_ref[row0:row0 + tq, pair * LANES:(pair + 1) * LANES] = jnp.where(qlow, o[:tq], o[tq:])


def _attend_blocks(blocks, sink_ref, o_ref):
    tq = blocks[0][0].shape[0]
    qlane = lax.broadcasted_iota(jnp.int32, (tq, LANES), 1)
    qlow = qlane < HEAD_DIM
    row_top = lax.broadcasted_iota(jnp.int32, (2 * tq, 1), 0) < tq
    zero = jnp.zeros((), BF16)
    scores, sinks = [], []
    for q, kk, vv, key_rows, mask_add, row0 in blocks:
        for pair in range(N_HEADS // 2):
            kv = pair // (N_HEADS // N_KV_HEADS // 2)
            qp = q[:, pair * LANES:(pair + 1) * LANES]
            qs = jnp.concatenate([jnp.where(qlow, qp, zero), jnp.where(qlow, zero, qp)], axis=0)
            s = lax.dot_general(qs, kk[kv][key_rows, :], (((1,), (1,)), ((), ())), preferred_element_type=F32)
            scores.append(s if mask_add is None else s + mask_add)
            sinks.append(jnp.where(row_top, sink_ref[2 * pair], sink_ref[2 * pair + 1]))
    s = jnp.concatenate(scores, axis=0)
    sink = jnp.concatenate(sinks, axis=0)
    m = jnp.maximum(jnp.max(s, axis=-1, keepdims=True), sink)
    e = jnp.exp(s - m)
    p = (e / (jnp.sum(e, axis=-1, keepdims=True) + jnp.exp(sink - m))).astype(BF16)
    piece = 0
    for q, kk, vv, key_rows, mask_add, row0 in blocks:
        for pair in range(N_HEADS // 2):
            kv = pair // (N_HEADS // N_KV_HEADS // 2)
            o = jnp.dot(p[piece * 2 * tq:(piece + 1) * 2 * tq, :], vv[kv][key_rows, :], preferred_element_type=F32)
            o_ref[row0:row0 + tq, pair * LANES:(pair + 1) * LANES] = jnp.where(qlow, o[:tq], o[tq:])
            piece += 1


ATTN_SUB = WINDOW
ATTN_TILE = 2 * ATTN_SUB


def _attn_prompt_kernel(sink_ref, ma_ref, mb_ref, q_ref, kp_ref, kc_ref, vp_ref, vc_ref, o_ref):
    kk, vv = _kv_pairs(jnp.concatenate([kp_ref[...], kc_ref[...]], axis=0),
                       jnp.concatenate([vp_ref[...], vc_ref[...]], axis=0))
    for s, m_ref in enumerate((ma_ref, mb_ref)):
        rows = slice(s * ATTN_SUB, s * ATTN_SUB + 2 * WINDOW)
        _attend_pairs(q_ref[s * ATTN_SUB:(s + 1) * ATTN_SUB, :], kk, vv, rows, m_ref[...], sink_ref, o_ref, s * ATTN_SUB)


def _band_masks():
    r = (jnp.arange(2 * ATTN_SUB) % ATTN_SUB)[:, None] // CHUNK
    c = jnp.arange(2 * WINDOW)[None, :]
    band = (c // CHUNK >= r) & (c // CHUNK <= r + WINDOW // CHUNK)
    masks = jnp.stack([band, band & (c >= WINDOW)])
    return jnp.where(masks, 0.0, NEG_INF).astype(F32)


def _attn_prompt(q, k, v, sink, batch, seq):
    nt = seq // ATTN_TILE
    per_seq = seq // ATTN_SUB
    cur = lambda b, i: (b * nt + i, 0)
    prev = lambda b, i: (b * per_seq + jnp.maximum(2 * i - 1, 0), 0)
    masks = _band_masks()
    return pl.pallas_call(
        _attn_prompt_kernel,
        grid=(batch, nt),
        in_specs=[pl.BlockSpec(memory_space=pltpu.SMEM),
                  pl.BlockSpec((None, 2 * ATTN_SUB, 2 * WINDOW), lambda b, i: (jnp.where(i == 0, 1, 0), 0, 0)),
                  pl.BlockSpec((None, 2 * ATTN_SUB, 2 * WINDOW), lambda b, i: (0, 0, 0)),
                  pl.BlockSpec((ATTN_TILE, ATTN_WIDTH), cur),
                  pl.BlockSpec((ATTN_SUB, KV_WIDTH), prev),
                  pl.BlockSpec((ATTN_TILE, KV_WIDTH), cur),
                  pl.BlockSpec((ATTN_SUB, KV_WIDTH), prev),
                  pl.BlockSpec((ATTN_TILE, KV_WIDTH), cur)],
        out_specs=pl.BlockSpec((ATTN_TILE, ATTN_WIDTH), cur),
        out_shape=jax.ShapeDtypeStruct((batch * seq, ATTN_WIDTH), F32),
        compiler_params=_cparams("parallel", "parallel"),
        name="attn_prompt",
    )(sink, masks, masks, q, k, k, v, v)


def _attn_sample_kernel(sink_ref, q_ref, ck_ref, cv_ref, k_ref, v_ref, o_ref):
    nb = ck_ref.shape[0]
    t = q_ref.shape[0] // nb
    blocks = []
    for b in range(nb):
        rows = slice(b * t, (b + 1) * t)
        kk, vv = _kv_pairs(jnp.concatenate([ck_ref[b], k_ref[rows, :]], axis=0),
                           jnp.concatenate([cv_ref[b], v_ref[rows, :]], axis=0))
        blocks.append((q_ref[rows, :], kk, vv, slice(None), None, b * t))
    _attend_blocks(blocks, sink_ref, o_ref)


def _attn_sample(q, k, v, cache_k, cache_v, l, sink, n_prompt, dec_batch, t, nb):
    w = cache_k.shape[2]
    rows = nb * t
    base = n_prompt // rows
    tok = lambda width: pl.BlockSpec((rows, width), lambda i: (base + i, 0))
    cache = pl.BlockSpec((None, nb, w, KV_WIDTH), lambda i: (l, i, 0, 0))
    return pl.pallas_call(
        _attn_sample_kernel,
        grid=(dec_batch // nb,),
        in_specs=[pl.BlockSpec(memory_space=pltpu.SMEM),
                  tok(ATTN_WIDTH), cache, cache, tok(KV_WIDTH), tok(KV_WIDTH)],
        out_specs=pl.BlockSpec((rows, ATTN_WIDTH), lambda i: (i, 0)),
        out_shape=jax.ShapeDtypeStruct((dec_batch * t, ATTN_WIDTH), F32),
        compiler_params=_cparams("parallel"),
        name="attn_sample",
    )(sink, q, cache_k, cache_v, k, v)


SSM_BLOCK = SUBLANES


def _cmul(ar, ai, br, bi):
    return ar * br - ai * bi, ar * bi + ai * br


def _ssm_disc_kernel(lre_ref, lim_ref, dt_ref, bre_ref, bim_ref, cre_ref, cim_ref,
                     we_ref, tv_ref, coef_ref, vt_ref, wb_ref):
    we_ref[...] = jnp.zeros(we_ref.shape, we_ref.dtype)
    vt_ref[...] = jnp.zeros(vt_ref.shape, vt_ref.dtype)
    wb_ref[...] = jnp.zeros(wb_ref.shape, wb_ref.dtype)
    lane = lax.broadcasted_iota(jnp.int32, (SSM_GROUP, LANES), 1)
    half = [lane < SSM_STATE, lane >= SSM_STATE]
    row8 = lax.broadcasted_iota(jnp.int32, (SUBLANES, LANES), 0)
    for q in range(STATE_TILES):
        lre = lre_ref[q]
        lim = lim_ref[q]
        dt = dt_ref[q]
        mag = jnp.exp(lre * dt)
        ang = lim * dt
        lbr = mag * jnp.cos(ang)
        lbi = mag * jnp.sin(ang)
        nr, ni = lbr - 1.0, lbi
        den = lre * lre + lim * lim
        fr = (nr * lre + ni * lim) / den
        fi = (ni * lre - nr * lim) / den
        bbr, bbi = _cmul(fr, fi, bre_ref[q], bim_ref[q])
        cr, ci = cre_ref[q], cim_ref[q]
        pw = [(jnp.ones_like(lbr), jnp.zeros_like(lbr))]
        for _ in range(SSM_BLOCK):
            pw.append(_cmul(pw[-1][0], pw[-1][1], lbr, lbi))
        re_l = slice(q * 2 * LANES, q * 2 * LANES + LANES)
        im_l = slice(q * 2 * LANES + LANES, (q + 1) * 2 * LANES)
        for h in range(2):
            g = 2 * q + h
            grow = lambda blk: slice(blk * LANES + g * SSM_GROUP, blk * LANES + (g + 1) * SSM_GROUP)
            for j in range(SSM_BLOCK):
                wr, wi = _cmul(pw[SSM_BLOCK - 1 - j][0], pw[SSM_BLOCK - 1 - j][1], bbr, bbi)
                we_ref[grow(j), re_l] = jnp.where(half[h], wr, 0.0).astype(we_ref.dtype)
                we_ref[grow(j), im_l] = jnp.where(half[h], wi, 0.0).astype(we_ref.dtype)
            for d in range(SSM_BLOCK + 1):
                xr, xi = _cmul(cr, ci, pw[d][0], pw[d][1])
                vt_ref[grow(d), re_l] = jnp.where(half[h], xr, 0.0)
                vt_ref[grow(d), im_l] = jnp.where(half[h], -xi, 0.0)
            wb_ref[grow(0), re_l] = jnp.where(half[h], bbr, 0.0)
            wb_ref[grow(0), im_l] = jnp.where(half[h], bbi, 0.0)
        l8 = [pw[SSM_BLOCK]]
        for _ in range(SUBLANES - 1):
            l8.append(_cmul(l8[-1][0], l8[-1][1], pw[SSM_BLOCK][0], pw[SSM_BLOCK][1]))
        co = slice(q * LANES, (q + 1) * LANES)
        for kind, sh in enumerate((1, 2, 4)):
            for a in range(2):
                coef_ref[kind, a, :, co] = jnp.where(row8 >= sh, l8[sh - 1][a], 0.0)
        for a in range(2):
            tab = jnp.zeros((SUBLANES, LANES), F32)
            for k in range(SUBLANES):
                tab = jnp.where(row8 == k, l8[k][a], tab)
            coef_ref[3, a, :, co] = tab
    t0 = lax.dot_general(wb_ref[...], vt_ref[0:BLOCK_LANES, :], (((1,), (1,)), ((), ())),
                         preferred_element_type=F32, precision=lax.Precision.HIGHEST)
    for j in range(SSM_BLOCK):
        if j:
            tv_ref[j * LANES:(j + 1) * LANES, 0:j * LANES] = jnp.zeros((LANES, j * LANES), tv_ref.dtype)
        tv_ref[j * LANES:(j + 1) * LANES, j * LANES:] = t0[:, 0:BLOCK_LANES - j * LANES].astype(tv_ref.dtype)
    tv_ref[BLOCK_LANES:, :] = jnp.transpose(vt_ref[LANES:, :]).astype(tv_ref.dtype)


def _ssm_discretize(lam_re, lam_im, log_dt, b_re, b_im, c_re, c_im):
    g, p = lam_re.shape
    npair = g // 2
    pair = lambda a: a.reshape(npair, 1, 2 * p)
    rows = lambda a: a.reshape(npair, 2, SSM_GROUP, p).transpose(0, 2, 1, 3).reshape(npair, SSM_GROUP, 2 * p)
    dt = jnp.repeat(jnp.exp(log_dt), p).reshape(npair, 1, 2 * p)
    vec = pl.BlockSpec((STATE_TILES, 1, LANES), lambda m: (m, 0, 0))
    mat = pl.BlockSpec((STATE_TILES, SSM_GROUP, LANES), lambda m: (m, 0, 0))
    return pl.pallas_call(
        _ssm_disc_kernel,
        grid=(N_CH_BLOCKS,),
        in_specs=[vec, vec, vec, mat, mat, mat, mat],
        out_specs=[pl.BlockSpec((None, BLOCK_LANES, STATE_LANES), lambda m: (m, 0, 0)),
                   pl.BlockSpec((None, BLOCK_LANES + STATE_LANES, BLOCK_LANES), lambda m: (m, 0, 0)),
                   pl.BlockSpec((None, 4, 2, SUBLANES, STATE_LANES // 2), lambda m: (m, 0, 0, 0, 0))],
        out_shape=[jax.ShapeDtypeStruct((N_CH_BLOCKS, BLOCK_LANES, STATE_LANES), BF16),
                   jax.ShapeDtypeStruct((N_CH_BLOCKS, BLOCK_LANES + STATE_LANES, BLOCK_LANES), BF16),
                   jax.ShapeDtypeStruct((N_CH_BLOCKS, 4, 2, SUBLANES, STATE_LANES // 2), F32)],
        scratch_shapes=[pltpu.VMEM(((SSM_BLOCK + 1) * LANES, STATE_LANES), F32),
                        pltpu.VMEM((LANES, STATE_LANES), F32)],
        compiler_params=_cparams("parallel"),
        name="ssm_discretize",
    )(pair(lam_re), pair(lam_im), dt, rows(b_re.transpose(0, 2, 1)), rows(b_im.transpose(0, 2, 1)),
      rows(c_re), rows(c_im))


GROUPS_PER_CH_BLOCK = LANES // SSM_GROUP
N_CH_BLOCKS = SSM_WIDTH // LANES
STATE_LANES = 2 * GROUPS_PER_CH_BLOCK * SSM_STATE
STATE_TILES = STATE_LANES // (2 * LANES)
BLOCK_LANES = SSM_BLOCK * LANES


def _ssm_kernel(nb, u_ref, s0_ref, we_ref, tv_ref, d_ref, coef_ref, y_ref, fin_ref, st_ref, sprev_ref, ucat_ref):
    t_idx = pl.program_id(2)
    r = st_ref.shape[1] - SUBLANES
    rows = nb * r

    @pl.when(t_idx == 0)
    def _():
        for b in range(nb):
            st_ref[b, 0:SUBLANES, :] = jnp.broadcast_to(s0_ref[b], (SUBLANES, STATE_LANES))

    for j in range(SSM_BLOCK):
        ucat_ref[:, j * LANES:(j + 1) * LANES] = u_ref[pl.ds(j, rows, stride=SSM_BLOCK), :].astype(BF16)
    e = jnp.dot(ucat_ref[...], we_ref[...], preferred_element_type=F32)
    st_ref[:, SUBLANES:, :] = e.reshape(nb, r, STATE_LANES)

    first_row = lax.broadcasted_iota(jnp.int32, (SUBLANES, LANES), 0) == 0

    def group(rg, carry):
        r0 = pl.multiple_of(SUBLANES + rg * SUBLANES, SUBLANES)
        rp = pl.multiple_of(rg * SUBLANES, SUBLANES)
        for b in range(nb):
            for q in range(STATE_TILES):
                re_sl = pl.ds(q * 2 * LANES, LANES)
                im_sl = pl.ds(q * 2 * LANES + LANES, LANES)
                co = pl.ds(q * LANES, LANES)
                xr = st_ref[b, pl.ds(r0, SUBLANES), re_sl]
                xi = st_ref[b, pl.ds(r0, SUBLANES), im_sl]
                for step, sh in enumerate((1, 2, 4)):
                    ar, ai = _cmul(coef_ref[step, 0, :, co], coef_ref[step, 1, :, co],
                                   pltpu.roll(xr, sh, 0), pltpu.roll(xi, sh, 0))
                    xr = xr + ar
                    xi = xi + ai
                pr = jnp.broadcast_to(st_ref[b, pl.ds(rp, SUBLANES), re_sl][SUBLANES - 1:, :], (SUBLANES, LANES))
                pi = jnp.broadcast_to(st_ref[b, pl.ds(rp, SUBLANES), im_sl][SUBLANES - 1:, :], (SUBLANES, LANES))
                ar, ai = _cmul(coef_ref[3, 0, :, co], coef_ref[3, 1, :, co], pr, pi)
                xr = xr + ar
                xi = xi + ai
                st_ref[b, pl.ds(r0, SUBLANES), re_sl] = xr
                st_ref[b, pl.ds(r0, SUBLANES), im_sl] = xi
                out_rows = pl.ds(pl.multiple_of(b * r + rg * SUBLANES, SUBLANES), SUBLANES)
                sprev_ref[out_rows, re_sl] = jnp.where(first_row, pr, pltpu.roll(xr, 1, 0))
                sprev_ref[out_rows, im_sl] = jnp.where(first_row, pi, pltpu.roll(xi, 1, 0))
        return carry

    lax.fori_loop(0, r // SUBLANES, group, 0)

    lhs = jnp.concatenate([ucat_ref[...], sprev_ref[...].astype(BF16)], axis=1)
    ycat = jnp.dot(lhs, tv_ref[...], preferred_element_type=F32)
    d = d_ref[...]
    for t in range(SSM_BLOCK):
        tok = pl.ds(t, rows, stride=SSM_BLOCK)
        y_ref[tok, :] = ycat[:, t * LANES:(t + 1) * LANES] + d * u_ref[tok, :]

    for b in range(nb):
        tail = st_ref[b, r:r + SUBLANES, :]
        st_ref[b, 0:SUBLANES, :] = tail
        fin_ref[b] = tail[SUBLANES - 1:SUBLANES, :]


def _ssm(u, s0, we, tv, d, l, coef, row_base, n_seq, seq_len, nb, tt):
    rows = nb * tt
    r = tt // SSM_BLOCK
    nt = seq_len // tt
    base = row_base // rows
    return pl.pallas_call(
        functools.partial(_ssm_kernel, nb),
        grid=(N_CH_BLOCKS, n_seq // nb, nt),
        in_specs=[pl.BlockSpec((rows, LANES), lambda m, i, j: (base + i * nt + j, m)),
                  pl.BlockSpec((nb, 1, STATE_LANES), lambda m, i, j: (i, 0, m)),
                  pl.BlockSpec((None, BLOCK_LANES, STATE_LANES), lambda m, i, j: (m, 0, 0)),
                  pl.BlockSpec((None, BLOCK_LANES + STATE_LANES, BLOCK_LANES), lambda m, i, j: (m, 0, 0)),
                  pl.BlockSpec((None, 1, LANES), lambda m, i, j: (l, 0, m)),
                  pl.BlockSpec((None, 4, 2, SUBLANES, STATE_LANES // 2), lambda m, i, j: (m, 0, 0, 0, 0))],
        out_specs=[pl.BlockSpec((rows, LANES), lambda m, i, j: (i * nt + j, m)),
                   pl.BlockSpec((nb, 1, STATE_LANES), lambda m, i, j: (i, 0, m))],
        out_shape=[jax.ShapeDtypeStruct((n_seq * seq_len, SSM_WIDTH), F32),
                   jax.ShapeDtypeStruct((n_seq, 1, N_CH_BLOCKS * STATE_LANES), F32)],
        scratch_shapes=[pltpu.VMEM((nb, SUBLANES + r, STATE_LANES), F32),
                        pltpu.VMEM((nb * r, STATE_LANES), F32),
                        pltpu.VMEM((nb * r, BLOCK_LANES), BF16)],
        compiler_params=_cparams("parallel", "parallel", "arbitrary"),
        name="ssm_scan",
    )(u, s0, we, tv, d, coef)


def _state_to_tiles(s):
    lead = s.shape[:-3]
    t = s.reshape(lead + (N_CH_BLOCKS, STATE_TILES, 2, SSM_STATE, 2))
    t = jnp.moveaxis(t, -1, -3)
    return t.reshape(lead + (1, N_CH_BLOCKS * STATE_LANES))


def _tiles_to_state(f):
    b = f.shape[0]
    t = f.reshape(b, N_CH_BLOCKS, STATE_TILES, 2, 2, SSM_STATE)
    t = jnp.moveaxis(t, 3, -1)
    return t.reshape(b, N_SSM_GROUPS, SSM_STATE, 2)


def _merge_kernel(n_prompt_tiles, xp_ref, xs_ref, ap_ref, as_ref, yp_ref, ys_ref, wglu_ref, bglu_ref, ga_ref, gs_ref,
                  wout_ref, o_ref):
    is_prompt = pl.program_id(0) < n_prompt_tiles
    attn = jnp.where(is_prompt, ap_ref[...], as_ref[...])
    g = jax.nn.gelu(jnp.where(is_prompt, yp_ref[...], ys_ref[...]))
    glu = g * jax.nn.sigmoid(jnp.dot(g.astype(BF16), wglu_ref[...], preferred_element_type=F32) + bglu_ref[...])
    na = _rms(attn, ga_ref[...]).astype(BF16)
    ns = _rms(glu, gs_ref[...]).astype(BF16)
    o = jnp.dot(na, wout_ref[0:ATTN_WIDTH, :], preferred_element_type=F32)
    o = o + jnp.dot(ns, wout_ref[ATTN_WIDTH:, :], preferred_element_type=F32)
    o_ref[...] = jnp.where(is_prompt, xp_ref[...], xs_ref[...]) + o


def _merge(xp, xs, n, attn_p, attn_s, y_p, y_s, wglu, bglu, ga, gs, wout, l, tm):
    npt = attn_p.shape[0] // tm
    nst = attn_s.shape[0] // tm
    row = lambda w: pl.BlockSpec((tm, w), lambda i: (i, 0))
    prow = lambda w: pl.BlockSpec((tm, w), lambda i: (jnp.minimum(i, npt - 1), 0))
    srow = lambda w: pl.BlockSpec((tm, w), lambda i: (jnp.clip(i - npt, 0, nst - 1), 0))
    return pl.pallas_call(
        functools.partial(_merge_kernel, npt),
        grid=(n // tm,),
        in_specs=[*_two_source_specs(xp, xs, attn_p.shape[0], tm, D_MODEL),
                  prow(ATTN_WIDTH), srow(ATTN_WIDTH), prow(SSM_WIDTH), srow(SSM_WIDTH),
                  _layer_spec(wglu, l), _layer_spec(bglu, l), _layer_spec(ga, l), _layer_spec(gs, l),
                  _layer_spec(wout, l)],
        out_specs=row(D_MODEL),
        out_shape=jax.ShapeDtypeStruct((n, D_MODEL), F32),
        compiler_params=_cparams("parallel"),
        name="merge_heads",
    )(xp, xs, attn_p, attn_s, y_p, y_s, wglu, bglu, ga, gs, wout)


def _mem_kv_kernel(m_ref, g_ref, wk_ref, wv_ref, k_ref, v_ref):
    mn = _rms(m_ref[...], g_ref[...]).astype(BF16)
    k_ref[...] = jnp.dot(mn, wk_ref[...], preferred_element_type=F32)
    v_ref[...] = jnp.dot(mn, wv_ref[...], preferred_element_type=F32)


def _mem_kv(mem, g, wk, wv, l, tm):
    n = mem.shape[0]
    row = pl.BlockSpec((tm, D_MODEL), lambda i: (i, 0))
    return pl.pallas_call(
        _mem_kv_kernel,
        grid=(n // tm,),
        in_specs=[row, _layer_spec(g, l), _layer_spec(wk, l), _layer_spec(wv, l)],
        out_specs=[row, row],
        out_shape=[jax.ShapeDtypeStruct((n, D_MODEL), F32)] * 2,
        compiler_params=_cparams("parallel"),
        name="mem_kv",
    )(mem, g, wk, wv)


def _xattn_kernel(nb, x_ref, g_ref, wq_ref, wo_ref, mk_ref, mv_ref, o_ref, att_ref):
    t = x_ref.shape[0] // nb
    x = x_ref[...]
    hn = _rms(x, g_ref[...]).astype(BF16)
    q = jnp.dot(hn, wq_ref[...], preferred_element_type=F32) * (1.0 / math.sqrt(XHEAD_DIM))
    q = q.astype(BF16)
    heads = [slice(h * XHEAD_DIM, (h + 1) * XHEAD_DIM) for h in range(N_XHEADS)]
    scores = []
    for b in range(nb):
        mk = mk_ref[b * N_MEM:(b + 1) * N_MEM, :].astype(BF16)
        for sl in heads:
            scores.append(lax.dot_general(q[b * t:(b + 1) * t, sl], mk[:, sl], (((1,), (1,)), ((), ())),
                                          preferred_element_type=F32))
    s = jnp.concatenate(scores, axis=0)
    e = jnp.exp(s - jnp.max(s, axis=-1, keepdims=True))
    p = (e / jnp.sum(e, axis=-1, keepdims=True)).astype(BF16)
    for b in range(nb):
        mv = mv_ref[b * N_MEM:(b + 1) * N_MEM, :].astype(BF16)
        for h, sl in enumerate(heads):
            r0 = (b * N_XHEADS + h) * t
            att_ref[b * t:(b + 1) * t, sl] = jnp.dot(p[r0:r0 + t, :], mv[:, sl], preferred_element_type=F32)
    o = jnp.dot(att_ref[...].astype(BF16), wo_ref[...], preferred_element_type=F32)
    o_ref[...] = x + o


def _xattn(x, g, wq, wo, l, mk, mv, mem_spec, row_base, n_rows, nb, tm):
    base = row_base // tm
    xspec = pl.BlockSpec((tm, D_MODEL), lambda i: (base + i, 0))
    return pl.pallas_call(
        functools.partial(_xattn_kernel, nb),
        grid=(n_rows // tm,),
        in_specs=[xspec, _layer_spec(g, l), _layer_spec(wq, l), _layer_spec(wo, l), mem_spec, mem_spec],
        out_specs=xspec,
        out_shape=jax.ShapeDtypeStruct(x.shape, F32),
        scratch_shapes=[pltpu.VMEM((tm, D_MODEL), F32)],
        input_output_aliases={0: 0},
        compiler_params=_cparams("parallel"),
        name="cross_attn",
    )(x, g, wq, wo, mk, mv)


ROUTER_LANES = LANES
EXPERT_LANE0 = N_EXPERT_GROUPS
EXPERTS_PER_STEP = 4
MOE_SRC_TILE = 512
MOE_RUN_ALIGN = 16
MOE_SORTED_ROWS = 640
MOE_TILE = 1024


def _dot_f32_3pass(x, w):
    xh = x.astype(BF16)
    xl = (x - xh.astype(F32)).astype(BF16)
    wh = w.astype(BF16)
    wl = (w - wh.astype(F32)).astype(BF16)
    dot = lambda a, b: jnp.dot(a, b, preferred_element_type=F32)
    return dot(xh, wh) + (dot(xl, wh) + dot(xh, wl))


def _route(logits):
    lane_i = lax.broadcasted_iota(jnp.int32, logits.shape, 1)
    lane = lane_i.astype(F32)
    neg = jnp.float32(-jnp.inf)
    is_g = lane_i < N_EXPERT_GROUPS
    gl = jnp.where(is_g, logits, neg)
    gmax = jnp.max(gl, axis=-1, keepdims=True)
    gidx = jnp.min(jnp.where(gl == gmax, lane, float(ROUTER_LANES)), axis=-1, keepdims=True)
    g_w = 1.0 / jnp.sum(jnp.where(is_g, jnp.exp(gl - gmax), 0.0), axis=-1, keepdims=True)
    first = EXPERT_LANE0 + gidx * EXPERTS_PER_GROUP
    sel = (lane >= first) & (lane < first + EXPERTS_PER_GROUP)
    el = jnp.where(sel, logits, neg)
    m1 = jnp.max(el, axis=-1, keepdims=True)
    i1 = jnp.min(jnp.where(el == m1, lane, float(ROUTER_LANES)), axis=-1, keepdims=True)
    el2 = jnp.where(lane == i1, neg, el)
    m2 = jnp.max(el2, axis=-1, keepdims=True)
    i2 = jnp.min(jnp.where(el2 == m2, lane, float(ROUTER_LANES)), axis=-1, keepdims=True)
    r = jnp.exp(m2 - m1)
    w1 = g_w / (1.0 + r)
    w2 = w1 * r
    return jnp.where(lane == i1, w1, jnp.where(lane == i2, w2, 0.0)), gidx


def _moe_pre_kernel(x_ref, g_ref, wr_ref, br_ref, xn_ref, gate_ref, meta_ref, cnt_ref, tri_ref):
    tm = x_ref.shape[0]

    @pl.when(pl.program_id(0) == 0)
    def _():
        r = lax.broadcasted_iota(jnp.int32, (tm, tm), 0)
        c = lax.broadcasted_iota(jnp.int32, (tm, tm), 1)
        tri_ref[...] = (c <= r).astype(BF16)

    xn = _rms(x_ref[...], g_ref[...])
    xn_ref[...] = xn.astype(BF16)
    gates, gidx = _route(_dot_f32_3pass(xn, wr_ref[...]) + br_ref[...])
    gate_ref[...] = gates
    lane = lax.broadcasted_iota(jnp.int32, gates.shape, 1).astype(F32)
    onehot = lane == gidx
    incl = jnp.dot(tri_ref[...], onehot.astype(BF16), preferred_element_type=F32)
    rank = jnp.sum(jnp.where(onehot, incl, 0.0), axis=-1, keepdims=True) - 1.0
    meta_ref[...] = jnp.where(lane == 0.0, gidx, jnp.where(lane == 1.0, rank, 0.0))
    cnt_ref[...] = incl[tm - 1:tm, :]


def _moe_pre(x, g, wr, br, l):
    n = x.shape[0]
    tm = MOE_SRC_TILE
    row = lambda w: pl.BlockSpec((tm, w), lambda i: (i, 0))
    return pl.pallas_call(
        _moe_pre_kernel,
        grid=(n // tm,),
        in_specs=[row(D_MODEL), _layer_spec(g, l), _layer_spec(wr, l), _layer_spec(br, l)],
        out_specs=[row(D_MODEL), row(ROUTER_LANES), row(ROUTER_LANES),
                   pl.BlockSpec((None, 1, ROUTER_LANES), lambda i: (i, 0, 0))],
        out_shape=[jax.ShapeDtypeStruct((n, D_MODEL), BF16), jax.ShapeDtypeStruct((n, ROUTER_LANES), F32),
                   jax.ShapeDtypeStruct((n, ROUTER_LANES), F32),
                   jax.ShapeDtypeStruct((n // tm, 1, ROUTER_LANES), F32)],
        scratch_shapes=[pltpu.VMEM((tm, tm), BF16)],
        compiler_params=_cparams("arbitrary"),
        name="moe_pre",
    )(x, g, wr, br)


def _sort_onehot(meta, start_ref, t):
    gid = meta[:, 0:1]
    pos = meta[:, 1:2]
    for grp in range(N_EXPERT_GROUPS):
        pos = pos + jnp.where(gid == float(grp), start_ref[t * N_EXPERT_GROUPS + grp].astype(F32), 0.0)
    col = lax.broadcasted_iota(jnp.int32, (meta.shape[0], MOE_SORTED_ROWS), 1).astype(F32)
    return (col == pos).astype(BF16)


def _run_copies(t, start_ref, off_ref, nblk_ref, pairs, sem, to_hbm):
    for grp in range(N_EXPERT_GROUPS):
        k = t * N_EXPERT_GROUPS + grp
        src0 = start_ref[k]
        dst0 = off_ref[k]

        def body(b, carry):
            lo = pl.multiple_of(src0 + b * MOE_RUN_ALIGN, MOE_RUN_ALIGN)
            hi = pl.multiple_of(dst0 + b * MOE_RUN_ALIGN, MOE_RUN_ALIGN)
            for buf, arr in pairs:
                a, h = buf.at[pl.ds(lo, MOE_RUN_ALIGN)], arr.at[pl.ds(hi, MOE_RUN_ALIGN)]
                (pltpu.make_async_copy(a, h, sem) if to_hbm else pltpu.make_async_copy(h, a, sem)).start()
            return carry

        lax.fori_loop(0, nblk_ref[k], body, 0)


def _run_wait(t, nblk_ref, pairs, sem, to_hbm):
    total = nblk_ref[t * N_EXPERT_GROUPS]
    for grp in range(1, N_EXPERT_GROUPS):
        total = total + nblk_ref[t * N_EXPERT_GROUPS + grp]
    rows = total * MOE_RUN_ALIGN

    @pl.when(rows > 0)
    def _():
        for buf, arr in pairs:
            a, h = buf.at[pl.ds(0, rows)], arr.at[pl.ds(0, rows)]
            (pltpu.make_async_copy(a, h, sem) if to_hbm else pltpu.make_async_copy(h, a, sem)).wait()


def _moe_pack_kernel(start_ref, off_ref, nblk_ref, end_ref, xn_ref, gate_ref, meta_ref, xs_hbm, gs_hbm,
                     xbuf, gbuf, zx, zg, sem):
    t = pl.program_id(0)
    onehot = _sort_onehot(meta_ref[...], start_ref, t)
    tn = (((0,), (0,)), ((), ()))
    xbuf[...] = lax.dot_general(onehot, xn_ref[...], tn, preferred_element_type=F32).astype(BF16)
    gates = gate_ref[...]
    gh = gates.astype(BF16)
    gl = (gates - gh.astype(F32)).astype(BF16)
    gbuf[...] = (lax.dot_general(onehot, gh, tn, preferred_element_type=F32)
                 + lax.dot_general(onehot, gl, tn, preferred_element_type=F32))
    pairs = [(xbuf, xs_hbm), (gbuf, gs_hbm)]
    _run_copies(t, start_ref, off_ref, nblk_ref, pairs, sem, True)
    _run_wait(t, nblk_ref, pairs, sem, True)

    @pl.when(t == pl.num_programs(0) - 1)
    def _():
        zx[...] = jnp.zeros(zx.shape, zx.dtype)
        zg[...] = jnp.zeros(zg.shape, zg.dtype)
        copies = []
        for grp in range(N_EXPERT_GROUPS):
            rows = pl.ds(pl.multiple_of(end_ref[grp], MOE_RUN_ALIGN), MOE_TILE)
            copies += [pltpu.make_async_copy(zx, xs_hbm.at[rows], sem), pltpu.make_async_copy(zg, gs_hbm.at[rows], sem)]
        for cp in copies:
            cp.start()
        for cp in copies:
            cp.wait()


def _moe_pack(xn, gates, meta, tabs, n_rows):
    n = xn.shape[0]
    tm = MOE_SRC_TILE
    row = lambda w: pl.BlockSpec((tm, w), lambda i, *_: (i, 0))
    any_spec = pl.BlockSpec(memory_space=pl.ANY)
    grid_spec = pltpu.PrefetchScalarGridSpec(
        num_scalar_prefetch=4, grid=(n // tm,),
        in_specs=[row(D_MODEL), row(ROUTER_LANES), row(ROUTER_LANES)],
        out_specs=[any_spec, any_spec],
        scratch_shapes=[pltpu.VMEM((MOE_SORTED_ROWS, D_MODEL), BF16), pltpu.VMEM((MOE_SORTED_ROWS, ROUTER_LANES), F32),
                        pltpu.VMEM((MOE_TILE, D_MODEL), BF16), pltpu.VMEM((MOE_TILE, ROUTER_LANES), F32),
                        pltpu.SemaphoreType.DMA(())])
    return pl.pallas_call(
        _moe_pack_kernel,
        grid_spec=grid_spec,
        out_shape=[jax.ShapeDtypeStruct((n_rows, D_MODEL), BF16), jax.ShapeDtypeStruct((n_rows, ROUTER_LANES), F32)],
        compiler_params=_cparams("arbitrary"),
        name="moe_pack",
    )(tabs["start"], tabs["off"], tabs["nblk"], tabs["end"], xn, gates, meta)


def _moe_expert_kernel(blk_ref, grp_ref, valid_ref, x_ref, gate_ref, wgu_ref, wd_ref, o_ref):
    i = pl.program_id(0)
    e_step = pl.program_id(1)

    @pl.when(valid_ref[i] > 0)
    def _():
        xn = x_ref[...]
        gates = gate_ref[...]
        lane = lax.broadcasted_iota(jnp.int32, gates.shape, 1)
        first = EXPERT_LANE0 + grp_ref[i] * EXPERTS_PER_GROUP + e_step * EXPERTS_PER_STEP
        hids = []
        for j in range(EXPERTS_PER_STEP):
            h = jnp.dot(xn, wgu_ref[j], preferred_element_type=F32)
            ge = jnp.sum(jnp.where(lane == first + j, gates, 0.0), axis=-1, keepdims=True)
            hids.append((jax.nn.silu(h[:, :EXPERT_FF]) * h[:, EXPERT_FF:] * ge).astype(BF16))
        y = jnp.dot(jnp.concatenate(hids, axis=1), wd_ref[...], preferred_element_type=F32)

        @pl.when(e_step == 0)
        def _():
            o_ref[...] = y

        @pl.when(e_step > 0)
        def _():
            o_ref[...] += y


def _moe_experts(xs, gs, tabs, wgu, wd, l, n_tiles):
    es = EXPERTS_PER_STEP
    steps = EXPERTS_PER_GROUP // es
    row = lambda w: pl.BlockSpec((MOE_TILE, w), lambda i, e, blk, grp, valid: (blk[i], 0))
    grid_spec = pltpu.PrefetchScalarGridSpec(
        num_scalar_prefetch=3, grid=(n_tiles, steps),
        in_specs=[row(D_MODEL), row(ROUTER_LANES),
                  pl.BlockSpec((None, es, D_MODEL, 2 * EXPERT_FF),
                               lambda i, e, blk, grp, valid: (l, grp[i] * steps + e, 0, 0)),
                  pl.BlockSpec((None, es * EXPERT_FF, D_MODEL), lambda i, e, blk, grp, valid: (l, grp[i] * steps + e, 0))],
        out_specs=row(D_MODEL))
    return pl.pallas_call(
        _moe_expert_kernel,
        grid_spec=grid_spec,
        out_shape=jax.ShapeDtypeStruct((xs.shape[0], D_MODEL), F32),
        compiler_params=_cparams("arbitrary", "arbitrary"),
        name="hier_moe",
    )(tabs["tile_blk"], tabs["tile_grp"], tabs["tile_valid"], xs, gs, wgu, wd)


def _moe_unpack_kernel(start_ref, off_ref, nblk_ref, x_ref, meta_ref, ys_hbm, o_ref, ybuf, sem):
    t = pl.program_id(0)

    @pl.when(t == 0)
    def _():
        ybuf[...] = jnp.zeros(ybuf.shape, ybuf.dtype)

    pairs = [(ybuf, ys_hbm)]
    _run_copies(t, start_ref, off_ref, nblk_ref, pairs, sem, False)
    onehot = _sort_onehot(meta_ref[...], start_ref, t)
    _run_wait(t, nblk_ref, pairs, sem, False)
    y = ybuf[...]
    yh = y.astype(BF16)
    yl = (y - yh.astype(F32)).astype(BF16)
    o_ref[...] = x_ref[...] + (jnp.dot(onehot, yh, preferred_element_type=F32)
                               + jnp.dot(onehot, yl, preferred_element_type=F32))


def _moe_unpack(x, meta, ys, tabs):
    n = x.shape[0]
    tm = MOE_SRC_TILE
    row = lambda w: pl.BlockSpec((tm, w), lambda i, *_: (i, 0))
    grid_spec = pltpu.PrefetchScalarGridSpec(
        num_scalar_prefetch=3, grid=(n // tm,),
        in_specs=[row(D_MODEL), row(ROUTER_LANES), pl.BlockSpec(memory_space=pl.ANY)],
        out_specs=row(D_MODEL),
        scratch_shapes=[pltpu.VMEM((MOE_SORTED_ROWS, D_MODEL), F32), pltpu.SemaphoreType.DMA(())])
    return pl.pallas_call(
        _moe_unpack_kernel,
        grid_spec=grid_spec,
        out_shape=jax.ShapeDtypeStruct((n, D_MODEL), F32),
        compiler_params=_cparams("arbitrary"),
        name="moe_unpack",
    )(tabs["start"], tabs["off"], tabs["nblk"], x, meta, ys)


def _moe_tables(cnt, n):
    n_src = cnt.shape[0]
    pad = (cnt + MOE_RUN_ALIGN - 1) // MOE_RUN_ALIGN * MOE_RUN_ALIGN
    worst = n + n_src * (MOE_RUN_ALIGN - 1)
    cap = (worst + 2 * MOE_TILE - 1) // MOE_TILE * MOE_TILE
    start = jnp.cumsum(pad, axis=1) - pad
    total = jnp.sum(pad, axis=0)
    base = jnp.arange(N_EXPERT_GROUPS, dtype=jnp.int32) * cap
    off = base[None, :] + jnp.cumsum(pad, axis=0) - pad
    tiles_g = (total + MOE_TILE - 1) // MOE_TILE
    tile_end = jnp.cumsum(tiles_g)
    n_tiles = n // MOE_TILE + N_EXPERT_GROUPS + (n_src * N_EXPERT_GROUPS * MOE_RUN_ALIGN + MOE_TILE - 1) // MOE_TILE
    i = jnp.minimum(jnp.arange(n_tiles, dtype=jnp.int32), jnp.maximum(tile_end[-1] - 1, 0))
    grp = jnp.minimum(jnp.sum((i[:, None] >= tile_end[None, :]).astype(jnp.int32), axis=1), N_EXPERT_GROUPS - 1)
    first_tile = (tile_end - tiles_g)
    blk = jnp.zeros_like(i)
    for g in range(N_EXPERT_GROUPS):
        blk = blk + jnp.where(grp == g, g * (cap // MOE_TILE) + i - first_tile[g], 0)
    i32 = lambda a: a.astype(jnp.int32)
    tabs = dict(start=i32(start.reshape(-1)), off=i32(off.reshape(-1)), nblk=i32((pad // MOE_RUN_ALIGN).reshape(-1)),
                end=i32(base + total), tile_blk=i32(blk), tile_grp=i32(grp),
                tile_valid=i32(jnp.arange(n_tiles) < tile_end[-1]))
    return tabs, N_EXPERT_GROUPS * cap, n_tiles


def _moe(x, g, wr, br, wgu, wd, l):
    n = x.shape[0]
    xn, gates, meta, cnt = _moe_pre(x, g, wr, br, l)
    tabs, n_rows, n_tiles = _moe_tables(cnt[:, 0, :N_EXPERT_GROUPS].astype(jnp.int32), n)
    xs, gs = _moe_pack(xn, gates, meta, tabs, n_rows)
    ys = _moe_experts(xs, gs, tabs, wgu, wd, l, n_tiles)
    return _moe_unpack(x, meta, ys, tabs)


def _final_norm_kernel(x_ref, g_ref, o_ref):
    o_ref[...] = _rms(x_ref[...], g_ref[...])


def _final_norm(x, g, row_base, n_rows, tm):
    base = row_base // tm
    return pl.pallas_call(
        _final_norm_kernel,
        grid=(n_rows // tm,),
        in_specs=[pl.BlockSpec((tm, D_MODEL), lambda i: (base + i, 0)),
                  pl.BlockSpec((1, D_MODEL), lambda i: (0, 0))],
        out_specs=pl.BlockSpec((tm, D_MODEL), lambda i: (i, 0)),
        out_shape=jax.ShapeDtypeStruct((n_rows, D_MODEL), F32),
        compiler_params=_cparams("parallel"),
        name="final_norm",
    )(x, g)


def _rope_tables(seq, t_len, tm):
    half = HEAD_DIM // 2
    inv = ROPE_THETA ** (-jnp.arange(half, dtype=F32) / half)
    pos_s = PAST_LEN + jnp.arange(t_len)
    pos = jnp.concatenate([jnp.arange(seq), jnp.tile(pos_s, tm // t_len)]).astype(F32)
    ang = pos[:, None] * inv[None, :]
    cos = jnp.tile(jnp.cos(ang), (1, LANES // half))
    sign = jnp.where((jnp.arange(LANES) % HEAD_DIM) < half, -1.0, 1.0).astype(F32)
    sin = jnp.tile(jnp.sin(ang), (1, LANES // half)) * sign[None, :]
    return cos, sin


def kernel(x_prompt, x_sample, cache_win_k, cache_win_v, state_ssm, cache_mem_k, cache_mem_v, mem_prompt, w_in, attn_sink, lam_re, lam_im, log_dt, ssm_b_re, ssm_b_im, ssm_c_re, ssm_c_im, ssm_d, w_glu, b_glu, g_attn_out, g_ssm_out, w_out, g_mix, g_xattn, g_mem, wq_x, wk_x, wv_x, wo_x, g_ffn, w_group, b_group, w_router, b_router, w_gate, w_up, w_down, g_final):
    batch, seq, _ = x_prompt.shape
    dec_batch, t_len, _ = x_sample.shape
    depth = w_in.shape[0]
    win_rows = cache_win_k.shape[2]
    n_p = batch * seq
    n_s = dec_batch * t_len
    tm_wide = 1024 if (n_p + n_s) % 1024 == 0 else 512
    tm = tm_wide
    tm_x = 512
    sample_nb = tm_x // t_len

    n = n_p + n_s
    xp = x_prompt.reshape(n_p, D_MODEL)
    xs = x_sample.reshape(n_s, D_MODEL)
    cos_tab, sin_tab = _rope_tables(seq, t_len, tm)
    mem_flat = mem_prompt.reshape(batch * N_MEM, D_MODEL)
    zero_state = jnp.zeros((batch, 1, 2 * N_STATE), F32)
    vec = lambda a: a.reshape(depth, 1, a.shape[-1])

    w_in_b, w_glu_b, w_out_b = w_in.astype(BF16), w_glu.astype(BF16), w_out.astype(BF16)
    wq_b, wk_b, wv_b, wo_b = (w.astype(BF16) for w in (wq_x, wk_x, wv_x, wo_x))
    wgu_b = jnp.concatenate([w_gate, w_up], axis=-1).astype(BF16)
    wd_b = w_down.reshape(depth, N_EXPERTS * EXPERT_FF, D_MODEL).astype(BF16)
    wr = jnp.concatenate([w_group, w_router.transpose(0, 2, 1, 3).reshape(depth, D_MODEL, N_EXPERTS)], axis=-1)
    wr = jnp.pad(wr, ((0, 0), (0, 0), (0, ROUTER_LANES - wr.shape[-1])))
    br = jnp.concatenate([b_group, b_router.reshape(depth, N_EXPERTS)], axis=-1)
    br = jnp.pad(br, ((0, 0), (0, ROUTER_LANES - br.shape[-1]))).reshape(depth, 1, ROUTER_LANES)
    g_mix_r, g_xattn_r, g_mem_r, g_ffn_r = vec(g_mix), vec(g_xattn), vec(g_mem), vec(g_ffn)
    g_a_r, g_s_r, b_glu_r, ssm_d_r = vec(g_attn_out), vec(g_ssm_out), vec(b_glu), vec(ssm_d)
    cache_k = cache_win_k.reshape(depth, dec_batch, win_rows, KV_WIDTH)
    cache_v = cache_win_v.reshape(depth, dec_batch, win_rows, KV_WIDTH)
    cmem_k = cache_mem_k.reshape(depth, dec_batch * N_MEM, D_MODEL)
    cmem_v = cache_mem_v.reshape(depth, dec_batch * N_MEM, D_MODEL)
    state_in = _state_to_tiles(state_ssm)

    outs = {k: [] for k in ("wk_p", "wv_p", "ssm_p", "mk_p", "mv_p", "wk_s", "wv_s", "ssm_s")}
    for l in range(depth):
        q, k, v, u = _in_proj(xp, xs, n, g_mix_r, w_in_b, l, cos_tab, sin_tab, n_p, seq, tm)
        attn_p = _attn_prompt(q, k, v, attn_sink[l], batch, seq)
        attn_s = _attn_sample(q, k, v, cache_k, cache_v, l, attn_sink[l], n_p, dec_batch, t_len, 4)
        tail = lambda a: jnp.stack([a[(b + 1) * seq - WINDOW:(b + 1) * seq] for b in range(batch)])
        outs["wk_p"].append(tail(k).reshape(batch, WINDOW, N_KV_HEADS, HEAD_DIM))
        outs["wv_p"].append(tail(v).reshape(batch, WINDOW, N_KV_HEADS, HEAD_DIM))
        ks = k[n_p:].reshape(dec_batch, t_len, KV_WIDTH)
        vs = v[n_p:].reshape(dec_batch, t_len, KV_WIDTH)
        k_all = jnp.concatenate([cache_k[l], ks], axis=1)[:, -win_rows:]
        v_all = jnp.concatenate([cache_v[l], vs], axis=1)[:, -win_rows:]
        outs["wk_s"].append(k_all.reshape(dec_batch, win_rows, N_KV_HEADS, HEAD_DIM))
        outs["wv_s"].append(v_all.reshape(dec_batch, win_rows, N_KV_HEADS, HEAD_DIM))

        we, tv, coef = _ssm_discretize(lam_re[l], lam_im[l], log_dt[l], ssm_b_re[l], ssm_b_im[l],
                                       ssm_c_re[l], ssm_c_im[l])
        y_p, fin_p = _ssm(u, zero_state, we, tv, ssm_d_r, l, coef, 0, batch, seq, 1, seq)
        y_s, fin_s = _ssm(u, state_in[l], we, tv, ssm_d_r, l, coef, n_p, dec_batch, t_len, dec_batch, t_len)
        outs["ssm_p"].append(_tiles_to_state(fin_p))
        outs["ssm_s"].append(_tiles_to_state(fin_s))
        x = _merge(xp, xs, n, attn_p, attn_s, y_p, y_s, w_glu_b, b_glu_r, g_a_r, g_s_r, w_out_b, l, tm)

        mk_p, mv_p = _mem_kv(mem_flat, g_mem_r, wk_b, wv_b, l, 512)
        outs["mk_p"].append(mk_p.reshape(batch, N_MEM, N_XHEADS, XHEAD_DIM))
        outs["mv_p"].append(mv_p.reshape(batch, N_MEM, N_XHEADS, XHEAD_DIM))
        tiles_per_seq = seq // tm
        x = _xattn(x, g_xattn_r, wq_b, wo_b, l, mk_p, mv_p,
                   pl.BlockSpec((N_MEM, D_MODEL), lambda i: (i // tiles_per_seq, 0)), 0, n_p, 1, tm)
        x = _xattn(x, g_xattn_r, wq_b, wo_b, l, cmem_k, cmem_v,
                   pl.BlockSpec((None, sample_nb * N_MEM, D_MODEL), lambda i: (l, i, 0)), n_p, n_s, sample_nb, tm_x)

        x = _moe(x, g_ffn_r, wr, br, wgu_b, wd_b, l)
        xp = xs = x

    gf = g_final.reshape(1, D_MODEL)
    y_p = _final_norm(x, gf, 0, n_p, tm)
    y_s = _final_norm(x, gf, n_p, n_s, tm)
    st = lambda name: jnp.stack(outs[name], axis=0)
    return (y_p.reshape(batch, seq, D_MODEL), y_s.reshape(dec_batch, t_len, D_MODEL),
            st("wk_p"), st("wv_p"), st("ssm_p"), st("mk_p"), st("mv_p"), st("wk_s"), st("wv_s"), st("ssm_s"))
```

```python
import functools
import math

import jax
import jax.numpy as jnp
from jax import lax
from jax.experimental import pallas as pl
from jax.experimental.pallas import tpu as pltpu

F32 = jnp.float32
BF16 = jnp.bfloat16

D_MODEL = 1024
CHUNK = 64
EPS = 1e-6
NEG_INF = -1e30
N_HEADS = 8
N_KV_HEADS = 2
HEAD_DIM = 64
ATTN_WIDTH = N_HEADS * HEAD_DIM
KV_WIDTH = N_KV_HEADS * HEAD_DIM
WINDOW = 128
ROPE_THETA = 10000.0
SSM_WIDTH = D_MODEL - ATTN_WIDTH
SSM_GROUP = 16
N_SSM_GROUPS = SSM_WIDTH // SSM_GROUP
SSM_STATE = 64
N_STATE = N_SSM_GROUPS * SSM_STATE
IN_WIDTH = ATTN_WIDTH + 2 * KV_WIDTH + SSM_WIDTH
N_MEM = 256
N_XHEADS = 4
XHEAD_DIM = D_MODEL // N_XHEADS
N_EXPERT_GROUPS = 4
EXPERTS_PER_GROUP = 8
N_EXPERTS = N_EXPERT_GROUPS * EXPERTS_PER_GROUP
EXPERT_FF = 128
PAST_LEN = 4096

LANES = 128
SUBLANES = 8
VMEM_LIMIT = 56 * 1024 * 1024


def _cparams(*sem):
    return pltpu.CompilerParams(dimension_semantics=sem, vmem_limit_bytes=VMEM_LIMIT)


def _rms(x, g):
    return x * lax.rsqrt(jnp.mean(x * x, axis=-1, keepdims=True) + EPS) * g


def _layer_spec(arr, l):
    shape = arr.shape[1:]
    zeros = (0,) * len(shape)
    return pl.BlockSpec((None,) + shape, lambda *_: (l,) + zeros, pipeline_mode=pl.Buffered(1))


def _rope_pairs(t, cos, sin_signed, first_half):
    swapped = jnp.where(first_half, pltpu.roll(t, LANES - HEAD_DIM // 2, 1), pltpu.roll(t, HEAD_DIM // 2, 1))
    return t * cos + swapped * sin_signed


def _two_source_specs(xp, xs, n_prompt, tm, width):
    npt = n_prompt // tm
    s_off = 0 if xs is xp else npt
    s_last = xs.shape[0] // tm - 1
    pspec = pl.BlockSpec((tm, width), lambda i, *_: (jnp.minimum(i, npt - 1), 0))
    sspec = pl.BlockSpec((tm, width), lambda i, *_: (jnp.clip(i - s_off, npt - s_off, s_last), 0))
    return pspec, sspec


def _in_proj_kernel(n_prompt_tiles, xp_ref, xs_ref, g_ref, w_ref, cos_ref, sin_ref, q_ref, k_ref, v_ref, u_ref):
    x = jnp.where(pl.program_id(0) < n_prompt_tiles, xp_ref[...], xs_ref[...])
    xn = _rms(x, g_ref[...])
    z = jnp.dot(xn.astype(BF16), w_ref[...], preferred_element_type=F32)
    cos = cos_ref[...]
    sin = sin_ref[...]
    lane = lax.broadcasted_iota(jnp.int32, cos.shape, 1)
    first_half = (lane % HEAD_DIM) < (HEAD_DIM // 2)
    scale = 1.0 / math.sqrt(HEAD_DIM)
    for j in range(ATTN_WIDTH // LANES):
        t = z[:, j * LANES:(j + 1) * LANES]
        q_ref[:, j * LANES:(j + 1) * LANES] = (_rope_pairs(t, cos, sin, first_half) * scale).astype(BF16)
    k_ref[...] = _rope_pairs(z[:, ATTN_WIDTH:ATTN_WIDTH + KV_WIDTH], cos, sin, first_half)
    v_ref[...] = z[:, ATTN_WIDTH + KV_WIDTH:ATTN_WIDTH + 2 * KV_WIDTH]
    u_ref[...] = z[:, ATTN_WIDTH + 2 * KV_WIDTH:]


def _in_proj(xp, xs, n, g, w_bf16, l, cos_tab, sin_tab, n_prompt, seq, tm):
    n_prompt_tiles = n_prompt // tm
    tiles_per_seq = seq // tm

    def tab_map(i):
        return (jnp.where(i < n_prompt_tiles, i % tiles_per_seq, tiles_per_seq), 0)

    row = lambda w: pl.BlockSpec((tm, w), lambda i: (i, 0))
    return pl.pallas_call(
        functools.partial(_in_proj_kernel, n_prompt_tiles),
        grid=(n // tm,),
        in_specs=[*_two_source_specs(xp, xs, n_prompt, tm, D_MODEL), _layer_spec(g, l), _layer_spec(w_bf16, l),
                  pl.BlockSpec((tm, LANES), tab_map),
                  pl.BlockSpec((tm, LANES), tab_map)],
        out_specs=[row(ATTN_WIDTH), row(KV_WIDTH), row(KV_WIDTH), row(SSM_WIDTH)],
        out_shape=[jax.ShapeDtypeStruct((n, ATTN_WIDTH), BF16),
                   jax.ShapeDtypeStruct((n, KV_WIDTH), F32),
                   jax.ShapeDtypeStruct((n, KV_WIDTH), F32),
                   jax.ShapeDtypeStruct((n, SSM_WIDTH), F32)],
        compiler_params=_cparams("parallel"),
        name="in_proj",
    )(xp, xs, g, w_bf16, cos_tab, sin_tab)


def _kv_pairs(keys, vals):
    lane = lax.broadcasted_iota(jnp.int32, keys.shape, 1)
    low = lane < HEAD_DIM
    k_sw = pltpu.roll(keys, HEAD_DIM, 1)
    v_sw = pltpu.roll(vals, HEAD_DIM, 1)
    kk = [jnp.where(low, keys, k_sw).astype(BF16), jnp.where(low, k_sw, keys).astype(BF16)]
    vv = [jnp.where(low, vals, v_sw).astype(BF16), jnp.where(low, v_sw, vals).astype(BF16)]
    return kk, vv


def _attend_pairs(q, kk, vv, key_rows, mask_add, sink_ref, o_ref, row0):
    tq = q.shape[0]
    qlane = lax.broadcasted_iota(jnp.int32, (tq, LANES), 1)
    qlow = qlane < HEAD_DIM
    row_top = lax.broadcasted_iota(jnp.int32, (2 * tq, 1), 0) < tq
    zero = jnp.zeros((), BF16)
    for pair in range(N_HEADS // 2):
        kv = pair // (N_HEADS // N_KV_HEADS // 2)
        qp = q[:, pair * LANES:(pair + 1) * LANES]
        qs = jnp.concatenate([jnp.where(qlow, qp, zero), jnp.where(qlow, zero, qp)], axis=0)
        s = lax.dot_general(qs, kk[kv][key_rows, :], (((1,), (1,)), ((), ())), preferred_element_type=F32)
        if mask_add is not None:
            s = s + mask_add
        sink = jnp.where(row_top, sink_ref[2 * pair], sink_ref[2 * pair + 1])
        m = jnp.maximum(jnp.max(s, axis=-1, keepdims=True), sink)
        e = jnp.exp(s - m)
        p = e / (jnp.sum(e, axis=-1, keepdims=True) + jnp.exp(sink - m))
        o = jnp.dot(p.astype(BF16), vv[kv][key_rows, :], preferred_element_type=F32)
        o_ref[row0:row0 + tq, pair * LANES:(pair + 1) * LANES] = jnp.where(qlow, o[:tq], o[tq:])


def _attend_blocks(blocks, sink_ref, o_ref):
    tq = blocks[0][0].shape[0]
    qlane = lax.broadcasted_iota(jnp.int32, (tq, LANES), 1)
    qlow = qlane < HEAD_DIM
    row_top = lax.broadcasted_iota(jnp.int32, (2 * tq, 1), 0) < tq
    zero = jnp.zeros((), BF16)
    scores, sinks = [], []
    for q, kk, vv, key_rows, mask_add, row0 in blocks:
        for pair in range(N_HEADS // 2):
            kv = pair // (N_HEADS // N_KV_HEADS // 2)
            qp = q[:, pair * LANES:(pair + 1) * LANES]
            qs = jnp.concatenate([jnp.where(qlow, qp, zero), jnp.where(qlow, zero, qp)], axis=0)
            s = lax.dot_general(qs, kk[kv][key_rows, :], (((1,), (1,)), ((), ())), preferred_element_type=F32)
            scores.append(s if mask_add is None else s + mask_add)
            sinks.append(jnp.where(row_top, sink_ref[2 * pair], sink_ref[2 * pair + 1]))
    s = jnp.concatenate(scores, axis=0)
    sink = jnp.concatenate(sinks, axis=0)
    m = jnp.maximum(jnp.max(s, axis=-1, keepdims=True), sink)
    e = jnp.exp(s - m)
    p = (e / (jnp.sum(e, axis=-1, keepdims=True) + jnp.exp(sink - m))).astype(BF16)
    piece = 0
    for q, kk, vv, key_rows, mask_add, row0 in blocks:
        for pair in range(N_HEADS // 2):
            kv = pair // (N_HEADS // N_KV_HEADS // 2)
            o = jnp.dot(p[piece * 2 * tq:(piece + 1) * 2 * tq, :], vv[kv][key_rows, :], preferred_element_type=F32)
            o_ref[row0:row0 + tq, pair * LANES:(pair + 1) * LANES] = jnp.where(qlow, o[:tq], o[tq:])
            piece += 1


ATTN_SUB = WINDOW
ATTN_TILE = 2 * ATTN_SUB


def _attn_prompt_kernel(sink_ref, ma_ref, mb_ref, q_ref, kp_ref, kc_ref, vp_ref, vc_ref, o_ref):
    kk, vv = _kv_pairs(jnp.concatenate([kp_ref[...], kc_ref[...]], axis=0),
                       jnp.concatenate([vp_ref[...], vc_ref[...]], axis=0))
    for s, m_ref in enumerate((ma_ref, mb_ref)):
        rows = slice(s * ATTN_SUB, s * ATTN_SUB + 2 * WINDOW)
        _attend_pairs(q_ref[s * ATTN_SUB:(s + 1) * ATTN_SUB, :], kk, vv, rows, m_ref[...], sink_ref, o_ref, s * ATTN_SUB)


def _band_masks():
    r = (jnp.arange(2 * ATTN_SUB) % ATTN_SUB)[:, None] // CHUNK
    c = jnp.arange(2 * WINDOW)[None, :]
    band = (c // CHUNK >= r) & (c // CHUNK <= r + WINDOW // CHUNK)
    masks = jnp.stack([band, band & (c >= WINDOW)])
    return jnp.where(masks, 0.0, NEG_INF).astype(F32)


def _attn_prompt(q, k, v, sink, batch, seq):
    nt = seq // ATTN_TILE
    per_seq = seq // ATTN_SUB
    cur = lambda b, i: (b * nt + i, 0)
    prev = lambda b, i: (b * per_seq + jnp.maximum(2 * i - 1, 0), 0)
    masks = _band_masks()
    return pl.pallas_call(
        _attn_prompt_kernel,
        grid=(batch, nt),
        in_specs=[pl.BlockSpec(memory_space=pltpu.SMEM),
                  pl.BlockSpec((None, 2 * ATTN_SUB, 2 * WINDOW), lambda b, i: (jnp.where(i == 0, 1, 0), 0, 0)),
                  pl.BlockSpec((None, 2 * ATTN_SUB, 2 * WINDOW), lambda b, i: (0, 0, 0)),
                  pl.BlockSpec((ATTN_TILE, ATTN_WIDTH), cur),
                  pl.BlockSpec((ATTN_SUB, KV_WIDTH), prev),
                  pl.BlockSpec((ATTN_TILE, KV_WIDTH), cur),
                  pl.BlockSpec((ATTN_SUB, KV_WIDTH), prev),
                  pl.BlockSpec((ATTN_TILE, KV_WIDTH), cur)],
        out_specs=pl.BlockSpec((ATTN_TILE, ATTN_WIDTH), cur),
        out_shape=jax.ShapeDtypeStruct((batch * seq, ATTN_WIDTH), F32),
        compiler_params=_cparams("parallel", "parallel"),
        name="attn_prompt",
    )(sink, masks, masks, q, k, k, v, v)


def _attn_sample_kernel(sink_ref, q_ref, ck_ref, cv_ref, k_ref, v_ref, o_ref):
    nb = ck_ref.shape[0]
    t = q_ref.shape[0] // nb
    blocks = []
    for b in range(nb):
        rows = slice(b * t, (b + 1) * t)
        kk, vv = _kv_pairs(jnp.concatenate([ck_ref[b], k_ref[rows, :]], axis=0),
                           jnp.concatenate([cv_ref[b], v_ref[rows, :]], axis=0))
        blocks.append((q_ref[rows, :], kk, vv, slice(None), None, b * t))
    _attend_blocks(blocks, sink_ref, o_ref)


def _attn_sample(q, k, v, cache_k, cache_v, l, sink, n_prompt, dec_batch, t, nb):
    w = cache_k.shape[2]
    rows = nb * t
    base = n_prompt // rows
    tok = lambda width: pl.BlockSpec((rows, width), lambda i: (base + i, 0))
    cache = pl.BlockSpec((None, nb, w, KV_WIDTH), lambda i: (l, i, 0, 0))
    return pl.pallas_call(
        _attn_sample_kernel,
        grid=(dec_batch // nb,),
        in_specs=[pl.BlockSpec(memory_space=pltpu.SMEM),
                  tok(ATTN_WIDTH), cache, cache, tok(KV_WIDTH), tok(KV_WIDTH)],
        out_specs=pl.BlockSpec((rows, ATTN_WIDTH), lambda i: (i, 0)),
        out_shape=jax.ShapeDtypeStruct((dec_batch * t, ATTN_WIDTH), F32),
        compiler_params=_cparams("parallel"),
        name="attn_sample",
    )(sink, q, cache_k, cache_v, k, v)


SSM_BLOCK = SUBLANES


def _cmul(ar, ai, br, bi):
    return ar * br - ai * bi, ar * bi + ai * br


def _ssm_disc_kernel(lre_ref, lim_ref, dt_ref, bre_ref, bim_ref, cre_ref, cim_ref,
                     we_ref, tv_ref, coef_ref, vt_ref, wb_ref):
    we_ref[...] = jnp.zeros(we_ref.shape, we_ref.dtype)
    vt_ref[...] = jnp.zeros(vt_ref.shape, vt_ref.dtype)
    wb_ref[...] = jnp.zeros(wb_ref.shape, wb_ref.dtype)
    lane = lax.broadcasted_iota(jnp.int32, (SSM_GROUP, LANES), 1)
    half = [lane < SSM_STATE, lane >= SSM_STATE]
    row8 = lax.broadcasted_iota(jnp.int32, (SUBLANES, LANES), 0)
    for q in range(STATE_TILES):
        lre = lre_ref[q]
        lim = lim_ref[q]
        dt = dt_ref[q]
        mag = jnp.exp(lre * dt)
        ang = lim * dt
        lbr = mag * jnp.cos(ang)
        lbi = mag * jnp.sin(ang)
        nr, ni = lbr - 1.0, lbi
        den = lre * lre + lim * lim
        fr = (nr * lre + ni * lim) / den
        fi = (ni * lre - nr * lim) / den
        bbr, bbi = _cmul(fr, fi, bre_ref[q], bim_ref[q])
        cr, ci = cre_ref[q], cim_ref[q]
        pw = [(jnp.ones_like(lbr), jnp.zeros_like(lbr))]
        for _ in range(SSM_BLOCK):
            pw.append(_cmul(pw[-1][0], pw[-1][1], lbr, lbi))
        re_l = slice(q * 2 * LANES, q * 2 * LANES + LANES)
        im_l = slice(q * 2 * LANES + LANES, (q + 1) * 2 * LANES)
        for h in range(2):
            g = 2 * q + h
            grow = lambda blk: slice(blk * LANES + g * SSM_GROUP, blk * LANES + (g + 1) * SSM_GROUP)
            for j in range(SSM_BLOCK):
                wr, wi = _cmul(pw[SSM_BLOCK - 1 - j][0], pw[SSM_BLOCK - 1 - j][1], bbr, bbi)
                we_ref[grow(j), re_l] = jnp.where(half[h], wr, 0.0).astype(we_ref.dtype)
                we_ref[grow(j), im_l] = jnp.where(half[h], wi, 0.0).astype(we_ref.dtype)
            for d in range(SSM_BLOCK + 1):
                xr, xi = _cmul(cr, ci, pw[d][0], pw[d][1])
                vt_ref[grow(d), re_l] = jnp.where(half[h], xr, 0.0)
                vt_ref[grow(d), im_l] = jnp.where(half[h], -xi, 0.0)
            wb_ref[grow(0), re_l] = jnp.where(half[h], bbr, 0.0)
            wb_ref[grow(0), im_l] = jnp.where(half[h], bbi, 0.0)
        l8 = [pw[SSM_BLOCK]]
        for _ in range(SUBLANES - 1):
            l8.append(_cmul(l8[-1][0], l8[-1][1], pw[SSM_BLOCK][0], pw[SSM_BLOCK][1]))
        co = slice(q * LANES, (q + 1) * LANES)
        for kind, sh in enumerate((1, 2, 4)):
            for a in range(2):
                coef_ref[kind, a, :, co] = jnp.where(row8 >= sh, l8[sh - 1][a], 0.0)
        for a in range(2):
            tab = jnp.zeros((SUBLANES, LANES), F32)
            for k in range(SUBLANES):
                tab = jnp.where(row8 == k, l8[k][a], tab)
            coef_ref[3, a, :, co] = tab
    t0 = lax.dot_general(wb_ref[...], vt_ref[0:BLOCK_LANES, :], (((1,), (1,)), ((), ())),
                         preferred_element_type=F32, precision=lax.Precision.HIGHEST)
    for j in range(SSM_BLOCK):
        if j:
            tv_ref[j * LANES:(j + 1) * LANES, 0:j * LANES] = jnp.zeros((LANES, j * LANES), tv_ref.dtype)
        tv_ref[j * LANES:(j + 1) * LANES, j * LANES:] = t0[:, 0:BLOCK_LANES - j * LANES].astype(tv_ref.dtype)
    tv_ref[BLOCK_LANES:, :] = jnp.transpose(vt_ref[LANES:, :]).astype(tv_ref.dtype)


def _ssm_discretize(lam_re, lam_im, log_dt, b_re, b_im, c_re, c_im):
    g, p = lam_re.shape
    npair = g // 2
    pair = lambda a: a.reshape(npair, 1, 2 * p)
    rows = lambda a: a.reshape(npair, 2, SSM_GROUP, p).transpose(0, 2, 1, 3).reshape(npair, SSM_GROUP, 2 * p)
    dt = jnp.repeat(jnp.exp(log_dt), p).reshape(npair, 1, 2 * p)
    vec = pl.BlockSpec((STATE_TILES, 1, LANES), lambda m: (m, 0, 0))
    mat = pl.BlockSpec((STATE_TILES, SSM_GROUP, LANES), lambda m: (m, 0, 0))
    return pl.pallas_call(
        _ssm_disc_kernel,
        grid=(N_CH_BLOCKS,),
        in_specs=[vec, vec, vec, mat, mat, mat, mat],
        out_specs=[pl.BlockSpec((None, BLOCK_LANES, STATE_LANES), lambda m: (m, 0, 0)),
                   pl.BlockSpec((None, BLOCK_LANES + STATE_LANES, BLOCK_LANES), lambda m: (m, 0, 0)),
                   pl.BlockSpec((None, 4, 2, SUBLANES, STATE_LANES // 2), lambda m: (m, 0, 0, 0, 0))],
        out_shape=[jax.ShapeDtypeStruct((N_CH_BLOCKS, BLOCK_LANES, STATE_LANES), BF16),
                   jax.ShapeDtypeStruct((N_CH_BLOCKS, BLOCK_LANES + STATE_LANES, BLOCK_LANES), BF16),
                   jax.ShapeDtypeStruct((N_CH_BLOCKS, 4, 2, SUBLANES, STATE_LANES // 2), F32)],
        scratch_shapes=[pltpu.VMEM(((SSM_BLOCK + 1) * LANES, STATE_LANES), F32),
                        pltpu.VMEM((LANES, STATE_LANES), F32)],
        compiler_params=_cparams("parallel"),
        name="ssm_discretize",
    )(pair(lam_re), pair(lam_im), dt, rows(b_re.transpose(0, 2, 1)), rows(b_im.transpose(0, 2, 1)),
      rows(c_re), rows(c_im))


GROUPS_PER_CH_BLOCK = LANES // SSM_GROUP
N_CH_BLOCKS = SSM_WIDTH // LANES
STATE_LANES = 2 * GROUPS_PER_CH_BLOCK * SSM_STATE
STATE_TILES = STATE_LANES // (2 * LANES)
BLOCK_LANES = SSM_BLOCK * LANES


def _ssm_kernel(nb, u_ref, s0_ref, we_ref, tv_ref, d_ref, coef_ref, y_ref, fin_ref, st_ref, sprev_ref, ucat_ref):
    t_idx = pl.program_id(2)
    r = st_ref.shape[1] - SUBLANES
    rows = nb * r

    @pl.when(t_idx == 0)
    def _():
        for b in range(nb):
            st_ref[b, 0:SUBLANES, :] = jnp.broadcast_to(s0_ref[b], (SUBLANES, STATE_LANES))

    for j in range(SSM_BLOCK):
        ucat_ref[:, j * LANES:(j + 1) * LANES] = u_ref[pl.ds(j, rows, stride=SSM_BLOCK), :].astype(BF16)
    e = jnp.dot(ucat_ref[...], we_ref[...], preferred_element_type=F32)
    st_ref[:, SUBLANES:, :] = e.reshape(nb, r, STATE_LANES)

    first_row = lax.broadcasted_iota(jnp.int32, (SUBLANES, LANES), 0) == 0

    def group(rg, carry):
        r0 = pl.multiple_of(SUBLANES + rg * SUBLANES, SUBLANES)
        rp = pl.multiple_of(rg * SUBLANES, SUBLANES)
        for b in range(nb):
            for q in range(STATE_TILES):
                re_sl = pl.ds(q * 2 * LANES, LANES)
                im_sl = pl.ds(q * 2 * LANES + LANES, LANES)
                co = pl.ds(q * LANES, LANES)
                xr = st_ref[b, pl.ds(r0, SUBLANES), re_sl]
                xi = st_ref[b, pl.ds(r0, SUBLANES), im_sl]
                for step, sh in enumerate((1, 2, 4)):
                    ar, ai = _cmul(coef_ref[step, 0, :, co], coef_ref[step, 1, :, co],
                                   pltpu.roll(xr, sh, 0), pltpu.roll(xi, sh, 0))
                    xr = xr + ar
                    xi = xi + ai
                pr = jnp.broadcast_to(st_ref[b, pl.ds(rp, SUBLANES), re_sl][SUBLANES - 1:, :], (SUBLANES, LANES))
                pi = jnp.broadcast_to(st_ref[b, pl.ds(rp, SUBLANES), im_sl][SUBLANES - 1:, :], (SUBLANES, LANES))
                ar, ai = _cmul(coef_ref[3, 0, :, co], coef_ref[3, 1, :, co], pr, pi)
                xr = xr + ar
                xi = xi + ai
                st_ref[b, pl.ds(r0, SUBLANES), re_sl] = xr
                st_ref[b, pl.ds(r0, SUBLANES), im_sl] = xi
                out_rows = pl.ds(pl.multiple_of(b * r + rg * SUBLANES, SUBLANES), SUBLANES)
                sprev_ref[out_rows, re_sl] = jnp.where(first_row, pr, pltpu.roll(xr, 1, 0))
                sprev_ref[out_rows, im_sl] = jnp.where(first_row, pi, pltpu.roll(xi, 1, 0))
        return carry

    lax.fori_loop(0, r // SUBLANES, group, 0)

    lhs = jnp.concatenate([ucat_ref[...], sprev_ref[...].astype(BF16)], axis=1)
    ycat = jnp.dot(lhs, tv_ref[...], preferred_element_type=F32)
    d = d_ref[...]
    for t in range(SSM_BLOCK):
        tok = pl.ds(t, rows, stride=SSM_BLOCK)
        y_ref[tok, :] = ycat[:, t * LANES:(t + 1) * LANES] + d * u_ref[tok, :]

    for b in range(nb):
        tail = st_ref[b, r:r + SUBLANES, :]
        st_ref[b, 0:SUBLANES, :] = tail
        fin_ref[b] = tail[SUBLANES - 1:SUBLANES, :]


def _ssm(u, s0, we, tv, d, l, coef, row_base, n_seq, seq_len, nb, tt):
    rows = nb * tt
    r = tt // SSM_BLOCK
    nt = seq_len // tt
    base = row_base // rows
    return pl.pallas_call(
        functools.partial(_ssm_kernel, nb),
        grid=(N_CH_BLOCKS, n_seq // nb, nt),
        in_specs=[pl.BlockSpec((rows, LANES), lambda m, i, j: (base + i * nt + j, m)),
                  pl.BlockSpec((nb, 1, STATE_LANES), lambda m, i, j: (i, 0, m)),
                  pl.BlockSpec((None, BLOCK_LANES, STATE_LANES), lambda m, i, j: (m, 0, 0)),
                  pl.BlockSpec((None, BLOCK_LANES + STATE_LANES, BLOCK_LANES), lambda m, i, j: (m, 0, 0)),
                  pl.BlockSpec((None, 1, LANES), lambda m, i, j: (l, 0, m)),
                  pl.BlockSpec((None, 4, 2, SUBLANES, STATE_LANES // 2), lambda m, i, j: (m, 0, 0, 0, 0))],
        out_specs=[pl.BlockSpec((rows, LANES), lambda m, i, j: (i * nt + j, m)),
                   pl.BlockSpec((nb, 1, STATE_LANES), lambda m, i, j: (i, 0, m))],
        out_shape=[jax.ShapeDtypeStruct((n_seq * seq_len, SSM_WIDTH), F32),
                   jax.ShapeDtypeStruct((n_seq, 1, N_CH_BLOCKS * STATE_LANES), F32)],
        scratch_shapes=[pltpu.VMEM((nb, SUBLANES + r, STATE_LANES), F32),
                        pltpu.VMEM((nb * r, STATE_LANES), F32),
                        pltpu.VMEM((nb * r, BLOCK_LANES), BF16)],
        compiler_params=_cparams("parallel", "parallel", "arbitrary"),
        name="ssm_scan",
    )(u, s0, we, tv, d, coef)


def _state_to_tiles(s):
    lead = s.shape[:-3]
    t = s.reshape(lead + (N_CH_BLOCKS, STATE_TILES, 2, SSM_STATE, 2))
    t = jnp.moveaxis(t, -1, -3)
    return t.reshape(lead + (1, N_CH_BLOCKS * STATE_LANES))


def _tiles_to_state(f):
    b = f.shape[0]
    t = f.reshape(b, N_CH_BLOCKS, STATE_TILES, 2, 2, SSM_STATE)
    t = jnp.moveaxis(t, 3, -1)
    return t.reshape(b, N_SSM_GROUPS, SSM_STATE, 2)


def _merge_kernel(n_prompt_tiles, xp_ref, xs_ref, ap_ref, as_ref, yp_ref, ys_ref, wglu_ref, bglu_ref, ga_ref, gs_ref,
                  wout_ref, o_ref):
    is_prompt = pl.program_id(0) < n_prompt_tiles
    attn = jnp.where(is_prompt, ap_ref[...], as_ref[...])
    g = jax.nn.gelu(jnp.where(is_prompt, yp_ref[...], ys_ref[...]))
    glu = g * jax.nn.sigmoid(jnp.dot(g.astype(BF16), wglu_ref[...], preferred_element_type=F32) + bglu_ref[...])
    na = _rms(attn, ga_ref[...]).astype(BF16)
    ns = _rms(glu, gs_ref[...]).astype(BF16)
    o = jnp.dot(na, wout_ref[0:ATTN_WIDTH, :], preferred_element_type=F32)
    o = o + jnp.dot(ns, wout_ref[ATTN_WIDTH:, :], preferred_element_type=F32)
    o_ref[...] = jnp.where(is_prompt, xp_ref[...], xs_ref[...]) + o


def _merge(xp, xs, n, attn_p, attn_s, y_p, y_s, wglu, bglu, ga, gs, wout, l, tm):
    npt = attn_p.shape[0] // tm
    nst = attn_s.shape[0] // tm
    row = lambda w: pl.BlockSpec((tm, w), lambda i: (i, 0))
    prow = lambda w: pl.BlockSpec((tm, w), lambda i: (jnp.minimum(i, npt - 1), 0))
    srow = lambda w: pl.BlockSpec((tm, w), lambda i: (jnp.clip(i - npt, 0, nst - 1), 0))
    return pl.pallas_call(
        functools.partial(_merge_kernel, npt),
        grid=(n // tm,),
        in_specs=[*_two_source_specs(xp, xs, attn_p.shape[0], tm, D_MODEL),
                  prow(ATTN_WIDTH), srow(ATTN_WIDTH), prow(SSM_WIDTH), srow(SSM_WIDTH),
                  _layer_spec(wglu, l), _layer_spec(bglu, l), _layer_spec(ga, l), _layer_spec(gs, l),
                  _layer_spec(wout, l)],
        out_specs=row(D_MODEL),
        out_shape=jax.ShapeDtypeStruct((n, D_MODEL), F32),
        compiler_params=_cparams("parallel"),
        name="merge_heads",
    )(xp, xs, attn_p, attn_s, y_p, y_s, wglu, bglu, ga, gs, wout)


def _mem_kv_kernel(m_ref, g_ref, wk_ref, wv_ref, k_ref, v_ref):
    mn = _rms(m_ref[...], g_ref[...]).astype(BF16)
    k_ref[...] = jnp.dot(mn, wk_ref[...], preferred_element_type=F32)
    v_ref[...] = jnp.dot(mn, wv_ref[...], preferred_element_type=F32)


def _mem_kv(mem, g, wk, wv, l, tm):
    n = mem.shape[0]
    row = pl.BlockSpec((tm, D_MODEL), lambda i: (i, 0))
    return pl.pallas_call(
        _mem_kv_kernel,
        grid=(n // tm,),
        in_specs=[row, _layer_spec(g, l), _layer_spec(wk, l), _layer_spec(wv, l)],
        out_specs=[row, row],
        out_shape=[jax.ShapeDtypeStruct((n, D_MODEL), F32)] * 2,
        compiler_params=_cparams("parallel"),
        name="mem_kv",
    )(mem, g, wk, wv)


def _xattn_kernel(nb, x_ref, g_ref, wq_ref, wo_ref, mk_ref, mv_ref, o_ref, att_ref):
    t = x_ref.shape[0] // nb
    x = x_ref[...]
    hn = _rms(x, g_ref[...]).astype(BF16)
    q = jnp.dot(hn, wq_ref[...], preferred_element_type=F32) * (1.0 / math.sqrt(XHEAD_DIM))
    q = q.astype(BF16)
    heads = [slice(h * XHEAD_DIM, (h + 1) * XHEAD_DIM) for h in range(N_XHEADS)]
    scores = []
    for b in range(nb):
        mk = mk_ref[b * N_MEM:(b + 1) * N_MEM, :].astype(BF16)
        for sl in heads:
            scores.append(lax.dot_general(q[b * t:(b + 1) * t, sl], mk[:, sl], (((1,), (1,)), ((), ())),
                                          preferred_element_type=F32))
    s = jnp.concatenate(scores, axis=0)
    e = jnp.exp(s - jnp.max(s, axis=-1, keepdims=True))
    p = (e / jnp.sum(e, axis=-1, keepdims=True)).astype(BF16)
    for b in range(nb):
        mv = mv_ref[b * N_MEM:(b + 1) * N_MEM, :].astype(BF16)
        for h, sl in enumerate(heads):
            r0 = (b * N_XHEADS + h) * t
            att_ref[b * t:(b + 1) * t, sl] = jnp.dot(p[r0:r0 + t, :], mv[:, sl], preferred_element_type=F32)
    o = jnp.dot(att_ref[...].astype(BF16), wo_ref[...], preferred_element_type=F32)
    o_ref[...] = x + o


def _xattn(x, g, wq, wo, l, mk, mv, mem_spec, row_base, n_rows, nb, tm):
    base = row_base // tm
    xspec = pl.BlockSpec((tm, D_MODEL), lambda i: (base + i, 0))
    return pl.pallas_call(
        functools.partial(_xattn_kernel, nb),
        grid=(n_rows // tm,),
        in_specs=[xspec, _layer_spec(g, l), _layer_spec(wq, l), _layer_spec(wo, l), mem_spec, mem_spec],
        out_specs=xspec,
        out_shape=jax.ShapeDtypeStruct(x.shape, F32),
        scratch_shapes=[pltpu.VMEM((tm, D_MODEL), F32)],
        input_output_aliases={0: 0},
        compiler_params=_cparams("parallel"),
        name="cross_attn",
    )(x, g, wq, wo, mk, mv)


ROUTER_LANES = LANES
EXPERT_LANE0 = N_EXPERT_GROUPS
EXPERTS_PER_STEP = 4
MOE_SRC_TILE = 512
MOE_RUN_ALIGN = 16
MOE_SORTED_ROWS = 640
MOE_TILE = 1024


def _dot_f32_3pass(x, w):
    xh = x.astype(BF16)
    xl = (x - xh.astype(F32)).astype(BF16)
    wh = w.astype(BF16)
    wl = (w - wh.astype(F32)).astype(BF16)
    dot = lambda a, b: jnp.dot(a, b, preferred_element_type=F32)
    return dot(xh, wh) + (dot(xl, wh) + dot(xh, wl))


def _route(logits):
    lane_i = lax.broadcasted_iota(jnp.int32, logits.shape, 1)
    lane = lane_i.astype(F32)
    neg = jnp.float32(-jnp.inf)
    is_g = lane_i < N_EXPERT_GROUPS
    gl = jnp.where(is_g, logits, neg)
    gmax = jnp.max(gl, axis=-1, keepdims=True)
    gidx = jnp.min(jnp.where(gl == gmax, lane, float(ROUTER_LANES)), axis=-1, keepdims=True)
    g_w = 1.0 / jnp.sum(jnp.where(is_g, jnp.exp(gl - gmax), 0.0), axis=-1, keepdims=True)
    first = EXPERT_LANE0 + gidx * EXPERTS_PER_GROUP
    sel = (lane >= first) & (lane < first + EXPERTS_PER_GROUP)
    el = jnp.where(sel, logits, neg)
    m1 = jnp.max(el, axis=-1, keepdims=True)
    i1 = jnp.min(jnp.where(el == m1, lane, float(ROUTER_LANES)), axis=-1, keepdims=True)
    el2 = jnp.where(lane == i1, neg, el)
    m2 = jnp.max(el2, axis=-1, keepdims=True)
    i2 = jnp.min(jnp.where(el2 == m2, lane, float(ROUTER_LANES)), axis=-1, keepdims=True)
    r = jnp.exp(m2 - m1)
    w1 = g_w / (1.0 + r)
    w2 = w1 * r
    return jnp.where(lane == i1, w1, jnp.where(lane == i2, w2, 0.0)), gidx


def _moe_pre_kernel(x_ref, g_ref, wr_ref, br_ref, xn_ref, gate_ref, meta_ref, cnt_ref, tri_ref):
    tm = x_ref.shape[0]

    @pl.when(pl.program_id(0) == 0)
    def _():
        r = lax.broadcasted_iota(jnp.int32, (tm, tm), 0)
        c = lax.broadcasted_iota(jnp.int32, (tm, tm), 1)
        tri_ref[...] = (c <= r).astype(BF16)

    xn = _rms(x_ref[...], g_ref[...])
    xn_ref[...] = xn.astype(BF16)
    gates, gidx = _route(_dot_f32_3pass(xn, wr_ref[...]) + br_ref[...])
    gate_ref[...] = gates
    lane = lax.broadcasted_iota(jnp.int32, gates.shape, 1).astype(F32)
    onehot = lane == gidx
    incl = jnp.dot(tri_ref[...], onehot.astype(BF16), preferred_element_type=F32)
    rank = jnp.sum(jnp.where(onehot, incl, 0.0), axis=-1, keepdims=True) - 1.0
    meta_ref[...] = jnp.where(lane == 0.0, gidx, jnp.where(lane == 1.0, rank, 0.0))
    cnt_ref[...] = incl[tm - 1:tm, :]


def _moe_pre(x, g, wr, br, l):
    n = x.shape[0]
    tm = MOE_SRC_TILE
    row = lambda w: pl.BlockSpec((tm, w), lambda i: (i, 0))
    return pl.pallas_call(
        _moe_pre_kernel,
        grid=(n // tm,),
        in_specs=[row(D_MODEL), _layer_spec(g, l), _layer_spec(wr, l), _layer_spec(br, l)],
        out_specs=[row(D_MODEL), row(ROUTER_LANES), row(ROUTER_LANES),
                   pl.BlockSpec((None, 1, ROUTER_LANES), lambda i: (i, 0, 0))],
        out_shape=[jax.ShapeDtypeStruct((n, D_MODEL), BF16), jax.ShapeDtypeStruct((n, ROUTER_LANES), F32),
                   jax.ShapeDtypeStruct((n, ROUTER_LANES), F32),
                   jax.ShapeDtypeStruct((n // tm, 1, ROUTER_LANES), F32)],
        scratch_shapes=[pltpu.VMEM((tm, tm), BF16)],
        compiler_params=_cparams("arbitrary"),
        name="moe_pre",
    )(x, g, wr, br)


def _sort_onehot(meta, start_ref, t):
    gid = meta[:, 0:1]
    pos = meta[:, 1:2]
    for grp in range(N_EXPERT_GROUPS):
        pos = pos + jnp.where(gid == float(grp), start_ref[t * N_EXPERT_GROUPS + grp].astype(F32), 0.0)
    col = lax.broadcasted_iota(jnp.int32, (meta.shape[0], MOE_SORTED_ROWS), 1).astype(F32)
    return (col == pos).astype(BF16)


def _run_copies(t, base, start_ref, off_ref, nblk_ref, pairs, sem, to_hbm):
    for grp in range(N_EXPERT_GROUPS):
        k = t * N_EXPERT_GROUPS + grp
        src0 = base + start_ref[k]
        dst0 = off_ref[k]

        def body(b, carry):
            lo = pl.multiple_of(src0 + b * MOE_RUN_ALIGN, MOE_RUN_ALIGN)
            hi = pl.multiple_of(dst0 + b * MOE_RUN_ALIGN, MOE_RUN_ALIGN)
            for buf, arr in pairs:
                a, h = buf.at[pl.ds(lo, MOE_RUN_ALIGN)], arr.at[pl.ds(hi, MOE_RUN_ALIGN)]
                (pltpu.make_async_copy(a, h, sem) if to_hbm else pltpu.make_async_copy(h, a, sem)).start()
            return carry

        lax.fori_loop(0, nblk_ref[k], body, 0)


def _run_wait(t, nblk_ref, pairs, sem, to_hbm):
    total = nblk_ref[t * N_EXPERT_GROUPS]
    for grp in range(1, N_EXPERT_GROUPS):
        total = total + nblk_ref[t * N_EXPERT_GROUPS + grp]
    rows = total * MOE_RUN_ALIGN

    @pl.when(rows > 0)
    def _():
        for buf, arr in pairs:
            a, h = buf.at[pl.ds(0, rows)], arr.at[pl.ds(0, rows)]
            (pltpu.make_async_copy(a, h, sem) if to_hbm else pltpu.make_async_copy(h, a, sem)).wait()


def _moe_pack_kernel(start_ref, off_ref, nblk_ref, end_ref, xn_ref, gate_ref, meta_ref, xs_hbm, gs_hbm,
                     xbuf, gbuf, zx, zg, sems):
    t = pl.program_id(0)
    last = pl.num_programs(0) - 1
    slot = t % 2
    base = pl.multiple_of(slot * MOE_SORTED_ROWS, MOE_SORTED_ROWS)
    pairs = [(xbuf, xs_hbm), (gbuf, gs_hbm)]

    @pl.when(t >= 2)
    def _():
        _run_wait(t - 2, nblk_ref, pairs, sems.at[slot], True)

    onehot = _sort_onehot(meta_ref[...], start_ref, t)
    tn = (((0,), (0,)), ((), ()))
    rows = pl.ds(base, MOE_SORTED_ROWS)
    xbuf[rows, :] = lax.dot_general(onehot, xn_ref[...], tn, preferred_element_type=F32).astype(BF16)
    gates = gate_ref[...]
    gh = gates.astype(BF16)
    gl = (gates - gh.astype(F32)).astype(BF16)
    gbuf[rows, :] = (lax.dot_general(onehot, gh, tn, preferred_element_type=F32)
                     + lax.dot_general(onehot, gl, tn, preferred_element_type=F32))
    _run_copies(t, base, start_ref, off_ref, nblk_ref, pairs, sems.at[slot], True)

    @pl.when(t == last)
    def _():
        @pl.when(t >= 1)
        def _():
            _run_wait(t - 1, nblk_ref, pairs, sems.at[1 - slot], True)

        _run_wait(t, nblk_ref, pairs, sems.at[slot], True)
        zx[...] = jnp.zeros(zx.shape, zx.dtype)
        zg[...] = jnp.zeros(zg.shape, zg.dtype)
        copies = []
        for grp in range(N_EXPERT_GROUPS):
            tail = pl.ds(pl.multiple_of(end_ref[grp], MOE_RUN_ALIGN), MOE_TILE)
            copies += [pltpu.make_async_copy(zx, xs_hbm.at[tail], sems.at[0]),
                       pltpu.make_async_copy(zg, gs_hbm.at[tail], sems.at[0])]
        for cp in copies:
            cp.start()
        for cp in copies:
            cp.wait()


def _moe_pack(xn, gates, meta, tabs, n_rows):
    n = xn.shape[0]
    tm = MOE_SRC_TILE
    row = lambda w: pl.BlockSpec((tm, w), lambda i, *_: (i, 0))
    any_spec = pl.BlockSpec(memory_space=pl.ANY)
    grid_spec = pltpu.PrefetchScalarGridSpec(
        num_scalar_prefetch=4, grid=(n // tm,),
        in_specs=[row(D_MODEL), row(ROUTER_LANES), row(ROUTER_LANES)],
        out_specs=[any_spec, any_spec],
        scratch_shapes=[pltpu.VMEM((2 * MOE_SORTED_ROWS, D_MODEL), BF16),
                        pltpu.VMEM((2 * MOE_SORTED_ROWS, ROUTER_LANES), F32),
                        pltpu.VMEM((MOE_TILE, D_MODEL), BF16), pltpu.VMEM((MOE_TILE, ROUTER_LANES), F32),
                        pltpu.SemaphoreType.DMA((2,))])
    return pl.pallas_call(
        _moe_pack_kernel,
        grid_spec=grid_spec,
        out_shape=[jax.ShapeDtypeStruct((n_rows, D_MODEL), BF16), jax.ShapeDtypeStruct((n_rows, ROUTER_LANES), F32)],
        compiler_params=_cparams("arbitrary"),
        name="moe_pack",
    )(tabs["start"], tabs["off"], tabs["nblk"], tabs["end"], xn, gates, meta)


def _moe_expert_kernel(blk_ref, grp_ref, valid_ref, x_ref, gate_ref, wgu_ref, wd_ref, o_ref):
    i = pl.program_id(0)
    e_step = pl.program_id(1)

    @pl.when(valid_ref[i] > 0)
    def _():
        xn = x_ref[...]
        gates = gate_ref[...]
        lane = lax.broadcasted_iota(jnp.int32, gates.shape, 1)
        first = EXPERT_LANE0 + grp_ref[i] * EXPERTS_PER_GROUP + e_step * EXPERTS_PER_STEP
        hids = []
        for j in range(EXPERTS_PER_STEP):
            h = jnp.dot(xn, wgu_ref[j], preferred_element_type=F32)
            ge = jnp.sum(jnp.where(lane == first + j, gates, 0.0), axis=-1, keepdims=True)
            hids.append((jax.nn.silu(h[:, :EXPERT_FF]) * h[:, EXPERT_FF:] * ge).astype(BF16))
        y = jnp.dot(jnp.concatenate(hids, axis=1), wd_ref[...], preferred_element_type=F32)

        @pl.when(e_step == 0)
        def _():
            o_ref[...] = y

        @pl.when(e_step > 0)
        def _():
            o_ref[...] += y


def _moe_experts(xs, gs, tabs, wgu, wd, l, n_tiles):
    es = EXPERTS_PER_STEP
    steps = EXPERTS_PER_GROUP // es
    row = lambda w: pl.BlockSpec((MOE_TILE, w), lambda i, e, blk, grp, valid: (blk[i], 0))
    grid_spec = pltpu.PrefetchScalarGridSpec(
        num_scalar_prefetch=3, grid=(n_tiles, steps),
        in_specs=[row(D_MODEL), row(ROUTER_LANES),
                  pl.BlockSpec((None, es, D_MODEL, 2 * EXPERT_FF),
                               lambda i, e, blk, grp, valid: (l, grp[i] * steps + e, 0, 0)),
                  pl.BlockSpec((None, es * EXPERT_FF, D_MODEL), lambda i, e, blk, grp, valid: (l, grp[i] * steps + e, 0))],
        out_specs=row(D_MODEL))
    return pl.pallas_call(
        _moe_expert_kernel,
        grid_spec=grid_spec,
        out_shape=jax.ShapeDtypeStruct((xs.shape[0], D_MODEL), F32),
        compiler_params=_cparams("arbitrary", "arbitrary"),
        name="hier_moe",
    )(tabs["tile_blk"], tabs["tile_grp"], tabs["tile_valid"], xs, gs, wgu, wd)


def _moe_unpack_kernel(start_ref, off_ref, nblk_ref, x_ref, meta_ref, ys_hbm, o_ref, ybuf, sems):
    t = pl.program_id(0)
    slot = t % 2
    base = pl.multiple_of(slot * MOE_SORTED_ROWS, MOE_SORTED_ROWS)
    pairs = [(ybuf, ys_hbm)]

    @pl.when(t == 0)
    def _():
        ybuf[...] = jnp.zeros(ybuf.shape, ybuf.dtype)
        _run_copies(t, base, start_ref, off_ref, nblk_ref, pairs, sems.at[slot], False)

    @pl.when(t + 1 < pl.num_programs(0))
    def _():
        nxt = pl.multiple_of((1 - slot) * MOE_SORTED_ROWS, MOE_SORTED_ROWS)
        _run_copies(t + 1, nxt, start_ref, off_ref, nblk_ref, pairs, sems.at[1 - slot], False)

    onehot = _sort_onehot(meta_ref[...], start_ref, t)
    _run_wait(t, nblk_ref, pairs, sems.at[slot], False)
    y = ybuf[pl.ds(base, MOE_SORTED_ROWS), :]
    yh = y.astype(BF16)
    yl = (y - yh.astype(F32)).astype(BF16)
    o_ref[...] = x_ref[...] + (jnp.dot(onehot, yh, preferred_element_type=F32)
                               + jnp.dot(onehot, yl, preferred_element_type=F32))


def _moe_unpack(x, meta, ys, tabs):
    n = x.shape[0]
    tm = MOE_SRC_TILE
    row = lambda w: pl.BlockSpec((tm, w), lambda i, *_: (i, 0))
    grid_spec = pltpu.PrefetchScalarGridSpec(
        num_scalar_prefetch=3, grid=(n // tm,),
        in_specs=[row(D_MODEL), row(ROUTER_LANES), pl.BlockSpec(memory_space=pl.ANY)],
        out_specs=row(D_MODEL),
        scratch_shapes=[pltpu.VMEM((2 * MOE_SORTED_ROWS, D_MODEL), F32), pltpu.SemaphoreType.DMA((2,))])
    return pl.pallas_call(
        _moe_unpack_kernel,
        grid_spec=grid_spec,
        out_shape=jax.ShapeDtypeStruct((n, D_MODEL), F32),
        compiler_params=_cparams("arbitrary"),
        name="moe_unpack",
    )(tabs["start"], tabs["off"], tabs["nblk"], x, meta, ys)


def _moe_tables(cnt, n):
    n_src = cnt.shape[0]
    pad = (cnt + MOE_RUN_ALIGN - 1) // MOE_RUN_ALIGN * MOE_RUN_ALIGN
    worst = n + n_src * (MOE_RUN_ALIGN - 1)
    cap = (worst + 2 * MOE_TILE - 1) // MOE_TILE * MOE_TILE
    start = jnp.cumsum(pad, axis=1) - pad
    total = jnp.sum(pad, axis=0)
    base = jnp.arange(N_EXPERT_GROUPS, dtype=jnp.int32) * cap
    off = base[None, :] + jnp.cumsum(pad, axis=0) - pad
    tiles_g = (total + MOE_TILE - 1) // MOE_TILE
    tile_end = jnp.cumsum(tiles_g)
    n_tiles = n // MOE_TILE + N_EXPERT_GROUPS + (n_src * N_EXPERT_GROUPS * MOE_RUN_ALIGN + MOE_TILE - 1) // MOE_TILE
    i = jnp.minimum(jnp.arange(n_tiles, dtype=jnp.int32), jnp.maximum(tile_end[-1] - 1, 0))
    grp = jnp.minimum(jnp.sum((i[:, None] >= tile_end[None, :]).astype(jnp.int32), axis=1), N_EXPERT_GROUPS - 1)
    first_tile = (tile_end - tiles_g)
    blk = jnp.zeros_like(i)
    for g in range(N_EXPERT_GROUPS):
        blk = blk + jnp.where(grp == g, g * (cap // MOE_TILE) + i - first_tile[g], 0)
    i32 = lambda a: a.astype(jnp.int32)
    tabs = dict(start=i32(start.reshape(-1)), off=i32(off.reshape(-1)), nblk=i32((pad // MOE_RUN_ALIGN).reshape(-1)),
                end=i32(base + total), tile_blk=i32(blk), tile_grp=i32(grp),
                tile_valid=i32(jnp.arange(n_tiles) < tile_end[-1]))
    return tabs, N_EXPERT_GROUPS * cap, n_tiles


def _moe(x, g, wr, br, wgu, wd, l):
    n = x.shape[0]
    xn, gates, meta, cnt = _moe_pre(x, g, wr, br, l)
    tabs, n_rows, n_tiles = _moe_tables(cnt[:, 0, :N_EXPERT_GROUPS].astype(jnp.int32), n)
    xs, gs = _moe_pack(xn, gates, meta, tabs, n_rows)
    ys = _moe_experts(xs, gs, tabs, wgu, wd, l, n_tiles)
    return _moe_unpack(x, meta, ys, tabs)


def _final_norm_kernel(x_ref, g_ref, o_ref):
    o_ref[...] = _rms(x_ref[...], g_ref[...])


def _final_norm(x, g, row_base, n_rows, tm):
    base = row_base // tm
    return pl.pallas_call(
        _final_norm_kernel,
        grid=(n_rows // tm,),
        in_specs=[pl.BlockSpec((tm, D_MODEL), lambda i: (base + i, 0)),
                  pl.BlockSpec((1, D_MODEL), lambda i: (0, 0))],
        out_specs=pl.BlockSpec((tm, D_MODEL), lambda i: (i, 0)),
        out_shape=jax.ShapeDtypeStruct((n_rows, D_MODEL), F32),
        compiler_params=_cparams("parallel"),
        name="final_norm",
    )(x, g)


def _rope_tables(seq, t_len, tm):
    half = HEAD_DIM // 2
    inv = ROPE_THETA ** (-jnp.arange(half, dtype=F32) / half)
    pos_s = PAST_LEN + jnp.arange(t_len)
    pos = jnp.concatenate([jnp.arange(seq), jnp.tile(pos_s, tm // t_len)]).astype(F32)
    ang = pos[:, None] * inv[None, :]
    cos = jnp.tile(jnp.cos(ang), (1, LANES // half))
    sign = jnp.where((jnp.arange(LANES) % HEAD_DIM) < half, -1.0, 1.0).astype(F32)
    sin = jnp.tile(jnp.sin(ang), (1, LANES // half)) * sign[None, :]
    return cos, sin


def kernel(x_prompt, x_sample, cache_win_k, cache_win_v, state_ssm, cache_mem_k, cache_mem_v, mem_prompt, w_in, attn_sink, lam_re, lam_im, log_dt, ssm_b_re, ssm_b_im, ssm_c_re, ssm_c_im, ssm_d, w_glu, b_glu, g_attn_out, g_ssm_out, w_out, g_mix, g_xattn, g_mem, wq_x, wk_x, wv_x, wo_x, g_ffn, w_group, b_group, w_router, b_router, w_gate, w_up, w_down, g_final):
    batch, seq, _ = x_prompt.shape
    dec_batch, t_len, _ = x_sample.shape
    depth = w_in.shape[0]
    win_rows = cache_win_k.shape[2]
    n_p = batch * seq
    n_s = dec_batch * t_len
    tm_wide = 1024 if (n_p + n_s) % 1024 == 0 else 512
    tm = tm_wide
    tm_x = 512
    sample_nb = tm_x // t_len

    n = n_p + n_s
    xp = x_prompt.reshape(n_p, D_MODEL)
    xs = x_sample.reshape(n_s, D_MODEL)
    cos_tab, sin_tab = _rope_tables(seq, t_len, tm)
    mem_flat = mem_prompt.reshape(batch * N_MEM, D_MODEL)
    zero_state = jnp.zeros((batch, 1, 2 * N_STATE), F32)
    vec = lambda a: a.reshape(depth, 1, a.shape[-1])

    w_in_b, w_glu_b, w_out_b = w_in.astype(BF16), w_glu.astype(BF16), w_out.astype(BF16)
    wq_b, wk_b, wv_b, wo_b = (w.astype(BF16) for w in (wq_x, wk_x, wv_x, wo_x))
    wgu_b = jnp.concatenate([w_gate, w_up], axis=-1).astype(BF16)
    wd_b = w_down.reshape(depth, N_EXPERTS * EXPERT_FF, D_MODEL).astype(BF16)
    wr = jnp.concatenate([w_group, w_router.transpose(0, 2, 1, 3).reshape(depth, D_MODEL, N_EXPERTS)], axis=-1)
    wr = jnp.pad(wr, ((0, 0), (0, 0), (0, ROUTER_LANES - wr.shape[-1])))
    br = jnp.concatenate([b_group, b_router.reshape(depth, N_EXPERTS)], axis=-1)
    br = jnp.pad(br, ((0, 0), (0, ROUTER_LANES - br.shape[-1]))).reshape(depth, 1, ROUTER_LANES)
    g_mix_r, g_xattn_r, g_mem_r, g_ffn_r = vec(g_mix), vec(g_xattn), vec(g_mem), vec(g_ffn)
    g_a_r, g_s_r, b_glu_r, ssm_d_r = vec(g_attn_out), vec(g_ssm_out), vec(b_glu), vec(ssm_d)
    cache_k = cache_win_k.reshape(depth, dec_batch, win_rows, KV_WIDTH)
    cache_v = cache_win_v.reshape(depth, dec_batch, win_rows, KV_WIDTH)
    cmem_k = cache_mem_k.reshape(depth, dec_batch * N_MEM, D_MODEL)
    cmem_v = cache_mem_v.reshape(depth, dec_batch * N_MEM, D_MODEL)
    state_in = _state_to_tiles(state_ssm)

    outs = {k: [] for k in ("wk_p", "wv_p", "ssm_p", "mk_p", "mv_p", "wk_s", "wv_s", "ssm_s")}
    for l in range(depth):
        q, k, v, u = _in_proj(xp, xs, n, g_mix_r, w_in_b, l, cos_tab, sin_tab, n_p, seq, tm)
        attn_p = _attn_prompt(q, k, v, attn_sink[l], batch, seq)
        attn_s = _attn_sample(q, k, v, cache_k, cache_v, l, attn_sink[l], n_p, dec_batch, t_len, 4)
        tail = lambda a: jnp.stack([a[(b + 1) * seq - WINDOW:(b + 1) * seq] for b in range(batch)])
        outs["wk_p"].append(tail(k).reshape(batch, WINDOW, N_KV_HEADS, HEAD_DIM))
        outs["wv_p"].append(tail(v).reshape(batch, WINDOW, N_KV_HEADS, HEAD_DIM))
        ks = k[n_p:].reshape(dec_batch, t_len, KV_WIDTH)
        vs = v[n_p:].reshape(dec_batch, t_len, KV_WIDTH)
        k_all = jnp.concatenate([cache_k[l], ks], axis=1)[:, -win_rows:]
        v_all = jnp.concatenate([cache_v[l], vs], axis=1)[:, -win_rows:]
        outs["wk_s"].append(k_all.reshape(dec_batch, win_rows, N_KV_HEADS, HEAD_DIM))
        outs["wv_s"].append(v_all.reshape(dec_batch, win_rows, N_KV_HEADS, HEAD_DIM))

        we, tv, coef = _ssm_discretize(lam_re[l], lam_im[l], log_dt[l], ssm_b_re[l], ssm_b_im[l],
                                       ssm_c_re[l], ssm_c_im[l])
        y_p, fin_p = _ssm(u, zero_state, we, tv, ssm_d_r, l, coef, 0, batch, seq, 1, seq)
        y_s, fin_s = _ssm(u, state_in[l], we, tv, ssm_d_r, l, coef, n_p, dec_batch, t_len, dec_batch, t_len)
        outs["ssm_p"].append(_tiles_to_state(fin_p))
        outs["ssm_s"].append(_tiles_to_state(fin_s))
        x = _merge(xp, xs, n, attn_p, attn_s, y_p, y_s, w_glu_b, b_glu_r, g_a_r, g_s_r, w_out_b, l, tm)

        mk_p, mv_p = _mem_kv(mem_flat, g_mem_r, wk_b, wv_b, l, 512)
        outs["mk_p"].append(mk_p.reshape(batch, N_MEM, N_XHEADS, XHEAD_DIM))
        outs["mv_p"].append(mv_p.reshape(batch, N_MEM, N_XHEADS, XHEAD_DIM))
        tiles_per_seq = seq // tm
        x = _xattn(x, g_xattn_r, wq_b, wo_b, l, mk_p, mv_p,
                   pl.BlockSpec((N_MEM, D_MODEL), lambda i: (i // tiles_per_seq, 0)), 0, n_p, 1, tm)
        x = _xattn(x, g_xattn_r, wq_b, wo_b, l, cmem_k, cmem_v,
                   pl.BlockSpec((None, sample_nb * N_MEM, D_MODEL), lambda i: (l, i, 0)), n_p, n_s, sample_nb, tm_x)

        x = _moe(x, g_ffn_r, wr, br, wgu_b, wd_b, l)
        xp = xs = x

    gf = g_final.reshape(1, D_MODEL)
    y_p = _final_norm(x, gf, 0, n_p, tm)
    y_s = _final_norm(x, gf, n_p, n_s, tm)
    st = lambda name: jnp.stack(outs[name], axis=0)
    return (y_p.reshape(batch, seq, D_MODEL), y_s.reshape(dec_batch, t_len, D_MODEL),
            st("wk_p"), st("wv_p"), st("ssm_p"), st("mk_p"), st("mv_p"), st("wk_s"), st("wv_s"), st("ssm_s"))
```

```python
import functools
import math

import jax
import jax.numpy as jnp
from jax import lax
from jax.experimental import pallas as pl
from jax.experimental.pallas import tpu as pltpu

F32 = jnp.float32
BF16 = jnp.bfloat16

D_MODEL = 1024
CHUNK = 64
EPS = 1e-6
NEG_INF = -1e30
N_HEADS = 8
N_KV_HEADS = 2
HEAD_DIM = 64
ATTN_WIDTH = N_HEADS * HEAD_DIM
KV_WIDTH = N_KV_HEADS * HEAD_DIM
WINDOW = 128
ROPE_THETA = 10000.0
SSM_WIDTH = D_MODEL - ATTN_WIDTH
SSM_GROUP = 16
N_SSM_GROUPS = SSM_WIDTH // SSM_GROUP
SSM_STATE = 64
N_STATE = N_SSM_GROUPS * SSM_STATE
IN_WIDTH = ATTN_WIDTH + 2 * KV_WIDTH + SSM_WIDTH
N_MEM = 256
N_XHEADS = 4
XHEAD_DIM = D_MODEL // N_XHEADS
N_EXPERT_GROUPS = 4
EXPERTS_PER_GROUP = 8
N_EXPERTS = N_EXPERT_GROUPS * EXPERTS_PER_GROUP
EXPERT_FF = 128
PAST_LEN = 4096

LANES = 128
SUBLANES = 8
VMEM_LIMIT = 56 * 1024 * 1024


def _cparams(*sem):
    return pltpu.CompilerParams(dimension_semantics=sem, vmem_limit_bytes=VMEM_LIMIT)


def _rms(x, g):
    return x * lax.rsqrt(jnp.mean(x * x, axis=-1, keepdims=True) + EPS) * g


def _layer_spec(arr, l):
    shape = arr.shape[1:]
    zeros = (0,) * len(shape)
    return pl.BlockSpec((None,) + shape, lambda *_: (l,) + zeros, pipeline_mode=pl.Buffered(1))


def _rope_pairs(t, cos, sin_signed, first_half):
    swapped = jnp.where(first_half, pltpu.roll(t, LANES - HEAD_DIM // 2, 1), pltpu.roll(t, HEAD_DIM // 2, 1))
    return t * cos + swapped * sin_signed


def _two_source_specs(xp, xs, n_prompt, tm, width):
    npt = n_prompt // tm
    s_off = 0 if xs is xp else npt
    s_last = xs.shape[0] // tm - 1
    pspec = pl.BlockSpec((tm, width), lambda i, *_: (jnp.minimum(i, npt - 1), 0))
    sspec = pl.BlockSpec((tm, width), lambda i, *_: (jnp.clip(i - s_off, npt - s_off, s_last), 0))
    return pspec, sspec


def _in_proj_kernel(n_prompt_tiles, xp_ref, xs_ref, g_ref, w_ref, cos_ref, sin_ref, q_ref, k_ref, v_ref, u_ref):
    x = jnp.where(pl.program_id(0) < n_prompt_tiles, xp_ref[...], xs_ref[...])
    xn = _rms(x, g_ref[...])
    z = jnp.dot(xn.astype(BF16), w_ref[...], preferred_element_type=F32)
    cos = cos_ref[...]
    sin = sin_ref[...]
    lane = lax.broadcasted_iota(jnp.int32, cos.shape, 1)
    first_half = (lane % HEAD_DIM) < (HEAD_DIM // 2)
    scale = 1.0 / math.sqrt(HEAD_DIM)
    for j in range(ATTN_WIDTH // LANES):
        t = z[:, j * LANES:(j + 1) * LANES]
        q_ref[:, j * LANES:(j + 1) * LANES] = (_rope_pairs(t, cos, sin, first_half) * scale).astype(BF16)
    k_ref[...] = _rope_pairs(z[:, ATTN_WIDTH:ATTN_WIDTH + KV_WIDTH], cos, sin, first_half)
    v_ref[...] = z[:, ATTN_WIDTH + KV_WIDTH:ATTN_WIDTH + 2 * KV_WIDTH]
    u_ref[...] = z[:, ATTN_WIDTH + 2 * KV_WIDTH:]


def _in_proj(xp, xs, n, g, w_bf16, l, cos_tab, sin_tab, n_prompt, seq, tm):
    n_prompt_tiles = n_prompt // tm
    tiles_per_seq = seq // tm

    def tab_map(i):
        return (jnp.where(i < n_prompt_tiles, i % tiles_per_seq, tiles_per_seq), 0)

    row = lambda w: pl.BlockSpec((tm, w), lambda i: (i, 0))
    return pl.pallas_call(
        functools.partial(_in_proj_kernel, n_prompt_tiles),
        grid=(n // tm,),
        in_specs=[*_two_source_specs(xp, xs, n_prompt, tm, D_MODEL), _layer_spec(g, l), _layer_spec(w_bf16, l),
                  pl.BlockSpec((tm, LANES), tab_map),
                  pl.BlockSpec((tm, LANES), tab_map)],
        out_specs=[row(ATTN_WIDTH), row(KV_WIDTH), row(KV_WIDTH), row(SSM_WIDTH)],
        out_shape=[jax.ShapeDtypeStruct((n, ATTN_WIDTH), BF16),
                   jax.ShapeDtypeStruct((n, KV_WIDTH), F32),
                   jax.ShapeDtypeStruct((n, KV_WIDTH), F32),
                   jax.ShapeDtypeStruct((n, SSM_WIDTH), F32)],
        compiler_params=_cparams("parallel"),
        name="in_proj",
    )(xp, xs, g, w_bf16, cos_tab, sin_tab)


def _kv_pairs(keys, vals):
    lane = lax.broadcasted_iota(jnp.int32, keys.shape, 1)
    low = lane < HEAD_DIM
    k_sw = pltpu.roll(keys, HEAD_DIM, 1)
    v_sw = pltpu.roll(vals, HEAD_DIM, 1)
    kk = [jnp.where(low, keys, k_sw).astype(BF16), jnp.where(low, k_sw, keys).astype(BF16)]
    vv = [jnp.where(low, vals, v_sw).astype(BF16), jnp.where(low, v_sw, vals).astype(BF16)]
    return kk, vv


def _attend_pairs(q, kk, vv, key_rows, mask_add, sink_ref, o_ref, row0):
    tq = q.shape[0]
    qlane = lax.broadcasted_iota(jnp.int32, (tq, LANES), 1)
    qlow = qlane < HEAD_DIM
    row_top = lax.broadcasted_iota(jnp.int32, (2 * tq, 1), 0) < tq
    zero = jnp.zeros((), BF16)
    for pair in range(N_HEADS // 2):
        kv = pair // (N_HEADS // N_KV_HEADS // 2)
        qp = q[:, pair * LANES:(pair + 1) * LANES]
        qs = jnp.concatenate([jnp.where(qlow, qp, zero), jnp.where(qlow, zero, qp)], axis=0)
        s = lax.dot_general(qs, kk[kv][key_rows, :], (((1,), (1,)), ((), ())), preferred_element_type=F32)
        if mask_add is not None:
            s = s + mask_add
        sink = jnp.where(row_top, sink_ref[2 * pair], sink_ref[2 * pair + 1])
        m = jnp.maximum(jnp.max(s, axis=-1, keepdims=True), sink)
        e = jnp.exp(s - m)
        p = e / (jnp.sum(e, axis=-1, keepdims=True) + jnp.exp(sink - m))
        o = jnp.dot(p.astype(BF16), vv[kv][key_rows, :], preferred_element_type=F32)
        o_ref[row0:row0 + tq, pair * LANES:(pair + 1) * LANES] = jnp.where(qlow, o[:tq], o[tq:])


def _attend_blocks(blocks, sink_ref, o_ref):
    tq = blocks[0][0].shape[0]
    qlane = lax.broadcasted_iota(jnp.int32, (tq, LANES), 1)
    qlow = qlane < HEAD_DIM
    row_top = lax.broadcasted_iota(jnp.int32, (2 * tq, 1), 0) < tq
    zero = jnp.zeros((), BF16)
    scores, sinks = [], []
    for q, kk, vv, key_rows, mask_add, row0 in blocks:
        for pair in range(N_HEADS // 2):
            kv = pair // (N_HEADS // N_KV_HEADS // 2)
            qp = q[:, pair * LANES:(pair + 1) * LANES]
            qs = jnp.concatenate([jnp.where(qlow, qp, zero), jnp.where(qlow, zero, qp)], axis=0)
            s = lax.dot_general(qs, kk[kv][key_rows, :], (((1,), (1,)), ((), ())), preferred_element_type=F32)
            scores.append(s if mask_add is None else s + mask_add)
            sinks.append(jnp.where(row_top, sink_ref[2 * pair], sink_ref[2 * pair + 1]))
    s = jnp.concatenate(scores, axis=0)
    sink = jnp.concatenate(sinks, axis=0)
    m = jnp.maximum(jnp.max(s, axis=-1, keepdims=True), sink)
    e = jnp.exp(s - m)
    p = (e / (jnp.sum(e, axis=-1, keepdims=True) + jnp.exp(sink - m))).astype(BF16)
    piece = 0
    for q, kk, vv, key_rows, mask_add, row0 in blocks:
        for pair in range(N_HEADS // 2):
            kv = pair // (N_HEADS // N_KV_HEADS // 2)
            o = jnp.dot(p[piece * 2 * tq:(piece + 1) * 2 * tq, :], vv[kv][key_rows, :], preferred_element_type=F32)
            o_ref[row0:row0 + tq, pair * LANES:(pair + 1) * LANES] = jnp.where(qlow, o[:tq], o[tq:])
            piece += 1


ATTN_SUB = WINDOW
ATTN_TILE = 2 * ATTN_SUB


def _attn_prompt_kernel(sink_ref, ma_ref, mb_ref, q_ref, kp_ref, kc_ref, vp_ref, vc_ref, o_ref):
    kk, vv = _kv_pairs(jnp.concatenate([kp_ref[...], kc_ref[...]], axis=0),
                       jnp.concatenate([vp_ref[...], vc_ref[...]], axis=0))
    for s, m_ref in enumerate((ma_ref, mb_ref)):
        rows = slice(s * ATTN_SUB, s * ATTN_SUB + 2 * WINDOW)
        _attend_pairs(q_ref[s * ATTN_SUB:(s + 1) * ATTN_SUB, :], kk, vv, rows, m_ref[...], sink_ref, o_ref, s * ATTN_SUB)


def _band_masks():
    r = (jnp.arange(2 * ATTN_SUB) % ATTN_SUB)[:, None] // CHUNK
    c = jnp.arange(2 * WINDOW)[None, :]
    band = (c // CHUNK >= r) & (c // CHUNK <= r + WINDOW // CHUNK)
    masks = jnp.stack([band, band & (c >= WINDOW)])
    return jnp.where(masks, 0.0, NEG_INF).astype(F32)


def _attn_prompt(q, k, v, sink, batch, seq):
    nt = seq // ATTN_TILE
    per_seq = seq // ATTN_SUB
    cur = lambda b, i: (b * nt + i, 0)
    prev = lambda b, i: (b * per_seq + jnp.maximum(2 * i - 1, 0), 0)
    masks = _band_masks()
    return pl.pallas_call(
        _attn_prompt_kernel,
        grid=(batch, nt),
        in_specs=[pl.BlockSpec(memory_space=pltpu.SMEM),
                  pl.BlockSpec((None, 2 * ATTN_SUB, 2 * WINDOW), lambda b, i: (jnp.where(i == 0, 1, 0), 0, 0)),
                  pl.BlockSpec((None, 2 * ATTN_SUB, 2 * WINDOW), lambda b, i: (0, 0, 0)),
                  pl.BlockSpec((ATTN_TILE, ATTN_WIDTH), cur),
                  pl.BlockSpec((ATTN_SUB, KV_WIDTH), prev),
                  pl.BlockSpec((ATTN_TILE, KV_WIDTH), cur),
                  pl.BlockSpec((ATTN_SUB, KV_WIDTH), prev),
                  pl.BlockSpec((ATTN_TILE, KV_WIDTH), cur)],
        out_specs=pl.BlockSpec((ATTN_TILE, ATTN_WIDTH), cur),
        out_shape=jax.ShapeDtypeStruct((batch * seq, ATTN_WIDTH), F32),
        compiler_params=_cparams("parallel", "parallel"),
        name="attn_prompt",
    )(sink, masks, masks, q, k, k, v, v)


def _attn_sample_kernel(sink_ref, q_ref, ck_ref, cv_ref, k_ref, v_ref, o_ref):
    nb = ck_ref.shape[0]
    t = q_ref.shape[0] // nb
    blocks = []
    for b in range(nb):
        rows = slice(b * t, (b + 1) * t)
        kk, vv = _kv_pairs(jnp.concatenate([ck_ref[b], k_ref[rows, :]], axis=0),
                           jnp.concatenate([cv_ref[b], v_ref[rows, :]], axis=0))
        blocks.append((q_ref[rows, :], kk, vv, slice(None), None, b * t))
    _attend_blocks(blocks, sink_ref, o_ref)


def _attn_sample(q, k, v, cache_k, cache_v, l, sink, n_prompt, dec_batch, t, nb):
    w = cache_k.shape[2]
    rows = nb * t
    base = n_prompt // rows
    tok = lambda width: pl.BlockSpec((rows, width), lambda i: (base + i, 0))
    cache = pl.BlockSpec((None, nb, w, KV_WIDTH), lambda i: (l, i, 0, 0))
    return pl.pallas_call(
        _attn_sample_kernel,
        grid=(dec_batch // nb,),
        in_specs=[pl.BlockSpec(memory_space=pltpu.SMEM),
                  tok(ATTN_WIDTH), cache, cache, tok(KV_WIDTH), tok(KV_WIDTH)],
        out_specs=pl.BlockSpec((rows, ATTN_WIDTH), lambda i: (i, 0)),
        out_shape=jax.ShapeDtypeStruct((dec_batch * t, ATTN_WIDTH), F32),
        compiler_params=_cparams("parallel"),
        name="attn_sample",
    )(sink, q, cache_k, cache_v, k, v)


SSM_BLOCK = SUBLANES


def _cmul(ar, ai, br, bi):
    return ar * br - ai * bi, ar * bi + ai * br


def _ssm_disc_kernel(lre_ref, lim_ref, dt_ref, bre_ref, bim_ref, cre_ref, cim_ref,
                     we_ref, tv_ref, coef_ref, vt_ref, wb_ref):
    we_ref[...] = jnp.zeros(we_ref.shape, we_ref.dtype)
    vt_ref[...] = jnp.zeros(vt_ref.shape, vt_ref.dtype)
    wb_ref[...] = jnp.zeros(wb_ref.shape, wb_ref.dtype)
    lane = lax.broadcasted_iota(jnp.int32, (SSM_GROUP, LANES), 1)
    half = [lane < SSM_STATE, lane >= SSM_STATE]
    row8 = lax.broadcasted_iota(jnp.int32, (SUBLANES, LANES), 0)
    for q in range(STATE_TILES):
        lre = lre_ref[q]
        lim = lim_ref[q]
        dt = dt_ref[q]
        mag = jnp.exp(lre * dt)
        ang = lim * dt
        lbr = mag * jnp.cos(ang)
        lbi = mag * jnp.sin(ang)
        nr, ni = lbr - 1.0, lbi
        den = lre * lre + lim * lim
        fr = (nr * lre + ni * lim) / den
        fi = (ni * lre - nr * lim) / den
        bbr, bbi = _cmul(fr, fi, bre_ref[q], bim_ref[q])
        cr, ci = cre_ref[q], cim_ref[q]
        pw = [(jnp.ones_like(lbr), jnp.zeros_like(lbr))]
        for _ in range(SSM_BLOCK):
            pw.append(_cmul(pw[-1][0], pw[-1][1], lbr, lbi))
        re_l = slice(q * 2 * LANES, q * 2 * LANES + LANES)
        im_l = slice(q * 2 * LANES + LANES, (q + 1) * 2 * LANES)
        for h in range(2):
            g = 2 * q + h
            grow = lambda blk: slice(blk * LANES + g * SSM_GROUP, blk * LANES + (g + 1) * SSM_GROUP)
            for j in range(SSM_BLOCK):
                wr, wi = _cmul(pw[SSM_BLOCK - 1 - j][0], pw[SSM_BLOCK - 1 - j][1], bbr, bbi)
                we_ref[grow(j), re_l] = jnp.where(half[h], wr, 0.0).astype(we_ref.dtype)
                we_ref[grow(j), im_l] = jnp.where(half[h], wi, 0.0).astype(we_ref.dtype)
            for d in range(SSM_BLOCK + 1):
                xr, xi = _cmul(cr, ci, pw[d][0], pw[d][1])
                vt_ref[grow(d), re_l] = jnp.where(half[h], xr, 0.0)
                vt_ref[grow(d), im_l] = jnp.where(half[h], -xi, 0.0)
            wb_ref[grow(0), re_l] = jnp.where(half[h], bbr, 0.0)
            wb_ref[grow(0), im_l] = jnp.where(half[h], bbi, 0.0)
        l8 = [pw[SSM_BLOCK]]
        for _ in range(SUBLANES - 1):
            l8.append(_cmul(l8[-1][0], l8[-1][1], pw[SSM_BLOCK][0], pw[SSM_BLOCK][1]))
        co = slice(q * LANES, (q + 1) * LANES)
        for kind, sh in enumerate((1, 2, 4)):
            for a in range(2):
                coef_ref[kind, a, :, co] = jnp.where(row8 >= sh, l8[sh - 1][a], 0.0)
        for a in range(2):
            tab = jnp.zeros((SUBLANES, LANES), F32)
            for k in range(SUBLANES):
                tab = jnp.where(row8 == k, l8[k][a], tab)
            coef_ref[3, a, :, co] = tab
    t0 = lax.dot_general(wb_ref[...], vt_ref[0:BLOCK_LANES, :], (((1,), (1,)), ((), ())),
                         preferred_element_type=F32, precision=lax.Precision.HIGHEST)
    for j in range(SSM_BLOCK):
        if j:
            tv_ref[j * LANES:(j + 1) * LANES, 0:j * LANES] = jnp.zeros((LANES, j * LANES), tv_ref.dtype)
        tv_ref[j * LANES:(j + 1) * LANES, j * LANES:] = t0[:, 0:BLOCK_LANES - j * LANES].astype(tv_ref.dtype)
    tv_ref[BLOCK_LANES:, :] = jnp.transpose(vt_ref[LANES:, :]).astype(tv_ref.dtype)


def _ssm_discretize(lam_re, lam_im, log_dt, b_re, b_im, c_re, c_im):
    g, p = lam_re.shape
    npair = g // 2
    pair = lambda a: a.reshape(npair, 1, 2 * p)
    rows = lambda a: a.reshape(npair, 2, SSM_GROUP, p).transpose(0, 2, 1, 3).reshape(npair, SSM_GROUP, 2 * p)
    dt = jnp.repeat(jnp.exp(log_dt), p).reshape(npair, 1, 2 * p)
    vec = pl.BlockSpec((STATE_TILES, 1, LANES), lambda m: (m, 0, 0))
    mat = pl.BlockSpec((STATE_TILES, SSM_GROUP, LANES), lambda m: (m, 0, 0))
    return pl.pallas_call(
        _ssm_disc_kernel,
        grid=(N_CH_BLOCKS,),
        in_specs=[vec, vec, vec, mat, mat, mat, mat],
        out_specs=[pl.BlockSpec((None, BLOCK_LANES, STATE_LANES), lambda m: (m, 0, 0)),
                   pl.BlockSpec((None, BLOCK_LANES + STATE_LANES, BLOCK_LANES), lambda m: (m, 0, 0)),
                   pl.BlockSpec((None, 4, 2, SUBLANES, STATE_LANES // 2), lambda m: (m, 0, 0, 0, 0))],
        out_shape=[jax.ShapeDtypeStruct((N_CH_BLOCKS, BLOCK_LANES, STATE_LANES), BF16),
                   jax.ShapeDtypeStruct((N_CH_BLOCKS, BLOCK_LANES + STATE_LANES, BLOCK_LANES), BF16),
                   jax.ShapeDtypeStruct((N_CH_BLOCKS, 4, 2, SUBLANES, STATE_LANES // 2), F32)],
        scratch_shapes=[pltpu.VMEM(((SSM_BLOCK + 1) * LANES, STATE_LANES), F32),
                        pltpu.VMEM((LANES, STATE_LANES), F32)],
        compiler_params=_cparams("parallel"),
        name="ssm_discretize",
    )(pair(lam_re), pair(lam_im), dt, rows(b_re.transpose(0, 2, 1)), rows(b_im.transpose(0, 2, 1)),
      rows(c_re), rows(c_im))


GROUPS_PER_CH_BLOCK = LANES // SSM_GROUP
N_CH_BLOCKS = SSM_WIDTH // LANES
STATE_LANES = 2 * GROUPS_PER_CH_BLOCK * SSM_STATE
STATE_TILES = STATE_LANES // (2 * LANES)
BLOCK_LANES = SSM_BLOCK * LANES


def _ssm_kernel(nb, u_ref, s0_ref, we_ref, tv_ref, d_ref, coef_ref, y_ref, fin_ref, st_ref, sprev_ref, ucat_ref):
    t_idx = pl.program_id(2)
    r = st_ref.shape[1] - SUBLANES
    rows = nb * r

    @pl.when(t_idx == 0)
    def _():
        for b in range(nb):
            st_ref[b, 0:SUBLANES, :] = jnp.broadcast_to(s0_ref[b], (SUBLANES, STATE_LANES))

    for j in range(SSM_BLOCK):
        ucat_ref[:, j * LANES:(j + 1) * LANES] = u_ref[pl.ds(j, rows, stride=SSM_BLOCK), :].astype(BF16)
    e = jnp.dot(ucat_ref[...], we_ref[...], preferred_element_type=F32)
    st_ref[:, SUBLANES:, :] = e.reshape(nb, r, STATE_LANES)

    first_row = lax.broadcasted_iota(jnp.int32, (SUBLANES, LANES), 0) == 0

    def group(rg, carry):
        r0 = pl.multiple_of(SUBLANES + rg * SUBLANES, SUBLANES)
        rp = pl.multiple_of(rg * SUBLANES, SUBLANES)
        for b in range(nb):
            for q in range(STATE_TILES):
                re_sl = pl.ds(q * 2 * LANES, LANES)
                im_sl = pl.ds(q * 2 * LANES + LANES, LANES)
                co = pl.ds(q * LANES, LANES)
                xr = st_ref[b, pl.ds(r0, SUBLANES), re_sl]
                xi = st_ref[b, pl.ds(r0, SUBLANES), im_sl]
                for step, sh in enumerate((1, 2, 4)):
                    ar, ai = _cmul(coef_ref[step, 0, :, co], coef_ref[step, 1, :, co],
                                   pltpu.roll(xr, sh, 0), pltpu.roll(xi, sh, 0))
                    xr = xr + ar
                    xi = xi + ai
                pr = jnp.broadcast_to(st_ref[b, pl.ds(rp, SUBLANES), re_sl][SUBLANES - 1:, :], (SUBLANES, LANES))
                pi = jnp.broadcast_to(st_ref[b, pl.ds(rp, SUBLANES), im_sl][SUBLANES - 1:, :], (SUBLANES, LANES))
                ar, ai = _cmul(coef_ref[3, 0, :, co], coef_ref[3, 1, :, co], pr, pi)
                xr = xr + ar
                xi = xi + ai
                st_ref[b, pl.ds(r0, SUBLANES), re_sl] = xr
                st_ref[b, pl.ds(r0, SUBLANES), im_sl] = xi
                out_rows = pl.ds(pl.multiple_of(b * r + rg * SUBLANES, SUBLANES), SUBLANES)
                sprev_ref[out_rows, re_sl] = jnp.where(first_row, pr, pltpu.roll(xr, 1, 0))
                sprev_ref[out_rows, im_sl] = jnp.where(first_row, pi, pltpu.roll(xi, 1, 0))
        return carry

    lax.fori_loop(0, r // SUBLANES, group, 0)

    lhs = jnp.concatenate([ucat_ref[...], sprev_ref[...].astype(BF16)], axis=1)
    ycat = jnp.dot(lhs, tv_ref[...], preferred_element_type=F32)
    d = d_ref[...]
    for t in range(SSM_BLOCK):
        tok = pl.ds(t, rows, stride=SSM_BLOCK)
        y_ref[tok, :] = ycat[:, t * LANES:(t + 1) * LANES] + d * u_ref[tok, :]

    for b in range(nb):
        tail = st_ref[b, r:r + SUBLANES, :]
        st_ref[b, 0:SUBLANES, :] = tail
        fin_ref[b] = tail[SUBLANES - 1:SUBLANES, :]


def _ssm(u, s0, we, tv, d, l, coef, row_base, n_seq, seq_len, nb, tt):
    rows = nb * tt
    r = tt // SSM_BLOCK
    nt = seq_len // tt
    base = row_base // rows
    return pl.pallas_call(
        functools.partial(_ssm_kernel, nb),
        grid=(N_CH_BLOCKS, n_seq // nb, nt),
        in_specs=[pl.BlockSpec((rows, LANES), lambda m, i, j: (base + i * nt + j, m)),
                  pl.BlockSpec((nb, 1, STATE_LANES), lambda m, i, j: (i, 0, m)),
                  pl.BlockSpec((None, BLOCK_LANES, STATE_LANES), lambda m, i, j: (m, 0, 0)),
                  pl.BlockSpec((None, BLOCK_LANES + STATE_LANES, BLOCK_LANES), lambda m, i, j: (m, 0, 0)),
                  pl.BlockSpec((None, 1, LANES), lambda m, i, j: (l, 0, m)),
                  pl.BlockSpec((None, 4, 2, SUBLANES, STATE_LANES // 2), lambda m, i, j: (m, 0, 0, 0, 0))],
        out_specs=[pl.BlockSpec((rows, LANES), lambda m, i, j: (i * nt + j, m)),
                   pl.BlockSpec((nb, 1, STATE_LANES), lambda m, i, j: (i, 0, m))],
        out_shape=[jax.ShapeDtypeStruct((n_seq * seq_len, SSM_WIDTH), F32),
                   jax.ShapeDtypeStruct((n_seq, 1, N_CH_BLOCKS * STATE_LANES), F32)],
        scratch_shapes=[pltpu.VMEM((nb, SUBLANES + r, STATE_LANES), F32),
                        pltpu.VMEM((nb * r, STATE_LANES), F32),
                        pltpu.VMEM((nb * r, BLOCK_LANES), BF16)],
        compiler_params=_cparams("parallel", "parallel", "arbitrary"),
        name="ssm_scan",
    )(u, s0, we, tv, d, coef)


def _state_to_tiles(s):
    lead = s.shape[:-3]
    t = s.reshape(lead + (N_CH_BLOCKS, STATE_TILES, 2, SSM_STATE, 2))
    t = jnp.moveaxis(t, -1, -3)
    return t.reshape(lead + (1, N_CH_BLOCKS * STATE_LANES))


def _tiles_to_state(f):
    b = f.shape[0]
    t = f.reshape(b, N_CH_BLOCKS, STATE_TILES, 2, 2, SSM_STATE)
    t = jnp.moveaxis(t, 3, -1)
    return t.reshape(b, N_SSM_GROUPS, SSM_STATE, 2)


def _merge_kernel(n_prompt_tiles, xp_ref, xs_ref, ap_ref, as_ref, yp_ref, ys_ref, wglu_ref, bglu_ref, ga_ref, gs_ref,
                  wout_ref, o_ref):
    is_prompt = pl.program_id(0) < n_prompt_tiles
    attn = jnp.where(is_prompt, ap_ref[...], as_ref[...])
    g = jax.nn.gelu(jnp.where(is_prompt, yp_ref[...], ys_ref[...]))
    glu = g * jax.nn.sigmoid(jnp.dot(g.astype(BF16), wglu_ref[...], preferred_element_type=F32) + bglu_ref[...])
    na = _rms(attn, ga_ref[...]).astype(BF16)
    ns = _rms(glu, gs_ref[...]).astype(BF16)
    o = jnp.dot(na, wout_ref[0:ATTN_WIDTH, :], preferred_element_type=F32)
    o = o + jnp.dot(ns, wout_ref[ATTN_WIDTH:, :], preferred_element_type=F32)
    o_ref[...] = jnp.where(is_prompt, xp_ref[...], xs_ref[...]) + o


def _merge(xp, xs, n, attn_p, attn_s, y_p, y_s, wglu, bglu, ga, gs, wout, l, tm):
    npt = attn_p.shape[0] // tm
    nst = attn_s.shape[0] // tm
    row = lambda w: pl.BlockSpec((tm, w), lambda i: (i, 0))
    prow = lambda w: pl.BlockSpec((tm, w), lambda i: (jnp.minimum(i, npt - 1), 0))
    srow = lambda w: pl.BlockSpec((tm, w), lambda i: (jnp.clip(i - npt, 0, nst - 1), 0))
    return pl.pallas_call(
        functools.partial(_merge_kernel, npt),
        grid=(n // tm,),
        in_specs=[*_two_source_specs(xp, xs, attn_p.shape[0], tm, D_MODEL),
                  prow(ATTN_WIDTH), srow(ATTN_WIDTH), prow(SSM_WIDTH), srow(SSM_WIDTH),
                  _layer_spec(wglu, l), _layer_spec(bglu, l), _layer_spec(ga, l), _layer_spec(gs, l),
                  _layer_spec(wout, l)],
        out_specs=row(D_MODEL),
        out_shape=jax.ShapeDtypeStruct((n, D_MODEL), F32),
        compiler_params=_cparams("parallel"),
        name="merge_heads",
    )(xp, xs, attn_p, attn_s, y_p, y_s, wglu, bglu, ga, gs, wout)


def _mem_kv_kernel(m_ref, g_ref, wk_ref, wv_ref, k_ref, v_ref):
    mn = _rms(m_ref[...], g_ref[...]).astype(BF16)
    k_ref[...] = jnp.dot(mn, wk_ref[...], preferred_element_type=F32)
    v_ref[...] = jnp.dot(mn, wv_ref[...], preferred_element_type=F32)


def _mem_kv(mem, g, wk, wv, l, tm):
    n = mem.shape[0]
    row = pl.BlockSpec((tm, D_MODEL), lambda i: (i, 0))
    return pl.pallas_call(
        _mem_kv_kernel,
        grid=(n // tm,),
        in_specs=[row, _layer_spec(g, l), _layer_spec(wk, l), _layer_spec(wv, l)],
        out_specs=[row, row],
        out_shape=[jax.ShapeDtypeStruct((n, D_MODEL), F32)] * 2,
        compiler_params=_cparams("parallel"),
        name="mem_kv",
    )(mem, g, wk, wv)


def _xattn_kernel(nb, x_ref, g_ref, wq_ref, wo_ref, mk_ref, mv_ref, o_ref, att_ref):
    t = x_ref.shape[0] // nb
    x = x_ref[...]
    hn = _rms(x, g_ref[...]).astype(BF16)
    q = jnp.dot(hn, wq_ref[...], preferred_element_type=F32) * (1.0 / math.sqrt(XHEAD_DIM))
    q = q.astype(BF16)
    heads = [slice(h * XHEAD_DIM, (h + 1) * XHEAD_DIM) for h in range(N_XHEADS)]
    scores = []
    for b in range(nb):
        mk = mk_ref[b * N_MEM:(b + 1) * N_MEM, :].astype(BF16)
        for sl in heads:
            scores.append(lax.dot_general(q[b * t:(b + 1) * t, sl], mk[:, sl], (((1,), (1,)), ((), ())),
                                          preferred_element_type=F32))
    s = jnp.concatenate(scores, axis=0)
    e = jnp.exp(s - jnp.max(s, axis=-1, keepdims=True))
    p = (e / jnp.sum(e, axis=-1, keepdims=True)).astype(BF16)
    for b in range(nb):
        mv = mv_ref[b * N_MEM:(b + 1) * N_MEM, :].astype(BF16)
        for h, sl in enumerate(heads):
            r0 = (b * N_XHEADS + h) * t
            att_ref[b * t:(b + 1) * t, sl] = jnp.dot(p[r0:r0 + t, :], mv[:, sl], preferred_element_type=F32)
    o = jnp.dot(att_ref[...].astype(BF16), wo_ref[...], preferred_element_type=F32)
    o_ref[...] = x + o


def _xattn(x, g, wq, wo, l, mk, mv, mem_spec, row_base, n_rows, nb, tm):
    base = row_base // tm
    xspec = pl.BlockSpec((tm, D_MODEL), lambda i: (base + i, 0))
    return pl.pallas_call(
        functools.partial(_xattn_kernel, nb),
        grid=(n_rows // tm,),
        in_specs=[xspec, _layer_spec(g, l), _layer_spec(wq, l), _layer_spec(wo, l), mem_spec, mem_spec],
        out_specs=xspec,
        out_shape=jax.ShapeDtypeStruct(x.shape, F32),
        scratch_shapes=[pltpu.VMEM((tm, D_MODEL), F32)],
        input_output_aliases={0: 0},
        compiler_params=_cparams("parallel"),
        name="cross_attn",
    )(x, g, wq, wo, mk, mv)


ROUTER_LANES = LANES
EXPERT_LANE0 = N_EXPERT_GROUPS
EXPERTS_PER_STEP = EXPERTS_PER_GROUP
MOE_SRC_TILE = 512
MOE_RUN_ALIGN = 16
MOE_SORTED_ROWS = 640
MOE_TILE = 1024


def _dot_f32_3pass(x, w):
    xh = x.astype(BF16)
    xl = (x - xh.astype(F32)).astype(BF16)
    wh = w.astype(BF16)
    wl = (w - wh.astype(F32)).astype(BF16)
    dot = lambda a, b: jnp.dot(a, b, preferred_element_type=F32)
    return dot(xh, wh) + (dot(xl, wh) + dot(xh, wl))


def _route(logits):
    lane_i = lax.broadcasted_iota(jnp.int32, logits.shape, 1)
    lane = lane_i.astype(F32)
    neg = jnp.float32(-jnp.inf)
    is_g = lane_i < N_EXPERT_GROUPS
    gl = jnp.where(is_g, logits, neg)
    gmax = jnp.max(gl, axis=-1, keepdims=True)
    gidx = jnp.min(jnp.where(gl == gmax, lane, float(ROUTER_LANES)), axis=-1, keepdims=True)
    g_w = 1.0 / jnp.sum(jnp.where(is_g, jnp.exp(gl - gmax), 0.0), axis=-1, keepdims=True)
    first = EXPERT_LANE0 + gidx * EXPERTS_PER_GROUP
    sel = (lane >= first) & (lane < first + EXPERTS_PER_GROUP)
    el = jnp.where(sel, logits, neg)
    m1 = jnp.max(el, axis=-1, keepdims=True)
    i1 = jnp.min(jnp.where(el == m1, lane, float(ROUTER_LANES)), axis=-1, keepdims=True)
    el2 = jnp.where(lane == i1, neg, el)
    m2 = jnp.max(el2, axis=-1, keepdims=True)
    i2 = jnp.min(jnp.where(el2 == m2, lane, float(ROUTER_LANES)), axis=-1, keepdims=True)
    r = jnp.exp(m2 - m1)
    w1 = g_w / (1.0 + r)
    w2 = w1 * r
    return jnp.where(lane == i1, w1, jnp.where(lane == i2, w2, 0.0)), gidx


def _moe_pre_kernel(x_ref, g_ref, wr_ref, br_ref, xn_ref, gate_ref, meta_ref, cnt_ref, tri_ref):
    tm = x_ref.shape[0]

    @pl.when(pl.program_id(0) == 0)
    def _():
        r = lax.broadcasted_iota(jnp.int32, (tm, tm), 0)
        c = lax.broadcasted_iota(jnp.int32, (tm, tm), 1)
        tri_ref[...] = (c <= r).astype(BF16)

    xn = _rms(x_ref[...], g_ref[...])
    xn_ref[...] = xn.astype(BF16)
    gates, gidx = _route(_dot_f32_3pass(xn, wr_ref[...]) + br_ref[...])
    gate_ref[...] = gates
    lane = lax.broadcasted_iota(jnp.int32, gates.shape, 1).astype(F32)
    onehot = lane == gidx
    incl = jnp.dot(tri_ref[...], onehot.astype(BF16), preferred_element_type=F32)
    rank = jnp.sum(jnp.where(onehot, incl, 0.0), axis=-1, keepdims=True) - 1.0
    meta_ref[...] = jnp.where(lane == 0.0, gidx, jnp.where(lane == 1.0, rank, 0.0))
    cnt_ref[...] = incl[tm - 1:tm, :]


def _moe_pre(x, g, wr, br, l):
    n = x.shape[0]
    tm = MOE_SRC_TILE
    row = lambda w: pl.BlockSpec((tm, w), lambda i: (i, 0))
    return pl.pallas_call(
        _moe_pre_kernel,
        grid=(n // tm,),
        in_specs=[row(D_MODEL), _layer_spec(g, l), _layer_spec(wr, l), _layer_spec(br, l)],
        out_specs=[row(D_MODEL), row(ROUTER_LANES), row(ROUTER_LANES),
                   pl.BlockSpec((None, 1, ROUTER_LANES), lambda i: (i, 0, 0))],
        out_shape=[jax.ShapeDtypeStruct((n, D_MODEL), BF16), jax.ShapeDtypeStruct((n, ROUTER_LANES), F32),
                   jax.ShapeDtypeStruct((n, ROUTER_LANES), F32),
                   jax.ShapeDtypeStruct((n // tm, 1, ROUTER_LANES), F32)],
        scratch_shapes=[pltpu.VMEM((tm, tm), BF16)],
        compiler_params=_cparams("arbitrary"),
        name="moe_pre",
    )(x, g, wr, br)


def _sort_onehot(meta, start_ref, t):
    gid = meta[:, 0:1]
    pos = meta[:, 1:2]
    for grp in range(N_EXPERT_GROUPS):
        pos = pos + jnp.where(gid == float(grp), start_ref[t * N_EXPERT_GROUPS + grp].astype(F32), 0.0)
    col = lax.broadcasted_iota(jnp.int32, (meta.shape[0], MOE_SORTED_ROWS), 1).astype(F32)
    return (col == pos).astype(BF16)


def _run_copies(t, base, start_ref, off_ref, nblk_ref, pairs, sem, to_hbm):
    for grp in range(N_EXPERT_GROUPS):
        k = t * N_EXPERT_GROUPS + grp
        src0 = base + start_ref[k]
        dst0 = off_ref[k]

        def body(b, carry):
            lo = pl.multiple_of(src0 + b * MOE_RUN_ALIGN, MOE_RUN_ALIGN)
            hi = pl.multiple_of(dst0 + b * MOE_RUN_ALIGN, MOE_RUN_ALIGN)
            for buf, arr in pairs:
                a, h = buf.at[pl.ds(lo, MOE_RUN_ALIGN)], arr.at[pl.ds(hi, MOE_RUN_ALIGN)]
                (pltpu.make_async_copy(a, h, sem) if to_hbm else pltpu.make_async_copy(h, a, sem)).start()
            return carry

        lax.fori_loop(0, nblk_ref[k], body, 0)


def _run_wait(t, nblk_ref, pairs, sem, to_hbm):
    total = nblk_ref[t * N_EXPERT_GROUPS]
    for grp in range(1, N_EXPERT_GROUPS):
        total = total + nblk_ref[t * N_EXPERT_GROUPS + grp]
    rows = total * MOE_RUN_ALIGN

    @pl.when(rows > 0)
    def _():
        for buf, arr in pairs:
            a, h = buf.at[pl.ds(0, rows)], arr.at[pl.ds(0, rows)]
            (pltpu.make_async_copy(a, h, sem) if to_hbm else pltpu.make_async_copy(h, a, sem)).wait()


def _moe_pack_kernel(start_ref, off_ref, nblk_ref, end_ref, xn_ref, gate_ref, meta_ref, xs_hbm, gs_hbm,
                     xbuf, gbuf, zx, zg, sems):
    t = pl.program_id(0)
    last = pl.num_programs(0) - 1
    slot = t % 2
    base = pl.multiple_of(slot * MOE_SORTED_ROWS, MOE_SORTED_ROWS)
    pairs = [(xbuf, xs_hbm), (gbuf, gs_hbm)]

    @pl.when(t >= 2)
    def _():
        _run_wait(t - 2, nblk_ref, pairs, sems.at[slot], True)

    onehot = _sort_onehot(meta_ref[...], start_ref, t)
    tn = (((0,), (0,)), ((), ()))
    rows = pl.ds(base, MOE_SORTED_ROWS)
    xbuf[rows, :] = lax.dot_general(onehot, xn_ref[...], tn, preferred_element_type=F32).astype(BF16)
    gates = gate_ref[...]
    gh = gates.astype(BF16)
    gl = (gates - gh.astype(F32)).astype(BF16)
    gbuf[rows, :] = (lax.dot_general(onehot, gh, tn, preferred_element_type=F32)
                     + lax.dot_general(onehot, gl, tn, preferred_element_type=F32))
    _run_copies(t, base, start_ref, off_ref, nblk_ref, pairs, sems.at[slot], True)

    @pl.when(t == last)
    def _():
        @pl.when(t >= 1)
        def _():
            _run_wait(t - 1, nblk_ref, pairs, sems.at[1 - slot], True)

        _run_wait(t, nblk_ref, pairs, sems.at[slot], True)
        zx[...] = jnp.zeros(zx.shape, zx.dtype)
        zg[...] = jnp.zeros(zg.shape, zg.dtype)
        copies = []
        for grp in range(N_EXPERT_GROUPS):
            tail = pl.ds(pl.multiple_of(end_ref[grp], MOE_RUN_ALIGN), MOE_TILE)
            copies += [pltpu.make_async_copy(zx, xs_hbm.at[tail], sems.at[0]),
                       pltpu.make_async_copy(zg, gs_hbm.at[tail], sems.at[0])]
        for cp in copies:
            cp.start()
        for cp in copies:
            cp.wait()


def _moe_pack(xn, gates, meta, tabs, n_rows):
    n = xn.shape[0]
    tm = MOE_SRC_TILE
    row = lambda w: pl.BlockSpec((tm, w), lambda i, *_: (i, 0))
    any_spec = pl.BlockSpec(memory_space=pl.ANY)
    grid_spec = pltpu.PrefetchScalarGridSpec(
        num_scalar_prefetch=4, grid=(n // tm,),
        in_specs=[row(D_MODEL), row(ROUTER_LANES), row(ROUTER_LANES)],
        out_specs=[any_spec, any_spec],
        scratch_shapes=[pltpu.VMEM((2 * MOE_SORTED_ROWS, D_MODEL), BF16),
                        pltpu.VMEM((2 * MOE_SORTED_ROWS, ROUTER_LANES), F32),
                        pltpu.VMEM((MOE_TILE, D_MODEL), BF16), pltpu.VMEM((MOE_TILE, ROUTER_LANES), F32),
                        pltpu.SemaphoreType.DMA((2,))])
    return pl.pallas_call(
        _moe_pack_kernel,
        grid_spec=grid_spec,
        out_shape=[jax.ShapeDtypeStruct((n_rows, D_MODEL), BF16), jax.ShapeDtypeStruct((n_rows, ROUTER_LANES), F32)],
        compiler_params=_cparams("arbitrary"),
        name="moe_pack",
    )(tabs["start"], tabs["off"], tabs["nblk"], tabs["end"], xn, gates, meta)


def _moe_expert_kernel(blk_ref, grp_ref, valid_ref, x_ref, gate_ref, wg_ref, wu_ref, wd_ref, o_ref):
    i = pl.program_id(0)
    e_step = pl.program_id(1)

    @pl.when(valid_ref[i] > 0)
    def _():
        xn = x_ref[...]
        gates = gate_ref[...]
        lane = lax.broadcasted_iota(jnp.int32, gates.shape, 1)
        first = EXPERT_LANE0 + grp_ref[i] * EXPERTS_PER_GROUP + e_step * EXPERTS_PER_STEP
        hids = []
        for j in range(EXPERTS_PER_STEP):
            w = jnp.concatenate([wg_ref[j], wu_ref[j]], axis=1)
            h = jnp.dot(xn, w, preferred_element_type=F32)
            ge = jnp.sum(jnp.where(lane == first + j, gates, 0.0), axis=-1, keepdims=True)
            hids.append((jax.nn.silu(h[:, :EXPERT_FF]) * h[:, EXPERT_FF:] * ge).astype(BF16))
        o_ref[...] = jnp.dot(jnp.concatenate(hids, axis=1), wd_ref[...], preferred_element_type=F32)


def _moe_experts(xs, gs, tabs, wg, wu, wd, l, n_tiles):
    es = EXPERTS_PER_STEP
    steps = EXPERTS_PER_GROUP // es
    row = lambda w: pl.BlockSpec((MOE_TILE, w), lambda i, e, blk, grp, valid: (blk[i], 0))
    grid_spec = pltpu.PrefetchScalarGridSpec(
        num_scalar_prefetch=3, grid=(n_tiles, steps),
        in_specs=[row(D_MODEL), row(ROUTER_LANES),
                  pl.BlockSpec((None, es, D_MODEL, EXPERT_FF), lambda i, e, blk, grp, valid: (l, grp[i] * steps + e, 0, 0)),
                  pl.BlockSpec((None, es, D_MODEL, EXPERT_FF), lambda i, e, blk, grp, valid: (l, grp[i] * steps + e, 0, 0)),
                  pl.BlockSpec((None, es * EXPERT_FF, D_MODEL), lambda i, e, blk, grp, valid: (l, grp[i] * steps + e, 0))],
        out_specs=row(D_MODEL))
    return pl.pallas_call(
        _moe_expert_kernel,
        grid_spec=grid_spec,
        out_shape=jax.ShapeDtypeStruct((xs.shape[0], D_MODEL), F32),
        compiler_params=_cparams("arbitrary", "arbitrary"),
        name="hier_moe",
    )(tabs["tile_blk"], tabs["tile_grp"], tabs["tile_valid"], xs, gs, wg, wu, wd)


def _moe_unpack_kernel(start_ref, off_ref, nblk_ref, x_ref, meta_ref, ys_hbm, o_ref, ybuf, sems):
    t = pl.program_id(0)
    slot = t % 2
    base = pl.multiple_of(slot * MOE_SORTED_ROWS, MOE_SORTED_ROWS)
    pairs = [(ybuf, ys_hbm)]

    @pl.when(t == 0)
    def _():
        ybuf[...] = jnp.zeros(ybuf.shape, ybuf.dtype)
        _run_copies(t, base, start_ref, off_ref, nblk_ref, pairs, sems.at[slot], False)

    @pl.when(t + 1 < pl.num_programs(0))
    def _():
        nxt = pl.multiple_of((1 - slot) * MOE_SORTED_ROWS, MOE_SORTED_ROWS)
        _run_copies(t + 1, nxt, start_ref, off_ref, nblk_ref, pairs, sems.at[1 - slot], False)

    onehot = _sort_onehot(meta_ref[...], start_ref, t)
    _run_wait(t, nblk_ref, pairs, sems.at[slot], False)
    y = ybuf[pl.ds(base, MOE_SORTED_ROWS), :]
    yh = y.astype(BF16)
    yl = (y - yh.astype(F32)).astype(BF16)
    o_ref[...] = x_ref[...] + (jnp.dot(onehot, yh, preferred_element_type=F32)
                               + jnp.dot(onehot, yl, preferred_element_type=F32))


def _moe_unpack(x, meta, ys, tabs):
    n = x.shape[0]
    tm = MOE_SRC_TILE
    row = lambda w: pl.BlockSpec((tm, w), lambda i, *_: (i, 0))
    grid_spec = pltpu.PrefetchScalarGridSpec(
        num_scalar_prefetch=3, grid=(n // tm,),
        in_specs=[row(D_MODEL), row(ROUTER_LANES), pl.BlockSpec(memory_space=pl.ANY)],
        out_specs=row(D_MODEL),
        scratch_shapes=[pltpu.VMEM((2 * MOE_SORTED_ROWS, D_MODEL), F32), pltpu.SemaphoreType.DMA((2,))])
    return pl.pallas_call(
        _moe_unpack_kernel,
        grid_spec=grid_spec,
        out_shape=jax.ShapeDtypeStruct((n, D_MODEL), F32),
        compiler_params=_cparams("arbitrary"),
        name="moe_unpack",
    )(tabs["start"], tabs["off"], tabs["nblk"], x, meta, ys)


def _moe_tables(cnt, n):
    n_src = cnt.shape[0]
    pad = (cnt + MOE_RUN_ALIGN - 1) // MOE_RUN_ALIGN * MOE_RUN_ALIGN
    worst = n + n_src * (MOE_RUN_ALIGN - 1)
    cap = (worst + 2 * MOE_TILE - 1) // MOE_TILE * MOE_TILE
    start = jnp.cumsum(pad, axis=1) - pad
    total = jnp.sum(pad, axis=0)
    base = jnp.arange(N_EXPERT_GROUPS, dtype=jnp.int32) * cap
    off = base[None, :] + jnp.cumsum(pad, axis=0) - pad
    tiles_g = (total + MOE_TILE - 1) // MOE_TILE
    tile_end = jnp.cumsum(tiles_g)
    n_tiles = n // MOE_TILE + N_EXPERT_GROUPS + (n_src * N_EXPERT_GROUPS * MOE_RUN_ALIGN + MOE_TILE - 1) // MOE_TILE
    i = jnp.minimum(jnp.arange(n_tiles, dtype=jnp.int32), jnp.maximum(tile_end[-1] - 1, 0))
    grp = jnp.minimum(jnp.sum((i[:, None] >= tile_end[None, :]).astype(jnp.int32), axis=1), N_EXPERT_GROUPS - 1)
    first_tile = (tile_end - tiles_g)
    blk = jnp.zeros_like(i)
    for g in range(N_EXPERT_GROUPS):
        blk = blk + jnp.where(grp == g, g * (cap // MOE_TILE) + i - first_tile[g], 0)
    i32 = lambda a: a.astype(jnp.int32)
    tabs = dict(start=i32(start.reshape(-1)), off=i32(off.reshape(-1)), nblk=i32((pad // MOE_RUN_ALIGN).reshape(-1)),
                end=i32(base + total), tile_blk=i32(blk), tile_grp=i32(grp),
                tile_valid=i32(jnp.arange(n_tiles) < tile_end[-1]))
    return tabs, N_EXPERT_GROUPS * cap, n_tiles


def _moe(x, g, wr, br, wg, wu, wd, l):
    n = x.shape[0]
    xn, gates, meta, cnt = _moe_pre(x, g, wr, br, l)
    tabs, n_rows, n_tiles = _moe_tables(cnt[:, 0, :N_EXPERT_GROUPS].astype(jnp.int32), n)
    xs, gs = _moe_pack(xn, gates, meta, tabs, n_rows)
    ys = _moe_experts(xs, gs, tabs, wg, wu, wd, l, n_tiles)
    return _moe_unpack(x, meta, ys, tabs)


def _final_norm_kernel(x_ref, g_ref, o_ref):
    o_ref[...] = _rms(x_ref[...], g_ref[...])


def _final_norm(x, g, row_base, n_rows, tm):
    base = row_base // tm
    return pl.pallas_call(
        _final_norm_kernel,
        grid=(n_rows // tm,),
        in_specs=[pl.BlockSpec((tm, D_MODEL), lambda i: (base + i, 0)),
                  pl.BlockSpec((1, D_MODEL), lambda i: (0, 0))],
        out_specs=pl.BlockSpec((tm, D_MODEL), lambda i: (i, 0)),
        out_shape=jax.ShapeDtypeStruct((n_rows, D_MODEL), F32),
        compiler_params=_cparams("parallel"),
        name="final_norm",
    )(x, g)


def _rope_tables(seq, t_len, tm):
    half = HEAD_DIM // 2
    inv = ROPE_THETA ** (-jnp.arange(half, dtype=F32) / half)
    pos_s = PAST_LEN + jnp.arange(t_len)
    pos = jnp.concatenate([jnp.arange(seq), jnp.tile(pos_s, tm // t_len)]).astype(F32)
    ang = pos[:, None] * inv[None, :]
    cos = jnp.tile(jnp.cos(ang), (1, LANES // half))
    sign = jnp.where((jnp.arange(LANES) % HEAD_DIM) < half, -1.0, 1.0).astype(F32)
    sin = jnp.tile(jnp.sin(ang), (1, LANES // half)) * sign[None, :]
    return cos, sin


def kernel(x_prompt, x_sample, cache_win_k, cache_win_v, state_ssm, cache_mem_k, cache_mem_v, mem_prompt, w_in, attn_sink, lam_re, lam_im, log_dt, ssm_b_re, ssm_b_im, ssm_c_re, ssm_c_im, ssm_d, w_glu, b_glu, g_attn_out, g_ssm_out, w_out, g_mix, g_xattn, g_mem, wq_x, wk_x, wv_x, wo_x, g_ffn, w_group, b_group, w_router, b_router, w_gate, w_up, w_down, g_final):
    batch, seq, _ = x_prompt.shape
    dec_batch, t_len, _ = x_sample.shape
    depth = w_in.shape[0]
    win_rows = cache_win_k.shape[2]
    n_p = batch * seq
    n_s = dec_batch * t_len
    tm_wide = 1024 if (n_p + n_s) % 1024 == 0 else 512
    tm = tm_wide
    tm_x = 512
    sample_nb = tm_x // t_len

    n = n_p + n_s
    xp = x_prompt.reshape(n_p, D_MODEL)
    xs = x_sample.reshape(n_s, D_MODEL)
    cos_tab, sin_tab = _rope_tables(seq, t_len, tm)
    mem_flat = mem_prompt.reshape(batch * N_MEM, D_MODEL)
    zero_state = jnp.zeros((batch, 1, 2 * N_STATE), F32)
    vec = lambda a: a.reshape(depth, 1, a.shape[-1])

    w_in_b, w_glu_b, w_out_b = w_in.astype(BF16), w_glu.astype(BF16), w_out.astype(BF16)
    wq_b, wk_b, wv_b, wo_b = (w.astype(BF16) for w in (wq_x, wk_x, wv_x, wo_x))
    wg_b, wu_b = w_gate.astype(BF16), w_up.astype(BF16)
    wd_b = w_down.reshape(depth, N_EXPERTS * EXPERT_FF, D_MODEL).astype(BF16)
    wr = jnp.concatenate([w_group, w_router.transpose(0, 2, 1, 3).reshape(depth, D_MODEL, N_EXPERTS)], axis=-1)
    wr = jnp.pad(wr, ((0, 0), (0, 0), (0, ROUTER_LANES - wr.shape[-1])))
    br = jnp.concatenate([b_group, b_router.reshape(depth, N_EXPERTS)], axis=-1)
    br = jnp.pad(br, ((0, 0), (0, ROUTER_LANES - br.shape[-1]))).reshape(depth, 1, ROUTER_LANES)
    g_mix_r, g_xattn_r, g_mem_r, g_ffn_r = vec(g_mix), vec(g_xattn), vec(g_mem), vec(g_ffn)
    g_a_r, g_s_r, b_glu_r, ssm_d_r = vec(g_attn_out), vec(g_ssm_out), vec(b_glu), vec(ssm_d)
    cache_k = cache_win_k.reshape(depth, dec_batch, win_rows, KV_WIDTH)
    cache_v = cache_win_v.reshape(depth, dec_batch, win_rows, KV_WIDTH)
    cmem_k = cache_mem_k.reshape(depth, dec_batch * N_MEM, D_MODEL)
    cmem_v = cache_mem_v.reshape(depth, dec_batch * N_MEM, D_MODEL)
    state_in = _state_to_tiles(state_ssm)

    outs = {k: [] for k in ("wk_p", "wv_p", "ssm_p", "mk_p", "mv_p", "wk_s", "wv_s", "ssm_s")}
    for l in range(depth):
        q, k, v, u = _in_proj(xp, xs, n, g_mix_r, w_in_b, l, cos_tab, sin_tab, n_p, seq, tm)
        attn_p = _attn_prompt(q, k, v, attn_sink[l], batch, seq)
        attn_s = _attn_sample(q, k, v, cache_k, cache_v, l, attn_sink[l], n_p, dec_batch, t_len, 4)
        tail = lambda a: jnp.stack([a[(b + 1) * seq - WINDOW:(b + 1) * seq] for b in range(batch)])
        outs["wk_p"].append(tail(k).reshape(batch, WINDOW, N_KV_HEADS, HEAD_DIM))
        outs["wv_p"].append(tail(v).reshape(batch, WINDOW, N_KV_HEADS, HEAD_DIM))
        ks = k[n_p:].reshape(dec_batch, t_len, KV_WIDTH)
        vs = v[n_p:].reshape(dec_batch, t_len, KV_WIDTH)
        k_all = jnp.concatenate([cache_k[l], ks], axis=1)[:, -win_rows:]
        v_all = jnp.concatenate([cache_v[l], vs], axis=1)[:, -win_rows:]
        outs["wk_s"].append(k_all.reshape(dec_batch, win_rows, N_KV_HEADS, HEAD_DIM))
        outs["wv_s"].append(v_all.reshape(dec_batch, win_rows, N_KV_HEADS, HEAD_DIM))

        we, tv, coef = _ssm_discretize(lam_re[l], lam_im[l], log_dt[l], ssm_b_re[l], ssm_b_im[l],
                                       ssm_c_re[l], ssm_c_im[l])
        y_p, fin_p = _ssm(u, zero_state, we, tv, ssm_d_r, l, coef, 0, batch, seq, 1, seq)
        y_s, fin_s = _ssm(u, state_in[l], we, tv, ssm_d_r, l, coef, n_p, dec_batch, t_len, dec_batch, t_len)
        outs["ssm_p"].append(_tiles_to_state(fin_p))
        outs["ssm_s"].append(_tiles_to_state(fin_s))
        x = _merge(xp, xs, n, attn_p, attn_s, y_p, y_s, w_glu_b, b_glu_r, g_a_r, g_s_r, w_out_b, l, tm)

        mk_p, mv_p = _mem_kv(mem_flat, g_mem_r, wk_b, wv_b, l, 512)
        outs["mk_p"].append(mk_p.reshape(batch, N_MEM, N_XHEADS, XHEAD_DIM))
        outs["mv_p"].append(mv_p.reshape(batch, N_MEM, N_XHEADS, XHEAD_DIM))
        tiles_per_seq = seq // tm
        x = _xattn(x, g_xattn_r, wq_b, wo_b, l, mk_p, mv_p,
                   pl.BlockSpec((N_MEM, D_MODEL), lambda i: (i // tiles_per_seq, 0)), 0, n_p, 1, tm)
        x = _xattn(x, g_xattn_r, wq_b, wo_b, l, cmem_k, cmem_v,
                   pl.BlockSpec((None, sample_nb * N_MEM, D_MODEL), lambda i: (l, i, 0)), n_p, n_s, sample_nb, tm_x)

        x = _moe(x, g_ffn_r, wr, br, wg_b, wu_b, wd_b, l)
        xp = xs = x

    gf = g_final.reshape(1, D_MODEL)
    y_p = _final_norm(x, gf, 0, n_p, tm)
    y_s = _final_norm(x, gf, n_p, n_s, tm)
    st = lambda name: jnp.stack(outs[name], axis=0)
    return (y_p.reshape(batch, seq, D_MODEL), y_s.reshape(dec_batch, t_len, D_MODEL),
            st("wk_p"), st("wv_p"), st("ssm_p"), st("mk_p"), st("mv_p"), st("wk_s"), st("wv_s"), st("ssm_s"))
```

```python
import functools
import math

import jax
import jax.numpy as jnp
from jax import lax
from jax.experimental import pallas as pl
from jax.experimental.pallas import tpu as pltpu

F32 = jnp.float32
BF16 = jnp.bfloat16

D_MODEL = 1024
CHUNK = 64
EPS = 1e-6
NEG_INF = -1e30
N_HEADS = 8
N_KV_HEADS = 2
HEAD_DIM = 64
ATTN_WIDTH = N_HEADS * HEAD_DIM
KV_WIDTH = N_KV_HEADS * HEAD_DIM
WINDOW = 128
ROPE_THETA = 10000.0
SSM_WIDTH = D_MODEL - ATTN_WIDTH
SSM_GROUP = 16
N_SSM_GROUPS = SSM_WIDTH // SSM_GROUP
SSM_STATE = 64
N_STATE = N_SSM_GROUPS * SSM_STATE
IN_WIDTH = ATTN_WIDTH + 2 * KV_WIDTH + SSM_WIDTH
N_MEM = 256
N_XHEADS = 4
XHEAD_DIM = D_MODEL // N_XHEADS
N_EXPERT_GROUPS = 4
EXPERTS_PER_GROUP = 8
N_EXPERTS = N_EXPERT_GROUPS * EXPERTS_PER_GROUP
EXPERT_FF = 128
PAST_LEN = 4096

LANES = 128
SUBLANES = 8
VMEM_LIMIT = 56 * 1024 * 1024


def _cparams(*sem):
    return pltpu.CompilerParams(dimension_semantics=sem, vmem_limit_bytes=VMEM_LIMIT)


def _rms(x, g):
    return x * lax.rsqrt(jnp.mean(x * x, axis=-1, keepdims=True) + EPS) * g


def _layer_spec(arr, l):
    shape = arr.shape[1:]
    zeros = (0,) * len(shape)
    return pl.BlockSpec((None,) + shape, lambda *_: (l,) + zeros, pipeline_mode=pl.Buffered(1))


def _rope_pairs(t, cos, sin_signed, first_half):
    swapped = jnp.where(first_half, pltpu.roll(t, LANES - HEAD_DIM // 2, 1), pltpu.roll(t, HEAD_DIM // 2, 1))
    return t * cos + swapped * sin_signed


def _two_source_specs(xp, xs, n_prompt, tm, width):
    npt = n_prompt // tm
    s_off = 0 if xs is xp else npt
    s_last = xs.shape[0] // tm - 1
    pspec = pl.BlockSpec((tm, width), lambda i, *_: (jnp.minimum(i, npt - 1), 0))
    sspec = pl.BlockSpec((tm, width), lambda i, *_: (jnp.clip(i - s_off, npt - s_off, s_last), 0))
    return pspec, sspec


def _in_proj_kernel(n_prompt_tiles, xp_ref, xs_ref, g_ref, w_ref, cos_ref, sin_ref, q_ref, k_ref, v_ref, u_ref):
    x = jnp.where(pl.program_id(0) < n_prompt_tiles, xp_ref[...], xs_ref[...])
    xn = _rms(x, g_ref[...])
    z = jnp.dot(xn.astype(BF16), w_ref[...], preferred_element_type=F32)
    cos = cos_ref[...]
    sin = sin_ref[...]
    lane = lax.broadcasted_iota(jnp.int32, cos.shape, 1)
    first_half = (lane % HEAD_DIM) < (HEAD_DIM // 2)
    scale = 1.0 / math.sqrt(HEAD_DIM)
    for j in range(ATTN_WIDTH // LANES):
        t = z[:, j * LANES:(j + 1) * LANES]
        q_ref[:, j * LANES:(j + 1) * LANES] = (_rope_pairs(t, cos, sin, first_half) * scale).astype(BF16)
    k_ref[...] = _rope_pairs(z[:, ATTN_WIDTH:ATTN_WIDTH + KV_WIDTH], cos, sin, first_half)
    v_ref[...] = z[:, ATTN_WIDTH + KV_WIDTH:ATTN_WIDTH + 2 * KV_WIDTH]
    u_ref[...] = z[:, ATTN_WIDTH + 2 * KV_WIDTH:]


def _in_proj(xp, xs, n, g, w_bf16, l, cos_tab, sin_tab, n_prompt, seq, tm):
    n_prompt_tiles = n_prompt // tm
    tiles_per_seq = seq // tm

    def tab_map(i):
        return (jnp.where(i < n_prompt_tiles, i % tiles_per_seq, tiles_per_seq), 0)

    row = lambda w: pl.BlockSpec((tm, w), lambda i: (i, 0))
    return pl.pallas_call(
        functools.partial(_in_proj_kernel, n_prompt_tiles),
        grid=(n // tm,),
        in_specs=[*_two_source_specs(xp, xs, n_prompt, tm, D_MODEL), _layer_spec(g, l), _layer_spec(w_bf16, l),
                  pl.BlockSpec((tm, LANES), tab_map),
                  pl.BlockSpec((tm, LANES), tab_map)],
        out_specs=[row(ATTN_WIDTH), row(KV_WIDTH), row(KV_WIDTH), row(SSM_WIDTH)],
        out_shape=[jax.ShapeDtypeStruct((n, ATTN_WIDTH), BF16),
                   jax.ShapeDtypeStruct((n, KV_WIDTH), F32),
                   jax.ShapeDtypeStruct((n, KV_WIDTH), F32),
                   jax.ShapeDtypeStruct((n, SSM_WIDTH), F32)],
        compiler_params=_cparams("parallel"),
        name="in_proj",
    )(xp, xs, g, w_bf16, cos_tab, sin_tab)


def _kv_pairs(keys, vals):
    lane = lax.broadcasted_iota(jnp.int32, keys.shape, 1)
    low = lane < HEAD_DIM
    k_sw = pltpu.roll(keys, HEAD_DIM, 1)
    v_sw = pltpu.roll(vals, HEAD_DIM, 1)
    kk = [jnp.where(low, keys, k_sw).astype(BF16), jnp.where(low, k_sw, keys).astype(BF16)]
    vv = [jnp.where(low, vals, v_sw).astype(BF16), jnp.where(low, v_sw, vals).astype(BF16)]
    return kk, vv


def _attend_pairs(q, kk, vv, key_rows, mask_add, sink_ref, o_ref, row0):
    tq = q.shape[0]
    qlane = lax.broadcasted_iota(jnp.int32, (tq, LANES), 1)
    qlow = qlane < HEAD_DIM
    row_top = lax.broadcasted_iota(jnp.int32, (2 * tq, 1), 0) < tq
    zero = jnp.zeros((), BF16)
    for pair in range(N_HEADS // 2):
        kv = pair // (N_HEADS // N_KV_HEADS // 2)
        qp = q[:, pair * LANES:(pair + 1) * LANES]
        qs = jnp.concatenate([jnp.where(qlow, qp, zero), jnp.where(qlow, zero, qp)], axis=0)
        s = lax.dot_general(qs, kk[kv][key_rows, :], (((1,), (1,)), ((), ())), preferred_element_type=F32)
        if mask_add is not None:
            s = s + mask_add
        sink = jnp.where(row_top, sink_ref[2 * pair], sink_ref[2 * pair + 1])
        m = jnp.maximum(jnp.max(s, axis=-1, keepdims=True), sink)
        e = jnp.exp(s - m)
        p = e / (jnp.sum(e, axis=-1, keepdims=True) + jnp.exp(sink - m))
        o = jnp.dot(p.astype(BF16), vv[kv][key_rows, :], preferred_element_type=F32)
        o_ref[row0:row0 + tq, pair * LANES:(pair + 1) * LANES] = jnp.where(qlow, o[:tq], o[tq:])


def _attend_blocks(blocks, sink_ref, o_ref):
    tq = blocks[0][0].shape[0]
    qlane = lax.broadcasted_iota(jnp.int32, (tq, LANES), 1)
    qlow = qlane < HEAD_DIM
    row_top = lax.broadcasted_iota(jnp.int32, (2 * tq, 1), 0) < tq
    zero = jnp.zeros((), BF16)
    scores, sinks = [], []
    for q, kk, vv, key_rows, mask_add, row0 in blocks:
        for pair in range(N_HEADS // 2):
            kv = pair // (N_HEADS // N_KV_HEADS // 2)
            qp = q[:, pair * LANES:(pair + 1) * LANES]
            qs = jnp.concatenate([jnp.where(qlow, qp, zero), jnp.where(qlow, zero, qp)], axis=0)
            s = lax.dot_general(qs, kk[kv][key_rows, :], (((1,), (1,)), ((), ())), preferred_element_type=F32)
            scores.append(s if mask_add is None else s + mask_add)
            sinks.append(jnp.where(row_top, sink_ref[2 * pair], sink_ref[2 * pair + 1]))
    s = jnp.concatenate(scores, axis=0)
    sink = jnp.concatenate(sinks, axis=0)
    m = jnp.maximum(jnp.max(s, axis=-1, keepdims=True), sink)
    e = jnp.exp(s - m)
    p = (e / (jnp.sum(e, axis=-1, keepdims=True) + jnp.exp(sink - m))).astype(BF16)
    piece = 0
    for q, kk, vv, key_rows, mask_add, row0 in blocks:
        for pair in range(N_HEADS // 2):
            kv = pair // (N_HEADS // N_KV_HEADS // 2)
            o = jnp.dot(p[piece * 2 * tq:(piece + 1) * 2 * tq, :], vv[kv][key_rows, :], preferred_element_type=F32)
            o_ref[row0:row0 + tq, pair * LANES:(pair + 1) * LANES] = jnp.where(qlow, o[:tq], o[tq:])
            piece += 1


ATTN_SUB = WINDOW
ATTN_TILE = 2 * ATTN_SUB


def _attn_prompt_kernel(sink_ref, ma_ref, mb_ref, q_ref, kp_ref, kc_ref, vp_ref, vc_ref, o_ref):
    kk, vv = _kv_pairs(jnp.concatenate([kp_ref[...], kc_ref[...]], axis=0),
                       jnp.concatenate([vp_ref[...], vc_ref[...]], axis=0))
    for s, m_ref in enumerate((ma_ref, mb_ref)):
        rows = slice(s * ATTN_SUB, s * ATTN_SUB + 2 * WINDOW)
        _attend_pairs(q_ref[s * ATTN_SUB:(s + 1) * ATTN_SUB, :], kk, vv, rows, m_ref[...], sink_ref, o_ref, s * ATTN_SUB)


def _band_masks():
    r = (jnp.arange(2 * ATTN_SUB) % ATTN_SUB)[:, None] // CHUNK
    c = jnp.arange(2 * WINDOW)[None, :]
    band = (c // CHUNK >= r) & (c // CHUNK <= r + WINDOW // CHUNK)
    masks = jnp.stack([band, band & (c >= WINDOW)])
    return jnp.where(masks, 0.0, NEG_INF).astype(F32)


def _attn_prompt(q, k, v, sink, batch, seq):
    nt = seq // ATTN_TILE
    per_seq = seq // ATTN_SUB
    cur = lambda b, i: (b * nt + i, 0)
    prev = lambda b, i: (b * per_seq + jnp.maximum(2 * i - 1, 0), 0)
    masks = _band_masks()
    return pl.pallas_call(
        _attn_prompt_kernel,
        grid=(batch, nt),
        in_specs=[pl.BlockSpec(memory_space=pltpu.SMEM),
                  pl.BlockSpec((None, 2 * ATTN_SUB, 2 * WINDOW), lambda b, i: (jnp.where(i == 0, 1, 0), 0, 0)),
                  pl.BlockSpec((None, 2 * ATTN_SUB, 2 * WINDOW), lambda b, i: (0, 0, 0)),
                  pl.BlockSpec((ATTN_TILE, ATTN_WIDTH), cur),
                  pl.BlockSpec((ATTN_SUB, KV_WIDTH), prev),
                  pl.BlockSpec((ATTN_TILE, KV_WIDTH), cur),
                  pl.BlockSpec((ATTN_SUB, KV_WIDTH), prev),
                  pl.BlockSpec((ATTN_TILE, KV_WIDTH), cur)],
        out_specs=pl.BlockSpec((ATTN_TILE, ATTN_WIDTH), cur),
        out_shape=jax.ShapeDtypeStruct((batch * seq, ATTN_WIDTH), F32),
        compiler_params=_cparams("parallel", "parallel"),
        name="attn_prompt",
    )(sink, masks, masks, q, k, k, v, v)


def _attn_sample_kernel(sink_ref, q_ref, ck_ref, cv_ref, k_ref, v_ref, o_ref):
    nb = ck_ref.shape[0]
    t = q_ref.shape[0] // nb
    blocks = []
    for b in range(nb):
        rows = slice(b * t, (b + 1) * t)
        kk, vv = _kv_pairs(jnp.concatenate([ck_ref[b], k_ref[rows, :]], axis=0),
                           jnp.concatenate([cv_ref[b], v_ref[rows, :]], axis=0))
        blocks.append((q_ref[rows, :], kk, vv, slice(None), None, b * t))
    _attend_blocks(blocks, sink_ref, o_ref)


def _attn_sample(q, k, v, cache_k, cache_v, l, sink, n_prompt, dec_batch, t, nb):
    w = cache_k.shape[2]
    rows = nb * t
    base = n_prompt // rows
    tok = lambda width: pl.BlockSpec((rows, width), lambda i: (base + i, 0))
    cache = pl.BlockSpec((None, nb, w, KV_WIDTH), lambda i: (l, i, 0, 0))
    return pl.pallas_call(
        _attn_sample_kernel,
        grid=(dec_batch // nb,),
        in_specs=[pl.BlockSpec(memory_space=pltpu.SMEM),
                  tok(ATTN_WIDTH), cache, cache, tok(KV_WIDTH), tok(KV_WIDTH)],
        out_specs=pl.BlockSpec((rows, ATTN_WIDTH), lambda i: (i, 0)),
        out_shape=jax.ShapeDtypeStruct((dec_batch * t, ATTN_WIDTH), F32),
        compiler_params=_cparams("parallel"),
        name="attn_sample",
    )(sink, q, cache_k, cache_v, k, v)


SSM_BLOCK = SUBLANES


def _cmul(ar, ai, br, bi):
    return ar * br - ai * bi, ar * bi + ai * br


def _ssm_disc_kernel(lre_ref, lim_ref, dt_ref, bre_ref, bim_ref, cre_ref, cim_ref,
                     we_ref, tv_ref, coef_ref, vt_ref, wb_ref):
    we_ref[...] = jnp.zeros(we_ref.shape, we_ref.dtype)
    vt_ref[...] = jnp.zeros(vt_ref.shape, vt_ref.dtype)
    wb_ref[...] = jnp.zeros(wb_ref.shape, wb_ref.dtype)
    lane = lax.broadcasted_iota(jnp.int32, (SSM_GROUP, LANES), 1)
    half = [lane < SSM_STATE, lane >= SSM_STATE]
    row8 = lax.broadcasted_iota(jnp.int32, (SUBLANES, LANES), 0)
    for q in range(STATE_TILES):
        lre = lre_ref[q]
        lim = lim_ref[q]
        dt = dt_ref[q]
        mag = jnp.exp(lre * dt)
        ang = lim * dt
        lbr = mag * jnp.cos(ang)
        lbi = mag * jnp.sin(ang)
        nr, ni = lbr - 1.0, lbi
        den = lre * lre + lim * lim
        fr = (nr * lre + ni * lim) / den
        fi = (ni * lre - nr * lim) / den
        bbr, bbi = _cmul(fr, fi, bre_ref[q], bim_ref[q])
        cr, ci = cre_ref[q], cim_ref[q]
        pw = [(jnp.ones_like(lbr), jnp.zeros_like(lbr))]
        for _ in range(SSM_BLOCK):
            pw.append(_cmul(pw[-1][0], pw[-1][1], lbr, lbi))
        re_l = slice(q * 2 * LANES, q * 2 * LANES + LANES)
        im_l = slice(q * 2 * LANES + LANES, (q + 1) * 2 * LANES)
        for h in range(2):
            g = 2 * q + h
            grow = lambda blk: slice(blk * LANES + g * SSM_GROUP, blk * LANES + (g + 1) * SSM_GROUP)
            for j in range(SSM_BLOCK):
                wr, wi = _cmul(pw[SSM_BLOCK - 1 - j][0], pw[SSM_BLOCK - 1 - j][1], bbr, bbi)
                we_ref[grow(j), re_l] = jnp.where(half[h], wr, 0.0).astype(we_ref.dtype)
                we_ref[grow(j), im_l] = jnp.where(half[h], wi, 0.0).astype(we_ref.dtype)
            for d in range(SSM_BLOCK + 1):
                xr, xi = _cmul(cr, ci, pw[d][0], pw[d][1])
                vt_ref[grow(d), re_l] = jnp.where(half[h], xr, 0.0)
                vt_ref[grow(d), im_l] = jnp.where(half[h], -xi, 0.0)
            wb_ref[grow(0), re_l] = jnp.where(half[h], bbr, 0.0)
            wb_ref[grow(0), im_l] = jnp.where(half[h], bbi, 0.0)
        l8 = [pw[SSM_BLOCK]]
        for _ in range(SUBLANES - 1):
            l8.append(_cmul(l8[-1][0], l8[-1][1], pw[SSM_BLOCK][0], pw[SSM_BLOCK][1]))
        co = slice(q * LANES, (q + 1) * LANES)
        for kind, sh in enumerate((1, 2, 4)):
            for a in range(2):
                coef_ref[kind, a, :, co] = jnp.where(row8 >= sh, l8[sh - 1][a], 0.0)
        for a in range(2):
            tab = jnp.zeros((SUBLANES, LANES), F32)
            for k in range(SUBLANES):
                tab = jnp.where(row8 == k, l8[k][a], tab)
            coef_ref[3, a, :, co] = tab
    t0 = lax.dot_general(wb_ref[...], vt_ref[0:BLOCK_LANES, :], (((1,), (1,)), ((), ())),
                         preferred_element_type=F32, precision=lax.Precision.HIGHEST)
    for j in range(SSM_BLOCK):
        if j:
            tv_ref[j * LANES:(j + 1) * LANES, 0:j * LANES] = jnp.zeros((LANES, j * LANES), tv_ref.dtype)
        tv_ref[j * LANES:(j + 1) * LANES, j * LANES:] = t0[:, 0:BLOCK_LANES - j * LANES].astype(tv_ref.dtype)
    tv_ref[BLOCK_LANES:, :] = jnp.transpose(vt_ref[LANES:, :]).astype(tv_ref.dtype)


def _ssm_discretize(lam_re, lam_im, log_dt, b_re, b_im, c_re, c_im):
    g, p = lam_re.shape
    npair = g // 2
    pair = lambda a: a.reshape(npair, 1, 2 * p)
    rows = lambda a: a.reshape(npair, 2, SSM_GROUP, p).transpose(0, 2, 1, 3).reshape(npair, SSM_GROUP, 2 * p)
    dt = jnp.repeat(jnp.exp(log_dt), p).reshape(npair, 1, 2 * p)
    vec = pl.BlockSpec((STATE_TILES, 1, LANES), lambda m: (m, 0, 0))
    mat = pl.BlockSpec((STATE_TILES, SSM_GROUP, LANES), lambda m: (m, 0, 0))
    return pl.pallas_call(
        _ssm_disc_kernel,
        grid=(N_CH_BLOCKS,),
        in_specs=[vec, vec, vec, mat, mat, mat, mat],
        out_specs=[pl.BlockSpec((None, BLOCK_LANES, STATE_LANES), lambda m: (m, 0, 0)),
                   pl.BlockSpec((None, BLOCK_LANES + STATE_LANES, BLOCK_LANES), lambda m: (m, 0, 0)),
                   pl.BlockSpec((None, 4, 2, SUBLANES, STATE_LANES // 2), lambda m: (m, 0, 0, 0, 0))],
        out_shape=[jax.ShapeDtypeStruct((N_CH_BLOCKS, BLOCK_LANES, STATE_LANES), BF16),
                   jax.ShapeDtypeStruct((N_CH_BLOCKS, BLOCK_LANES + STATE_LANES, BLOCK_LANES), BF16),
                   jax.ShapeDtypeStruct((N_CH_BLOCKS, 4, 2, SUBLANES, STATE_LANES // 2), F32)],
        scratch_shapes=[pltpu.VMEM(((SSM_BLOCK + 1) * LANES, STATE_LANES), F32),
                        pltpu.VMEM((LANES, STATE_LANES), F32)],
        compiler_params=_cparams("parallel"),
        name="ssm_discretize",
    )(pair(lam_re), pair(lam_im), dt, rows(b_re.transpose(0, 2, 1)), rows(b_im.transpose(0, 2, 1)),
      rows(c_re), rows(c_im))


GROUPS_PER_CH_BLOCK = LANES // SSM_GROUP
N_CH_BLOCKS = SSM_WIDTH // LANES
STATE_LANES = 2 * GROUPS_PER_CH_BLOCK * SSM_STATE
STATE_TILES = STATE_LANES // (2 * LANES)
BLOCK_LANES = SSM_BLOCK * LANES


def _ssm_kernel(nb, u_ref, s0_ref, we_ref, tv_ref, d_ref, coef_ref, y_ref, fin_ref, st_ref, sprev_ref, ucat_ref):
    t_idx = pl.program_id(2)
    r = st_ref.shape[1] - SUBLANES
    rows = nb * r

    @pl.when(t_idx == 0)
    def _():
        for b in range(nb):
            st_ref[b, 0:SUBLANES, :] = jnp.broadcast_to(s0_ref[b], (SUBLANES, STATE_LANES))

    for j in range(SSM_BLOCK):
        ucat_ref[:, j * LANES:(j + 1) * LANES] = u_ref[pl.ds(j, rows, stride=SSM_BLOCK), :].astype(BF16)
    e = jnp.dot(ucat_ref[...], we_ref[...], preferred_element_type=F32)
    st_ref[:, SUBLANES:, :] = e.reshape(nb, r, STATE_LANES)

    first_row = lax.broadcasted_iota(jnp.int32, (SUBLANES, LANES), 0) == 0

    def group(rg, carry):
        r0 = pl.multiple_of(SUBLANES + rg * SUBLANES, SUBLANES)
        rp = pl.multiple_of(rg * SUBLANES, SUBLANES)
        for b in range(nb):
            for q in range(STATE_TILES):
                re_sl = pl.ds(q * 2 * LANES, LANES)
                im_sl = pl.ds(q * 2 * LANES + LANES, LANES)
                co = pl.ds(q * LANES, LANES)
                xr = st_ref[b, pl.ds(r0, SUBLANES), re_sl]
                xi = st_ref[b, pl.ds(r0, SUBLANES), im_sl]
                for step, sh in enumerate((1, 2, 4)):
                    ar, ai = _cmul(coef_ref[step, 0, :, co], coef_ref[step, 1, :, co],
                                   pltpu.roll(xr, sh, 0), pltpu.roll(xi, sh, 0))
                    xr = xr + ar
                    xi = xi + ai
                pr = jnp.broadcast_to(st_ref[b, pl.ds(rp, SUBLANES), re_sl][SUBLANES - 1:, :], (SUBLANES, LANES))
                pi = jnp.broadcast_to(st_ref[b, pl.ds(rp, SUBLANES), im_sl][SUBLANES - 1:, :], (SUBLANES, LANES))
                ar, ai = _cmul(coef_ref[3, 0, :, co], coef_ref[3, 1, :, co], pr, pi)
                xr = xr + ar
                xi = xi + ai
                st_ref[b, pl.ds(r0, SUBLANES), re_sl] = xr
                st_ref[b, pl.ds(r0, SUBLANES), im_sl] = xi
                out_rows = pl.ds(pl.multiple_of(b * r + rg * SUBLANES, SUBLANES), SUBLANES)
                sprev_ref[out_rows, re_sl] = jnp.where(first_row, pr, pltpu.roll(xr, 1, 0))
                sprev_ref[out_rows, im_sl] = jnp.where(first_row, pi, pltpu.roll(xi, 1, 0))
        return carry

    lax.fori_loop(0, r // SUBLANES, group, 0)

    lhs = jnp.concatenate([ucat_ref[...], sprev_ref[...].astype(BF16)], axis=1)
    ycat = jnp.dot(lhs, tv_ref[...], preferred_element_type=F32)
    d = d_ref[...]
    for t in range(SSM_BLOCK):
        tok = pl.ds(t, rows, stride=SSM_BLOCK)
        y_ref[tok, :] = ycat[:, t * LANES:(t + 1) * LANES] + d * u_ref[tok, :]

    for b in range(nb):
        tail = st_ref[b, r:r + SUBLANES, :]
        st_ref[b, 0:SUBLANES, :] = tail
        fin_ref[b] = tail[SUBLANES - 1:SUBLANES, :]


def _ssm(u, s0, we, tv, d, l, coef, row_base, n_seq, seq_len, nb, tt):
    rows = nb * tt
    r = tt // SSM_BLOCK
    nt = seq_len // tt
    base = row_base // rows
    return pl.pallas_call(
        functools.partial(_ssm_kernel, nb),
        grid=(N_CH_BLOCKS, n_seq // nb, nt),
        in_specs=[pl.BlockSpec((rows, LANES), lambda m, i, j: (base + i * nt + j, m)),
                  pl.BlockSpec((nb, 1, STATE_LANES), lambda m, i, j: (i, 0, m)),
                  pl.BlockSpec((None, BLOCK_LANES, STATE_LANES), lambda m, i, j: (m, 0, 0)),
                  pl.BlockSpec((None, BLOCK_LANES + STATE_LANES, BLOCK_LANES), lambda m, i, j: (m, 0, 0)),
                  pl.BlockSpec((None, 1, LANES), lambda m, i, j: (l, 0, m)),
                  pl.BlockSpec((None, 4, 2, SUBLANES, STATE_LANES // 2), lambda m, i, j: (m, 0, 0, 0, 0))],
        out_specs=[pl.BlockSpec((rows, LANES), lambda m, i, j: (i * nt + j, m)),
                   pl.BlockSpec((nb, 1, STATE_LANES), lambda m, i, j: (i, 0, m))],
        out_shape=[jax.ShapeDtypeStruct((n_seq * seq_len, SSM_WIDTH), F32),
                   jax.ShapeDtypeStruct((n_seq, 1, N_CH_BLOCKS * STATE_LANES), F32)],
        scratch_shapes=[pltpu.VMEM((nb, SUBLANES + r, STATE_LANES), F32),
                        pltpu.VMEM((nb * r, STATE_LANES), F32),
                        pltpu.VMEM((nb * r, BLOCK_LANES), BF16)],
        compiler_params=_cparams("parallel", "parallel", "arbitrary"),
        name="ssm_scan",
    )(u, s0, we, tv, d, coef)


def _state_to_tiles(s):
    lead = s.shape[:-3]
    t = s.reshape(lead + (N_CH_BLOCKS, STATE_TILES, 2, SSM_STATE, 2))
    t = jnp.moveaxis(t, -1, -3)
    return t.reshape(lead + (1, N_CH_BLOCKS * STATE_LANES))


def _tiles_to_state(f):
    b = f.shape[0]
    t = f.reshape(b, N_CH_BLOCKS, STATE_TILES, 2, 2, SSM_STATE)
    t = jnp.moveaxis(t, 3, -1)
    return t.reshape(b, N_SSM_GROUPS, SSM_STATE, 2)


def _merge_kernel(n_prompt_tiles, xp_ref, xs_ref, ap_ref, as_ref, yp_ref, ys_ref, wglu_ref, bglu_ref, ga_ref, gs_ref,
                  wout_ref, o_ref):
    is_prompt = pl.program_id(0) < n_prompt_tiles
    attn = jnp.where(is_prompt, ap_ref[...], as_ref[...])
    g = jax.nn.gelu(jnp.where(is_prompt, yp_ref[...], ys_ref[...]))
    glu = g * jax.nn.sigmoid(jnp.dot(g.astype(BF16), wglu_ref[...], preferred_element_type=F32) + bglu_ref[...])
    na = _rms(attn, ga_ref[...]).astype(BF16)
    ns = _rms(glu, gs_ref[...]).astype(BF16)
    o = jnp.dot(na, wout_ref[0:ATTN_WIDTH, :], preferred_element_type=F32)
    o = o + jnp.dot(ns, wout_ref[ATTN_WIDTH:, :], preferred_element_type=F32)
    o_ref[...] = jnp.where(is_prompt, xp_ref[...], xs_ref[...]) + o


def _merge(xp, xs, n, attn_p, attn_s, y_p, y_s, wglu, bglu, ga, gs, wout, l, tm):
    npt = attn_p.shape[0] // tm
    nst = attn_s.shape[0] // tm
    row = lambda w: pl.BlockSpec((tm, w), lambda i: (i, 0))
    prow = lambda w: pl.BlockSpec((tm, w), lambda i: (jnp.minimum(i, npt - 1), 0))
    srow = lambda w: pl.BlockSpec((tm, w), lambda i: (jnp.clip(i - npt, 0, nst - 1), 0))
    return pl.pallas_call(
        functools.partial(_merge_kernel, npt),
        grid=(n // tm,),
        in_specs=[*_two_source_specs(xp, xs, attn_p.shape[0], tm, D_MODEL),
                  prow(ATTN_WIDTH), srow(ATTN_WIDTH), prow(SSM_WIDTH), srow(SSM_WIDTH),
                  _layer_spec(wglu, l), _layer_spec(bglu, l), _layer_spec(ga, l), _layer_spec(gs, l),
                  _layer_spec(wout, l)],
        out_specs=row(D_MODEL),
        out_shape=jax.ShapeDtypeStruct((n, D_MODEL), F32),
        compiler_params=_cparams("parallel"),
        name="merge_heads",
    )(xp, xs, attn_p, attn_s, y_p, y_s, wglu, bglu, ga, gs, wout)


def _mem_kv_kernel(m_ref, g_ref, wk_ref, wv_ref, k_ref, v_ref):
    mn = _rms(m_ref[...], g_ref[...]).astype(BF16)
    k_ref[...] = jnp.dot(mn, wk_ref[...], preferred_element_type=F32)
    v_ref[...] = jnp.dot(mn, wv_ref[...], preferred_element_type=F32)


def _mem_kv(mem, g, wk, wv, l, tm):
    n = mem.shape[0]
    row = pl.BlockSpec((tm, D_MODEL), lambda i: (i, 0))
    return pl.pallas_call(
        _mem_kv_kernel,
        grid=(n // tm,),
        in_specs=[row, _layer_spec(g, l), _layer_spec(wk, l), _layer_spec(wv, l)],
        out_specs=[row, row],
        out_shape=[jax.ShapeDtypeStruct((n, D_MODEL), F32)] * 2,
        compiler_params=_cparams("parallel"),
        name="mem_kv",
    )(mem, g, wk, wv)


def _xattn_kernel(nb, x_ref, g_ref, wq_ref, wo_ref, mk_ref, mv_ref, o_ref, att_ref):
    t = x_ref.shape[0] // nb
    x = x_ref[...]
    hn = _rms(x, g_ref[...]).astype(BF16)
    q = jnp.dot(hn, wq_ref[...], preferred_element_type=F32) * (1.0 / math.sqrt(XHEAD_DIM))
    q = q.astype(BF16)
    heads = [slice(h * XHEAD_DIM, (h + 1) * XHEAD_DIM) for h in range(N_XHEADS)]
    scores = []
    for b in range(nb):
        mk = mk_ref[b * N_MEM:(b + 1) * N_MEM, :].astype(BF16)
        for sl in heads:
            scores.append(lax.dot_general(q[b * t:(b + 1) * t, sl], mk[:, sl], (((1,), (1,)), ((), ())),
                                          preferred_element_type=F32))
    s = jnp.concatenate(scores, axis=0)
    e = jnp.exp(s - jnp.max(s, axis=-1, keepdims=True))
    p = (e / jnp.sum(e, axis=-1, keepdims=True)).astype(BF16)
    for b in range(nb):
        mv = mv_ref[b * N_MEM:(b + 1) * N_MEM, :].astype(BF16)
        for h, sl in enumerate(heads):
            r0 = (b * N_XHEADS + h) * t
            att_ref[b * t:(b + 1) * t, sl] = jnp.dot(p[r0:r0 + t, :], mv[:, sl], preferred_element_type=F32)
    o = jnp.dot(att_ref[...].astype(BF16), wo_ref[...], preferred_element_type=F32)
    o_ref[...] = x + o


def _xattn(x, g, wq, wo, l, mk, mv, mem_spec, row_base, n_rows, nb, tm):
    base = row_base // tm
    xspec = pl.BlockSpec((tm, D_MODEL), lambda i: (base + i, 0))
    return pl.pallas_call(
        functools.partial(_xattn_kernel, nb),
        grid=(n_rows // tm,),
        in_specs=[xspec, _layer_spec(g, l), _layer_spec(wq, l), _layer_spec(wo, l), mem_spec, mem_spec],
        out_specs=xspec,
        out_shape=jax.ShapeDtypeStruct(x.shape, F32),
        scratch_shapes=[pltpu.VMEM((tm, D_MODEL), F32)],
        input_output_aliases={0: 0},
        compiler_params=_cparams("parallel"),
        name="cross_attn",
    )(x, g, wq, wo, mk, mv)


ROUTER_LANES = LANES
EXPERT_LANE0 = N_EXPERT_GROUPS
EXPERTS_PER_STEP = EXPERTS_PER_GROUP
MOE_SRC_TILE = 512
MOE_RUN_ALIGN = 16
MOE_SORTED_ROWS = 640
MOE_TILE = 1024


def _dot_f32_3pass(x, w):
    xh = x.astype(BF16)
    xl = (x - xh.astype(F32)).astype(BF16)
    wh = w.astype(BF16)
    wl = (w - wh.astype(F32)).astype(BF16)
    dot = lambda a, b: jnp.dot(a, b, preferred_element_type=F32)
    return dot(xh, wh) + (dot(xl, wh) + dot(xh, wl))


def _route(logits):
    lane_i = lax.broadcasted_iota(jnp.int32, logits.shape, 1)
    lane = lane_i.astype(F32)
    neg = jnp.float32(-jnp.inf)
    is_g = lane_i < N_EXPERT_GROUPS
    gl = jnp.where(is_g, logits, neg)
    gmax = jnp.max(gl, axis=-1, keepdims=True)
    gidx = jnp.min(jnp.where(gl == gmax, lane, float(ROUTER_LANES)), axis=-1, keepdims=True)
    g_w = 1.0 / jnp.sum(jnp.where(is_g, jnp.exp(gl - gmax), 0.0), axis=-1, keepdims=True)
    first = EXPERT_LANE0 + gidx * EXPERTS_PER_GROUP
    sel = (lane >= first) & (lane < first + EXPERTS_PER_GROUP)
    el = jnp.where(sel, logits, neg)
    m1 = jnp.max(el, axis=-1, keepdims=True)
    i1 = jnp.min(jnp.where(el == m1, lane, float(ROUTER_LANES)), axis=-1, keepdims=True)
    el2 = jnp.where(lane == i1, neg, el)
    m2 = jnp.max(el2, axis=-1, keepdims=True)
    i2 = jnp.min(jnp.where(el2 == m2, lane, float(ROUTER_LANES)), axis=-1, keepdims=True)
    r = jnp.exp(m2 - m1)
    w1 = g_w / (1.0 + r)
    w2 = w1 * r
    return jnp.where(lane == i1, w1, jnp.where(lane == i2, w2, 0.0)), gidx


def _moe_pre_kernel(x_ref, g_ref, wr_ref, br_ref, xn_ref, gate_ref, meta_ref, cnt_ref, tri_ref):
    tm = x_ref.shape[0]

    @pl.when(pl.program_id(0) == 0)
    def _():
        r = lax.broadcasted_iota(jnp.int32, (tm, tm), 0)
        c = lax.broadcasted_iota(jnp.int32, (tm, tm), 1)
        tri_ref[...] = (c <= r).astype(BF16)

    xn = _rms(x_ref[...], g_ref[...])
    xn_ref[...] = xn.astype(BF16)
    gates, gidx = _route(_dot_f32_3pass(xn, wr_ref[...]) + br_ref[...])
    gate_ref[...] = gates
    lane = lax.broadcasted_iota(jnp.int32, gates.shape, 1).astype(F32)
    onehot = lane == gidx
    incl = jnp.dot(tri_ref[...], onehot.astype(BF16), preferred_element_type=F32)
    rank = jnp.sum(jnp.where(onehot, incl, 0.0), axis=-1, keepdims=True) - 1.0
    meta_ref[...] = jnp.where(lane == 0.0, gidx, jnp.where(lane == 1.0, rank, 0.0))
    cnt_ref[...] = incl[tm - 1:tm, :]


def _moe_pre(x, g, wr, br, l):
    n = x.shape[0]
    tm = MOE_SRC_TILE
    row = lambda w: pl.BlockSpec((tm, w), lambda i: (i, 0))
    return pl.pallas_call(
        _moe_pre_kernel,
        grid=(n // tm,),
        in_specs=[row(D_MODEL), _layer_spec(g, l), _layer_spec(wr, l), _layer_spec(br, l)],
        out_specs=[row(D_MODEL), row(ROUTER_LANES), row(ROUTER_LANES),
                   pl.BlockSpec((None, 1, ROUTER_LANES), lambda i: (i, 0, 0))],
        out_shape=[jax.ShapeDtypeStruct((n, D_MODEL), BF16), jax.ShapeDtypeStruct((n, ROUTER_LANES), F32),
                   jax.ShapeDtypeStruct((n, ROUTER_LANES), F32),
                   jax.ShapeDtypeStruct((n // tm, 1, ROUTER_LANES), F32)],
        scratch_shapes=[pltpu.VMEM((tm, tm), BF16)],
        compiler_params=_cparams("arbitrary"),
        name="moe_pre",
    )(x, g, wr, br)


def _sort_onehot(meta, start_ref, t):
    gid = meta[:, 0:1]
    pos = meta[:, 1:2]
    for grp in range(N_EXPERT_GROUPS):
        pos = pos + jnp.where(gid == float(grp), start_ref[t * N_EXPERT_GROUPS + grp].astype(F32), 0.0)
    col = lax.broadcasted_iota(jnp.int32, (meta.shape[0], MOE_SORTED_ROWS), 1).astype(F32)
    return (col == pos).astype(BF16)


def _run_copies(t, base, start_ref, off_ref, nblk_ref, pairs, sem, to_hbm):
    for grp in range(N_EXPERT_GROUPS):
        k = t * N_EXPERT_GROUPS + grp
        src0 = base + start_ref[k]
        dst0 = off_ref[k]

        def body(b, carry):
            lo = pl.multiple_of(src0 + b * MOE_RUN_ALIGN, MOE_RUN_ALIGN)
            hi = pl.multiple_of(dst0 + b * MOE_RUN_ALIGN, MOE_RUN_ALIGN)
            for buf, arr in pairs:
                a, h = buf.at[pl.ds(lo, MOE_RUN_ALIGN)], arr.at[pl.ds(hi, MOE_RUN_ALIGN)]
                (pltpu.make_async_copy(a, h, sem) if to_hbm else pltpu.make_async_copy(h, a, sem)).start()
            return carry

        lax.fori_loop(0, nblk_ref[k], body, 0)


def _run_wait(t, nblk_ref, pairs, sem, to_hbm):
    total = nblk_ref[t * N_EXPERT_GROUPS]
    for grp in range(1, N_EXPERT_GROUPS):
        total = total + nblk_ref[t * N_EXPERT_GROUPS + grp]
    rows = total * MOE_RUN_ALIGN

    @pl.when(rows > 0)
    def _():
        for buf, arr in pairs:
            a, h = buf.at[pl.ds(0, rows)], arr.at[pl.ds(0, rows)]
            (pltpu.make_async_copy(a, h, sem) if to_hbm else pltpu.make_async_copy(h, a, sem)).wait()


def _moe_pack_kernel(start_ref, off_ref, nblk_ref, end_ref, xn_ref, gate_ref, meta_ref, xs_hbm, gs_hbm,
                     xbuf, gbuf, zx, zg, sems):
    t = pl.program_id(0)
    last = pl.num_programs(0) - 1
    slot = t % 2
    base = pl.multiple_of(slot * MOE_SORTED_ROWS, MOE_SORTED_ROWS)
    pairs = [(xbuf, xs_hbm), (gbuf, gs_hbm)]

    @pl.when(t >= 2)
    def _():
        _run_wait(t - 2, nblk_ref, pairs, sems.at[slot], True)

    onehot = _sort_onehot(meta_ref[...], start_ref, t)
    tn = (((0,), (0,)), ((), ()))
    rows = pl.ds(base, MOE_SORTED_ROWS)
    xbuf[rows, :] = lax.dot_general(onehot, xn_ref[...], tn, preferred_element_type=F32).astype(BF16)
    gates = gate_ref[...]
    gh = gates.astype(BF16)
    gl = (gates - gh.astype(F32)).astype(BF16)
    gbuf[rows, :] = (lax.dot_general(onehot, gh, tn, preferred_element_type=F32)
                     + lax.dot_general(onehot, gl, tn, preferred_element_type=F32))
    _run_copies(t, base, start_ref, off_ref, nblk_ref, pairs, sems.at[slot], True)

    @pl.when(t == last)
    def _():
        @pl.when(t >= 1)
        def _():
            _run_wait(t - 1, nblk_ref, pairs, sems.at[1 - slot], True)

        _run_wait(t, nblk_ref, pairs, sems.at[slot], True)
        zx[...] = jnp.zeros(zx.shape, zx.dtype)
        zg[...] = jnp.zeros(zg.shape, zg.dtype)
        copies = []
        for grp in range(N_EXPERT_GROUPS):
            tail = pl.ds(pl.multiple_of(end_ref[grp], MOE_RUN_ALIGN), MOE_TILE)
            copies += [pltpu.make_async_copy(zx, xs_hbm.at[tail], sems.at[0]),
                       pltpu.make_async_copy(zg, gs_hbm.at[tail], sems.at[0])]
        for cp in copies:
            cp.start()
        for cp in copies:
            cp.wait()


def _moe_pack(xn, gates, meta, tabs, n_rows):
    n = xn.shape[0]
    tm = MOE_SRC_TILE
    row = lambda w: pl.BlockSpec((tm, w), lambda i, *_: (i, 0))
    any_spec = pl.BlockSpec(memory_space=pl.ANY)
    grid_spec = pltpu.PrefetchScalarGridSpec(
        num_scalar_prefetch=4, grid=(n // tm,),
        in_specs=[row(D_MODEL), row(ROUTER_LANES), row(ROUTER_LANES)],
        out_specs=[any_spec, any_spec],
        scratch_shapes=[pltpu.VMEM((2 * MOE_SORTED_ROWS, D_MODEL), BF16),
                        pltpu.VMEM((2 * MOE_SORTED_ROWS, ROUTER_LANES), F32),
                        pltpu.VMEM((MOE_TILE, D_MODEL), BF16), pltpu.VMEM((MOE_TILE, ROUTER_LANES), F32),
                        pltpu.SemaphoreType.DMA((2,))])
    return pl.pallas_call(
        _moe_pack_kernel,
        grid_spec=grid_spec,
        out_shape=[jax.ShapeDtypeStruct((n_rows, D_MODEL), BF16), jax.ShapeDtypeStruct((n_rows, ROUTER_LANES), F32)],
        compiler_params=_cparams("arbitrary"),
        name="moe_pack",
    )(tabs["start"], tabs["off"], tabs["nblk"], tabs["end"], xn, gates, meta)


def _moe_expert_kernel(blk_ref, grp_ref, valid_ref, x_ref, gate_ref, wg_ref, wu_ref, wd_ref, o_ref, wgu_s, wd_s):
    i = pl.program_id(0)
    group = grp_ref[i]

    @pl.when((i == 0) | (group != grp_ref[jnp.maximum(i - 1, 0)]))
    def _():
        for j in range(EXPERTS_PER_STEP):
            wgu_s[j] = jnp.concatenate([wg_ref[j].astype(BF16), wu_ref[j].astype(BF16)], axis=1)
        wd_s[...] = wd_ref[...].astype(BF16)

    @pl.when(valid_ref[i] > 0)
    def _():
        xn = x_ref[...]
        gates = gate_ref[...]
        lane = lax.broadcasted_iota(jnp.int32, gates.shape, 1)
        first = EXPERT_LANE0 + group * EXPERTS_PER_GROUP
        hids = []
        for j in range(EXPERTS_PER_STEP):
            h = jnp.dot(xn, wgu_s[j], preferred_element_type=F32)
            ge = jnp.sum(jnp.where(lane == first + j, gates, 0.0), axis=-1, keepdims=True)
            hids.append((jax.nn.silu(h[:, :EXPERT_FF]) * h[:, EXPERT_FF:] * ge).astype(BF16))
        o_ref[...] = jnp.dot(jnp.concatenate(hids, axis=1), wd_s[...], preferred_element_type=F32)


def _moe_experts(xs, gs, tabs, wg, wu, wd, l, n_tiles):
    es = EXPERTS_PER_STEP
    row = lambda w: pl.BlockSpec((MOE_TILE, w), lambda i, blk, grp, valid: (blk[i], 0))
    grid_spec = pltpu.PrefetchScalarGridSpec(
        num_scalar_prefetch=3, grid=(n_tiles,),
        in_specs=[row(D_MODEL), row(ROUTER_LANES),
                  pl.BlockSpec((None, es, D_MODEL, EXPERT_FF), lambda i, blk, grp, valid: (l, grp[i], 0, 0)),
                  pl.BlockSpec((None, es, D_MODEL, EXPERT_FF), lambda i, blk, grp, valid: (l, grp[i], 0, 0)),
                  pl.BlockSpec((None, es * EXPERT_FF, D_MODEL), lambda i, blk, grp, valid: (l, grp[i], 0))],
        out_specs=row(D_MODEL),
        scratch_shapes=[pltpu.VMEM((es, D_MODEL, 2 * EXPERT_FF), BF16), pltpu.VMEM((es * EXPERT_FF, D_MODEL), BF16)])
    return pl.pallas_call(
        _moe_expert_kernel,
        grid_spec=grid_spec,
        out_shape=jax.ShapeDtypeStruct((xs.shape[0], D_MODEL), F32),
        compiler_params=_cparams("arbitrary"),
        name="hier_moe",
    )(tabs["tile_blk"], tabs["tile_grp"], tabs["tile_valid"], xs, gs, wg, wu, wd)


def _moe_unpack_kernel(n_prompt_tiles, start_ref, off_ref, nblk_ref, x_ref, meta_ref, ys_hbm, *rest):
    ybuf, sems = rest[-2:]
    t = pl.program_id(0)
    slot = t % 2
    base = pl.multiple_of(slot * MOE_SORTED_ROWS, MOE_SORTED_ROWS)
    pairs = [(ybuf, ys_hbm)]

    @pl.when(t == 0)
    def _():
        ybuf[...] = jnp.zeros(ybuf.shape, ybuf.dtype)
        _run_copies(t, base, start_ref, off_ref, nblk_ref, pairs, sems.at[slot], False)

    @pl.when(t + 1 < pl.num_programs(0))
    def _():
        nxt = pl.multiple_of((1 - slot) * MOE_SORTED_ROWS, MOE_SORTED_ROWS)
        _run_copies(t + 1, nxt, start_ref, off_ref, nblk_ref, pairs, sems.at[1 - slot], False)

    onehot = _sort_onehot(meta_ref[...], start_ref, t)
    _run_wait(t, nblk_ref, pairs, sems.at[slot], False)
    y = ybuf[pl.ds(base, MOE_SORTED_ROWS), :]
    yh = y.astype(BF16)
    yl = (y - yh.astype(F32)).astype(BF16)
    res = x_ref[...] + (jnp.dot(onehot, yh, preferred_element_type=F32)
                        + jnp.dot(onehot, yl, preferred_element_type=F32))
    if len(rest) == 3:
        rest[0][...] = res
    else:
        gf_ref, yp_ref, ysm_ref = rest[:3]
        yn = _rms(res, gf_ref[...])

        @pl.when(t < n_prompt_tiles)
        def _():
            yp_ref[...] = yn

        @pl.when(t >= n_prompt_tiles)
        def _():
            ysm_ref[...] = yn


def _moe_unpack(x, meta, ys, tabs, g_final=None, n_prompt=0):
    n = x.shape[0]
    tm = MOE_SRC_TILE
    npt = n_prompt // tm
    row = lambda w: pl.BlockSpec((tm, w), lambda i, *_: (i, 0))
    in_specs = [row(D_MODEL), row(ROUTER_LANES), pl.BlockSpec(memory_space=pl.ANY)]
    operands = [x, meta, ys]
    if g_final is None:
        out_specs = row(D_MODEL)
        out_shape = jax.ShapeDtypeStruct((n, D_MODEL), F32)
    else:
        nst = (n - n_prompt) // tm
        in_specs.append(pl.BlockSpec((1, D_MODEL), lambda i, *_: (0, 0)))
        operands.append(g_final)
        out_specs = [pl.BlockSpec((tm, D_MODEL), lambda i, *_: (jnp.minimum(i, npt - 1), 0)),
                     pl.BlockSpec((tm, D_MODEL), lambda i, *_: (jnp.clip(i - npt, 0, nst - 1), 0))]
        out_shape = [jax.ShapeDtypeStruct((n_prompt, D_MODEL), F32), jax.ShapeDtypeStruct((n - n_prompt, D_MODEL), F32)]
    grid_spec = pltpu.PrefetchScalarGridSpec(
        num_scalar_prefetch=3, grid=(n // tm,),
        in_specs=in_specs, out_specs=out_specs,
        scratch_shapes=[pltpu.VMEM((2 * MOE_SORTED_ROWS, D_MODEL), F32), pltpu.SemaphoreType.DMA((2,))])
    return pl.pallas_call(
        functools.partial(_moe_unpack_kernel, npt),
        grid_spec=grid_spec,
        out_shape=out_shape,
        compiler_params=_cparams("arbitrary"),
        name="moe_unpack",
    )(tabs["start"], tabs["off"], tabs["nblk"], *operands)


def _moe_tables(cnt, n):
    n_src = cnt.shape[0]
    pad = (cnt + MOE_RUN_ALIGN - 1) // MOE_RUN_ALIGN * MOE_RUN_ALIGN
    worst = n + n_src * (MOE_RUN_ALIGN - 1)
    cap = (worst + 2 * MOE_TILE - 1) // MOE_TILE * MOE_TILE
    start = jnp.cumsum(pad, axis=1) - pad
    total = jnp.sum(pad, axis=0)
    base = jnp.arange(N_EXPERT_GROUPS, dtype=jnp.int32) * cap
    off = base[None, :] + jnp.cumsum(pad, axis=0) - pad
    tiles_g = (total + MOE_TILE - 1) // MOE_TILE
    tile_end = jnp.cumsum(tiles_g)
    n_tiles = n // MOE_TILE + N_EXPERT_GROUPS + (n_src * N_EXPERT_GROUPS * MOE_RUN_ALIGN + MOE_TILE - 1) // MOE_TILE
    i = jnp.minimum(jnp.arange(n_tiles, dtype=jnp.int32), jnp.maximum(tile_end[-1] - 1, 0))
    grp = jnp.minimum(jnp.sum((i[:, None] >= tile_end[None, :]).astype(jnp.int32), axis=1), N_EXPERT_GROUPS - 1)
    first_tile = (tile_end - tiles_g)
    blk = jnp.zeros_like(i)
    for g in range(N_EXPERT_GROUPS):
        blk = blk + jnp.where(grp == g, g * (cap // MOE_TILE) + i - first_tile[g], 0)
    i32 = lambda a: a.astype(jnp.int32)
    tabs = dict(start=i32(start.reshape(-1)), off=i32(off.reshape(-1)), nblk=i32((pad // MOE_RUN_ALIGN).reshape(-1)),
                end=i32(base + total), tile_blk=i32(blk), tile_grp=i32(grp),
                tile_valid=i32(jnp.arange(n_tiles) < tile_end[-1]))
    return tabs, N_EXPERT_GROUPS * cap, n_tiles


def _moe(x, g, wr, br, wg, wu, wd, l, g_final=None, n_prompt=0):
    n = x.shape[0]
    xn, gates, meta, cnt = _moe_pre(x, g, wr, br, l)
    tabs, n_rows, n_tiles = _moe_tables(cnt[:, 0, :N_EXPERT_GROUPS].astype(jnp.int32), n)
    xs, gs = _moe_pack(xn, gates, meta, tabs, n_rows)
    ys = _moe_experts(xs, gs, tabs, wg, wu, wd, l, n_tiles)
    return _moe_unpack(x, meta, ys, tabs, g_final, n_prompt)


def _rope_tables(seq, t_len, tm):
    half = HEAD_DIM // 2
    inv = ROPE_THETA ** (-jnp.arange(half, dtype=F32) / half)
    pos_s = PAST_LEN + jnp.arange(t_len)
    pos = jnp.concatenate([jnp.arange(seq), jnp.tile(pos_s, tm // t_len)]).astype(F32)
    ang = pos[:, None] * inv[None, :]
    cos = jnp.tile(jnp.cos(ang), (1, LANES // half))
    sign = jnp.where((jnp.arange(LANES) % HEAD_DIM) < half, -1.0, 1.0).astype(F32)
    sin = jnp.tile(jnp.sin(ang), (1, LANES // half)) * sign[None, :]
    return cos, sin


def kernel(x_prompt, x_sample, cache_win_k, cache_win_v, state_ssm, cache_mem_k, cache_mem_v, mem_prompt, w_in, attn_sink, lam_re, lam_im, log_dt, ssm_b_re, ssm_b_im, ssm_c_re, ssm_c_im, ssm_d, w_glu, b_glu, g_attn_out, g_ssm_out, w_out, g_mix, g_xattn, g_mem, wq_x, wk_x, wv_x, wo_x, g_ffn, w_group, b_group, w_router, b_router, w_gate, w_up, w_down, g_final):
    batch, seq, _ = x_prompt.shape
    dec_batch, t_len, _ = x_sample.shape
    depth = w_in.shape[0]
    win_rows = cache_win_k.shape[2]
    n_p = batch * seq
    n_s = dec_batch * t_len
    tm_wide = 1024 if (n_p + n_s) % 1024 == 0 else 512
    tm = tm_wide
    tm_x = 512
    sample_nb = tm_x // t_len

    n = n_p + n_s
    xp = x_prompt.reshape(n_p, D_MODEL)
    xs = x_sample.reshape(n_s, D_MODEL)
    cos_tab, sin_tab = _rope_tables(seq, t_len, tm)
    mem_flat = mem_prompt.reshape(batch * N_MEM, D_MODEL)
    zero_state = jnp.zeros((batch, 1, 2 * N_STATE), F32)
    vec = lambda a: a.reshape(depth, 1, a.shape[-1])

    w_in_b, w_glu_b, w_out_b = w_in.astype(BF16), w_glu.astype(BF16), w_out.astype(BF16)
    wq_b, wk_b, wv_b, wo_b = (w.astype(BF16) for w in (wq_x, wk_x, wv_x, wo_x))
    wd_r = w_down.reshape(depth, N_EXPERTS * EXPERT_FF, D_MODEL)
    wr = jnp.concatenate([w_group, w_router.transpose(0, 2, 1, 3).reshape(depth, D_MODEL, N_EXPERTS)], axis=-1)
    wr = jnp.pad(wr, ((0, 0), (0, 0), (0, ROUTER_LANES - wr.shape[-1])))
    br = jnp.concatenate([b_group, b_router.reshape(depth, N_EXPERTS)], axis=-1)
    br = jnp.pad(br, ((0, 0), (0, ROUTER_LANES - br.shape[-1]))).reshape(depth, 1, ROUTER_LANES)
    g_mix_r, g_xattn_r, g_mem_r, g_ffn_r = vec(g_mix), vec(g_xattn), vec(g_mem), vec(g_ffn)
    g_a_r, g_s_r, b_glu_r, ssm_d_r = vec(g_attn_out), vec(g_ssm_out), vec(b_glu), vec(ssm_d)
    cache_k = cache_win_k.reshape(depth, dec_batch, win_rows, KV_WIDTH)
    cache_v = cache_win_v.reshape(depth, dec_batch, win_rows, KV_WIDTH)
    cmem_k = cache_mem_k.reshape(depth, dec_batch * N_MEM, D_MODEL)
    cmem_v = cache_mem_v.reshape(depth, dec_batch * N_MEM, D_MODEL)
    state_in = _state_to_tiles(state_ssm)

    outs = {k: [] for k in ("wk_p", "wv_p", "ssm_p", "mk_p", "mv_p", "wk_s", "wv_s", "ssm_s")}
    for l in range(depth):
        q, k, v, u = _in_proj(xp, xs, n, g_mix_r, w_in_b, l, cos_tab, sin_tab, n_p, seq, tm)
        attn_p = _attn_prompt(q, k, v, attn_sink[l], batch, seq)
        attn_s = _attn_sample(q, k, v, cache_k, cache_v, l, attn_sink[l], n_p, dec_batch, t_len, 4)
        tail = lambda a: jnp.stack([a[(b + 1) * seq - WINDOW:(b + 1) * seq] for b in range(batch)])
        outs["wk_p"].append(tail(k).reshape(batch, WINDOW, N_KV_HEADS, HEAD_DIM))
        outs["wv_p"].append(tail(v).reshape(batch, WINDOW, N_KV_HEADS, HEAD_DIM))
        ks = k[n_p:].reshape(dec_batch, t_len, KV_WIDTH)
        vs = v[n_p:].reshape(dec_batch, t_len, KV_WIDTH)
        k_all = jnp.concatenate([cache_k[l], ks], axis=1)[:, -win_rows:]
        v_all = jnp.concatenate([cache_v[l], vs], axis=1)[:, -win_rows:]
        outs["wk_s"].append(k_all.reshape(dec_batch, win_rows, N_KV_HEADS, HEAD_DIM))
        outs["wv_s"].append(v_all.reshape(dec_batch, win_rows, N_KV_HEADS, HEAD_DIM))

        we, tv, coef = _ssm_discretize(lam_re[l], lam_im[l], log_dt[l], ssm_b_re[l], ssm_b_im[l],
                                       ssm_c_re[l], ssm_c_im[l])
        y_p, fin_p = _ssm(u, zero_state, we, tv, ssm_d_r, l, coef, 0, batch, seq, 1, seq)
        y_s, fin_s = _ssm(u, state_in[l], we, tv, ssm_d_r, l, coef, n_p, dec_batch, t_len, dec_batch, t_len)
        outs["ssm_p"].append(_tiles_to_state(fin_p))
        outs["ssm_s"].append(_tiles_to_state(fin_s))
        x = _merge(xp, xs, n, attn_p, attn_s, y_p, y_s, w_glu_b, b_glu_r, g_a_r, g_s_r, w_out_b, l, tm)

        mk_p, mv_p = _mem_kv(mem_flat, g_mem_r, wk_b, wv_b, l, 512)
        outs["mk_p"].append(mk_p.reshape(batch, N_MEM, N_XHEADS, XHEAD_DIM))
        outs["mv_p"].append(mv_p.reshape(batch, N_MEM, N_XHEADS, XHEAD_DIM))
        tiles_per_seq = seq // tm
        x = _xattn(x, g_xattn_r, wq_b, wo_b, l, mk_p, mv_p,
                   pl.BlockSpec((N_MEM, D_MODEL), lambda i: (i // tiles_per_seq, 0)), 0, n_p, 1, tm)
        x = _xattn(x, g_xattn_r, wq_b, wo_b, l, cmem_k, cmem_v,
                   pl.BlockSpec((None, sample_nb * N_MEM, D_MODEL), lambda i: (l, i, 0)), n_p, n_s, sample_nb, tm_x)

        if l + 1 < depth:
            x = _moe(x, g_ffn_r, wr, br, w_gate, w_up, wd_r, l)
            xp = xs = x
        else:
            y_p, y_s = _moe(x, g_ffn_r, wr, br, w_gate, w_up, wd_r, l, g_final.reshape(1, D_MODEL), n_p)

    st = lambda name: jnp.stack(outs[name], axis=0)
    return (y_p.reshape(batch, seq, D_MODEL), y_s.reshape(dec_batch, t_len, D_MODEL),
            st("wk_p"), st("wv_p"), st("ssm_p"), st("mk_p"), st("mv_p"), st("wk_s"), st("wv_s"), st("ssm_s"))
```

```python
import functools
import math

import jax
import jax.numpy as jnp
from jax import lax
from jax.experimental import pallas as pl
from jax.experimental.pallas import tpu as pltpu

F32 = jnp.float32
BF16 = jnp.bfloat16

D_MODEL = 1024
CHUNK = 64
EPS = 1e-6
NEG_INF = -1e30
N_HEADS = 8
N_KV_HEADS = 2
HEAD_DIM = 64
ATTN_WIDTH = N_HEADS * HEAD_DIM
KV_WIDTH = N_KV_HEADS * HEAD_DIM
WINDOW = 128
ROPE_THETA = 10000.0
SSM_WIDTH = D_MODEL - ATTN_WIDTH
SSM_GROUP = 16
N_SSM_GROUPS = SSM_WIDTH // SSM_GROUP
SSM_STATE = 64
N_STATE = N_SSM_GROUPS * SSM_STATE
IN_WIDTH = ATTN_WIDTH + 2 * KV_WIDTH + SSM_WIDTH
N_MEM = 256
N_XHEADS = 4
XHEAD_DIM = D_MODEL // N_XHEADS
N_EXPERT_GROUPS = 4
EXPERTS_PER_GROUP = 8
N_EXPERTS = N_EXPERT_GROUPS * EXPERTS_PER_GROUP
EXPERT_FF = 128
PAST_LEN = 4096

LANES = 128
SUBLANES = 8
VMEM_LIMIT = 56 * 1024 * 1024


def _cparams(*sem):
    return pltpu.CompilerParams(dimension_semantics=sem, vmem_limit_bytes=VMEM_LIMIT)


def _rms(x, g):
    return x * lax.rsqrt(jnp.mean(x * x, axis=-1, keepdims=True) + EPS) * g


def _layer_spec(arr, l):
    shape = arr.shape[1:]
    zeros = (0,) * len(shape)
    return pl.BlockSpec((None,) + shape, lambda *_: (l,) + zeros, pipeline_mode=pl.Buffered(1))


def _rope_pairs(t, cos, sin_signed, first_half):
    swapped = jnp.where(first_half, pltpu.roll(t, LANES - HEAD_DIM // 2, 1), pltpu.roll(t, HEAD_DIM // 2, 1))
    return t * cos + swapped * sin_signed


def _two_source_specs(xp, xs, n_prompt, tm, width):
    npt = n_prompt // tm
    s_off = 0 if xs is xp else npt
    s_last = xs.shape[0] // tm - 1
    pspec = pl.BlockSpec((tm, width), lambda i, *_: (jnp.minimum(i, npt - 1), 0))
    sspec = pl.BlockSpec((tm, width), lambda i, *_: (jnp.clip(i - s_off, npt - s_off, s_last), 0))
    return pspec, sspec


def _in_proj_kernel(n_prompt_tiles, xp_ref, xs_ref, g_ref, w_ref, cos_ref, sin_ref, q_ref, k_ref, v_ref, u_ref):
    x = jnp.where(pl.program_id(0) < n_prompt_tiles, xp_ref[...], xs_ref[...])
    xn = _rms(x, g_ref[...])
    z = jnp.dot(xn.astype(BF16), w_ref[...], preferred_element_type=F32)
    cos = cos_ref[...]
    sin = sin_ref[...]
    lane = lax.broadcasted_iota(jnp.int32, cos.shape, 1)
    first_half = (lane % HEAD_DIM) < (HEAD_DIM // 2)
    scale = 1.0 / math.sqrt(HEAD_DIM)
    for j in range(ATTN_WIDTH // LANES):
        t = z[:, j * LANES:(j + 1) * LANES]
        q_ref[:, j * LANES:(j + 1) * LANES] = (_rope_pairs(t, cos, sin, first_half) * scale).astype(BF16)
    k_ref[...] = _rope_pairs(z[:, ATTN_WIDTH:ATTN_WIDTH + KV_WIDTH], cos, sin, first_half)
    v_ref[...] = z[:, ATTN_WIDTH + KV_WIDTH:ATTN_WIDTH + 2 * KV_WIDTH]
    u_ref[...] = z[:, ATTN_WIDTH + 2 * KV_WIDTH:]


def _in_proj(xp, xs, n, g, w_bf16, l, cos_tab, sin_tab, n_prompt, seq, tm):
    n_prompt_tiles = n_prompt // tm
    tiles_per_seq = seq // tm

    def tab_map(i):
        return (jnp.where(i < n_prompt_tiles, i % tiles_per_seq, tiles_per_seq), 0)

    row = lambda w: pl.BlockSpec((tm, w), lambda i: (i, 0))
    return pl.pallas_call(
        functools.partial(_in_proj_kernel, n_prompt_tiles),
        grid=(n // tm,),
        in_specs=[*_two_source_specs(xp, xs, n_prompt, tm, D_MODEL), _layer_spec(g, l), _layer_spec(w_bf16, l),
                  pl.BlockSpec((tm, LANES), tab_map),
                  pl.BlockSpec((tm, LANES), tab_map)],
        out_specs=[row(ATTN_WIDTH), row(KV_WIDTH), row(KV_WIDTH), row(SSM_WIDTH)],
        out_shape=[jax.ShapeDtypeStruct((n, ATTN_WIDTH), BF16),
                   jax.ShapeDtypeStruct((n, KV_WIDTH), F32),
                   jax.ShapeDtypeStruct((n, KV_WIDTH), F32),
                   jax.ShapeDtypeStruct((n, SSM_WIDTH), F32)],
        compiler_params=_cparams("parallel"),
        name="in_proj",
    )(xp, xs, g, w_bf16, cos_tab, sin_tab)


def _kv_pairs(keys, vals):
    lane = lax.broadcasted_iota(jnp.int32, keys.shape, 1)
    low = lane < HEAD_DIM
    k_sw = pltpu.roll(keys, HEAD_DIM, 1)
    v_sw = pltpu.roll(vals, HEAD_DIM, 1)
    kk = [jnp.where(low, keys, k_sw).astype(BF16), jnp.where(low, k_sw, keys).astype(BF16)]
    vv = [jnp.where(low, vals, v_sw).astype(BF16), jnp.where(low, v_sw, vals).astype(BF16)]
    return kk, vv


def _attend_pairs(q, kk, vv, key_rows, mask_add, sink_ref, o_ref, row0):
    tq = q.shape[0]
    qlane = lax.broadcasted_iota(jnp.int32, (tq, LANES), 1)
    qlow = qlane < HEAD_DIM
    row_top = lax.broadcasted_iota(jnp.int32, (2 * tq, 1), 0) < tq
    zero = jnp.zeros((), BF16)
    for pair in range(N_HEADS // 2):
        kv = pair // (N_HEADS // N_KV_HEADS // 2)
        qp = q[:, pair * LANES:(pair + 1) * LANES]
        qs = jnp.concatenate([jnp.where(qlow, qp, zero), jnp.where(qlow, zero, qp)], axis=0)
        s = lax.dot_general(qs, kk[kv][key_rows, :], (((1,), (1,)), ((), ())), preferred_element_type=F32)
        if mask_add is not None:
            s = s + mask_add
        sink = jnp.where(row_top, sink_ref[2 * pair], sink_ref[2 * pair + 1])
        m = jnp.maximum(jnp.max(s, axis=-1, keepdims=True), sink)
        e = jnp.exp(s - m)
        p = e / (jnp.sum(e, axis=-1, keepdims=True) + jnp.exp(sink - m))
        o = jnp.dot(p.astype(BF16), vv[kv][key_rows, :], preferred_element_type=F32)
        o_ref[row0:row0 + tq, pair * LANES:(pair + 1) * LANES] = jnp.where(qlow, o[:tq], o[tq:])


def _attend_blocks(blocks, sink_ref, o_ref):
    tq = blocks[0][0].shape[0]
    qlane = lax.broadcasted_iota(jnp.int32, (tq, LANES), 1)
    qlow = qlane < HEAD_DIM
    row_top = lax.broadcasted_iota(jnp.int32, (2 * tq, 1), 0) < tq
    zero = jnp.zeros((), BF16)
    scores, sinks = [], []
    for q, kk, vv, key_rows, mask_add, row0 in blocks:
        for pair in range(N_HEADS // 2):
            kv = pair // (N_HEADS // N_KV_HEADS // 2)
            qp = q[:, pair * LANES:(pair + 1) * LANES]
            qs = jnp.concatenate([jnp.where(qlow, qp, zero), jnp.where(qlow, zero, qp)], axis=0)
            s = lax.dot_general(qs, kk[kv][key_rows, :], (((1,), (1,)), ((), ())), preferred_element_type=F32)
            scores.append(s if mask_add is None else s + mask_add)
            sinks.append(jnp.where(row_top, sink_ref[2 * pair], sink_ref[2 * pair + 1]))
    s = jnp.concatenate(scores, axis=0)
    sink = jnp.concatenate(sinks, axis=0)
    m = jnp.maximum(jnp.max(s, axis=-1, keepdims=True), sink)
    e = jnp.exp(s - m)
    p = (e / (jnp.sum(e, axis=-1, keepdims=True) + jnp.exp(sink - m))).astype(BF16)
    piece = 0
    for q, kk, vv, key_rows, mask_add, row0 in blocks:
        for pair in range(N_HEADS // 2):
            kv = pair // (N_HEADS // N_KV_HEADS // 2)
            o = jnp.dot(p[piece * 2 * tq:(piece + 1) * 2 * tq, :], vv[kv][key_rows, :], preferred_element_type=F32)
            o_ref[row0:row0 + tq, pair * LANES:(pair + 1) * LANES] = jnp.where(qlow, o[:tq], o[tq:])
            piece += 1


ATTN_SUB = WINDOW
ATTN_TILE = 4 * ATTN_SUB


def _attn_prompt_kernel(sink_ref, ma_ref, mb_ref, q_ref, kp_ref, kc_ref, vp_ref, vc_ref, o_ref):
    kk, vv = _kv_pairs(jnp.concatenate([kp_ref[...], kc_ref[...]], axis=0),
                       jnp.concatenate([vp_ref[...], vc_ref[...]], axis=0))
    for s in range(ATTN_TILE // ATTN_SUB):
        m_ref = ma_ref if s == 0 else mb_ref
        rows = slice(s * ATTN_SUB, s * ATTN_SUB + 2 * WINDOW)
        _attend_pairs(q_ref[s * ATTN_SUB:(s + 1) * ATTN_SUB, :], kk, vv, rows, m_ref[...], sink_ref, o_ref, s * ATTN_SUB)


def _band_masks():
    r = (jnp.arange(2 * ATTN_SUB) % ATTN_SUB)[:, None] // CHUNK
    c = jnp.arange(2 * WINDOW)[None, :]
    band = (c // CHUNK >= r) & (c // CHUNK <= r + WINDOW // CHUNK)
    masks = jnp.stack([band, band & (c >= WINDOW)])
    return jnp.where(masks, 0.0, NEG_INF).astype(F32)


def _attn_prompt(q, k, v, sink, batch, seq):
    nt = seq // ATTN_TILE
    per_seq = seq // ATTN_SUB
    cur = lambda b, i: (b * nt + i, 0)
    prev = lambda b, i: (b * per_seq + jnp.maximum((ATTN_TILE // ATTN_SUB) * i - 1, 0), 0)
    masks = _band_masks()
    return pl.pallas_call(
        _attn_prompt_kernel,
        grid=(batch, nt),
        in_specs=[pl.BlockSpec(memory_space=pltpu.SMEM),
                  pl.BlockSpec((None, 2 * ATTN_SUB, 2 * WINDOW), lambda b, i: (jnp.where(i == 0, 1, 0), 0, 0)),
                  pl.BlockSpec((None, 2 * ATTN_SUB, 2 * WINDOW), lambda b, i: (0, 0, 0)),
                  pl.BlockSpec((ATTN_TILE, ATTN_WIDTH), cur),
                  pl.BlockSpec((ATTN_SUB, KV_WIDTH), prev),
                  pl.BlockSpec((ATTN_TILE, KV_WIDTH), cur),
                  pl.BlockSpec((ATTN_SUB, KV_WIDTH), prev),
                  pl.BlockSpec((ATTN_TILE, KV_WIDTH), cur)],
        out_specs=pl.BlockSpec((ATTN_TILE, ATTN_WIDTH), cur),
        out_shape=jax.ShapeDtypeStruct((batch * seq, ATTN_WIDTH), F32),
        compiler_params=_cparams("parallel", "parallel"),
        name="attn_prompt",
    )(sink, masks, masks, q, k, k, v, v)


def _attn_sample_kernel(sink_ref, q_ref, ck_ref, cv_ref, k_ref, v_ref, o_ref):
    nb = ck_ref.shape[0]
    t = q_ref.shape[0] // nb
    blocks = []
    for b in range(nb):
        rows = slice(b * t, (b + 1) * t)
        kk, vv = _kv_pairs(jnp.concatenate([ck_ref[b], k_ref[rows, :]], axis=0),
                           jnp.concatenate([cv_ref[b], v_ref[rows, :]], axis=0))
        blocks.append((q_ref[rows, :], kk, vv, slice(None), None, b * t))
    _attend_blocks(blocks, sink_ref, o_ref)


def _attn_sample(q, k, v, cache_k, cache_v, l, sink, n_prompt, dec_batch, t, nb):
    w = cache_k.shape[2]
    rows = nb * t
    base = n_prompt // rows
    tok = lambda width: pl.BlockSpec((rows, width), lambda i: (base + i, 0))
    cache = pl.BlockSpec((None, nb, w, KV_WIDTH), lambda i: (l, i, 0, 0))
    return pl.pallas_call(
        _attn_sample_kernel,
        grid=(dec_batch // nb,),
        in_specs=[pl.BlockSpec(memory_space=pltpu.SMEM),
                  tok(ATTN_WIDTH), cache, cache, tok(KV_WIDTH), tok(KV_WIDTH)],
        out_specs=pl.BlockSpec((rows, ATTN_WIDTH), lambda i: (i, 0)),
        out_shape=jax.ShapeDtypeStruct((dec_batch * t, ATTN_WIDTH), F32),
        compiler_params=_cparams("parallel"),
        name="attn_sample",
    )(sink, q, cache_k, cache_v, k, v)


SSM_BLOCK = SUBLANES


def _cmul(ar, ai, br, bi):
    return ar * br - ai * bi, ar * bi + ai * br


def _ssm_disc_kernel(lre_ref, lim_ref, dt_ref, bre_ref, bim_ref, cre_ref, cim_ref,
                     we_ref, tv_ref, coef_ref, vt_ref, wb_ref):
    we_ref[...] = jnp.zeros(we_ref.shape, we_ref.dtype)
    vt_ref[...] = jnp.zeros(vt_ref.shape, vt_ref.dtype)
    wb_ref[...] = jnp.zeros(wb_ref.shape, wb_ref.dtype)
    lane = lax.broadcasted_iota(jnp.int32, (SSM_GROUP, LANES), 1)
    half = [lane < SSM_STATE, lane >= SSM_STATE]
    row8 = lax.broadcasted_iota(jnp.int32, (SUBLANES, LANES), 0)
    for q in range(STATE_TILES):
        lre = lre_ref[q]
        lim = lim_ref[q]
        dt = dt_ref[q]
        mag = jnp.exp(lre * dt)
        ang = lim * dt
        lbr = mag * jnp.cos(ang)
        lbi = mag * jnp.sin(ang)
        nr, ni = lbr - 1.0, lbi
        den = lre * lre + lim * lim
        fr = (nr * lre + ni * lim) / den
        fi = (ni * lre - nr * lim) / den
        bbr, bbi = _cmul(fr, fi, bre_ref[q], bim_ref[q])
        cr, ci = cre_ref[q], cim_ref[q]
        pw = [(jnp.ones_like(lbr), jnp.zeros_like(lbr))]
        for _ in range(SSM_BLOCK):
            pw.append(_cmul(pw[-1][0], pw[-1][1], lbr, lbi))
        re_l = slice(q * 2 * LANES, q * 2 * LANES + LANES)
        im_l = slice(q * 2 * LANES + LANES, (q + 1) * 2 * LANES)
        for h in range(2):
            g = 2 * q + h
            grow = lambda blk: slice(blk * LANES + g * SSM_GROUP, blk * LANES + (g + 1) * SSM_GROUP)
            for j in range(SSM_BLOCK):
                wr, wi = _cmul(pw[SSM_BLOCK - 1 - j][0], pw[SSM_BLOCK - 1 - j][1], bbr, bbi)
                we_ref[grow(j), re_l] = jnp.where(half[h], wr, 0.0).astype(we_ref.dtype)
                we_ref[grow(j), im_l] = jnp.where(half[h], wi, 0.0).astype(we_ref.dtype)
            for d in range(SSM_BLOCK + 1):
                xr, xi = _cmul(cr, ci, pw[d][0], pw[d][1])
                vt_ref[grow(d), re_l] = jnp.where(half[h], xr, 0.0)
                vt_ref[grow(d), im_l] = jnp.where(half[h], -xi, 0.0)
            wb_ref[grow(0), re_l] = jnp.where(half[h], bbr, 0.0)
            wb_ref[grow(0), im_l] = jnp.where(half[h], bbi, 0.0)
        l8 = [pw[SSM_BLOCK]]
        for _ in range(SUBLANES - 1):
            l8.append(_cmul(l8[-1][0], l8[-1][1], pw[SSM_BLOCK][0], pw[SSM_BLOCK][1]))
        co = slice(q * LANES, (q + 1) * LANES)
        for kind, sh in enumerate((1, 2, 4)):
            for a in range(2):
                coef_ref[kind, a, :, co] = jnp.where(row8 >= sh, l8[sh - 1][a], 0.0)
        for a in range(2):
            tab = jnp.zeros((SUBLANES, LANES), F32)
            for k in range(SUBLANES):
                tab = jnp.where(row8 == k, l8[k][a], tab)
            coef_ref[3, a, :, co] = tab
    t0 = lax.dot_general(wb_ref[...], vt_ref[0:BLOCK_LANES, :], (((1,), (1,)), ((), ())),
                         preferred_element_type=F32, precision=lax.Precision.HIGHEST)
    for j in range(SSM_BLOCK):
        if j:
            tv_ref[j * LANES:(j + 1) * LANES, 0:j * LANES] = jnp.zeros((LANES, j * LANES), tv_ref.dtype)
        tv_ref[j * LANES:(j + 1) * LANES, j * LANES:] = t0[:, 0:BLOCK_LANES - j * LANES].astype(tv_ref.dtype)
    tv_ref[BLOCK_LANES:, :] = jnp.transpose(vt_ref[LANES:, :]).astype(tv_ref.dtype)


def _ssm_discretize(lam_re, lam_im, log_dt, b_re, b_im, c_re, c_im):
    g, p = lam_re.shape
    npair = g // 2
    pair = lambda a: a.reshape(npair, 1, 2 * p)
    rows = lambda a: a.reshape(npair, 2, SSM_GROUP, p).transpose(0, 2, 1, 3).reshape(npair, SSM_GROUP, 2 * p)
    dt = jnp.repeat(jnp.exp(log_dt), p).reshape(npair, 1, 2 * p)
    vec = pl.BlockSpec((STATE_TILES, 1, LANES), lambda m: (m, 0, 0))
    mat = pl.BlockSpec((STATE_TILES, SSM_GROUP, LANES), lambda m: (m, 0, 0))
    return pl.pallas_call(
        _ssm_disc_kernel,
        grid=(N_CH_BLOCKS,),
        in_specs=[vec, vec, vec, mat, mat, mat, mat],
        out_specs=[pl.BlockSpec((None, BLOCK_LANES, STATE_LANES), lambda m: (m, 0, 0)),
                   pl.BlockSpec((None, BLOCK_LANES + STATE_LANES, BLOCK_LANES), lambda m: (m, 0, 0)),
                   pl.BlockSpec((None, 4, 2, SUBLANES, STATE_LANES // 2), lambda m: (m, 0, 0, 0, 0))],
        out_shape=[jax.ShapeDtypeStruct((N_CH_BLOCKS, BLOCK_LANES, STATE_LANES), BF16),
                   jax.ShapeDtypeStruct((N_CH_BLOCKS, BLOCK_LANES + STATE_LANES, BLOCK_LANES), BF16),
                   jax.ShapeDtypeStruct((N_CH_BLOCKS, 4, 2, SUBLANES, STATE_LANES // 2), F32)],
        scratch_shapes=[pltpu.VMEM(((SSM_BLOCK + 1) * LANES, STATE_LANES), F32),
                        pltpu.VMEM((LANES, STATE_LANES), F32)],
        compiler_params=_cparams("parallel"),
        name="ssm_discretize",
    )(pair(lam_re), pair(lam_im), dt, rows(b_re.transpose(0, 2, 1)), rows(b_im.transpose(0, 2, 1)),
      rows(c_re), rows(c_im))


GROUPS_PER_CH_BLOCK = LANES // SSM_GROUP
N_CH_BLOCKS = SSM_WIDTH // LANES
STATE_LANES = 2 * GROUPS_PER_CH_BLOCK * SSM_STATE
STATE_TILES = STATE_LANES // (2 * LANES)
BLOCK_LANES = SSM_BLOCK * LANES


def _ssm_kernel(nb, u_ref, s0_ref, we_ref, tv_ref, d_ref, coef_ref, y_ref, fin_ref, st_ref, sprev_ref, ucat_ref):
    t_idx = pl.program_id(2)
    r = st_ref.shape[1] - SUBLANES
    rows = nb * r

    @pl.when(t_idx == 0)
    def _():
        for b in range(nb):
            st_ref[b, 0:SUBLANES, :] = jnp.broadcast_to(s0_ref[b], (SUBLANES, STATE_LANES))

    for j in range(SSM_BLOCK):
        ucat_ref[:, j * LANES:(j + 1) * LANES] = u_ref[pl.ds(j, rows, stride=SSM_BLOCK), :].astype(BF16)
    e = jnp.dot(ucat_ref[...], we_ref[...], preferred_element_type=F32)
    st_ref[:, SUBLANES:, :] = e.reshape(nb, r, STATE_LANES)

    first_row = lax.broadcasted_iota(jnp.int32, (SUBLANES, LANES), 0) == 0

    def group(rg, carry):
        r0 = pl.multiple_of(SUBLANES + rg * SUBLANES, SUBLANES)
        rp = pl.multiple_of(rg * SUBLANES, SUBLANES)
        for b in range(nb):
            for q in range(STATE_TILES):
                re_sl = pl.ds(q * 2 * LANES, LANES)
                im_sl = pl.ds(q * 2 * LANES + LANES, LANES)
                co = pl.ds(q * LANES, LANES)
                xr = st_ref[b, pl.ds(r0, SUBLANES), re_sl]
                xi = st_ref[b, pl.ds(r0, SUBLANES), im_sl]
                for step, sh in enumerate((1, 2, 4)):
                    ar, ai = _cmul(coef_ref[step, 0, :, co], coef_ref[step, 1, :, co],
                                   pltpu.roll(xr, sh, 0), pltpu.roll(xi, sh, 0))
                    xr = xr + ar
                    xi = xi + ai
                pr = jnp.broadcast_to(st_ref[b, pl.ds(rp, SUBLANES), re_sl][SUBLANES - 1:, :], (SUBLANES, LANES))
                pi = jnp.broadcast_to(st_ref[b, pl.ds(rp, SUBLANES), im_sl][SUBLANES - 1:, :], (SUBLANES, LANES))
                ar, ai = _cmul(coef_ref[3, 0, :, co], coef_ref[3, 1, :, co], pr, pi)
                xr = xr + ar
                xi = xi + ai
                st_ref[b, pl.ds(r0, SUBLANES), re_sl] = xr
                st_ref[b, pl.ds(r0, SUBLANES), im_sl] = xi
                out_rows = pl.ds(pl.multiple_of(b * r + rg * SUBLANES, SUBLANES), SUBLANES)
                sprev_ref[out_rows, re_sl] = jnp.where(first_row, pr, pltpu.roll(xr, 1, 0))
                sprev_ref[out_rows, im_sl] = jnp.where(first_row, pi, pltpu.roll(xi, 1, 0))
        return carry

    lax.fori_loop(0, r // SUBLANES, group, 0)

    lhs = jnp.concatenate([ucat_ref[...], sprev_ref[...].astype(BF16)], axis=1)
    ycat = jnp.dot(lhs, tv_ref[...], preferred_element_type=F32)
    d = d_ref[...]
    for t in range(SSM_BLOCK):
        tok = pl.ds(t, rows, stride=SSM_BLOCK)
        y_ref[tok, :] = ycat[:, t * LANES:(t + 1) * LANES] + d * u_ref[tok, :]

    for b in range(nb):
        tail = st_ref[b, r:r + SUBLANES, :]
        st_ref[b, 0:SUBLANES, :] = tail
        fin_ref[b] = tail[SUBLANES - 1:SUBLANES, :]


def _ssm(u, s0, we, tv, d, l, coef, row_base, n_seq, seq_len, nb, tt):
    rows = nb * tt
    r = tt // SSM_BLOCK
    nt = seq_len // tt
    base = row_base // rows
    return pl.pallas_call(
        functools.partial(_ssm_kernel, nb),
        grid=(N_CH_BLOCKS, n_seq // nb, nt),
        in_specs=[pl.BlockSpec((rows, LANES), lambda m, i, j: (base + i * nt + j, m)),
                  pl.BlockSpec((nb, 1, STATE_LANES), lambda m, i, j: (i, 0, m)),
                  pl.BlockSpec((None, BLOCK_LANES, STATE_LANES), lambda m, i, j: (m, 0, 0)),
                  pl.BlockSpec((None, BLOCK_LANES + STATE_LANES, BLOCK_LANES), lambda m, i, j: (m, 0, 0)),
                  pl.BlockSpec((None, 1, LANES), lambda m, i, j: (l, 0, m)),
                  pl.BlockSpec((None, 4, 2, SUBLANES, STATE_LANES // 2), lambda m, i, j: (m, 0, 0, 0, 0))],
        out_specs=[pl.BlockSpec((rows, LANES), lambda m, i, j: (i * nt + j, m)),
                   pl.BlockSpec((nb, 1, STATE_LANES), lambda m, i, j: (i, 0, m))],
        out_shape=[jax.ShapeDtypeStruct((n_seq * seq_len, SSM_WIDTH), F32),
                   jax.ShapeDtypeStruct((n_seq, 1, N_CH_BLOCKS * STATE_LANES), F32)],
        scratch_shapes=[pltpu.VMEM((nb, SUBLANES + r, STATE_LANES), F32),
                        pltpu.VMEM((nb * r, STATE_LANES), F32),
                        pltpu.VMEM((nb * r, BLOCK_LANES), BF16)],
        compiler_params=_cparams("parallel", "parallel", "arbitrary"),
        name="ssm_scan",
    )(u, s0, we, tv, d, coef)


def _state_to_tiles(s):
    lead = s.shape[:-3]
    t = s.reshape(lead + (N_CH_BLOCKS, STATE_TILES, 2, SSM_STATE, 2))
    t = jnp.moveaxis(t, -1, -3)
    return t.reshape(lead + (1, N_CH_BLOCKS * STATE_LANES))


def _tiles_to_state(f):
    b = f.shape[0]
    t = f.reshape(b, N_CH_BLOCKS, STATE_TILES, 2, 2, SSM_STATE)
    t = jnp.moveaxis(t, 3, -1)
    return t.reshape(b, N_SSM_GROUPS, SSM_STATE, 2)


def _merge_kernel(n_prompt_tiles, xp_ref, xs_ref, ap_ref, as_ref, yp_ref, ys_ref, wglu_ref, bglu_ref, ga_ref, gs_ref,
                  wout_ref, o_ref):
    is_prompt = pl.program_id(0) < n_prompt_tiles
    attn = jnp.where(is_prompt, ap_ref[...], as_ref[...])
    g = jax.nn.gelu(jnp.where(is_prompt, yp_ref[...], ys_ref[...]))
    glu = g * jax.nn.sigmoid(jnp.dot(g.astype(BF16), wglu_ref[...], preferred_element_type=F32) + bglu_ref[...])
    na = _rms(attn, ga_ref[...]).astype(BF16)
    ns = _rms(glu, gs_ref[...]).astype(BF16)
    o = jnp.dot(na, wout_ref[0:ATTN_WIDTH, :], preferred_element_type=F32)
    o = o + jnp.dot(ns, wout_ref[ATTN_WIDTH:, :], preferred_element_type=F32)
    o_ref[...] = jnp.where(is_prompt, xp_ref[...], xs_ref[...]) + o


def _merge(xp, xs, n, attn_p, attn_s, y_p, y_s, wglu, bglu, ga, gs, wout, l, tm):
    npt = attn_p.shape[0] // tm
    nst = attn_s.shape[0] // tm
    row = lambda w: pl.BlockSpec((tm, w), lambda i: (i, 0))
    prow = lambda w: pl.BlockSpec((tm, w), lambda i: (jnp.minimum(i, npt - 1), 0))
    srow = lambda w: pl.BlockSpec((tm, w), lambda i: (jnp.clip(i - npt, 0, nst - 1), 0))
    return pl.pallas_call(
        functools.partial(_merge_kernel, npt),
        grid=(n // tm,),
        in_specs=[*_two_source_specs(xp, xs, attn_p.shape[0], tm, D_MODEL),
                  prow(ATTN_WIDTH), srow(ATTN_WIDTH), prow(SSM_WIDTH), srow(SSM_WIDTH),
                  _layer_spec(wglu, l), _layer_spec(bglu, l), _layer_spec(ga, l), _layer_spec(gs, l),
                  _layer_spec(wout, l)],
        out_specs=row(D_MODEL),
        out_shape=jax.ShapeDtypeStruct((n, D_MODEL), F32),
        compiler_params=_cparams("parallel"),
        name="merge_heads",
    )(xp, xs, attn_p, attn_s, y_p, y_s, wglu, bglu, ga, gs, wout)


def _mem_kv_kernel(m_ref, g_ref, wk_ref, wv_ref, k_ref, v_ref):
    mn = _rms(m_ref[...], g_ref[...]).astype(BF16)
    k_ref[...] = jnp.dot(mn, wk_ref[...], preferred_element_type=F32)
    v_ref[...] = jnp.dot(mn, wv_ref[...], preferred_element_type=F32)


def _mem_kv(mem, g, wk, wv, l, tm):
    n = mem.shape[0]
    row = pl.BlockSpec((tm, D_MODEL), lambda i: (i, 0))
    return pl.pallas_call(
        _mem_kv_kernel,
        grid=(n // tm,),
        in_specs=[row, _layer_spec(g, l), _layer_spec(wk, l), _layer_spec(wv, l)],
        out_specs=[row, row],
        out_shape=[jax.ShapeDtypeStruct((n, D_MODEL), F32)] * 2,
        compiler_params=_cparams("parallel"),
        name="mem_kv",
    )(mem, g, wk, wv)


def _xattn_kernel(nb, x_ref, g_ref, wq_ref, wo_ref, mk_ref, mv_ref, o_ref, att_ref):
    t = x_ref.shape[0] // nb
    x = x_ref[...]
    hn = _rms(x, g_ref[...]).astype(BF16)
    q = jnp.dot(hn, wq_ref[...], preferred_element_type=F32) * (1.0 / math.sqrt(XHEAD_DIM))
    q = q.astype(BF16)
    heads = [slice(h * XHEAD_DIM, (h + 1) * XHEAD_DIM) for h in range(N_XHEADS)]
    scores = []
    for b in range(nb):
        mk = mk_ref[b * N_MEM:(b + 1) * N_MEM, :].astype(BF16)
        for sl in heads:
            scores.append(lax.dot_general(q[b * t:(b + 1) * t, sl], mk[:, sl], (((1,), (1,)), ((), ())),
                                          preferred_element_type=F32))
    s = jnp.concatenate(scores, axis=0)
    e = jnp.exp(s - jnp.max(s, axis=-1, keepdims=True))
    p = (e / jnp.sum(e, axis=-1, keepdims=True)).astype(BF16)
    for b in range(nb):
        mv = mv_ref[b * N_MEM:(b + 1) * N_MEM, :].astype(BF16)
        for h, sl in enumerate(heads):
            r0 = (b * N_XHEADS + h) * t
            att_ref[b * t:(b + 1) * t, sl] = jnp.dot(p[r0:r0 + t, :], mv[:, sl], preferred_element_type=F32)
    o = jnp.dot(att_ref[...].astype(BF16), wo_ref[...], preferred_element_type=F32)
    o_ref[...] = x + o


def _xattn(x, g, wq, wo, l, mk, mv, mem_spec, row_base, n_rows, nb, tm):
    base = row_base // tm
    xspec = pl.BlockSpec((tm, D_MODEL), lambda i: (base + i, 0))
    return pl.pallas_call(
        functools.partial(_xattn_kernel, nb),
        grid=(n_rows // tm,),
        in_specs=[xspec, _layer_spec(g, l), _layer_spec(wq, l), _layer_spec(wo, l), mem_spec, mem_spec],
        out_specs=xspec,
        out_shape=jax.ShapeDtypeStruct(x.shape, F32),
        scratch_shapes=[pltpu.VMEM((tm, D_MODEL), F32)],
        input_output_aliases={0: 0},
        compiler_params=_cparams("parallel"),
        name="cross_attn",
    )(x, g, wq, wo, mk, mv)


ROUTER_LANES = LANES
EXPERT_LANE0 = N_EXPERT_GROUPS
EXPERTS_PER_STEP = EXPERTS_PER_GROUP
MOE_SRC_TILE = 512
MOE_RUN_ALIGN = 16
MOE_SORTED_ROWS = 640
MOE_TILE = 1024


def _dot_f32_3pass(x, w):
    xh = x.astype(BF16)
    xl = (x - xh.astype(F32)).astype(BF16)
    wh = w.astype(BF16)
    wl = (w - wh.astype(F32)).astype(BF16)
    dot = lambda a, b: jnp.dot(a, b, preferred_element_type=F32)
    return dot(xh, wh) + (dot(xl, wh) + dot(xh, wl))


def _route(logits):
    lane_i = lax.broadcasted_iota(jnp.int32, logits.shape, 1)
    lane = lane_i.astype(F32)
    neg = jnp.float32(-jnp.inf)
    is_g = lane_i < N_EXPERT_GROUPS
    gl = jnp.where(is_g, logits, neg)
    gmax = jnp.max(gl, axis=-1, keepdims=True)
    gidx = jnp.min(jnp.where(gl == gmax, lane, float(ROUTER_LANES)), axis=-1, keepdims=True)
    g_w = 1.0 / jnp.sum(jnp.where(is_g, jnp.exp(gl - gmax), 0.0), axis=-1, keepdims=True)
    first = EXPERT_LANE0 + gidx * EXPERTS_PER_GROUP
    sel = (lane >= first) & (lane < first + EXPERTS_PER_GROUP)
    el = jnp.where(sel, logits, neg)
    m1 = jnp.max(el, axis=-1, keepdims=True)
    i1 = jnp.min(jnp.where(el == m1, lane, float(ROUTER_LANES)), axis=-1, keepdims=True)
    el2 = jnp.where(lane == i1, neg, el)
    m2 = jnp.max(el2, axis=-1, keepdims=True)
    i2 = jnp.min(jnp.where(el2 == m2, lane, float(ROUTER_LANES)), axis=-1, keepdims=True)
    r = jnp.exp(m2 - m1)
    w1 = g_w / (1.0 + r)
    w2 = w1 * r
    return jnp.where(lane == i1, w1, jnp.where(lane == i2, w2, 0.0)), gidx


def _moe_pre_kernel(x_ref, g_ref, wr_ref, br_ref, xn_ref, gate_ref, meta_ref, cnt_ref, tri_ref):
    tm = x_ref.shape[0]

    @pl.when(pl.program_id(0) == 0)
    def _():
        r = lax.broadcasted_iota(jnp.int32, (tm, tm), 0)
        c = lax.broadcasted_iota(jnp.int32, (tm, tm), 1)
        tri_ref[...] = (c <= r).astype(BF16)

    xn = _rms(x_ref[...], g_ref[...])
    xn_ref[...] = xn.astype(BF16)
    gates, gidx = _route(_dot_f32_3pass(xn, wr_ref[...]) + br_ref[...])
    gate_ref[...] = gates
    lane = lax.broadcasted_iota(jnp.int32, gates.shape, 1).astype(F32)
    onehot = lane == gidx
    incl = jnp.dot(tri_ref[...], onehot.astype(BF16), preferred_element_type=F32)
    rank = jnp.sum(jnp.where(onehot, incl, 0.0), axis=-1, keepdims=True) - 1.0
    meta_ref[...] = jnp.where(lane == 0.0, gidx, jnp.where(lane == 1.0, rank, 0.0))
    cnt_ref[...] = incl[tm - 1:tm, :]


def _moe_pre(x, g, wr, br, l):
    n = x.shape[0]
    tm = MOE_SRC_TILE
    row = lambda w: pl.BlockSpec((tm, w), lambda i: (i, 0))
    return pl.pallas_call(
        _moe_pre_kernel,
        grid=(n // tm,),
        in_specs=[row(D_MODEL), _layer_spec(g, l), _layer_spec(wr, l), _layer_spec(br, l)],
        out_specs=[row(D_MODEL), row(ROUTER_LANES), row(ROUTER_LANES),
                   pl.BlockSpec((None, 1, ROUTER_LANES), lambda i: (i, 0, 0))],
        out_shape=[jax.ShapeDtypeStruct((n, D_MODEL), BF16), jax.ShapeDtypeStruct((n, ROUTER_LANES), F32),
                   jax.ShapeDtypeStruct((n, ROUTER_LANES), F32),
                   jax.ShapeDtypeStruct((n // tm, 1, ROUTER_LANES), F32)],
        scratch_shapes=[pltpu.VMEM((tm, tm), BF16)],
        compiler_params=_cparams("arbitrary"),
        name="moe_pre",
    )(x, g, wr, br)


def _sort_onehot(meta, start_ref, t):
    gid = meta[:, 0:1]
    pos = meta[:, 1:2]
    for grp in range(N_EXPERT_GROUPS):
        pos = pos + jnp.where(gid == float(grp), start_ref[t * N_EXPERT_GROUPS + grp].astype(F32), 0.0)
    col = lax.broadcasted_iota(jnp.int32, (meta.shape[0], MOE_SORTED_ROWS), 1).astype(F32)
    return (col == pos).astype(BF16)


def _run_copies(t, base, start_ref, off_ref, nblk_ref, pairs, sem, to_hbm):
    for grp in range(N_EXPERT_GROUPS):
        k = t * N_EXPERT_GROUPS + grp
        src0 = base + start_ref[k]
        dst0 = off_ref[k]

        def body(b, carry):
            lo = pl.multiple_of(src0 + b * MOE_RUN_ALIGN, MOE_RUN_ALIGN)
            hi = pl.multiple_of(dst0 + b * MOE_RUN_ALIGN, MOE_RUN_ALIGN)
            for buf, arr in pairs:
                a, h = buf.at[pl.ds(lo, MOE_RUN_ALIGN)], arr.at[pl.ds(hi, MOE_RUN_ALIGN)]
                (pltpu.make_async_copy(a, h, sem) if to_hbm else pltpu.make_async_copy(h, a, sem)).start()
            return carry

        lax.fori_loop(0, nblk_ref[k], body, 0)


def _run_wait(t, nblk_ref, pairs, sem, to_hbm):
    total = nblk_ref[t * N_EXPERT_GROUPS]
    for grp in range(1, N_EXPERT_GROUPS):
        total = total + nblk_ref[t * N_EXPERT_GROUPS + grp]
    rows = total * MOE_RUN_ALIGN

    @pl.when(rows > 0)
    def _():
        for buf, arr in pairs:
            a, h = buf.at[pl.ds(0, rows)], arr.at[pl.ds(0, rows)]
            (pltpu.make_async_copy(a, h, sem) if to_hbm else pltpu.make_async_copy(h, a, sem)).wait()


def _moe_pack_kernel(start_ref, off_ref, nblk_ref, end_ref, xn_ref, gate_ref, meta_ref, xs_hbm, gs_hbm,
                     xbuf, gbuf, zx, zg, sems):
    t = pl.program_id(0)
    last = pl.num_programs(0) - 1
    slot = t % 2
    base = pl.multiple_of(slot * MOE_SORTED_ROWS, MOE_SORTED_ROWS)
    pairs = [(xbuf, xs_hbm), (gbuf, gs_hbm)]

    @pl.when(t >= 2)
    def _():
        _run_wait(t - 2, nblk_ref, pairs, sems.at[slot], True)

    onehot = _sort_onehot(meta_ref[...], start_ref, t)
    tn = (((0,), (0,)), ((), ()))
    rows = pl.ds(base, MOE_SORTED_ROWS)
    xbuf[rows, :] = lax.dot_general(onehot, xn_ref[...], tn, preferred_element_type=F32).astype(BF16)
    gates = gate_ref[...]
    gh = gates.astype(BF16)
    gl = (gates - gh.astype(F32)).astype(BF16)
    gbuf[rows, :] = (lax.dot_general(onehot, gh, tn, preferred_element_type=F32)
                     + lax.dot_general(onehot, gl, tn, preferred_element_type=F32))
    _run_copies(t, base, start_ref, off_ref, nblk_ref, pairs, sems.at[slot], True)

    @pl.when(t == last)
    def _():
        @pl.when(t >= 1)
        def _():
            _run_wait(t - 1, nblk_ref, pairs, sems.at[1 - slot], True)

        _run_wait(t, nblk_ref, pairs, sems.at[slot], True)
        zx[...] = jnp.zeros(zx.shape, zx.dtype)
        zg[...] = jnp.zeros(zg.shape, zg.dtype)
        copies = []
        for grp in range(N_EXPERT_GROUPS):
            tail = pl.ds(pl.multiple_of(end_ref[grp], MOE_RUN_ALIGN), MOE_TILE)
            copies += [pltpu.make_async_copy(zx, xs_hbm.at[tail], sems.at[0]),
                       pltpu.make_async_copy(zg, gs_hbm.at[tail], sems.at[0])]
        for cp in copies:
            cp.start()
        for cp in copies:
            cp.wait()


def _moe_pack(xn, gates, meta, tabs, n_rows):
    n = xn.shape[0]
    tm = MOE_SRC_TILE
    row = lambda w: pl.BlockSpec((tm, w), lambda i, *_: (i, 0))
    any_spec = pl.BlockSpec(memory_space=pl.ANY)
    grid_spec = pltpu.PrefetchScalarGridSpec(
        num_scalar_prefetch=4, grid=(n // tm,),
        in_specs=[row(D_MODEL), row(ROUTER_LANES), row(ROUTER_LANES)],
        out_specs=[any_spec, any_spec],
        scratch_shapes=[pltpu.VMEM((2 * MOE_SORTED_ROWS, D_MODEL), BF16),
                        pltpu.VMEM((2 * MOE_SORTED_ROWS, ROUTER_LANES), F32),
                        pltpu.VMEM((MOE_TILE, D_MODEL), BF16), pltpu.VMEM((MOE_TILE, ROUTER_LANES), F32),
                        pltpu.SemaphoreType.DMA((2,))])
    return pl.pallas_call(
        _moe_pack_kernel,
        grid_spec=grid_spec,
        out_shape=[jax.ShapeDtypeStruct((n_rows, D_MODEL), BF16), jax.ShapeDtypeStruct((n_rows, ROUTER_LANES), F32)],
        compiler_params=_cparams("arbitrary"),
        name="moe_pack",
    )(tabs["start"], tabs["off"], tabs["nblk"], tabs["end"], xn, gates, meta)


def _moe_expert_kernel(blk_ref, grp_ref, valid_ref, x_ref, gate_ref, wg_ref, wu_ref, wd_ref, o_ref, wgu_s, wd_s):
    i = pl.program_id(0)
    group = grp_ref[i]

    @pl.when((i == 0) | (group != grp_ref[jnp.maximum(i - 1, 0)]))
    def _():
        for j in range(EXPERTS_PER_STEP):
            wgu_s[j] = jnp.concatenate([wg_ref[j].astype(BF16), wu_ref[j].astype(BF16)], axis=1)
        wd_s[...] = wd_ref[...].astype(BF16)

    @pl.when(valid_ref[i] > 0)
    def _():
        xn = x_ref[...]
        gates = gate_ref[...]
        lane = lax.broadcasted_iota(jnp.int32, gates.shape, 1)
        first = EXPERT_LANE0 + group * EXPERTS_PER_GROUP
        hids = []
        for j in range(EXPERTS_PER_STEP):
            h = jnp.dot(xn, wgu_s[j], preferred_element_type=F32)
            ge = jnp.sum(jnp.where(lane == first + j, gates, 0.0), axis=-1, keepdims=True)
            hids.append((jax.nn.silu(h[:, :EXPERT_FF]) * h[:, EXPERT_FF:] * ge).astype(BF16))
        o_ref[...] = jnp.dot(jnp.concatenate(hids, axis=1), wd_s[...], preferred_element_type=F32)


def _moe_experts(xs, gs, tabs, wg, wu, wd, l, n_tiles):
    es = EXPERTS_PER_STEP
    row = lambda w: pl.BlockSpec((MOE_TILE, w), lambda i, blk, grp, valid: (blk[i], 0))
    grid_spec = pltpu.PrefetchScalarGridSpec(
        num_scalar_prefetch=3, grid=(n_tiles,),
        in_specs=[row(D_MODEL), row(ROUTER_LANES),
                  pl.BlockSpec((None, es, D_MODEL, EXPERT_FF), lambda i, blk, grp, valid: (l, grp[i], 0, 0)),
                  pl.BlockSpec((None, es, D_MODEL, EXPERT_FF), lambda i, blk, grp, valid: (l, grp[i], 0, 0)),
                  pl.BlockSpec((None, es * EXPERT_FF, D_MODEL), lambda i, blk, grp, valid: (l, grp[i], 0))],
        out_specs=row(D_MODEL),
        scratch_shapes=[pltpu.VMEM((es, D_MODEL, 2 * EXPERT_FF), BF16), pltpu.VMEM((es * EXPERT_FF, D_MODEL), BF16)])
    return pl.pallas_call(
        _moe_expert_kernel,
        grid_spec=grid_spec,
        out_shape=jax.ShapeDtypeStruct((xs.shape[0], D_MODEL), F32),
        compiler_params=_cparams("arbitrary"),
        name="hier_moe",
    )(tabs["tile_blk"], tabs["tile_grp"], tabs["tile_valid"], xs, gs, wg, wu, wd)


def _moe_unpack_kernel(n_prompt_tiles, start_ref, off_ref, nblk_ref, x_ref, meta_ref, ys_hbm, *rest):
    ybuf, sems = rest[-2:]
    t = pl.program_id(0)
    slot = t % 2
    base = pl.multiple_of(slot * MOE_SORTED_ROWS, MOE_SORTED_ROWS)
    pairs = [(ybuf, ys_hbm)]

    @pl.when(t == 0)
    def _():
        ybuf[...] = jnp.zeros(ybuf.shape, ybuf.dtype)
        _run_copies(t, base, start_ref, off_ref, nblk_ref, pairs, sems.at[slot], False)

    @pl.when(t + 1 < pl.num_programs(0))
    def _():
        nxt = pl.multiple_of((1 - slot) * MOE_SORTED_ROWS, MOE_SORTED_ROWS)
        _run_copies(t + 1, nxt, start_ref, off_ref, nblk_ref, pairs, sems.at[1 - slot], False)

    onehot = _sort_onehot(meta_ref[...], start_ref, t)
    _run_wait(t, nblk_ref, pairs, sems.at[slot], False)
    y = ybuf[pl.ds(base, MOE_SORTED_ROWS), :]
    yh = y.astype(BF16)
    yl = (y - yh.astype(F32)).astype(BF16)
    res = x_ref[...] + (jnp.dot(onehot, yh, preferred_element_type=F32)
                        + jnp.dot(onehot, yl, preferred_element_type=F32))
    if len(rest) == 3:
        rest[0][...] = res
    else:
        gf_ref, yp_ref, ysm_ref = rest[:3]
        yn = _rms(res, gf_ref[...])

        @pl.when(t < n_prompt_tiles)
        def _():
            yp_ref[...] = yn

        @pl.when(t >= n_prompt_tiles)
        def _():
            ysm_ref[...] = yn


def _moe_unpack(x, meta, ys, tabs, g_final=None, n_prompt=0):
    n = x.shape[0]
    tm = MOE_SRC_TILE
    npt = n_prompt // tm
    row = lambda w: pl.BlockSpec((tm, w), lambda i, *_: (i, 0))
    in_specs = [row(D_MODEL), row(ROUTER_LANES), pl.BlockSpec(memory_space=pl.ANY)]
    operands = [x, meta, ys]
    if g_final is None:
        out_specs = row(D_MODEL)
        out_shape = jax.ShapeDtypeStruct((n, D_MODEL), F32)
    else:
        nst = (n - n_prompt) // tm
        in_specs.append(pl.BlockSpec((1, D_MODEL), lambda i, *_: (0, 0)))
        operands.append(g_final)
        out_specs = [pl.BlockSpec((tm, D_MODEL), lambda i, *_: (jnp.minimum(i, npt - 1), 0)),
                     pl.BlockSpec((tm, D_MODEL), lambda i, *_: (jnp.clip(i - npt, 0, nst - 1), 0))]
        out_shape = [jax.ShapeDtypeStruct((n_prompt, D_MODEL), F32), jax.ShapeDtypeStruct((n - n_prompt, D_MODEL), F32)]
    grid_spec = pltpu.PrefetchScalarGridSpec(
        num_scalar_prefetch=3, grid=(n // tm,),
        in_specs=in_specs, out_specs=out_specs,
        scratch_shapes=[pltpu.VMEM((2 * MOE_SORTED_ROWS, D_MODEL), F32), pltpu.SemaphoreType.DMA((2,))])
    return pl.pallas_call(
        functools.partial(_moe_unpack_kernel, npt),
        grid_spec=grid_spec,
        out_shape=out_shape,
        compiler_params=_cparams("arbitrary"),
        name="moe_unpack",
    )(tabs["start"], tabs["off"], tabs["nblk"], *operands)


def _moe_tables(cnt, n):
    n_src = cnt.shape[0]
    pad = (cnt + MOE_RUN_ALIGN - 1) // MOE_RUN_ALIGN * MOE_RUN_ALIGN
    worst = n + n_src * (MOE_RUN_ALIGN - 1)
    cap = (worst + 2 * MOE_TILE - 1) // MOE_TILE * MOE_TILE
    start = jnp.cumsum(pad, axis=1) - pad
    total = jnp.sum(pad, axis=0)
    base = jnp.arange(N_EXPERT_GROUPS, dtype=jnp.int32) * cap
    off = base[None, :] + jnp.cumsum(pad, axis=0) - pad
    tiles_g = (total + MOE_TILE - 1) // MOE_TILE
    tile_end = jnp.cumsum(tiles_g)
    n_tiles = n // MOE_TILE + N_EXPERT_GROUPS + (n_src * N_EXPERT_GROUPS * MOE_RUN_ALIGN + MOE_TILE - 1) // MOE_TILE
    i = jnp.minimum(jnp.arange(n_tiles, dtype=jnp.int32), jnp.maximum(tile_end[-1] - 1, 0))
    grp = jnp.minimum(jnp.sum((i[:, None] >= tile_end[None, :]).astype(jnp.int32), axis=1), N_EXPERT_GROUPS - 1)
    first_tile = (tile_end - tiles_g)
    blk = jnp.zeros_like(i)
    for g in range(N_EXPERT_GROUPS):
        blk = blk + jnp.where(grp == g, g * (cap // MOE_TILE) + i - first_tile[g], 0)
    i32 = lambda a: a.astype(jnp.int32)
    tabs = dict(start=i32(start.reshape(-1)), off=i32(off.reshape(-1)), nblk=i32((pad // MOE_RUN_ALIGN).reshape(-1)),
                end=i32(base + total), tile_blk=i32(blk), tile_grp=i32(grp),
                tile_valid=i32(jnp.arange(n_tiles) < tile_end[-1]))
    return tabs, N_EXPERT_GROUPS * cap, n_tiles


def _moe(x, g, wr, br, wg, wu, wd, l, g_final=None, n_prompt=0):
    n = x.shape[0]
    xn, gates, meta, cnt = _moe_pre(x, g, wr, br, l)
    tabs, n_rows, n_tiles = _moe_tables(cnt[:, 0, :N_EXPERT_GROUPS].astype(jnp.int32), n)
    xs, gs = _moe_pack(xn, gates, meta, tabs, n_rows)
    ys = _moe_experts(xs, gs, tabs, wg, wu, wd, l, n_tiles)
    return _moe_unpack(x, meta, ys, tabs, g_final, n_prompt)


def _rope_tables(seq, t_len, tm):
    half = HEAD_DIM // 2
    inv = ROPE_THETA ** (-jnp.arange(half, dtype=F32) / half)
    pos_s = PAST_LEN + jnp.arange(t_len)
    pos = jnp.concatenate([jnp.arange(seq), jnp.tile(pos_s, tm // t_len)]).astype(F32)
    ang = pos[:, None] * inv[None, :]
    cos = jnp.tile(jnp.cos(ang), (1, LANES // half))
    sign = jnp.where((jnp.arange(LANES) % HEAD_DIM) < half, -1.0, 1.0).astype(F32)
    sin = jnp.tile(jnp.sin(ang), (1, LANES // half)) * sign[None, :]
    return cos, sin


def kernel(x_prompt, x_sample, cache_win_k, cache_win_v, state_ssm, cache_mem_k, cache_mem_v, mem_prompt, w_in, attn_sink, lam_re, lam_im, log_dt, ssm_b_re, ssm_b_im, ssm_c_re, ssm_c_im, ssm_d, w_glu, b_glu, g_attn_out, g_ssm_out, w_out, g_mix, g_xattn, g_mem, wq_x, wk_x, wv_x, wo_x, g_ffn, w_group, b_group, w_router, b_router, w_gate, w_up, w_down, g_final):
    batch, seq, _ = x_prompt.shape
    dec_batch, t_len, _ = x_sample.shape
    depth = w_in.shape[0]
    win_rows = cache_win_k.shape[2]
    n_p = batch * seq
    n_s = dec_batch * t_len
    tm_wide = 1024 if (n_p + n_s) % 1024 == 0 else 512
    tm = tm_wide
    tm_x = 512
    sample_nb = tm_x // t_len

    n = n_p + n_s
    xp = x_prompt.reshape(n_p, D_MODEL)
    xs = x_sample.reshape(n_s, D_MODEL)
    cos_tab, sin_tab = _rope_tables(seq, t_len, tm)
    mem_flat = mem_prompt.reshape(batch * N_MEM, D_MODEL)
    zero_state = jnp.zeros((batch, 1, 2 * N_STATE), F32)
    vec = lambda a: a.reshape(depth, 1, a.shape[-1])

    w_in_b, w_glu_b, w_out_b = w_in.astype(BF16), w_glu.astype(BF16), w_out.astype(BF16)
    wq_b, wk_b, wv_b, wo_b = (w.astype(BF16) for w in (wq_x, wk_x, wv_x, wo_x))
    wd_r = w_down.reshape(depth, N_EXPERTS * EXPERT_FF, D_MODEL)
    wr = jnp.concatenate([w_group, w_router.transpose(0, 2, 1, 3).reshape(depth, D_MODEL, N_EXPERTS)], axis=-1)
    wr = jnp.pad(wr, ((0, 0), (0, 0), (0, ROUTER_LANES - wr.shape[-1])))
    br = jnp.concatenate([b_group, b_router.reshape(depth, N_EXPERTS)], axis=-1)
    br = jnp.pad(br, ((0, 0), (0, ROUTER_LANES - br.shape[-1]))).reshape(depth, 1, ROUTER_LANES)
    g_mix_r, g_xattn_r, g_mem_r, g_ffn_r = vec(g_mix), vec(g_xattn), vec(g_mem), vec(g_ffn)
    g_a_r, g_s_r, b_glu_r, ssm_d_r = vec(g_attn_out), vec(g_ssm_out), vec(b_glu), vec(ssm_d)
    cache_k = cache_win_k.reshape(depth, dec_batch, win_rows, KV_WIDTH)
    cache_v = cache_win_v.reshape(depth, dec_batch, win_rows, KV_WIDTH)
    cmem_k = cache_mem_k.reshape(depth, dec_batch * N_MEM, D_MODEL)
    cmem_v = cache_mem_v.reshape(depth, dec_batch * N_MEM, D_MODEL)
    state_in = _state_to_tiles(state_ssm)

    outs = {k: [] for k in ("wk_p", "wv_p", "ssm_p", "mk_p", "mv_p", "wk_s", "wv_s", "ssm_s")}
    for l in range(depth):
        q, k, v, u = _in_proj(xp, xs, n, g_mix_r, w_in_b, l, cos_tab, sin_tab, n_p, seq, tm)
        attn_p = _attn_prompt(q, k, v, attn_sink[l], batch, seq)
        attn_s = _attn_sample(q, k, v, cache_k, cache_v, l, attn_sink[l], n_p, dec_batch, t_len, 4)
        tail = lambda a: jnp.stack([a[(b + 1) * seq - WINDOW:(b + 1) * seq] for b in range(batch)])
        outs["wk_p"].append(tail(k).reshape(batch, WINDOW, N_KV_HEADS, HEAD_DIM))
        outs["wv_p"].append(tail(v).reshape(batch, WINDOW, N_KV_HEADS, HEAD_DIM))
        ks = k[n_p:].reshape(dec_batch, t_len, KV_WIDTH)
        vs = v[n_p:].reshape(dec_batch, t_len, KV_WIDTH)
        k_all = jnp.concatenate([cache_k[l], ks], axis=1)[:, -win_rows:]
        v_all = jnp.concatenate([cache_v[l], vs], axis=1)[:, -win_rows:]
        outs["wk_s"].append(k_all.reshape(dec_batch, win_rows, N_KV_HEADS, HEAD_DIM))
        outs["wv_s"].append(v_all.reshape(dec_batch, win_rows, N_KV_HEADS, HEAD_DIM))

        we, tv, coef = _ssm_discretize(lam_re[l], lam_im[l], log_dt[l], ssm_b_re[l], ssm_b_im[l],
                                       ssm_c_re[l], ssm_c_im[l])
        y_p, fin_p = _ssm(u, zero_state, we, tv, ssm_d_r, l, coef, 0, batch, seq, 1, seq)
        y_s, fin_s = _ssm(u, state_in[l], we, tv, ssm_d_r, l, coef, n_p, dec_batch, t_len, dec_batch, t_len)
        outs["ssm_p"].append(_tiles_to_state(fin_p))
        outs["ssm_s"].append(_tiles_to_state(fin_s))
        x = _merge(xp, xs, n, attn_p, attn_s, y_p, y_s, w_glu_b, b_glu_r, g_a_r, g_s_r, w_out_b, l, tm)

        mk_p, mv_p = _mem_kv(mem_flat, g_mem_r, wk_b, wv_b, l, 512)
        outs["mk_p"].append(mk_p.reshape(batch, N_MEM, N_XHEADS, XHEAD_DIM))
        outs["mv_p"].append(mv_p.reshape(batch, N_MEM, N_XHEADS, XHEAD_DIM))
        tiles_per_seq = seq // tm
        x = _xattn(x, g_xattn_r, wq_b, wo_b, l, mk_p, mv_p,
                   pl.BlockSpec((N_MEM, D_MODEL), lambda i: (i // tiles_per_seq, 0)), 0, n_p, 1, tm)
        x = _xattn(x, g_xattn_r, wq_b, wo_b, l, cmem_k, cmem_v,
                   pl.BlockSpec((None, sample_nb * N_MEM, D_MODEL), lambda i: (l, i, 0)), n_p, n_s, sample_nb, tm_x)

        if l + 1 < depth:
            x = _moe(x, g_ffn_r, wr, br, w_gate, w_up, wd_r, l)
            xp = xs = x
        else:
            y_p, y_s = _moe(x, g_ffn_r, wr, br, w_gate, w_up, wd_r, l, g_final.reshape(1, D_MODEL), n_p)

    st = lambda name: jnp.stack(outs[name], axis=0)
    return (y_p.reshape(batch, seq, D_MODEL), y_s.reshape(dec_batch, t_len, D_MODEL),
            st("wk_p"), st("wv_p"), st("ssm_p"), st("mk_p"), st("mv_p"), st("wk_s"), st("wv_s"), st("ssm_s"))
```

```python
import functools
import math

import jax
import jax.numpy as jnp
from jax import lax
from jax.experimental import pallas as pl
from jax.experimental.pallas import tpu as pltpu

F32 = jnp.float32
BF16 = jnp.bfloat16

D_MODEL = 1024
CHUNK = 64
EPS = 1e-6
NEG_INF = -1e30
N_HEADS = 8
N_KV_HEADS = 2
HEAD_DIM = 64
ATTN_WIDTH = N_HEADS * HEAD_DIM
KV_WIDTH = N_KV_HEADS * HEAD_DIM
WINDOW = 128
ROPE_THETA = 10000.0
SSM_WIDTH = D_MODEL - ATTN_WIDTH
SSM_GROUP = 16
N_SSM_GROUPS = SSM_WIDTH // SSM_GROUP
SSM_STATE = 64
N_STATE = N_SSM_GROUPS * SSM_STATE
IN_WIDTH = ATTN_WIDTH + 2 * KV_WIDTH + SSM_WIDTH
N_MEM = 256
N_XHEADS = 4
XHEAD_DIM = D_MODEL // N_XHEADS
N_EXPERT_GROUPS = 4
EXPERTS_PER_GROUP = 8
N_EXPERTS = N_EXPERT_GROUPS * EXPERTS_PER_GROUP
EXPERT_FF = 128
PAST_LEN = 4096

LANES = 128
SUBLANES = 8
VMEM_LIMIT = 56 * 1024 * 1024


def _cparams(*sem):
    return pltpu.CompilerParams(dimension_semantics=sem, vmem_limit_bytes=VMEM_LIMIT)


def _rms(x, g):
    return x * lax.rsqrt(jnp.mean(x * x, axis=-1, keepdims=True) + EPS) * g


def _layer_spec(arr, l):
    shape = arr.shape[1:]
    zeros = (0,) * len(shape)
    return pl.BlockSpec((None,) + shape, lambda *_: (l,) + zeros, pipeline_mode=pl.Buffered(1))


def _rope_pairs(t, cos, sin_signed, first_half):
    swapped = jnp.where(first_half, pltpu.roll(t, LANES - HEAD_DIM // 2, 1), pltpu.roll(t, HEAD_DIM // 2, 1))
    return t * cos + swapped * sin_signed


def _two_source_specs(xp, xs, n_prompt, tm, width):
    npt = n_prompt // tm
    s_off = 0 if xs is xp else npt
    s_last = xs.shape[0] // tm - 1
    pspec = pl.BlockSpec((tm, width), lambda i, *_: (jnp.minimum(i, npt - 1), 0))
    sspec = pl.BlockSpec((tm, width), lambda i, *_: (jnp.clip(i - s_off, npt - s_off, s_last), 0))
    return pspec, sspec


def _in_proj_kernel(n_prompt_tiles, xp_ref, xs_ref, g_ref, w_ref, cos_ref, sin_ref, q_ref, k_ref, v_ref, u_ref):
    x = jnp.where(pl.program_id(0) < n_prompt_tiles, xp_ref[...], xs_ref[...])
    xn = _rms(x, g_ref[...])
    z = jnp.dot(xn.astype(BF16), w_ref[...], preferred_element_type=F32)
    cos = cos_ref[...]
    sin = sin_ref[...]
    lane = lax.broadcasted_iota(jnp.int32, cos.shape, 1)
    first_half = (lane % HEAD_DIM) < (HEAD_DIM // 2)
    scale = 1.0 / math.sqrt(HEAD_DIM)
    for j in range(ATTN_WIDTH // LANES):
        t = z[:, j * LANES:(j + 1) * LANES]
        q_ref[:, j * LANES:(j + 1) * LANES] = (_rope_pairs(t, cos, sin, first_half) * scale).astype(BF16)
    k_ref[...] = _rope_pairs(z[:, ATTN_WIDTH:ATTN_WIDTH + KV_WIDTH], cos, sin, first_half)
    v_ref[...] = z[:, ATTN_WIDTH + KV_WIDTH:ATTN_WIDTH + 2 * KV_WIDTH]
    u_ref[...] = z[:, ATTN_WIDTH + 2 * KV_WIDTH:]


def _in_proj(xp, xs, n, g, w_bf16, l, cos_tab, sin_tab, n_prompt, seq, tm):
    n_prompt_tiles = n_prompt // tm
    tiles_per_seq = seq // tm

    def tab_map(i):
        return (jnp.where(i < n_prompt_tiles, i % tiles_per_seq, tiles_per_seq), 0)

    row = lambda w: pl.BlockSpec((tm, w), lambda i: (i, 0))
    return pl.pallas_call(
        functools.partial(_in_proj_kernel, n_prompt_tiles),
        grid=(n // tm,),
        in_specs=[*_two_source_specs(xp, xs, n_prompt, tm, D_MODEL), _layer_spec(g, l), _layer_spec(w_bf16, l),
                  pl.BlockSpec((tm, LANES), tab_map),
                  pl.BlockSpec((tm, LANES), tab_map)],
        out_specs=[row(ATTN_WIDTH), row(KV_WIDTH), row(KV_WIDTH), row(SSM_WIDTH)],
        out_shape=[jax.ShapeDtypeStruct((n, ATTN_WIDTH), BF16),
                   jax.ShapeDtypeStruct((n, KV_WIDTH), F32),
                   jax.ShapeDtypeStruct((n, KV_WIDTH), F32),
                   jax.ShapeDtypeStruct((n, SSM_WIDTH), F32)],
        compiler_params=_cparams("parallel"),
        name="in_proj",
    )(xp, xs, g, w_bf16, cos_tab, sin_tab)


def _kv_pairs(keys, vals):
    lane = lax.broadcasted_iota(jnp.int32, keys.shape, 1)
    low = lane < HEAD_DIM
    k_sw = pltpu.roll(keys, HEAD_DIM, 1)
    v_sw = pltpu.roll(vals, HEAD_DIM, 1)
    kk = [jnp.where(low, keys, k_sw).astype(BF16), jnp.where(low, k_sw, keys).astype(BF16)]
    vv = [jnp.where(low, vals, v_sw).astype(BF16), jnp.where(low, v_sw, vals).astype(BF16)]
    return kk, vv


def _attend_pairs(q, kk, vv, key_rows, mask_add, sink_ref, o_ref, row0):
    tq = q.shape[0]
    qlane = lax.broadcasted_iota(jnp.int32, (tq, LANES), 1)
    qlow = qlane < HEAD_DIM
    row_top = lax.broadcasted_iota(jnp.int32, (2 * tq, 1), 0) < tq
    zero = jnp.zeros((), BF16)
    for pair in range(N_HEADS // 2):
        kv = pair // (N_HEADS // N_KV_HEADS // 2)
        qp = q[:, pair * LANES:(pair + 1) * LANES]
        qs = jnp.concatenate([jnp.where(qlow, qp, zero), jnp.where(qlow, zero, qp)], axis=0)
        s = lax.dot_general(qs, kk[kv][key_rows, :], (((1,), (1,)), ((), ())), preferred_element_type=F32)
        if mask_add is not None:
            s = s + mask_add
        sink = jnp.where(row_top, sink_ref[2 * pair], sink_ref[2 * pair + 1])
        m = jnp.maximum(jnp.max(s, axis=-1, keepdims=True), sink)
        e = jnp.exp(s - m)
        p = e / (jnp.sum(e, axis=-1, keepdims=True) + jnp.exp(sink - m))
        o = jnp.dot(p.astype(BF16), vv[kv][key_rows, :], preferred_element_type=F32)
        o_ref[row0:row0 + tq, pair * LANES:(pair + 1) * LANES] = jnp.where(qlow, o[:tq], o[tq:])


def _attend_blocks(blocks, sink_ref, o_ref):
    tq = blocks[0][0].shape[0]
    qlane = lax.broadcasted_iota(jnp.int32, (tq, LANES), 1)
    qlow = qlane < HEAD_DIM
    row_top = lax.broadcasted_iota(jnp.int32, (2 * tq, 1), 0) < tq
    zero = jnp.zeros((), BF16)
    scores, sinks = [], []
    for q, kk, vv, key_rows, mask_add, row0 in blocks:
        for pair in range(N_HEADS // 2):
            kv = pair // (N_HEADS // N_KV_HEADS // 2)
            qp = q[:, pair * LANES:(pair + 1) * LANES]
            qs = jnp.concatenate([jnp.where(qlow, qp, zero), jnp.where(qlow, zero, qp)], axis=0)
            s = lax.dot_general(qs, kk[kv][key_rows, :], (((1,), (1,)), ((), ())), preferred_element_type=F32)
            scores.append(s if mask_add is None else s + mask_add)
            sinks.append(jnp.where(row_top, sink_ref[2 * pair], sink_ref[2 * pair + 1]))
    s = jnp.concatenate(scores, axis=0)
    sink = jnp.concatenate(sinks, axis=0)
    m = jnp.maximum(jnp.max(s, axis=-1, keepdims=True), sink)
    e = jnp.exp(s - m)
    p = (e / (jnp.sum(e, axis=-1, keepdims=True) + jnp.exp(sink - m))).astype(BF16)
    piece = 0
    for q, kk, vv, key_rows, mask_add, row0 in blocks:
        for pair in range(N_HEADS // 2):
            kv = pair // (N_HEADS // N_KV_HEADS // 2)
            o = jnp.dot(p[piece * 2 * tq:(piece + 1) * 2 * tq, :], vv[kv][key_rows, :], preferred_element_type=F32)
            o_ref[row0:row0 + tq, pair * LANES:(pair + 1) * LANES] = jnp.where(qlow, o[:tq], o[tq:])
            piece += 1


ATTN_SUB = WINDOW
ATTN_TILE = 8 * ATTN_SUB


def _attn_prompt_kernel(sink_ref, ma_ref, mb_ref, q_ref, kp_ref, kc_ref, vp_ref, vc_ref, o_ref):
    kk, vv = _kv_pairs(jnp.concatenate([kp_ref[...], kc_ref[...]], axis=0),
                       jnp.concatenate([vp_ref[...], vc_ref[...]], axis=0))
    for s in range(ATTN_TILE // ATTN_SUB):
        m_ref = ma_ref if s == 0 else mb_ref
        rows = slice(s * ATTN_SUB, s * ATTN_SUB + 2 * WINDOW)
        _attend_pairs(q_ref[s * ATTN_SUB:(s + 1) * ATTN_SUB, :], kk, vv, rows, m_ref[...], sink_ref, o_ref, s * ATTN_SUB)


def _band_masks():
    r = (jnp.arange(2 * ATTN_SUB) % ATTN_SUB)[:, None] // CHUNK
    c = jnp.arange(2 * WINDOW)[None, :]
    band = (c // CHUNK >= r) & (c // CHUNK <= r + WINDOW // CHUNK)
    masks = jnp.stack([band, band & (c >= WINDOW)])
    return jnp.where(masks, 0.0, NEG_INF).astype(F32)


def _attn_prompt(q, k, v, sink, batch, seq):
    nt = seq // ATTN_TILE
    per_seq = seq // ATTN_SUB
    cur = lambda b, i: (b * nt + i, 0)
    prev = lambda b, i: (b * per_seq + jnp.maximum((ATTN_TILE // ATTN_SUB) * i - 1, 0), 0)
    masks = _band_masks()
    return pl.pallas_call(
        _attn_prompt_kernel,
        grid=(batch, nt),
        in_specs=[pl.BlockSpec(memory_space=pltpu.SMEM),
                  pl.BlockSpec((None, 2 * ATTN_SUB, 2 * WINDOW), lambda b, i: (jnp.where(i == 0, 1, 0), 0, 0)),
                  pl.BlockSpec((None, 2 * ATTN_SUB, 2 * WINDOW), lambda b, i: (0, 0, 0)),
                  pl.BlockSpec((ATTN_TILE, ATTN_WIDTH), cur),
                  pl.BlockSpec((ATTN_SUB, KV_WIDTH), prev),
                  pl.BlockSpec((ATTN_TILE, KV_WIDTH), cur),
                  pl.BlockSpec((ATTN_SUB, KV_WIDTH), prev),
                  pl.BlockSpec((ATTN_TILE, KV_WIDTH), cur)],
        out_specs=pl.BlockSpec((ATTN_TILE, ATTN_WIDTH), cur),
        out_shape=jax.ShapeDtypeStruct((batch * seq, ATTN_WIDTH), F32),
        compiler_params=_cparams("parallel", "parallel"),
        name="attn_prompt",
    )(sink, masks, masks, q, k, k, v, v)


def _attn_sample_kernel(sink_ref, q_ref, ck_ref, cv_ref, k_ref, v_ref, o_ref):
    nb = ck_ref.shape[0]
    t = q_ref.shape[0] // nb
    blocks = []
    for b in range(nb):
        rows = slice(b * t, (b + 1) * t)
        kk, vv = _kv_pairs(jnp.concatenate([ck_ref[b], k_ref[rows, :]], axis=0),
                           jnp.concatenate([cv_ref[b], v_ref[rows, :]], axis=0))
        blocks.append((q_ref[rows, :], kk, vv, slice(None), None, b * t))
    _attend_blocks(blocks, sink_ref, o_ref)


def _attn_sample(q, k, v, cache_k, cache_v, l, sink, n_prompt, dec_batch, t, nb):
    w = cache_k.shape[2]
    rows = nb * t
    base = n_prompt // rows
    tok = lambda width: pl.BlockSpec((rows, width), lambda i: (base + i, 0))
    cache = pl.BlockSpec((None, nb, w, KV_WIDTH), lambda i: (l, i, 0, 0))
    return pl.pallas_call(
        _attn_sample_kernel,
        grid=(dec_batch // nb,),
        in_specs=[pl.BlockSpec(memory_space=pltpu.SMEM),
                  tok(ATTN_WIDTH), cache, cache, tok(KV_WIDTH), tok(KV_WIDTH)],
        out_specs=pl.BlockSpec((rows, ATTN_WIDTH), lambda i: (i, 0)),
        out_shape=jax.ShapeDtypeStruct((dec_batch * t, ATTN_WIDTH), F32),
        compiler_params=_cparams("parallel"),
        name="attn_sample",
    )(sink, q, cache_k, cache_v, k, v)


SSM_BLOCK = SUBLANES


def _cmul(ar, ai, br, bi):
    return ar * br - ai * bi, ar * bi + ai * br


def _ssm_disc_kernel(lre_ref, lim_ref, dt_ref, bre_ref, bim_ref, cre_ref, cim_ref,
                     we_ref, tv_ref, coef_ref, vt_ref, wb_ref):
    we_ref[...] = jnp.zeros(we_ref.shape, we_ref.dtype)
    vt_ref[...] = jnp.zeros(vt_ref.shape, vt_ref.dtype)
    wb_ref[...] = jnp.zeros(wb_ref.shape, wb_ref.dtype)
    lane = lax.broadcasted_iota(jnp.int32, (SSM_GROUP, LANES), 1)
    half = [lane < SSM_STATE, lane >= SSM_STATE]
    row8 = lax.broadcasted_iota(jnp.int32, (SUBLANES, LANES), 0)
    for q in range(STATE_TILES):
        lre = lre_ref[q]
        lim = lim_ref[q]
        dt = dt_ref[q]
        mag = jnp.exp(lre * dt)
        ang = lim * dt
        lbr = mag * jnp.cos(ang)
        lbi = mag * jnp.sin(ang)
        nr, ni = lbr - 1.0, lbi
        den = lre * lre + lim * lim
        fr = (nr * lre + ni * lim) / den
        fi = (ni * lre - nr * lim) / den
        bbr, bbi = _cmul(fr, fi, bre_ref[q], bim_ref[q])
        cr, ci = cre_ref[q], cim_ref[q]
        pw = [(jnp.ones_like(lbr), jnp.zeros_like(lbr))]
        for _ in range(SSM_BLOCK):
            pw.append(_cmul(pw[-1][0], pw[-1][1], lbr, lbi))
        re_l = slice(q * 2 * LANES, q * 2 * LANES + LANES)
        im_l = slice(q * 2 * LANES + LANES, (q + 1) * 2 * LANES)
        for h in range(2):
            g = 2 * q + h
            grow = lambda blk: slice(blk * LANES + g * SSM_GROUP, blk * LANES + (g + 1) * SSM_GROUP)
            for j in range(SSM_BLOCK):
                wr, wi = _cmul(pw[SSM_BLOCK - 1 - j][0], pw[SSM_BLOCK - 1 - j][1], bbr, bbi)
                we_ref[grow(j), re_l] = jnp.where(half[h], wr, 0.0).astype(we_ref.dtype)
                we_ref[grow(j), im_l] = jnp.where(half[h], wi, 0.0).astype(we_ref.dtype)
            for d in range(SSM_BLOCK + 1):
                xr, xi = _cmul(cr, ci, pw[d][0], pw[d][1])
                vt_ref[grow(d), re_l] = jnp.where(half[h], xr, 0.0)
                vt_ref[grow(d), im_l] = jnp.where(half[h], -xi, 0.0)
            wb_ref[grow(0), re_l] = jnp.where(half[h], bbr, 0.0)
            wb_ref[grow(0), im_l] = jnp.where(half[h], bbi, 0.0)
        l8 = [pw[SSM_BLOCK]]
        for _ in range(SUBLANES - 1):
            l8.append(_cmul(l8[-1][0], l8[-1][1], pw[SSM_BLOCK][0], pw[SSM_BLOCK][1]))
        co = slice(q * LANES, (q + 1) * LANES)
        for kind, sh in enumerate((1, 2, 4)):
            for a in range(2):
                coef_ref[kind, a, :, co] = jnp.where(row8 >= sh, l8[sh - 1][a], 0.0)
        for a in range(2):
            tab = jnp.zeros((SUBLANES, LANES), F32)
            for k in range(SUBLANES):
                tab = jnp.where(row8 == k, l8[k][a], tab)
            coef_ref[3, a, :, co] = tab
    t0 = lax.dot_general(wb_ref[...], vt_ref[0:BLOCK_LANES, :], (((1,), (1,)), ((), ())),
                         preferred_element_type=F32, precision=lax.Precision.HIGHEST)
    for j in range(SSM_BLOCK):
        if j:
            tv_ref[j * LANES:(j + 1) * LANES, 0:j * LANES] = jnp.zeros((LANES, j * LANES), tv_ref.dtype)
        tv_ref[j * LANES:(j + 1) * LANES, j * LANES:] = t0[:, 0:BLOCK_LANES - j * LANES].astype(tv_ref.dtype)
    tv_ref[BLOCK_LANES:, :] = jnp.transpose(vt_ref[LANES:, :]).astype(tv_ref.dtype)


def _ssm_discretize(lam_re, lam_im, log_dt, b_re, b_im, c_re, c_im):
    g, p = lam_re.shape
    npair = g // 2
    pair = lambda a: a.reshape(npair, 1, 2 * p)
    rows = lambda a: a.reshape(npair, 2, SSM_GROUP, p).transpose(0, 2, 1, 3).reshape(npair, SSM_GROUP, 2 * p)
    dt = jnp.repeat(jnp.exp(log_dt), p).reshape(npair, 1, 2 * p)
    vec = pl.BlockSpec((STATE_TILES, 1, LANES), lambda m: (m, 0, 0))
    mat = pl.BlockSpec((STATE_TILES, SSM_GROUP, LANES), lambda m: (m, 0, 0))
    return pl.pallas_call(
        _ssm_disc_kernel,
        grid=(N_CH_BLOCKS,),
        in_specs=[vec, vec, vec, mat, mat, mat, mat],
        out_specs=[pl.BlockSpec((None, BLOCK_LANES, STATE_LANES), lambda m: (m, 0, 0)),
                   pl.BlockSpec((None, BLOCK_LANES + STATE_LANES, BLOCK_LANES), lambda m: (m, 0, 0)),
                   pl.BlockSpec((None, 4, 2, SUBLANES, STATE_LANES // 2), lambda m: (m, 0, 0, 0, 0))],
        out_shape=[jax.ShapeDtypeStruct((N_CH_BLOCKS, BLOCK_LANES, STATE_LANES), BF16),
                   jax.ShapeDtypeStruct((N_CH_BLOCKS, BLOCK_LANES + STATE_LANES, BLOCK_LANES), BF16),
                   jax.ShapeDtypeStruct((N_CH_BLOCKS, 4, 2, SUBLANES, STATE_LANES // 2), F32)],
        scratch_shapes=[pltpu.VMEM(((SSM_BLOCK + 1) * LANES, STATE_LANES), F32),
                        pltpu.VMEM((LANES, STATE_LANES), F32)],
        compiler_params=_cparams("parallel"),
        name="ssm_discretize",
    )(pair(lam_re), pair(lam_im), dt, rows(b_re.transpose(0, 2, 1)), rows(b_im.transpose(0, 2, 1)),
      rows(c_re), rows(c_im))


GROUPS_PER_CH_BLOCK = LANES // SSM_GROUP
N_CH_BLOCKS = SSM_WIDTH // LANES
STATE_LANES = 2 * GROUPS_PER_CH_BLOCK * SSM_STATE
STATE_TILES = STATE_LANES // (2 * LANES)
BLOCK_LANES = SSM_BLOCK * LANES


def _ssm_kernel(nb, u_ref, s0_ref, we_ref, tv_ref, d_ref, coef_ref, y_ref, fin_ref, st_ref, sprev_ref, ucat_ref):
    t_idx = pl.program_id(2)
    r = st_ref.shape[1] - SUBLANES
    rows = nb * r

    @pl.when(t_idx == 0)
    def _():
        for b in range(nb):
            st_ref[b, 0:SUBLANES, :] = jnp.broadcast_to(s0_ref[b], (SUBLANES, STATE_LANES))

    for j in range(SSM_BLOCK):
        ucat_ref[:, j * LANES:(j + 1) * LANES] = u_ref[pl.ds(j, rows, stride=SSM_BLOCK), :].astype(BF16)
    e = jnp.dot(ucat_ref[...], we_ref[...], preferred_element_type=F32)
    st_ref[:, SUBLANES:, :] = e.reshape(nb, r, STATE_LANES)

    first_row = lax.broadcasted_iota(jnp.int32, (SUBLANES, LANES), 0) == 0

    def group(rg, carry):
        r0 = pl.multiple_of(SUBLANES + rg * SUBLANES, SUBLANES)
        rp = pl.multiple_of(rg * SUBLANES, SUBLANES)
        for b in range(nb):
            for q in range(STATE_TILES):
                re_sl = pl.ds(q * 2 * LANES, LANES)
                im_sl = pl.ds(q * 2 * LANES + LANES, LANES)
                co = pl.ds(q * LANES, LANES)
                xr = st_ref[b, pl.ds(r0, SUBLANES), re_sl]
                xi = st_ref[b, pl.ds(r0, SUBLANES), im_sl]
                for step, sh in enumerate((1, 2, 4)):
                    ar, ai = _cmul(coef_ref[step, 0, :, co], coef_ref[step, 1, :, co],
                                   pltpu.roll(xr, sh, 0), pltpu.roll(xi, sh, 0))
                    xr = xr + ar
                    xi = xi + ai
                pr = jnp.broadcast_to(st_ref[b, pl.ds(rp, SUBLANES), re_sl][SUBLANES - 1:, :], (SUBLANES, LANES))
                pi = jnp.broadcast_to(st_ref[b, pl.ds(rp, SUBLANES), im_sl][SUBLANES - 1:, :], (SUBLANES, LANES))
                ar, ai = _cmul(coef_ref[3, 0, :, co], coef_ref[3, 1, :, co], pr, pi)
                xr = xr + ar
                xi = xi + ai
                st_ref[b, pl.ds(r0, SUBLANES), re_sl] = xr
                st_ref[b, pl.ds(r0, SUBLANES), im_sl] = xi
                out_rows = pl.ds(pl.multiple_of(b * r + rg * SUBLANES, SUBLANES), SUBLANES)
                sprev_ref[out_rows, re_sl] = jnp.where(first_row, pr, pltpu.roll(xr, 1, 0))
                sprev_ref[out_rows, im_sl] = jnp.where(first_row, pi, pltpu.roll(xi, 1, 0))
        return carry

    lax.fori_loop(0, r // SUBLANES, group, 0)

    lhs = jnp.concatenate([ucat_ref[...], sprev_ref[...].astype(BF16)], axis=1)
    ycat = jnp.dot(lhs, tv_ref[...], preferred_element_type=F32)
    d = d_ref[...]
    for t in range(SSM_BLOCK):
        tok = pl.ds(t, rows, stride=SSM_BLOCK)
        y_ref[tok, :] = ycat[:, t * LANES:(t + 1) * LANES] + d * u_ref[tok, :]

    for b in range(nb):
        tail = st_ref[b, r:r + SUBLANES, :]
        st_ref[b, 0:SUBLANES, :] = tail
        fin_ref[b] = tail[SUBLANES - 1:SUBLANES, :]


def _ssm(u, s0, we, tv, d, l, coef, row_base, n_seq, seq_len, nb, tt):
    rows = nb * tt
    r = tt // SSM_BLOCK
    nt = seq_len // tt
    base = row_base // rows
    return pl.pallas_call(
        functools.partial(_ssm_kernel, nb),
        grid=(N_CH_BLOCKS, n_seq // nb, nt),
        in_specs=[pl.BlockSpec((rows, LANES), lambda m, i, j: (base + i * nt + j, m)),
                  pl.BlockSpec((nb, 1, STATE_LANES), lambda m, i, j: (i, 0, m)),
                  pl.BlockSpec((None, BLOCK_LANES, STATE_LANES), lambda m, i, j: (m, 0, 0)),
                  pl.BlockSpec((None, BLOCK_LANES + STATE_LANES, BLOCK_LANES), lambda m, i, j: (m, 0, 0)),
                  pl.BlockSpec((None, 1, LANES), lambda m, i, j: (l, 0, m)),
                  pl.BlockSpec((None, 4, 2, SUBLANES, STATE_LANES // 2), lambda m, i, j: (m, 0, 0, 0, 0))],
        out_specs=[pl.BlockSpec((rows, LANES), lambda m, i, j: (i * nt + j, m)),
                   pl.BlockSpec((nb, 1, STATE_LANES), lambda m, i, j: (i, 0, m))],
        out_shape=[jax.ShapeDtypeStruct((n_seq * seq_len, SSM_WIDTH), F32),
                   jax.ShapeDtypeStruct((n_seq, 1, N_CH_BLOCKS * STATE_LANES), F32)],
        scratch_shapes=[pltpu.VMEM((nb, SUBLANES + r, STATE_LANES), F32),
                        pltpu.VMEM((nb * r, STATE_LANES), F32),
                        pltpu.VMEM((nb * r, BLOCK_LANES), BF16)],
        compiler_params=_cparams("parallel", "parallel", "arbitrary"),
        name="ssm_scan",
    )(u, s0, we, tv, d, coef)


def _state_to_tiles(s):
    lead = s.shape[:-3]
    t = s.reshape(lead + (N_CH_BLOCKS, STATE_TILES, 2, SSM_STATE, 2))
    t = jnp.moveaxis(t, -1, -3)
    return t.reshape(lead + (1, N_CH_BLOCKS * STATE_LANES))


def _tiles_to_state(f):
    b = f.shape[0]
    t = f.reshape(b, N_CH_BLOCKS, STATE_TILES, 2, 2, SSM_STATE)
    t = jnp.moveaxis(t, 3, -1)
    return t.reshape(b, N_SSM_GROUPS, SSM_STATE, 2)


def _merge_kernel(n_prompt_tiles, xp_ref, xs_ref, ap_ref, as_ref, yp_ref, ys_ref, wglu_ref, bglu_ref, ga_ref, gs_ref,
                  wout_ref, o_ref):
    is_prompt = pl.program_id(0) < n_prompt_tiles
    attn = jnp.where(is_prompt, ap_ref[...], as_ref[...])
    g = jax.nn.gelu(jnp.where(is_prompt, yp_ref[...], ys_ref[...]))
    glu = g * jax.nn.sigmoid(jnp.dot(g.astype(BF16), wglu_ref[...], preferred_element_type=F32) + bglu_ref[...])
    na = _rms(attn, ga_ref[...]).astype(BF16)
    ns = _rms(glu, gs_ref[...]).astype(BF16)
    o = jnp.dot(na, wout_ref[0:ATTN_WIDTH, :], preferred_element_type=F32)
    o = o + jnp.dot(ns, wout_ref[ATTN_WIDTH:, :], preferred_element_type=F32)
    o_ref[...] = jnp.where(is_prompt, xp_ref[...], xs_ref[...]) + o


def _merge(xp, xs, n, attn_p, attn_s, y_p, y_s, wglu, bglu, ga, gs, wout, l, tm):
    npt = attn_p.shape[0] // tm
    nst = attn_s.shape[0] // tm
    row = lambda w: pl.BlockSpec((tm, w), lambda i: (i, 0))
    prow = lambda w: pl.BlockSpec((tm, w), lambda i: (jnp.minimum(i, npt - 1), 0))
    srow = lambda w: pl.BlockSpec((tm, w), lambda i: (jnp.clip(i - npt, 0, nst - 1), 0))
    return pl.pallas_call(
        functools.partial(_merge_kernel, npt),
        grid=(n // tm,),
        in_specs=[*_two_source_specs(xp, xs, attn_p.shape[0], tm, D_MODEL),
                  prow(ATTN_WIDTH), srow(ATTN_WIDTH), prow(SSM_WIDTH), srow(SSM_WIDTH),
                  _layer_spec(wglu, l), _layer_spec(bglu, l), _layer_spec(ga, l), _layer_spec(gs, l),
                  _layer_spec(wout, l)],
        out_specs=row(D_MODEL),
        out_shape=jax.ShapeDtypeStruct((n, D_MODEL), F32),
        compiler_params=_cparams("parallel"),
        name="merge_heads",
    )(xp, xs, attn_p, attn_s, y_p, y_s, wglu, bglu, ga, gs, wout)


def _mem_kv_kernel(m_ref, g_ref, wk_ref, wv_ref, k_ref, v_ref):
    mn = _rms(m_ref[...], g_ref[...]).astype(BF16)
    k_ref[...] = jnp.dot(mn, wk_ref[...], preferred_element_type=F32)
    v_ref[...] = jnp.dot(mn, wv_ref[...], preferred_element_type=F32)


def _mem_kv(mem, g, wk, wv, l, tm):
    n = mem.shape[0]
    row = pl.BlockSpec((tm, D_MODEL), lambda i: (i, 0))
    return pl.pallas_call(
        _mem_kv_kernel,
        grid=(n // tm,),
        in_specs=[row, _layer_spec(g, l), _layer_spec(wk, l), _layer_spec(wv, l)],
        out_specs=[row, row],
        out_shape=[jax.ShapeDtypeStruct((n, D_MODEL), F32)] * 2,
        compiler_params=_cparams("parallel"),
        name="mem_kv",
    )(mem, g, wk, wv)


def _xattn_kernel(nb, x_ref, g_ref, wq_ref, wo_ref, mk_ref, mv_ref, o_ref, att_ref):
    t = x_ref.shape[0] // nb
    x = x_ref[...]
    hn = _rms(x, g_ref[...]).astype(BF16)
    q = jnp.dot(hn, wq_ref[...], preferred_element_type=F32) * (1.0 / math.sqrt(XHEAD_DIM))
    q = q.astype(BF16)
    heads = [slice(h * XHEAD_DIM, (h + 1) * XHEAD_DIM) for h in range(N_XHEADS)]
    scores = []
    for b in range(nb):
        mk = mk_ref[b * N_MEM:(b + 1) * N_MEM, :].astype(BF16)
        for sl in heads:
            scores.append(lax.dot_general(q[b * t:(b + 1) * t, sl], mk[:, sl], (((1,), (1,)), ((), ())),
                                          preferred_element_type=F32))
    s = jnp.concatenate(scores, axis=0)
    e = jnp.exp(s - jnp.max(s, axis=-1, keepdims=True))
    p = (e / jnp.sum(e, axis=-1, keepdims=True)).astype(BF16)
    for b in range(nb):
        mv = mv_ref[b * N_MEM:(b + 1) * N_MEM, :].astype(BF16)
        for h, sl in enumerate(heads):
            r0 = (b * N_XHEADS + h) * t
            att_ref[b * t:(b + 1) * t, sl] = jnp.dot(p[r0:r0 + t, :], mv[:, sl], preferred_element_type=F32)
    o = jnp.dot(att_ref[...].astype(BF16), wo_ref[...], preferred_element_type=F32)
    o_ref[...] = x + o


def _xattn(x, g, wq, wo, l, mk, mv, mem_spec, row_base, n_rows, nb, tm):
    base = row_base // tm
    xspec = pl.BlockSpec((tm, D_MODEL), lambda i: (base + i, 0))
    return pl.pallas_call(
        functools.partial(_xattn_kernel, nb),
        grid=(n_rows // tm,),
        in_specs=[xspec, _layer_spec(g, l), _layer_spec(wq, l), _layer_spec(wo, l), mem_spec, mem_spec],
        out_specs=xspec,
        out_shape=jax.ShapeDtypeStruct(x.shape, F32),
        scratch_shapes=[pltpu.VMEM((tm, D_MODEL), F32)],
        input_output_aliases={0: 0},
        compiler_params=_cparams("parallel"),
        name="cross_attn",
    )(x, g, wq, wo, mk, mv)


ROUTER_LANES = LANES
EXPERT_LANE0 = N_EXPERT_GROUPS
EXPERTS_PER_STEP = EXPERTS_PER_GROUP
MOE_SRC_TILE = 512
MOE_RUN_ALIGN = 16
MOE_SORTED_ROWS = 640
MOE_TILE = 1024


def _dot_f32_3pass(x, w):
    xh = x.astype(BF16)
    xl = (x - xh.astype(F32)).astype(BF16)
    wh = w.astype(BF16)
    wl = (w - wh.astype(F32)).astype(BF16)
    dot = lambda a, b: jnp.dot(a, b, preferred_element_type=F32)
    return dot(xh, wh) + (dot(xl, wh) + dot(xh, wl))


def _route(logits):
    lane_i = lax.broadcasted_iota(jnp.int32, logits.shape, 1)
    lane = lane_i.astype(F32)
    neg = jnp.float32(-jnp.inf)
    is_g = lane_i < N_EXPERT_GROUPS
    gl = jnp.where(is_g, logits, neg)
    gmax = jnp.max(gl, axis=-1, keepdims=True)
    gidx = jnp.min(jnp.where(gl == gmax, lane, float(ROUTER_LANES)), axis=-1, keepdims=True)
    g_w = 1.0 / jnp.sum(jnp.where(is_g, jnp.exp(gl - gmax), 0.0), axis=-1, keepdims=True)
    first = EXPERT_LANE0 + gidx * EXPERTS_PER_GROUP
    sel = (lane >= first) & (lane < first + EXPERTS_PER_GROUP)
    el = jnp.where(sel, logits, neg)
    m1 = jnp.max(el, axis=-1, keepdims=True)
    i1 = jnp.min(jnp.where(el == m1, lane, float(ROUTER_LANES)), axis=-1, keepdims=True)
    el2 = jnp.where(lane == i1, neg, el)
    m2 = jnp.max(el2, axis=-1, keepdims=True)
    i2 = jnp.min(jnp.where(el2 == m2, lane, float(ROUTER_LANES)), axis=-1, keepdims=True)
    r = jnp.exp(m2 - m1)
    w1 = g_w / (1.0 + r)
    w2 = w1 * r
    return jnp.where(lane == i1, w1, jnp.where(lane == i2, w2, 0.0)), gidx


def _moe_pre_kernel(x_ref, g_ref, wr_ref, br_ref, xn_ref, gate_ref, meta_ref, cnt_ref, tri_ref):
    tm = x_ref.shape[0]

    @pl.when(pl.program_id(0) == 0)
    def _():
        r = lax.broadcasted_iota(jnp.int32, (tm, tm), 0)
        c = lax.broadcasted_iota(jnp.int32, (tm, tm), 1)
        tri_ref[...] = (c <= r).astype(BF16)

    xn = _rms(x_ref[...], g_ref[...])
    xn_ref[...] = xn.astype(BF16)
    gates, gidx = _route(_dot_f32_3pass(xn, wr_ref[...]) + br_ref[...])
    gate_ref[...] = gates
    lane = lax.broadcasted_iota(jnp.int32, gates.shape, 1).astype(F32)
    onehot = lane == gidx
    incl = jnp.dot(tri_ref[...], onehot.astype(BF16), preferred_element_type=F32)
    rank = jnp.sum(jnp.where(onehot, incl, 0.0), axis=-1, keepdims=True) - 1.0
    meta_ref[...] = jnp.where(lane == 0.0, gidx, jnp.where(lane == 1.0, rank, 0.0))
    cnt_ref[...] = incl[tm - 1:tm, :]


def _moe_pre(x, g, wr, br, l):
    n = x.shape[0]
    tm = MOE_SRC_TILE
    row = lambda w: pl.BlockSpec((tm, w), lambda i: (i, 0))
    return pl.pallas_call(
        _moe_pre_kernel,
        grid=(n // tm,),
        in_specs=[row(D_MODEL), _layer_spec(g, l), _layer_spec(wr, l), _layer_spec(br, l)],
        out_specs=[row(D_MODEL), row(ROUTER_LANES), row(ROUTER_LANES),
                   pl.BlockSpec((None, 1, ROUTER_LANES), lambda i: (i, 0, 0))],
        out_shape=[jax.ShapeDtypeStruct((n, D_MODEL), BF16), jax.ShapeDtypeStruct((n, ROUTER_LANES), F32),
                   jax.ShapeDtypeStruct((n, ROUTER_LANES), F32),
                   jax.ShapeDtypeStruct((n // tm, 1, ROUTER_LANES), F32)],
        scratch_shapes=[pltpu.VMEM((tm, tm), BF16)],
        compiler_params=_cparams("arbitrary"),
        name="moe_pre",
    )(x, g, wr, br)


def _sort_onehot(meta, start_ref, t):
    gid = meta[:, 0:1]
    pos = meta[:, 1:2]
    for grp in range(N_EXPERT_GROUPS):
        pos = pos + jnp.where(gid == float(grp), start_ref[t * N_EXPERT_GROUPS + grp].astype(F32), 0.0)
    col = lax.broadcasted_iota(jnp.int32, (meta.shape[0], MOE_SORTED_ROWS), 1).astype(F32)
    return (col == pos).astype(BF16)


def _run_copies(t, base, start_ref, off_ref, nblk_ref, pairs, sem, to_hbm):
    for grp in range(N_EXPERT_GROUPS):
        k = t * N_EXPERT_GROUPS + grp
        src0 = base + start_ref[k]
        dst0 = off_ref[k]

        def body(b, carry):
            lo = pl.multiple_of(src0 + b * MOE_RUN_ALIGN, MOE_RUN_ALIGN)
            hi = pl.multiple_of(dst0 + b * MOE_RUN_ALIGN, MOE_RUN_ALIGN)
            for buf, arr in pairs:
                a, h = buf.at[pl.ds(lo, MOE_RUN_ALIGN)], arr.at[pl.ds(hi, MOE_RUN_ALIGN)]
                (pltpu.make_async_copy(a, h, sem) if to_hbm else pltpu.make_async_copy(h, a, sem)).start()
            return carry

        lax.fori_loop(0, nblk_ref[k], body, 0)


def _run_wait(t, nblk_ref, pairs, sem, to_hbm):
    total = nblk_ref[t * N_EXPERT_GROUPS]
    for grp in range(1, N_EXPERT_GROUPS):
        total = total + nblk_ref[t * N_EXPERT_GROUPS + grp]
    rows = total * MOE_RUN_ALIGN

    @pl.when(rows > 0)
    def _():
        for buf, arr in pairs:
            a, h = buf.at[pl.ds(0, rows)], arr.at[pl.ds(0, rows)]
            (pltpu.make_async_copy(a, h, sem) if to_hbm else pltpu.make_async_copy(h, a, sem)).wait()


def _moe_pack_kernel(start_ref, off_ref, nblk_ref, end_ref, xn_ref, gate_ref, meta_ref, xs_hbm, gs_hbm,
                     xbuf, gbuf, zx, zg, sems):
    t = pl.program_id(0)
    last = pl.num_programs(0) - 1
    slot = t % 2
    base = pl.multiple_of(slot * MOE_SORTED_ROWS, MOE_SORTED_ROWS)
    pairs = [(xbuf, xs_hbm), (gbuf, gs_hbm)]

    @pl.when(t >= 2)
    def _():
        _run_wait(t - 2, nblk_ref, pairs, sems.at[slot], True)

    onehot = _sort_onehot(meta_ref[...], start_ref, t)
    tn = (((0,), (0,)), ((), ()))
    rows = pl.ds(base, MOE_SORTED_ROWS)
    xbuf[rows, :] = lax.dot_general(onehot, xn_ref[...], tn, preferred_element_type=F32).astype(BF16)
    gates = gate_ref[...]
    gh = gates.astype(BF16)
    gl = (gates - gh.astype(F32)).astype(BF16)
    gbuf[rows, :] = (lax.dot_general(onehot, gh, tn, preferred_element_type=F32)
                     + lax.dot_general(onehot, gl, tn, preferred_element_type=F32))
    _run_copies(t, base, start_ref, off_ref, nblk_ref, pairs, sems.at[slot], True)

    @pl.when(t == last)
    def _():
        @pl.when(t >= 1)
        def _():
            _run_wait(t - 1, nblk_ref, pairs, sems.at[1 - slot], True)

        _run_wait(t, nblk_ref, pairs, sems.at[slot], True)
        zx[...] = jnp.zeros(zx.shape, zx.dtype)
        zg[...] = jnp.zeros(zg.shape, zg.dtype)
        copies = []
        for grp in range(N_EXPERT_GROUPS):
            tail = pl.ds(pl.multiple_of(end_ref[grp], MOE_RUN_ALIGN), MOE_TILE)
            copies += [pltpu.make_async_copy(zx, xs_hbm.at[tail], sems.at[0]),
                       pltpu.make_async_copy(zg, gs_hbm.at[tail], sems.at[0])]
        for cp in copies:
            cp.start()
        for cp in copies:
            cp.wait()


def _moe_pack(xn, gates, meta, tabs, n_rows):
    n = xn.shape[0]
    tm = MOE_SRC_TILE
    row = lambda w: pl.BlockSpec((tm, w), lambda i, *_: (i, 0))
    any_spec = pl.BlockSpec(memory_space=pl.ANY)
    grid_spec = pltpu.PrefetchScalarGridSpec(
        num_scalar_prefetch=4, grid=(n // tm,),
        in_specs=[row(D_MODEL), row(ROUTER_LANES), row(ROUTER_LANES)],
        out_specs=[any_spec, any_spec],
        scratch_shapes=[pltpu.VMEM((2 * MOE_SORTED_ROWS, D_MODEL), BF16),
                        pltpu.VMEM((2 * MOE_SORTED_ROWS, ROUTER_LANES), F32),
                        pltpu.VMEM((MOE_TILE, D_MODEL), BF16), pltpu.VMEM((MOE_TILE, ROUTER_LANES), F32),
                        pltpu.SemaphoreType.DMA((2,))])
    return pl.pallas_call(
        _moe_pack_kernel,
        grid_spec=grid_spec,
        out_shape=[jax.ShapeDtypeStruct((n_rows, D_MODEL), BF16), jax.ShapeDtypeStruct((n_rows, ROUTER_LANES), F32)],
        compiler_params=_cparams("arbitrary"),
        name="moe_pack",
    )(tabs["start"], tabs["off"], tabs["nblk"], tabs["end"], xn, gates, meta)


def _moe_expert_kernel(blk_ref, grp_ref, valid_ref, x_ref, gate_ref, wg_ref, wu_ref, wd_ref, o_ref, wgu_s, wd_s):
    i = pl.program_id(0)
    group = grp_ref[i]

    @pl.when((i == 0) | (group != grp_ref[jnp.maximum(i - 1, 0)]))
    def _():
        for j in range(EXPERTS_PER_STEP):
            wgu_s[j] = jnp.concatenate([wg_ref[j].astype(BF16), wu_ref[j].astype(BF16)], axis=1)
        wd_s[...] = wd_ref[...].astype(BF16)

    @pl.when(valid_ref[i] > 0)
    def _():
        xn = x_ref[...]
        gates = gate_ref[...]
        lane = lax.broadcasted_iota(jnp.int32, gates.shape, 1)
        first = EXPERT_LANE0 + group * EXPERTS_PER_GROUP
        hids = []
        for j in range(EXPERTS_PER_STEP):
            h = jnp.dot(xn, wgu_s[j], preferred_element_type=F32)
            ge = jnp.sum(jnp.where(lane == first + j, gates, 0.0), axis=-1, keepdims=True)
            hids.append((jax.nn.silu(h[:, :EXPERT_FF]) * h[:, EXPERT_FF:] * ge).astype(BF16))
        o_ref[...] = jnp.dot(jnp.concatenate(hids, axis=1), wd_s[...], preferred_element_type=F32)


def _moe_experts(xs, gs, tabs, wg, wu, wd, l, n_tiles):
    es = EXPERTS_PER_STEP
    row = lambda w: pl.BlockSpec((MOE_TILE, w), lambda i, blk, grp, valid: (blk[i], 0))
    grid_spec = pltpu.PrefetchScalarGridSpec(
        num_scalar_prefetch=3, grid=(n_tiles,),
        in_specs=[row(D_MODEL), row(ROUTER_LANES),
                  pl.BlockSpec((None, es, D_MODEL, EXPERT_FF), lambda i, blk, grp, valid: (l, grp[i], 0, 0)),
                  pl.BlockSpec((None, es, D_MODEL, EXPERT_FF), lambda i, blk, grp, valid: (l, grp[i], 0, 0)),
                  pl.BlockSpec((None, es * EXPERT_FF, D_MODEL), lambda i, blk, grp, valid: (l, grp[i], 0))],
        out_specs=row(D_MODEL),
        scratch_shapes=[pltpu.VMEM((es, D_MODEL, 2 * EXPERT_FF), BF16), pltpu.VMEM((es * EXPERT_FF, D_MODEL), BF16)])
    return pl.pallas_call(
        _moe_expert_kernel,
        grid_spec=grid_spec,
        out_shape=jax.ShapeDtypeStruct((xs.shape[0], D_MODEL), F32),
        compiler_params=_cparams("arbitrary"),
        name="hier_moe",
    )(tabs["tile_blk"], tabs["tile_grp"], tabs["tile_valid"], xs, gs, wg, wu, wd)


def _moe_unpack_kernel(n_prompt_tiles, start_ref, off_ref, nblk_ref, x_ref, meta_ref, ys_hbm, *rest):
    ybuf, sems = rest[-2:]
    t = pl.program_id(0)
    slot = t % 2
    base = pl.multiple_of(slot * MOE_SORTED_ROWS, MOE_SORTED_ROWS)
    pairs = [(ybuf, ys_hbm)]

    @pl.when(t == 0)
    def _():
        ybuf[...] = jnp.zeros(ybuf.shape, ybuf.dtype)
        _run_copies(t, base, start_ref, off_ref, nblk_ref, pairs, sems.at[slot], False)

    @pl.when(t + 1 < pl.num_programs(0))
    def _():
        nxt = pl.multiple_of((1 - slot) * MOE_SORTED_ROWS, MOE_SORTED_ROWS)
        _run_copies(t + 1, nxt, start_ref, off_ref, nblk_ref, pairs, sems.at[1 - slot], False)

    onehot = _sort_onehot(meta_ref[...], start_ref, t)
    _run_wait(t, nblk_ref, pairs, sems.at[slot], False)
    y = ybuf[pl.ds(base, MOE_SORTED_ROWS), :]
    yh = y.astype(BF16)
    yl = (y - yh.astype(F32)).astype(BF16)
    res = x_ref[...] + (jnp.dot(onehot, yh, preferred_element_type=F32)
                        + jnp.dot(onehot, yl, preferred_element_type=F32))
    if len(rest) == 3:
        rest[0][...] = res
    else:
        gf_ref, yp_ref, ysm_ref = rest[:3]
        yn = _rms(res, gf_ref[...])

        @pl.when(t < n_prompt_tiles)
        def _():
            yp_ref[...] = yn

        @pl.when(t >= n_prompt_tiles)
        def _():
            ysm_ref[...] = yn


def _moe_unpack(x, meta, ys, tabs, g_final=None, n_prompt=0):
    n = x.shape[0]
    tm = MOE_SRC_TILE
    npt = n_prompt // tm
    row = lambda w: pl.BlockSpec((tm, w), lambda i, *_: (i, 0))
    in_specs = [row(D_MODEL), row(ROUTER_LANES), pl.BlockSpec(memory_space=pl.ANY)]
    operands = [x, meta, ys]
    if g_final is None:
        out_specs = row(D_MODEL)
        out_shape = jax.ShapeDtypeStruct((n, D_MODEL), F32)
    else:
        nst = (n - n_prompt) // tm
        in_specs.append(pl.BlockSpec((1, D_MODEL), lambda i, *_: (0, 0)))
        operands.append(g_final)
        out_specs = [pl.BlockSpec((tm, D_MODEL), lambda i, *_: (jnp.minimum(i, npt - 1), 0)),
                     pl.BlockSpec((tm, D_MODEL), lambda i, *_: (jnp.clip(i - npt, 0, nst - 1), 0))]
        out_shape = [jax.ShapeDtypeStruct((n_prompt, D_MODEL), F32), jax.ShapeDtypeStruct((n - n_prompt, D_MODEL), F32)]
    grid_spec = pltpu.PrefetchScalarGridSpec(
        num_scalar_prefetch=3, grid=(n // tm,),
        in_specs=in_specs, out_specs=out_specs,
        scratch_shapes=[pltpu.VMEM((2 * MOE_SORTED_ROWS, D_MODEL), F32), pltpu.SemaphoreType.DMA((2,))])
    return pl.pallas_call(
        functools.partial(_moe_unpack_kernel, npt),
        grid_spec=grid_spec,
        out_shape=out_shape,
        compiler_params=_cparams("arbitrary"),
        name="moe_unpack",
    )(tabs["start"], tabs["off"], tabs["nblk"], *operands)


def _moe_tables(cnt, n):
    n_src = cnt.shape[0]
    pad = (cnt + MOE_RUN_ALIGN - 1) // MOE_RUN_ALIGN * MOE_RUN_ALIGN
    worst = n + n_src * (MOE_RUN_ALIGN - 1)
    cap = (worst + 2 * MOE_TILE - 1) // MOE_TILE * MOE_TILE
    start = jnp.cumsum(pad, axis=1) - pad
    total = jnp.sum(pad, axis=0)
    base = jnp.arange(N_EXPERT_GROUPS, dtype=jnp.int32) * cap
    off = base[None, :] + jnp.cumsum(pad, axis=0) - pad
    tiles_g = (total + MOE_TILE - 1) // MOE_TILE
    tile_end = jnp.cumsum(tiles_g)
    n_tiles = n // MOE_TILE + N_EXPERT_GROUPS + (n_src * N_EXPERT_GROUPS * MOE_RUN_ALIGN + MOE_TILE - 1) // MOE_TILE
    i = jnp.minimum(jnp.arange(n_tiles, dtype=jnp.int32), jnp.maximum(tile_end[-1] - 1, 0))
    grp = jnp.minimum(jnp.sum((i[:, None] >= tile_end[None, :]).astype(jnp.int32), axis=1), N_EXPERT_GROUPS - 1)
    first_tile = (tile_end - tiles_g)
    blk = jnp.zeros_like(i)
    for g in range(N_EXPERT_GROUPS):
        blk = blk + jnp.where(grp == g, g * (cap // MOE_TILE) + i - first_tile[g], 0)
    i32 = lambda a: a.astype(jnp.int32)
    tabs = dict(start=i32(start.reshape(-1)), off=i32(off.reshape(-1)), nblk=i32((pad // MOE_RUN_ALIGN).reshape(-1)),
                end=i32(base + total), tile_blk=i32(blk), tile_grp=i32(grp),
                tile_valid=i32(jnp.arange(n_tiles) < tile_end[-1]))
    return tabs, N_EXPERT_GROUPS * cap, n_tiles


def _moe(x, g, wr, br, wg, wu, wd, l, g_final=None, n_prompt=0):
    n = x.shape[0]
    xn, gates, meta, cnt = _moe_pre(x, g, wr, br, l)
    tabs, n_rows, n_tiles = _moe_tables(cnt[:, 0, :N_EXPERT_GROUPS].astype(jnp.int32), n)
    xs, gs = _moe_pack(xn, gates, meta, tabs, n_rows)
    ys = _moe_experts(xs, gs, tabs, wg, wu, wd, l, n_tiles)
    return _moe_unpack(x, meta, ys, tabs, g_final, n_prompt)


def _rope_tables(seq, t_len, tm):
    half = HEAD_DIM // 2
    inv = ROPE_THETA ** (-jnp.arange(half, dtype=F32) / half)
    pos_s = PAST_LEN + jnp.arange(t_len)
    pos = jnp.concatenate([jnp.arange(seq), jnp.tile(pos_s, tm // t_len)]).astype(F32)
    ang = pos[:, None] * inv[None, :]
    cos = jnp.tile(jnp.cos(ang), (1, LANES // half))
    sign = jnp.where((jnp.arange(LANES) % HEAD_DIM) < half, -1.0, 1.0).astype(F32)
    sin = jnp.tile(jnp.sin(ang), (1, LANES // half)) * sign[None, :]
    return cos, sin


def kernel(x_prompt, x_sample, cache_win_k, cache_win_v, state_ssm, cache_mem_k, cache_mem_v, mem_prompt, w_in, attn_sink, lam_re, lam_im, log_dt, ssm_b_re, ssm_b_im, ssm_c_re, ssm_c_im, ssm_d, w_glu, b_glu, g_attn_out, g_ssm_out, w_out, g_mix, g_xattn, g_mem, wq_x, wk_x, wv_x, wo_x, g_ffn, w_group, b_group, w_router, b_router, w_gate, w_up, w_down, g_final):
    batch, seq, _ = x_prompt.shape
    dec_batch, t_len, _ = x_sample.shape
    depth = w_in.shape[0]
    win_rows = cache_win_k.shape[2]
    n_p = batch * seq
    n_s = dec_batch * t_len
    tm_wide = 1024 if (n_p + n_s) % 1024 == 0 else 512
    tm = tm_wide
    tm_x = 512
    sample_nb = tm_x // t_len

    n = n_p + n_s
    xp = x_prompt.reshape(n_p, D_MODEL)
    xs = x_sample.reshape(n_s, D_MODEL)
    cos_tab, sin_tab = _rope_tables(seq, t_len, tm)
    mem_flat = mem_prompt.reshape(batch * N_MEM, D_MODEL)
    zero_state = jnp.zeros((batch, 1, 2 * N_STATE), F32)
    vec = lambda a: a.reshape(depth, 1, a.shape[-1])

    w_in_b, w_glu_b, w_out_b = w_in.astype(BF16), w_glu.astype(BF16), w_out.astype(BF16)
    wq_b, wk_b, wv_b, wo_b = (w.astype(BF16) for w in (wq_x, wk_x, wv_x, wo_x))
    wd_r = w_down.reshape(depth, N_EXPERTS * EXPERT_FF, D_MODEL)
    wr = jnp.concatenate([w_group, w_router.transpose(0, 2, 1, 3).reshape(depth, D_MODEL, N_EXPERTS)], axis=-1)
    wr = jnp.pad(wr, ((0, 0), (0, 0), (0, ROUTER_LANES - wr.shape[-1])))
    br = jnp.concatenate([b_group, b_router.reshape(depth, N_EXPERTS)], axis=-1)
    br = jnp.pad(br, ((0, 0), (0, ROUTER_LANES - br.shape[-1]))).reshape(depth, 1, ROUTER_LANES)
    g_mix_r, g_xattn_r, g_mem_r, g_ffn_r = vec(g_mix), vec(g_xattn), vec(g_mem), vec(g_ffn)
    g_a_r, g_s_r, b_glu_r, ssm_d_r = vec(g_attn_out), vec(g_ssm_out), vec(b_glu), vec(ssm_d)
    cache_k = cache_win_k.reshape(depth, dec_batch, win_rows, KV_WIDTH)
    cache_v = cache_win_v.reshape(depth, dec_batch, win_rows, KV_WIDTH)
    cmem_k = cache_mem_k.reshape(depth, dec_batch * N_MEM, D_MODEL)
    cmem_v = cache_mem_v.reshape(depth, dec_batch * N_MEM, D_MODEL)
    state_in = _state_to_tiles(state_ssm)

    outs = {k: [] for k in ("wk_p", "wv_p", "ssm_p", "mk_p", "mv_p", "wk_s", "wv_s", "ssm_s")}
    for l in range(depth):
        q, k, v, u = _in_proj(xp, xs, n, g_mix_r, w_in_b, l, cos_tab, sin_tab, n_p, seq, tm)
        attn_p = _attn_prompt(q, k, v, attn_sink[l], batch, seq)
        attn_s = _attn_sample(q, k, v, cache_k, cache_v, l, attn_sink[l], n_p, dec_batch, t_len, 8)
        tail = lambda a: jnp.stack([a[(b + 1) * seq - WINDOW:(b + 1) * seq] for b in range(batch)])
        outs["wk_p"].append(tail(k).reshape(batch, WINDOW, N_KV_HEADS, HEAD_DIM))
        outs["wv_p"].append(tail(v).reshape(batch, WINDOW, N_KV_HEADS, HEAD_DIM))
        ks = k[n_p:].reshape(dec_batch, t_len, KV_WIDTH)
        vs = v[n_p:].reshape(dec_batch, t_len, KV_WIDTH)
        k_all = jnp.concatenate([cache_k[l], ks], axis=1)[:, -win_rows:]
        v_all = jnp.concatenate([cache_v[l], vs], axis=1)[:, -win_rows:]
        outs["wk_s"].append(k_all.reshape(dec_batch, win_rows, N_KV_HEADS, HEAD_DIM))
        outs["wv_s"].append(v_all.reshape(dec_batch, win_rows, N_KV_HEADS, HEAD_DIM))

        we, tv, coef = _ssm_discretize(lam_re[l], lam_im[l], log_dt[l], ssm_b_re[l], ssm_b_im[l],
                                       ssm_c_re[l], ssm_c_im[l])
        y_p, fin_p = _ssm(u, zero_state, we, tv, ssm_d_r, l, coef, 0, batch, seq, 1, seq)
        y_s, fin_s = _ssm(u, state_in[l], we, tv, ssm_d_r, l, coef, n_p, dec_batch, t_len, dec_batch, t_len)
        outs["ssm_p"].append(_tiles_to_state(fin_p))
        outs["ssm_s"].append(_tiles_to_state(fin_s))
        x = _merge(xp, xs, n, attn_p, attn_s, y_p, y_s, w_glu_b, b_glu_r, g_a_r, g_s_r, w_out_b, l, tm)

        mk_p, mv_p = _mem_kv(mem_flat, g_mem_r, wk_b, wv_b, l, 512)
        outs["mk_p"].append(mk_p.reshape(batch, N_MEM, N_XHEADS, XHEAD_DIM))
        outs["mv_p"].append(mv_p.reshape(batch, N_MEM, N_XHEADS, XHEAD_DIM))
        tiles_per_seq = seq // tm
        x = _xattn(x, g_xattn_r, wq_b, wo_b, l, mk_p, mv_p,
                   pl.BlockSpec((N_MEM, D_MODEL), lambda i: (i // tiles_per_seq, 0)), 0, n_p, 1, tm)
        x = _xattn(x, g_xattn_r, wq_b, wo_b, l, cmem_k, cmem_v,
                   pl.BlockSpec((None, sample_nb * N_MEM, D_MODEL), lambda i: (l, i, 0)), n_p, n_s, sample_nb, tm_x)

        if l + 1 < depth:
            x = _moe(x, g_ffn_r, wr, br, w_gate, w_up, wd_r, l)
            xp = xs = x
        else:
            y_p, y_s = _moe(x, g_ffn_r, wr, br, w_gate, w_up, wd_r, l, g_final.reshape(1, D_MODEL), n_p)

    st = lambda name: jnp.stack(outs[name], axis=0)
    return (y_p.reshape(batch, seq, D_MODEL), y_s.reshape(dec_batch, t_len, D_MODEL),
            st("wk_p"), st("wv_p"), st("ssm_p"), st("mk_p"), st("mv_p"), st("wk_s"), st("wv_s"), st("ssm_s"))
```

```python
import functools
import math

import jax
import jax.numpy as jnp
from jax import lax
from jax.experimental import pallas as pl
from jax.experimental.pallas import tpu as pltpu

F32 = jnp.float32
BF16 = jnp.bfloat16

D_MODEL = 1024
CHUNK = 64
EPS = 1e-6
NEG_INF = -1e30
N_HEADS = 8
N_KV_HEADS = 2
HEAD_DIM = 64
ATTN_WIDTH = N_HEADS * HEAD_DIM
KV_WIDTH = N_KV_HEADS * HEAD_DIM
WINDOW = 128
ROPE_THETA = 10000.0
SSM_WIDTH = D_MODEL - ATTN_WIDTH
SSM_GROUP = 16
N_SSM_GROUPS = SSM_WIDTH // SSM_GROUP
SSM_STATE = 64
N_STATE = N_SSM_GROUPS * SSM_STATE
IN_WIDTH = ATTN_WIDTH + 2 * KV_WIDTH + SSM_WIDTH
N_MEM = 256
N_XHEADS = 4
XHEAD_DIM = D_MODEL // N_XHEADS
N_EXPERT_GROUPS = 4
EXPERTS_PER_GROUP = 8
N_EXPERTS = N_EXPERT_GROUPS * EXPERTS_PER_GROUP
EXPERT_FF = 128
PAST_LEN = 4096

LANES = 128
SUBLANES = 8
VMEM_LIMIT = 56 * 1024 * 1024


def _cparams(*sem):
    return pltpu.CompilerParams(dimension_semantics=sem, vmem_limit_bytes=VMEM_LIMIT)


def _rms(x, g):
    return x * lax.rsqrt(jnp.mean(x * x, axis=-1, keepdims=True) + EPS) * g


def _layer_spec(arr, l):
    shape = arr.shape[1:]
    zeros = (0,) * len(shape)
    return pl.BlockSpec((None,) + shape, lambda *_: (l,) + zeros, pipeline_mode=pl.Buffered(1))


def _rope_pairs(t, cos, sin_signed, first_half):
    swapped = jnp.where(first_half, pltpu.roll(t, LANES - HEAD_DIM // 2, 1), pltpu.roll(t, HEAD_DIM // 2, 1))
    return t * cos + swapped * sin_signed


def _two_source_specs(xp, xs, n_prompt, tm, width):
    npt = n_prompt // tm
    s_off = 0 if xs is xp else npt
    s_last = xs.shape[0] // tm - 1
    pspec = pl.BlockSpec((tm, width), lambda i, *_: (jnp.minimum(i, npt - 1), 0))
    sspec = pl.BlockSpec((tm, width), lambda i, *_: (jnp.clip(i - s_off, npt - s_off, s_last), 0))
    return pspec, sspec


def _in_proj_kernel(n_prompt_tiles, xp_ref, xs_ref, g_ref, w_ref, cos_ref, sin_ref, q_ref, k_ref, v_ref, u_ref):
    x = jnp.where(pl.program_id(0) < n_prompt_tiles, xp_ref[...], xs_ref[...])
    xn = _rms(x, g_ref[...])
    z = jnp.dot(xn.astype(BF16), w_ref[...], preferred_element_type=F32)
    cos = cos_ref[...]
    sin = sin_ref[...]
    lane = lax.broadcasted_iota(jnp.int32, cos.shape, 1)
    first_half = (lane % HEAD_DIM) < (HEAD_DIM // 2)
    scale = 1.0 / math.sqrt(HEAD_DIM)
    for j in range(ATTN_WIDTH // LANES):
        t = z[:, j * LANES:(j + 1) * LANES]
        q_ref[:, j * LANES:(j + 1) * LANES] = (_rope_pairs(t, cos, sin, first_half) * scale).astype(BF16)
    k_ref[...] = _rope_pairs(z[:, ATTN_WIDTH:ATTN_WIDTH + KV_WIDTH], cos, sin, first_half)
    v_ref[...] = z[:, ATTN_WIDTH + KV_WIDTH:ATTN_WIDTH + 2 * KV_WIDTH]
    u_ref[...] = z[:, ATTN_WIDTH + 2 * KV_WIDTH:]


def _in_proj(xp, xs, n, g, w_bf16, l, cos_tab, sin_tab, n_prompt, seq, tm):
    n_prompt_tiles = n_prompt // tm
    tiles_per_seq = seq // tm

    def tab_map(i):
        return (jnp.where(i < n_prompt_tiles, i % tiles_per_seq, tiles_per_seq), 0)

    row = lambda w: pl.BlockSpec((tm, w), lambda i: (i, 0))
    return pl.pallas_call(
        functools.partial(_in_proj_kernel, n_prompt_tiles),
        grid=(n // tm,),
        in_specs=[*_two_source_specs(xp, xs, n_prompt, tm, D_MODEL), _layer_spec(g, l), _layer_spec(w_bf16, l),
                  pl.BlockSpec((tm, LANES), tab_map),
                  pl.BlockSpec((tm, LANES), tab_map)],
        out_specs=[row(ATTN_WIDTH), row(KV_WIDTH), row(KV_WIDTH), row(SSM_WIDTH)],
        out_shape=[jax.ShapeDtypeStruct((n, ATTN_WIDTH), BF16),
                   jax.ShapeDtypeStruct((n, KV_WIDTH), F32),
                   jax.ShapeDtypeStruct((n, KV_WIDTH), F32),
                   jax.ShapeDtypeStruct((n, SSM_WIDTH), F32)],
        compiler_params=_cparams("parallel"),
        name="in_proj",
    )(xp, xs, g, w_bf16, cos_tab, sin_tab)


def _kv_pairs(keys, vals):
    lane = lax.broadcasted_iota(jnp.int32, keys.shape, 1)
    low = lane < HEAD_DIM
    k_sw = pltpu.roll(keys, HEAD_DIM, 1)
    v_sw = pltpu.roll(vals, HEAD_DIM, 1)
    kk = [jnp.where(low, keys, k_sw).astype(BF16), jnp.where(low, k_sw, keys).astype(BF16)]
    vv = [jnp.where(low, vals, v_sw).astype(BF16), jnp.where(low, v_sw, vals).astype(BF16)]
    return kk, vv


def _attend_pairs(q, kk, vv, key_rows, mask_add, sink_ref, o_ref, row0):
    tq = q.shape[0]
    qlane = lax.broadcasted_iota(jnp.int32, (tq, LANES), 1)
    qlow = qlane < HEAD_DIM
    row_top = lax.broadcasted_iota(jnp.int32, (2 * tq, 1), 0) < tq
    zero = jnp.zeros((), BF16)
    for pair in range(N_HEADS // 2):
        kv = pair // (N_HEADS // N_KV_HEADS // 2)
        qp = q[:, pair * LANES:(pair + 1) * LANES]
        qs = jnp.concatenate([jnp.where(qlow, qp, zero), jnp.where(qlow, zero, qp)], axis=0)
        s = lax.dot_general(qs, kk[kv][key_rows, :], (((1,), (1,)), ((), ())), preferred_element_type=F32)
        if mask_add is not None:
            s = s + mask_add
        sink = jnp.where(row_top, sink_ref[2 * pair], sink_ref[2 * pair + 1])
        m = jnp.maximum(jnp.max(s, axis=-1, keepdims=True), sink)
        e = jnp.exp(s - m)
        p = e / (jnp.sum(e, axis=-1, keepdims=True) + jnp.exp(sink - m))
        o = jnp.dot(p.astype(BF16), vv[kv][key_rows, :], preferred_element_type=F32)
        o_ref[row0:row0 + tq, pair * LANES:(pair + 1) * LANES] = jnp.where(qlow, o[:tq], o[tq:])


def _attend_blocks(blocks, sink_ref, o_ref):
    tq = blocks[0][0].shape[0]
    qlane = lax.broadcasted_iota(jnp.int32, (tq, LANES), 1)
    qlow = qlane < HEAD_DIM
    row_top = lax.broadcasted_iota(jnp.int32, (2 * tq, 1), 0) < tq
    zero = jnp.zeros((), BF16)
    scores, sinks = [], []
    for q, kk, vv, key_rows, mask_add, row0 in blocks:
        for pair in range(N_HEADS // 2):
            kv = pair // (N_HEADS // N_KV_HEADS // 2)
            qp = q[:, pair * LANES:(pair + 1) * LANES]
            qs = jnp.concatenate([jnp.where(qlow, qp, zero), jnp.where(qlow, zero, qp)], axis=0)
            s = lax.dot_general(qs, kk[kv][key_rows, :], (((1,), (1,)), ((), ())), preferred_element_type=F32)
            scores.append(s if mask_add is None else s + mask_add)
            sinks.append(jnp.where(row_top, sink_ref[2 * pair], sink_ref[2 * pair + 1]))
    s = jnp.concatenate(scores, axis=0)
    sink = jnp.concatenate(sinks, axis=0)
    m = jnp.maximum(jnp.max(s, axis=-1, keepdims=True), sink)
    e = jnp.exp(s - m)
    p = (e / (jnp.sum(e, axis=-1, keepdims=True) + jnp.exp(sink - m))).astype(BF16)
    piece = 0
    for q, kk, vv, key_rows, mask_add, row0 in blocks:
        for pair in range(N_HEADS // 2):
            kv = pair // (N_HEADS // N_KV_HEADS // 2)
            o = jnp.dot(p[piece * 2 * tq:(piece + 1) * 2 * tq, :], vv[kv][key_rows, :], preferred_element_type=F32)
            o_ref[row0:row0 + tq, pair * LANES:(pair + 1) * LANES] = jnp.where(qlow, o[:tq], o[tq:])
            piece += 1


ATTN_SUB = WINDOW
ATTN_TILE = 8 * ATTN_SUB


def _attn_prompt_kernel(sink_ref, ma_ref, mb_ref, q_ref, kp_ref, kc_ref, vp_ref, vc_ref, o_ref):
    kk, vv = _kv_pairs(jnp.concatenate([kp_ref[...], kc_ref[...]], axis=0),
                       jnp.concatenate([vp_ref[...], vc_ref[...]], axis=0))
    for s in range(ATTN_TILE // ATTN_SUB):
        m_ref = ma_ref if s == 0 else mb_ref
        rows = slice(s * ATTN_SUB, s * ATTN_SUB + 2 * WINDOW)
        _attend_pairs(q_ref[s * ATTN_SUB:(s + 1) * ATTN_SUB, :], kk, vv, rows, m_ref[...], sink_ref, o_ref, s * ATTN_SUB)


def _band_masks():
    r = (jnp.arange(2 * ATTN_SUB) % ATTN_SUB)[:, None] // CHUNK
    c = jnp.arange(2 * WINDOW)[None, :]
    band = (c // CHUNK >= r) & (c // CHUNK <= r + WINDOW // CHUNK)
    masks = jnp.stack([band, band & (c >= WINDOW)])
    return jnp.where(masks, 0.0, NEG_INF).astype(F32)


def _attn_prompt(q, k, v, sink, batch, seq):
    nt = seq // ATTN_TILE
    per_seq = seq // ATTN_SUB
    cur = lambda b, i: (b * nt + i, 0)
    prev = lambda b, i: (b * per_seq + jnp.maximum((ATTN_TILE // ATTN_SUB) * i - 1, 0), 0)
    masks = _band_masks()
    return pl.pallas_call(
        _attn_prompt_kernel,
        grid=(batch, nt),
        in_specs=[pl.BlockSpec(memory_space=pltpu.SMEM),
                  pl.BlockSpec((None, 2 * ATTN_SUB, 2 * WINDOW), lambda b, i: (jnp.where(i == 0, 1, 0), 0, 0)),
                  pl.BlockSpec((None, 2 * ATTN_SUB, 2 * WINDOW), lambda b, i: (0, 0, 0)),
                  pl.BlockSpec((ATTN_TILE, ATTN_WIDTH), cur),
                  pl.BlockSpec((ATTN_SUB, KV_WIDTH), prev),
                  pl.BlockSpec((ATTN_TILE, KV_WIDTH), cur),
                  pl.BlockSpec((ATTN_SUB, KV_WIDTH), prev),
                  pl.BlockSpec((ATTN_TILE, KV_WIDTH), cur)],
        out_specs=pl.BlockSpec((ATTN_TILE, ATTN_WIDTH), cur),
        out_shape=jax.ShapeDtypeStruct((batch * seq, ATTN_WIDTH), F32),
        compiler_params=_cparams("parallel", "parallel"),
        name="attn_prompt",
    )(sink, masks, masks, q, k, k, v, v)


def _attn_sample_kernel(sink_ref, q_ref, ck_ref, cv_ref, k_ref, v_ref, o_ref):
    nb = ck_ref.shape[0]
    t = q_ref.shape[0] // nb
    blocks = []
    for b in range(nb):
        rows = slice(b * t, (b + 1) * t)
        kk, vv = _kv_pairs(jnp.concatenate([ck_ref[b], k_ref[rows, :]], axis=0),
                           jnp.concatenate([cv_ref[b], v_ref[rows, :]], axis=0))
        blocks.append((q_ref[rows, :], kk, vv, slice(None), None, b * t))
    _attend_blocks(blocks, sink_ref, o_ref)


def _attn_sample(q, k, v, cache_k, cache_v, l, sink, n_prompt, dec_batch, t, nb):
    w = cache_k.shape[2]
    rows = nb * t
    base = n_prompt // rows
    tok = lambda width: pl.BlockSpec((rows, width), lambda i: (base + i, 0))
    cache = pl.BlockSpec((None, nb, w, KV_WIDTH), lambda i: (l, i, 0, 0))
    return pl.pallas_call(
        _attn_sample_kernel,
        grid=(dec_batch // nb,),
        in_specs=[pl.BlockSpec(memory_space=pltpu.SMEM),
                  tok(ATTN_WIDTH), cache, cache, tok(KV_WIDTH), tok(KV_WIDTH)],
        out_specs=pl.BlockSpec((rows, ATTN_WIDTH), lambda i: (i, 0)),
        out_shape=jax.ShapeDtypeStruct((dec_batch * t, ATTN_WIDTH), F32),
        compiler_params=_cparams("parallel"),
        name="attn_sample",
    )(sink, q, cache_k, cache_v, k, v)


SSM_BLOCK = SUBLANES


def _cmul(ar, ai, br, bi):
    return ar * br - ai * bi, ar * bi + ai * br


def _ssm_disc_kernel(lre_ref, lim_ref, dt_ref, bre_ref, bim_ref, cre_ref, cim_ref,
                     we_ref, tv_ref, coef_ref, vt_ref, wb_ref):
    we_ref[...] = jnp.zeros(we_ref.shape, we_ref.dtype)
    vt_ref[...] = jnp.zeros(vt_ref.shape, vt_ref.dtype)
    wb_ref[...] = jnp.zeros(wb_ref.shape, wb_ref.dtype)
    lane = lax.broadcasted_iota(jnp.int32, (SSM_GROUP, LANES), 1)
    half = [lane < SSM_STATE, lane >= SSM_STATE]
    row8 = lax.broadcasted_iota(jnp.int32, (SUBLANES, LANES), 0)
    for q in range(STATE_TILES):
        lre = lre_ref[q]
        lim = lim_ref[q]
        dt = dt_ref[q]
        mag = jnp.exp(lre * dt)
        ang = lim * dt
        lbr = mag * jnp.cos(ang)
        lbi = mag * jnp.sin(ang)
        nr, ni = lbr - 1.0, lbi
        den = lre * lre + lim * lim
        fr = (nr * lre + ni * lim) / den
        fi = (ni * lre - nr * lim) / den
        bbr, bbi = _cmul(fr, fi, bre_ref[q], bim_ref[q])
        cr, ci = cre_ref[q], cim_ref[q]
        pw = [(jnp.ones_like(lbr), jnp.zeros_like(lbr))]
        for _ in range(SSM_BLOCK):
            pw.append(_cmul(pw[-1][0], pw[-1][1], lbr, lbi))
        re_l = slice(q * 2 * LANES, q * 2 * LANES + LANES)
        im_l = slice(q * 2 * LANES + LANES, (q + 1) * 2 * LANES)
        for h in range(2):
            g = 2 * q + h
            grow = lambda blk: slice(blk * LANES + g * SSM_GROUP, blk * LANES + (g + 1) * SSM_GROUP)
            for j in range(SSM_BLOCK):
                wr, wi = _cmul(pw[SSM_BLOCK - 1 - j][0], pw[SSM_BLOCK - 1 - j][1], bbr, bbi)
                we_ref[grow(j), re_l] = jnp.where(half[h], wr, 0.0).astype(we_ref.dtype)
                we_ref[grow(j), im_l] = jnp.where(half[h], wi, 0.0).astype(we_ref.dtype)
            for d in range(SSM_BLOCK + 1):
                xr, xi = _cmul(cr, ci, pw[d][0], pw[d][1])
                vt_ref[grow(d), re_l] = jnp.where(half[h], xr, 0.0)
                vt_ref[grow(d), im_l] = jnp.where(half[h], -xi, 0.0)
            wb_ref[grow(0), re_l] = jnp.where(half[h], bbr, 0.0)
            wb_ref[grow(0), im_l] = jnp.where(half[h], bbi, 0.0)
        l8 = [pw[SSM_BLOCK]]
        for _ in range(SUBLANES - 1):
            l8.append(_cmul(l8[-1][0], l8[-1][1], pw[SSM_BLOCK][0], pw[SSM_BLOCK][1]))
        co = slice(q * LANES, (q + 1) * LANES)
        for kind, sh in enumerate((1, 2, 4)):
            for a in range(2):
                coef_ref[kind, a, :, co] = jnp.where(row8 >= sh, l8[sh - 1][a], 0.0)
        for a in range(2):
            tab = jnp.zeros((SUBLANES, LANES), F32)
            for k in range(SUBLANES):
                tab = jnp.where(row8 == k, l8[k][a], tab)
            coef_ref[3, a, :, co] = tab
    t0 = lax.dot_general(wb_ref[...], vt_ref[0:BLOCK_LANES, :], (((1,), (1,)), ((), ())),
                         preferred_element_type=F32, precision=lax.Precision.HIGHEST)
    for j in range(SSM_BLOCK):
        if j:
            tv_ref[j * LANES:(j + 1) * LANES, 0:j * LANES] = jnp.zeros((LANES, j * LANES), tv_ref.dtype)
        tv_ref[j * LANES:(j + 1) * LANES, j * LANES:] = t0[:, 0:BLOCK_LANES - j * LANES].astype(tv_ref.dtype)
    tv_ref[BLOCK_LANES:, :] = jnp.transpose(vt_ref[LANES:, :]).astype(tv_ref.dtype)


def _ssm_discretize(lam_re, lam_im, log_dt, b_re, b_im, c_re, c_im):
    g, p = lam_re.shape
    npair = g // 2
    pair = lambda a: a.reshape(npair, 1, 2 * p)
    rows = lambda a: a.reshape(npair, 2, SSM_GROUP, p).transpose(0, 2, 1, 3).reshape(npair, SSM_GROUP, 2 * p)
    dt = jnp.repeat(jnp.exp(log_dt), p).reshape(npair, 1, 2 * p)
    vec = pl.BlockSpec((STATE_TILES, 1, LANES), lambda m: (m, 0, 0))
    mat = pl.BlockSpec((STATE_TILES, SSM_GROUP, LANES), lambda m: (m, 0, 0))
    return pl.pallas_call(
        _ssm_disc_kernel,
        grid=(N_CH_BLOCKS,),
        in_specs=[vec, vec, vec, mat, mat, mat, mat],
        out_specs=[pl.BlockSpec((None, BLOCK_LANES, STATE_LANES), lambda m: (m, 0, 0)),
                   pl.BlockSpec((None, BLOCK_LANES + STATE_LANES, BLOCK_LANES), lambda m: (m, 0, 0)),
                   pl.BlockSpec((None, 4, 2, SUBLANES, STATE_LANES // 2), lambda m: (m, 0, 0, 0, 0))],
        out_shape=[jax.ShapeDtypeStruct((N_CH_BLOCKS, BLOCK_LANES, STATE_LANES), BF16),
                   jax.ShapeDtypeStruct((N_CH_BLOCKS, BLOCK_LANES + STATE_LANES, BLOCK_LANES), BF16),
                   jax.ShapeDtypeStruct((N_CH_BLOCKS, 4, 2, SUBLANES, STATE_LANES // 2), F32)],
        scratch_shapes=[pltpu.VMEM(((SSM_BLOCK + 1) * LANES, STATE_LANES), F32),
                        pltpu.VMEM((LANES, STATE_LANES), F32)],
        compiler_params=_cparams("parallel"),
        name="ssm_discretize",
    )(pair(lam_re), pair(lam_im), dt, rows(b_re.transpose(0, 2, 1)), rows(b_im.transpose(0, 2, 1)),
      rows(c_re), rows(c_im))


GROUPS_PER_CH_BLOCK = LANES // SSM_GROUP
N_CH_BLOCKS = SSM_WIDTH // LANES
STATE_LANES = 2 * GROUPS_PER_CH_BLOCK * SSM_STATE
STATE_TILES = STATE_LANES // (2 * LANES)
BLOCK_LANES = SSM_BLOCK * LANES


def _ssm_kernel(nb, u_ref, s0_ref, we_ref, tv_ref, d_ref, coef_ref, y_ref, fin_ref, st_ref, sprev_ref, ucat_ref):
    t_idx = pl.program_id(2)
    r = st_ref.shape[1] - SUBLANES
    rows = nb * r

    @pl.when(t_idx == 0)
    def _():
        for b in range(nb):
            st_ref[b, 0:SUBLANES, :] = jnp.broadcast_to(s0_ref[b], (SUBLANES, STATE_LANES))

    for j in range(SSM_BLOCK):
        ucat_ref[:, j * LANES:(j + 1) * LANES] = u_ref[pl.ds(j, rows, stride=SSM_BLOCK), :].astype(BF16)
    e = jnp.dot(ucat_ref[...], we_ref[...], preferred_element_type=F32)
    st_ref[:, SUBLANES:, :] = e.reshape(nb, r, STATE_LANES)

    first_row = lax.broadcasted_iota(jnp.int32, (SUBLANES, LANES), 0) == 0

    def group(rg, carry):
        r0 = pl.multiple_of(SUBLANES + rg * SUBLANES, SUBLANES)
        rp = pl.multiple_of(rg * SUBLANES, SUBLANES)
        for b in range(nb):
            for q in range(STATE_TILES):
                re_sl = pl.ds(q * 2 * LANES, LANES)
                im_sl = pl.ds(q * 2 * LANES + LANES, LANES)
                co = pl.ds(q * LANES, LANES)
                xr = st_ref[b, pl.ds(r0, SUBLANES), re_sl]
                xi = st_ref[b, pl.ds(r0, SUBLANES), im_sl]
                for step, sh in enumerate((1, 2, 4)):
                    ar, ai = _cmul(coef_ref[step, 0, :, co], coef_ref[step, 1, :, co],
                                   pltpu.roll(xr, sh, 0), pltpu.roll(xi, sh, 0))
                    xr = xr + ar
                    xi = xi + ai
                pr = jnp.broadcast_to(st_ref[b, pl.ds(rp, SUBLANES), re_sl][SUBLANES - 1:, :], (SUBLANES, LANES))
                pi = jnp.broadcast_to(st_ref[b, pl.ds(rp, SUBLANES), im_sl][SUBLANES - 1:, :], (SUBLANES, LANES))
                ar, ai = _cmul(coef_ref[3, 0, :, co], coef_ref[3, 1, :, co], pr, pi)
                xr = xr + ar
                xi = xi + ai
                st_ref[b, pl.ds(r0, SUBLANES), re_sl] = xr
                st_ref[b, pl.ds(r0, SUBLANES), im_sl] = xi
                out_rows = pl.ds(pl.multiple_of(b * r + rg * SUBLANES, SUBLANES), SUBLANES)
                sprev_ref[out_rows, re_sl] = jnp.where(first_row, pr, pltpu.roll(xr, 1, 0))
                sprev_ref[out_rows, im_sl] = jnp.where(first_row, pi, pltpu.roll(xi, 1, 0))
        return carry

    lax.fori_loop(0, r // SUBLANES, group, 0)

    lhs = jnp.concatenate([ucat_ref[...], sprev_ref[...].astype(BF16)], axis=1)
    ycat = jnp.dot(lhs, tv_ref[...], preferred_element_type=F32)
    d = d_ref[...]
    for t in range(SSM_BLOCK):
        tok = pl.ds(t, rows, stride=SSM_BLOCK)
        y_ref[tok, :] = ycat[:, t * LANES:(t + 1) * LANES] + d * u_ref[tok, :]

    for b in range(nb):
        tail = st_ref[b, r:r + SUBLANES, :]
        st_ref[b, 0:SUBLANES, :] = tail
        fin_ref[b] = tail[SUBLANES - 1:SUBLANES, :]


def _ssm(u, s0, we, tv, d, l, coef, row_base, n_seq, seq_len, nb, tt):
    rows = nb * tt
    r = tt // SSM_BLOCK
    nt = seq_len // tt
    base = row_base // rows
    return pl.pallas_call(
        functools.partial(_ssm_kernel, nb),
        grid=(N_CH_BLOCKS, n_seq // nb, nt),
        in_specs=[pl.BlockSpec((rows, LANES), lambda m, i, j: (base + i * nt + j, m)),
                  pl.BlockSpec((nb, 1, STATE_LANES), lambda m, i, j: (i, 0, m)),
                  pl.BlockSpec((None, BLOCK_LANES, STATE_LANES), lambda m, i, j: (m, 0, 0)),
                  pl.BlockSpec((None, BLOCK_LANES + STATE_LANES, BLOCK_LANES), lambda m, i, j: (m, 0, 0)),
                  pl.BlockSpec((None, 1, LANES), lambda m, i, j: (l, 0, m)),
                  pl.BlockSpec((None, 4, 2, SUBLANES, STATE_LANES // 2), lambda m, i, j: (m, 0, 0, 0, 0))],
        out_specs=[pl.BlockSpec((rows, LANES), lambda m, i, j: (i * nt + j, m)),
                   pl.BlockSpec((nb, 1, STATE_LANES), lambda m, i, j: (i, 0, m))],
        out_shape=[jax.ShapeDtypeStruct((n_seq * seq_len, SSM_WIDTH), F32),
                   jax.ShapeDtypeStruct((n_seq, 1, N_CH_BLOCKS * STATE_LANES), F32)],
        scratch_shapes=[pltpu.VMEM((nb, SUBLANES + r, STATE_LANES), F32),
                        pltpu.VMEM((nb * r, STATE_LANES), F32),
                        pltpu.VMEM((nb * r, BLOCK_LANES), BF16)],
        compiler_params=_cparams("parallel", "parallel", "arbitrary"),
        name="ssm_scan",
    )(u, s0, we, tv, d, coef)


def _state_to_tiles(s):
    lead = s.shape[:-3]
    t = s.reshape(lead + (N_CH_BLOCKS, STATE_TILES, 2, SSM_STATE, 2))
    t = jnp.moveaxis(t, -1, -3)
    return t.reshape(lead + (1, N_CH_BLOCKS * STATE_LANES))


def _tiles_to_state(f):
    b = f.shape[0]
    t = f.reshape(b, N_CH_BLOCKS, STATE_TILES, 2, 2, SSM_STATE)
    t = jnp.moveaxis(t, 3, -1)
    return t.reshape(b, N_SSM_GROUPS, SSM_STATE, 2)


def _merge_kernel(n_prompt_tiles, xp_ref, xs_ref, ap_ref, as_ref, yp_ref, ys_ref, wglu_ref, bglu_ref, ga_ref, gs_ref,
                  wout_ref, o_ref):
    is_prompt = pl.program_id(0) < n_prompt_tiles
    attn = jnp.where(is_prompt, ap_ref[...], as_ref[...])
    g = jax.nn.gelu(jnp.where(is_prompt, yp_ref[...], ys_ref[...]))
    glu = g * jax.nn.sigmoid(jnp.dot(g.astype(BF16), wglu_ref[...], preferred_element_type=F32) + bglu_ref[...])
    na = _rms(attn, ga_ref[...]).astype(BF16)
    ns = _rms(glu, gs_ref[...]).astype(BF16)
    o = jnp.dot(na, wout_ref[0:ATTN_WIDTH, :], preferred_element_type=F32)
    o = o + jnp.dot(ns, wout_ref[ATTN_WIDTH:, :], preferred_element_type=F32)
    o_ref[...] = jnp.where(is_prompt, xp_ref[...], xs_ref[...]) + o


def _merge(xp, xs, n, attn_p, attn_s, y_p, y_s, wglu, bglu, ga, gs, wout, l, tm):
    npt = attn_p.shape[0] // tm
    nst = attn_s.shape[0] // tm
    row = lambda w: pl.BlockSpec((tm, w), lambda i: (i, 0))
    prow = lambda w: pl.BlockSpec((tm, w), lambda i: (jnp.minimum(i, npt - 1), 0))
    srow = lambda w: pl.BlockSpec((tm, w), lambda i: (jnp.clip(i - npt, 0, nst - 1), 0))
    return pl.pallas_call(
        functools.partial(_merge_kernel, npt),
        grid=(n // tm,),
        in_specs=[*_two_source_specs(xp, xs, attn_p.shape[0], tm, D_MODEL),
                  prow(ATTN_WIDTH), srow(ATTN_WIDTH), prow(SSM_WIDTH), srow(SSM_WIDTH),
                  _layer_spec(wglu, l), _layer_spec(bglu, l), _layer_spec(ga, l), _layer_spec(gs, l),
                  _layer_spec(wout, l)],
        out_specs=row(D_MODEL),
        out_shape=jax.ShapeDtypeStruct((n, D_MODEL), F32),
        compiler_params=_cparams("parallel"),
        name="merge_heads",
    )(xp, xs, attn_p, attn_s, y_p, y_s, wglu, bglu, ga, gs, wout)


def _mem_kv_kernel(m_ref, g_ref, wk_ref, wv_ref, k_ref, v_ref):
    mn = _rms(m_ref[...], g_ref[...]).astype(BF16)
    k_ref[...] = jnp.dot(mn, wk_ref[...], preferred_element_type=F32)
    v_ref[...] = jnp.dot(mn, wv_ref[...], preferred_element_type=F32)


def _mem_kv(mem, g, wk, wv, l, tm):
    n = mem.shape[0]
    row = pl.BlockSpec((tm, D_MODEL), lambda i: (i, 0))
    return pl.pallas_call(
        _mem_kv_kernel,
        grid=(n // tm,),
        in_specs=[row, _layer_spec(g, l), _layer_spec(wk, l), _layer_spec(wv, l)],
        out_specs=[row, row],
        out_shape=[jax.ShapeDtypeStruct((n, D_MODEL), F32)] * 2,
        compiler_params=_cparams("parallel"),
        name="mem_kv",
    )(mem, g, wk, wv)


def _xattn_kernel(nb, x_ref, g_ref, wq_ref, wo_ref, mk_ref, mv_ref, o_ref, att_ref):
    t = x_ref.shape[0] // nb
    x = x_ref[...]
    hn = _rms(x, g_ref[...]).astype(BF16)
    q = jnp.dot(hn, wq_ref[...], preferred_element_type=F32) * (1.0 / math.sqrt(XHEAD_DIM))
    q = q.astype(BF16)
    heads = [slice(h * XHEAD_DIM, (h + 1) * XHEAD_DIM) for h in range(N_XHEADS)]
    scores = []
    for b in range(nb):
        mk = mk_ref[b * N_MEM:(b + 1) * N_MEM, :].astype(BF16)
        for sl in heads:
            scores.append(lax.dot_general(q[b * t:(b + 1) * t, sl], mk[:, sl], (((1,), (1,)), ((), ())),
                                          preferred_element_type=F32))
    s = jnp.concatenate(scores, axis=0)
    e = jnp.exp(s - jnp.max(s, axis=-1, keepdims=True))
    p = (e / jnp.sum(e, axis=-1, keepdims=True)).astype(BF16)
    for b in range(nb):
        mv = mv_ref[b * N_MEM:(b + 1) * N_MEM, :].astype(BF16)
        for h, sl in enumerate(heads):
            r0 = (b * N_XHEADS + h) * t
            att_ref[b * t:(b + 1) * t, sl] = jnp.dot(p[r0:r0 + t, :], mv[:, sl], preferred_element_type=F32)
    o = jnp.dot(att_ref[...].astype(BF16), wo_ref[...], preferred_element_type=F32)
    o_ref[...] = x + o


def _xattn(x, g, wq, wo, l, mk, mv, mem_spec, row_base, n_rows, nb, tm):
    base = row_base // tm
    xspec = pl.BlockSpec((tm, D_MODEL), lambda i: (base + i, 0))
    return pl.pallas_call(
        functools.partial(_xattn_kernel, nb),
        grid=(n_rows // tm,),
        in_specs=[xspec, _layer_spec(g, l), _layer_spec(wq, l), _layer_spec(wo, l), mem_spec, mem_spec],
        out_specs=xspec,
        out_shape=jax.ShapeDtypeStruct(x.shape, F32),
        scratch_shapes=[pltpu.VMEM((tm, D_MODEL), F32)],
        input_output_aliases={0: 0},
        compiler_params=_cparams("parallel"),
        name="cross_attn",
    )(x, g, wq, wo, mk, mv)


ROUTER_LANES = LANES
EXPERT_LANE0 = N_EXPERT_GROUPS
EXPERTS_PER_STEP = EXPERTS_PER_GROUP
MOE_SRC_TILE = 512
MOE_RUN_ALIGN = 16
MOE_SORTED_ROWS = 640
MOE_TILE = 1024


def _dot_f32_3pass(x, w):
    xh = x.astype(BF16)
    xl = (x - xh.astype(F32)).astype(BF16)
    wh = w.astype(BF16)
    wl = (w - wh.astype(F32)).astype(BF16)
    dot = lambda a, b: jnp.dot(a, b, preferred_element_type=F32)
    return dot(xh, wh) + (dot(xl, wh) + dot(xh, wl))


def _route(logits):
    lane_i = lax.broadcasted_iota(jnp.int32, logits.shape, 1)
    lane = lane_i.astype(F32)
    neg = jnp.float32(-jnp.inf)
    is_g = lane_i < N_EXPERT_GROUPS
    gl = jnp.where(is_g, logits, neg)
    gmax = jnp.max(gl, axis=-1, keepdims=True)
    gidx = jnp.min(jnp.where(gl == gmax, lane, float(ROUTER_LANES)), axis=-1, keepdims=True)
    g_w = 1.0 / jnp.sum(jnp.where(is_g, jnp.exp(gl - gmax), 0.0), axis=-1, keepdims=True)
    first = EXPERT_LANE0 + gidx * EXPERTS_PER_GROUP
    sel = (lane >= first) & (lane < first + EXPERTS_PER_GROUP)
    el = jnp.where(sel, logits, neg)
    m1 = jnp.max(el, axis=-1, keepdims=True)
    i1 = jnp.min(jnp.where(el == m1, lane, float(ROUTER_LANES)), axis=-1, keepdims=True)
    el2 = jnp.where(lane == i1, neg, el)
    m2 = jnp.max(el2, axis=-1, keepdims=True)
    i2 = jnp.min(jnp.where(el2 == m2, lane, float(ROUTER_LANES)), axis=-1, keepdims=True)
    r = jnp.exp(m2 - m1)
    w1 = g_w / (1.0 + r)
    w2 = w1 * r
    return jnp.where(lane == i1, w1, jnp.where(lane == i2, w2, 0.0)), gidx


def _moe_pre_kernel(x_ref, g_ref, wr_ref, br_ref, xn_ref, gate_ref, cnt_ref, tri_ref):
    tm = x_ref.shape[0]

    @pl.when(pl.program_id(0) == 0)
    def _():
        r = lax.broadcasted_iota(jnp.int32, (tm, tm), 0)
        c = lax.broadcasted_iota(jnp.int32, (tm, tm), 1)
        tri_ref[...] = (c <= r).astype(BF16)

    xn = _rms(x_ref[...], g_ref[...])
    xn_ref[...] = xn.astype(BF16)
    gates, gidx = _route(_dot_f32_3pass(xn, wr_ref[...]) + br_ref[...])
    lane = lax.broadcasted_iota(jnp.int32, gates.shape, 1).astype(F32)
    onehot = lane == gidx
    incl = jnp.dot(tri_ref[...], onehot.astype(BF16), preferred_element_type=F32)
    rank = jnp.sum(jnp.where(onehot, incl, 0.0), axis=-1, keepdims=True) - 1.0
    gate_ref[...] = jnp.where(lane == 0.0, gidx, jnp.where(lane == 1.0, rank, gates))
    cnt_ref[...] = incl[tm - 1:tm, :]


def _moe_pre(x, g, wr, br, l):
    n = x.shape[0]
    tm = MOE_SRC_TILE
    row = lambda w: pl.BlockSpec((tm, w), lambda i: (i, 0))
    return pl.pallas_call(
        _moe_pre_kernel,
        grid=(n // tm,),
        in_specs=[row(D_MODEL), _layer_spec(g, l), _layer_spec(wr, l), _layer_spec(br, l)],
        out_specs=[row(D_MODEL), row(ROUTER_LANES), pl.BlockSpec((None, 1, ROUTER_LANES), lambda i: (i, 0, 0))],
        out_shape=[jax.ShapeDtypeStruct((n, D_MODEL), BF16), jax.ShapeDtypeStruct((n, ROUTER_LANES), F32),
                   jax.ShapeDtypeStruct((n // tm, 1, ROUTER_LANES), F32)],
        scratch_shapes=[pltpu.VMEM((tm, tm), BF16)],
        compiler_params=_cparams("arbitrary"),
        name="moe_pre",
    )(x, g, wr, br)


def _sort_onehot(meta, start_ref, t):
    gid = meta[:, 0:1]
    pos = meta[:, 1:2]
    for grp in range(N_EXPERT_GROUPS):
        pos = pos + jnp.where(gid == float(grp), start_ref[t * N_EXPERT_GROUPS + grp].astype(F32), 0.0)
    col = lax.broadcasted_iota(jnp.int32, (meta.shape[0], MOE_SORTED_ROWS), 1).astype(F32)
    return (col == pos).astype(BF16)


def _run_copies(t, base, start_ref, off_ref, nblk_ref, pairs, sem, to_hbm):
    for grp in range(N_EXPERT_GROUPS):
        k = t * N_EXPERT_GROUPS + grp
        src0 = base + start_ref[k]
        dst0 = off_ref[k]

        def body(b, carry):
            lo = pl.multiple_of(src0 + b * MOE_RUN_ALIGN, MOE_RUN_ALIGN)
            hi = pl.multiple_of(dst0 + b * MOE_RUN_ALIGN, MOE_RUN_ALIGN)
            for buf, arr in pairs:
                a, h = buf.at[pl.ds(lo, MOE_RUN_ALIGN)], arr.at[pl.ds(hi, MOE_RUN_ALIGN)]
                (pltpu.make_async_copy(a, h, sem) if to_hbm else pltpu.make_async_copy(h, a, sem)).start()
            return carry

        lax.fori_loop(0, nblk_ref[k], body, 0)


def _run_wait(t, nblk_ref, pairs, sem, to_hbm):
    total = nblk_ref[t * N_EXPERT_GROUPS]
    for grp in range(1, N_EXPERT_GROUPS):
        total = total + nblk_ref[t * N_EXPERT_GROUPS + grp]
    rows = total * MOE_RUN_ALIGN

    @pl.when(rows > 0)
    def _():
        for buf, arr in pairs:
            a, h = buf.at[pl.ds(0, rows)], arr.at[pl.ds(0, rows)]
            (pltpu.make_async_copy(a, h, sem) if to_hbm else pltpu.make_async_copy(h, a, sem)).wait()


def _moe_pack_kernel(start_ref, off_ref, nblk_ref, end_ref, xn_ref, gate_ref, xs_hbm, gs_hbm,
                     xbuf, gbuf, zx, zg, sems):
    t = pl.program_id(0)
    last = pl.num_programs(0) - 1
    slot = t % 2
    base = pl.multiple_of(slot * MOE_SORTED_ROWS, MOE_SORTED_ROWS)
    pairs = [(xbuf, xs_hbm), (gbuf, gs_hbm)]

    @pl.when(t >= 2)
    def _():
        _run_wait(t - 2, nblk_ref, pairs, sems.at[slot], True)

    onehot = _sort_onehot(gate_ref[...], start_ref, t)
    tn = (((0,), (0,)), ((), ()))
    rows = pl.ds(base, MOE_SORTED_ROWS)
    xbuf[rows, :] = lax.dot_general(onehot, xn_ref[...], tn, preferred_element_type=F32).astype(BF16)
    gates = gate_ref[...]
    gh = gates.astype(BF16)
    gl = (gates - gh.astype(F32)).astype(BF16)
    gbuf[rows, :] = (lax.dot_general(onehot, gh, tn, preferred_element_type=F32)
                     + lax.dot_general(onehot, gl, tn, preferred_element_type=F32))
    _run_copies(t, base, start_ref, off_ref, nblk_ref, pairs, sems.at[slot], True)

    @pl.when(t == last)
    def _():
        @pl.when(t >= 1)
        def _():
            _run_wait(t - 1, nblk_ref, pairs, sems.at[1 - slot], True)

        _run_wait(t, nblk_ref, pairs, sems.at[slot], True)
        zx[...] = jnp.zeros(zx.shape, zx.dtype)
        zg[...] = jnp.zeros(zg.shape, zg.dtype)
        copies = []
        for grp in range(N_EXPERT_GROUPS):
            tail = pl.ds(pl.multiple_of(end_ref[grp], MOE_RUN_ALIGN), MOE_TILE)
            copies += [pltpu.make_async_copy(zx, xs_hbm.at[tail], sems.at[0]),
                       pltpu.make_async_copy(zg, gs_hbm.at[tail], sems.at[0])]
        for cp in copies:
            cp.start()
        for cp in copies:
            cp.wait()


def _moe_pack(xn, gates, tabs, n_rows):
    n = xn.shape[0]
    tm = MOE_SRC_TILE
    row = lambda w: pl.BlockSpec((tm, w), lambda i, *_: (i, 0))
    any_spec = pl.BlockSpec(memory_space=pl.ANY)
    grid_spec = pltpu.PrefetchScalarGridSpec(
        num_scalar_prefetch=4, grid=(n // tm,),
        in_specs=[row(D_MODEL), row(ROUTER_LANES)],
        out_specs=[any_spec, any_spec],
        scratch_shapes=[pltpu.VMEM((2 * MOE_SORTED_ROWS, D_MODEL), BF16),
                        pltpu.VMEM((2 * MOE_SORTED_ROWS, ROUTER_LANES), F32),
                        pltpu.VMEM((MOE_TILE, D_MODEL), BF16), pltpu.VMEM((MOE_TILE, ROUTER_LANES), F32),
                        pltpu.SemaphoreType.DMA((2,))])
    return pl.pallas_call(
        _moe_pack_kernel,
        grid_spec=grid_spec,
        out_shape=[jax.ShapeDtypeStruct((n_rows, D_MODEL), BF16), jax.ShapeDtypeStruct((n_rows, ROUTER_LANES), F32)],
        compiler_params=_cparams("arbitrary"),
        name="moe_pack",
    )(tabs["start"], tabs["off"], tabs["nblk"], tabs["end"], xn, gates)


def _moe_expert_kernel(blk_ref, grp_ref, valid_ref, x_ref, gate_ref, wg_ref, wu_ref, wd_ref, o_ref, wgu_s, wd_s):
    i = pl.program_id(0)
    group = grp_ref[i]

    @pl.when((i == 0) | (group != grp_ref[jnp.maximum(i - 1, 0)]))
    def _():
        for j in range(EXPERTS_PER_STEP):
            wgu_s[j] = jnp.concatenate([wg_ref[j].astype(BF16), wu_ref[j].astype(BF16)], axis=1)
        wd_s[...] = wd_ref[...].astype(BF16)

    @pl.when(valid_ref[i] > 0)
    def _():
        xn = x_ref[...]
        gates = gate_ref[...]
        lane = lax.broadcasted_iota(jnp.int32, gates.shape, 1)
        first = EXPERT_LANE0 + group * EXPERTS_PER_GROUP
        hids = []
        for j in range(EXPERTS_PER_STEP):
            h = jnp.dot(xn, wgu_s[j], preferred_element_type=F32)
            ge = jnp.sum(jnp.where(lane == first + j, gates, 0.0), axis=-1, keepdims=True)
            hids.append((jax.nn.silu(h[:, :EXPERT_FF]) * h[:, EXPERT_FF:] * ge).astype(BF16))
        o_ref[...] = jnp.dot(jnp.concatenate(hids, axis=1), wd_s[...], preferred_element_type=F32)


def _moe_experts(xs, gs, tabs, wg, wu, wd, l, n_tiles):
    es = EXPERTS_PER_STEP
    row = lambda w: pl.BlockSpec((MOE_TILE, w), lambda i, blk, grp, valid: (blk[i], 0))
    grid_spec = pltpu.PrefetchScalarGridSpec(
        num_scalar_prefetch=3, grid=(n_tiles,),
        in_specs=[row(D_MODEL), row(ROUTER_LANES),
                  pl.BlockSpec((None, es, D_MODEL, EXPERT_FF), lambda i, blk, grp, valid: (l, grp[i], 0, 0)),
                  pl.BlockSpec((None, es, D_MODEL, EXPERT_FF), lambda i, blk, grp, valid: (l, grp[i], 0, 0)),
                  pl.BlockSpec((None, es * EXPERT_FF, D_MODEL), lambda i, blk, grp, valid: (l, grp[i], 0))],
        out_specs=row(D_MODEL),
        scratch_shapes=[pltpu.VMEM((es, D_MODEL, 2 * EXPERT_FF), BF16), pltpu.VMEM((es * EXPERT_FF, D_MODEL), BF16)])
    return pl.pallas_call(
        _moe_expert_kernel,
        grid_spec=grid_spec,
        out_shape=jax.ShapeDtypeStruct((xs.shape[0], D_MODEL), F32),
        compiler_params=_cparams("arbitrary"),
        name="hier_moe",
    )(tabs["tile_blk"], tabs["tile_grp"], tabs["tile_valid"], xs, gs, wg, wu, wd)


def _moe_unpack_kernel(n_prompt_tiles, start_ref, off_ref, nblk_ref, x_ref, meta_ref, ys_hbm, *rest):
    ybuf, sems = rest[-2:]
    t = pl.program_id(0)
    slot = t % 2
    base = pl.multiple_of(slot * MOE_SORTED_ROWS, MOE_SORTED_ROWS)
    pairs = [(ybuf, ys_hbm)]

    @pl.when(t == 0)
    def _():
        ybuf[...] = jnp.zeros(ybuf.shape, ybuf.dtype)
        _run_copies(t, base, start_ref, off_ref, nblk_ref, pairs, sems.at[slot], False)

    @pl.when(t + 1 < pl.num_programs(0))
    def _():
        nxt = pl.multiple_of((1 - slot) * MOE_SORTED_ROWS, MOE_SORTED_ROWS)
        _run_copies(t + 1, nxt, start_ref, off_ref, nblk_ref, pairs, sems.at[1 - slot], False)

    onehot = _sort_onehot(meta_ref[...], start_ref, t)
    _run_wait(t, nblk_ref, pairs, sems.at[slot], False)
    y = ybuf[pl.ds(base, MOE_SORTED_ROWS), :]
    yh = y.astype(BF16)
    yl = (y - yh.astype(F32)).astype(BF16)
    res = x_ref[...] + (jnp.dot(onehot, yh, preferred_element_type=F32)
                        + jnp.dot(onehot, yl, preferred_element_type=F32))
    if len(rest) == 3:
        rest[0][...] = res
    else:
        gf_ref, yp_ref, ysm_ref = rest[:3]
        yn = _rms(res, gf_ref[...])

        @pl.when(t < n_prompt_tiles)
        def _():
            yp_ref[...] = yn

        @pl.when(t >= n_prompt_tiles)
        def _():
            ysm_ref[...] = yn


def _moe_unpack(x, meta, ys, tabs, g_final=None, n_prompt=0):
    n = x.shape[0]
    tm = MOE_SRC_TILE
    npt = n_prompt // tm
    row = lambda w: pl.BlockSpec((tm, w), lambda i, *_: (i, 0))
    in_specs = [row(D_MODEL), row(ROUTER_LANES), pl.BlockSpec(memory_space=pl.ANY)]
    operands = [x, meta, ys]
    if g_final is None:
        out_specs = row(D_MODEL)
        out_shape = jax.ShapeDtypeStruct((n, D_MODEL), F32)
    else:
        nst = (n - n_prompt) // tm
        in_specs.append(pl.BlockSpec((1, D_MODEL), lambda i, *_: (0, 0)))
        operands.append(g_final)
        out_specs = [pl.BlockSpec((tm, D_MODEL), lambda i, *_: (jnp.minimum(i, npt - 1), 0)),
                     pl.BlockSpec((tm, D_MODEL), lambda i, *_: (jnp.clip(i - npt, 0, nst - 1), 0))]
        out_shape = [jax.ShapeDtypeStruct((n_prompt, D_MODEL), F32), jax.ShapeDtypeStruct((n - n_prompt, D_MODEL), F32)]
    grid_spec = pltpu.PrefetchScalarGridSpec(
        num_scalar_prefetch=3, grid=(n // tm,),
        in_specs=in_specs, out_specs=out_specs,
        scratch_shapes=[pltpu.VMEM((2 * MOE_SORTED_ROWS, D_MODEL), F32), pltpu.SemaphoreType.DMA((2,))])
    return pl.pallas_call(
        functools.partial(_moe_unpack_kernel, npt),
        grid_spec=grid_spec,
        out_shape=out_shape,
        compiler_params=_cparams("arbitrary"),
        name="moe_unpack",
    )(tabs["start"], tabs["off"], tabs["nblk"], *operands)


def _moe_tables(cnt, n):
    n_src = cnt.shape[0]
    pad = (cnt + MOE_RUN_ALIGN - 1) // MOE_RUN_ALIGN * MOE_RUN_ALIGN
    worst = n + n_src * (MOE_RUN_ALIGN - 1)
    cap = (worst + 2 * MOE_TILE - 1) // MOE_TILE * MOE_TILE
    start = jnp.cumsum(pad, axis=1) - pad
    total = jnp.sum(pad, axis=0)
    base = jnp.arange(N_EXPERT_GROUPS, dtype=jnp.int32) * cap
    off = base[None, :] + jnp.cumsum(pad, axis=0) - pad
    tiles_g = (total + MOE_TILE - 1) // MOE_TILE
    tile_end = jnp.cumsum(tiles_g)
    n_tiles = n // MOE_TILE + N_EXPERT_GROUPS + (n_src * N_EXPERT_GROUPS * MOE_RUN_ALIGN + MOE_TILE - 1) // MOE_TILE
    i = jnp.minimum(jnp.arange(n_tiles, dtype=jnp.int32), jnp.maximum(tile_end[-1] - 1, 0))
    grp = jnp.minimum(jnp.sum((i[:, None] >= tile_end[None, :]).astype(jnp.int32), axis=1), N_EXPERT_GROUPS - 1)
    first_tile = (tile_end - tiles_g)
    blk = jnp.zeros_like(i)
    for g in range(N_EXPERT_GROUPS):
        blk = blk + jnp.where(grp == g, g * (cap // MOE_TILE) + i - first_tile[g], 0)
    i32 = lambda a: a.astype(jnp.int32)
    tabs = dict(start=i32(start.reshape(-1)), off=i32(off.reshape(-1)), nblk=i32((pad // MOE_RUN_ALIGN).reshape(-1)),
                end=i32(base + total), tile_blk=i32(blk), tile_grp=i32(grp),
                tile_valid=i32(jnp.arange(n_tiles) < tile_end[-1]))
    return tabs, N_EXPERT_GROUPS * cap, n_tiles


def _moe(x, g, wr, br, wg, wu, wd, l, g_final=None, n_prompt=0):
    n = x.shape[0]
    xn, gates, cnt = _moe_pre(x, g, wr, br, l)
    tabs, n_rows, n_tiles = _moe_tables(cnt[:, 0, :N_EXPERT_GROUPS].astype(jnp.int32), n)
    xs, gs = _moe_pack(xn, gates, tabs, n_rows)
    ys = _moe_experts(xs, gs, tabs, wg, wu, wd, l, n_tiles)
    return _moe_unpack(x, gates, ys, tabs, g_final, n_prompt)


def _rope_tables(seq, t_len, tm):
    half = HEAD_DIM // 2
    inv = ROPE_THETA ** (-jnp.arange(half, dtype=F32) / half)
    pos_s = PAST_LEN + jnp.arange(t_len)
    pos = jnp.concatenate([jnp.arange(seq), jnp.tile(pos_s, tm // t_len)]).astype(F32)
    ang = pos[:, None] * inv[None, :]
    cos = jnp.tile(jnp.cos(ang), (1, LANES // half))
    sign = jnp.where((jnp.arange(LANES) % HEAD_DIM) < half, -1.0, 1.0).astype(F32)
    sin = jnp.tile(jnp.sin(ang), (1, LANES // half)) * sign[None, :]
    return cos, sin


def kernel(x_prompt, x_sample, cache_win_k, cache_win_v, state_ssm, cache_mem_k, cache_mem_v, mem_prompt, w_in, attn_sink, lam_re, lam_im, log_dt, ssm_b_re, ssm_b_im, ssm_c_re, ssm_c_im, ssm_d, w_glu, b_glu, g_attn_out, g_ssm_out, w_out, g_mix, g_xattn, g_mem, wq_x, wk_x, wv_x, wo_x, g_ffn, w_group, b_group, w_router, b_router, w_gate, w_up, w_down, g_final):
    batch, seq, _ = x_prompt.shape
    dec_batch, t_len, _ = x_sample.shape
    depth = w_in.shape[0]
    win_rows = cache_win_k.shape[2]
    n_p = batch * seq
    n_s = dec_batch * t_len
    tm_wide = 1024 if (n_p + n_s) % 1024 == 0 else 512
    tm = tm_wide
    tm_x = 512
    sample_nb = tm_x // t_len

    n = n_p + n_s
    xp = x_prompt.reshape(n_p, D_MODEL)
    xs = x_sample.reshape(n_s, D_MODEL)
    cos_tab, sin_tab = _rope_tables(seq, t_len, tm)
    mem_flat = mem_prompt.reshape(batch * N_MEM, D_MODEL)
    zero_state = jnp.zeros((batch, 1, 2 * N_STATE), F32)
    vec = lambda a: a.reshape(depth, 1, a.shape[-1])

    w_in_b, w_glu_b, w_out_b = w_in.astype(BF16), w_glu.astype(BF16), w_out.astype(BF16)
    wq_b, wk_b, wv_b, wo_b = (w.astype(BF16) for w in (wq_x, wk_x, wv_x, wo_x))
    wd_r = w_down.reshape(depth, N_EXPERTS * EXPERT_FF, D_MODEL)
    wr = jnp.concatenate([w_group, w_router.transpose(0, 2, 1, 3).reshape(depth, D_MODEL, N_EXPERTS)], axis=-1)
    wr = jnp.pad(wr, ((0, 0), (0, 0), (0, ROUTER_LANES - wr.shape[-1])))
    br = jnp.concatenate([b_group, b_router.reshape(depth, N_EXPERTS)], axis=-1)
    br = jnp.pad(br, ((0, 0), (0, ROUTER_LANES - br.shape[-1]))).reshape(depth, 1, ROUTER_LANES)
    g_mix_r, g_xattn_r, g_mem_r, g_ffn_r = vec(g_mix), vec(g_xattn), vec(g_mem), vec(g_ffn)
    g_a_r, g_s_r, b_glu_r, ssm_d_r = vec(g_attn_out), vec(g_ssm_out), vec(b_glu), vec(ssm_d)
    cache_k = cache_win_k.reshape(depth, dec_batch, win_rows, KV_WIDTH)
    cache_v = cache_win_v.reshape(depth, dec_batch, win_rows, KV_WIDTH)
    cmem_k = cache_mem_k.reshape(depth, dec_batch * N_MEM, D_MODEL)
    cmem_v = cache_mem_v.reshape(depth, dec_batch * N_MEM, D_MODEL)
    state_in = _state_to_tiles(state_ssm)

    outs = {k: [] for k in ("wk_p", "wv_p", "ssm_p", "mk_p", "mv_p", "wk_s", "wv_s", "ssm_s")}
    for l in range(depth):
        q, k, v, u = _in_proj(xp, xs, n, g_mix_r, w_in_b, l, cos_tab, sin_tab, n_p, seq, tm)
        attn_p = _attn_prompt(q, k, v, attn_sink[l], batch, seq)
        attn_s = _attn_sample(q, k, v, cache_k, cache_v, l, attn_sink[l], n_p, dec_batch, t_len, 8)
        tail = lambda a: jnp.stack([a[(b + 1) * seq - WINDOW:(b + 1) * seq] for b in range(batch)])
        outs["wk_p"].append(tail(k).reshape(batch, WINDOW, N_KV_HEADS, HEAD_DIM))
        outs["wv_p"].append(tail(v).reshape(batch, WINDOW, N_KV_HEADS, HEAD_DIM))
        ks = k[n_p:].reshape(dec_batch, t_len, KV_WIDTH)
        vs = v[n_p:].reshape(dec_batch, t_len, KV_WIDTH)
        k_all = jnp.concatenate([cache_k[l], ks], axis=1)[:, -win_rows:]
        v_all = jnp.concatenate([cache_v[l], vs], axis=1)[:, -win_rows:]
        outs["wk_s"].append(k_all.reshape(dec_batch, win_rows, N_KV_HEADS, HEAD_DIM))
        outs["wv_s"].append(v_all.reshape(dec_batch, win_rows, N_KV_HEADS, HEAD_DIM))

        we, tv, coef = _ssm_discretize(lam_re[l], lam_im[l], log_dt[l], ssm_b_re[l], ssm_b_im[l],
                                       ssm_c_re[l], ssm_c_im[l])
        y_p, fin_p = _ssm(u, zero_state, we, tv, ssm_d_r, l, coef, 0, batch, seq, 1, seq)
        y_s, fin_s = _ssm(u, state_in[l], we, tv, ssm_d_r, l, coef, n_p, dec_batch, t_len, dec_batch, t_len)
        outs["ssm_p"].append(_tiles_to_state(fin_p))
        outs["ssm_s"].append(_tiles_to_state(fin_s))
        x = _merge(xp, xs, n, attn_p, attn_s, y_p, y_s, w_glu_b, b_glu_r, g_a_r, g_s_r, w_out_b, l, tm)

        mk_p, mv_p = _mem_kv(mem_flat, g_mem_r, wk_b, wv_b, l, 512)
        outs["mk_p"].append(mk_p.reshape(batch, N_MEM, N_XHEADS, XHEAD_DIM))
        outs["mv_p"].append(mv_p.reshape(batch, N_MEM, N_XHEADS, XHEAD_DIM))
        tiles_per_seq = seq // tm
        x = _xattn(x, g_xattn_r, wq_b, wo_b, l, mk_p, mv_p,
                   pl.BlockSpec((N_MEM, D_MODEL), lambda i: (i // tiles_per_seq, 0)), 0, n_p, 1, tm)
        x = _xattn(x, g_xattn_r, wq_b, wo_b, l, cmem_k, cmem_v,
                   pl.BlockSpec((None, sample_nb * N_MEM, D_MODEL), lambda i: (l, i, 0)), n_p, n_s, sample_nb, tm_x)

        if l + 1 < depth:
            x = _moe(x, g_ffn_r, wr, br, w_gate, w_up, wd_r, l)
            xp = xs = x
        else:
            y_p, y_s = _moe(x, g_ffn_r, wr, br, w_gate, w_up, wd_r, l, g_final.reshape(1, D_MODEL), n_p)

    st = lambda name: jnp.stack(outs[name], axis=0)
    return (y_p.reshape(batch, seq, D_MODEL), y_s.reshape(dec_batch, t_len, D_MODEL),
            st("wk_p"), st("wv_p"), st("ssm_p"), st("mk_p"), st("mv_p"), st("wk_s"), st("wv_s"), st("ssm_s"))
```

```python
import functools
import math

import jax
import jax.numpy as jnp
from jax import lax
from jax.experimental import pallas as pl
from jax.experimental.pallas import tpu as pltpu

F32 = jnp.float32
BF16 = jnp.bfloat16

D_MODEL = 1024
CHUNK = 64
EPS = 1e-6
NEG_INF = -1e30
N_HEADS = 8
N_KV_HEADS = 2
HEAD_DIM = 64
ATTN_WIDTH = N_HEADS * HEAD_DIM
KV_WIDTH = N_KV_HEADS * HEAD_DIM
WINDOW = 128
ROPE_THETA = 10000.0
SSM_WIDTH = D_MODEL - ATTN_WIDTH
SSM_GROUP = 16
N_SSM_GROUPS = SSM_WIDTH // SSM_GROUP
SSM_STATE = 64
N_STATE = N_SSM_GROUPS * SSM_STATE
IN_WIDTH = ATTN_WIDTH + 2 * KV_WIDTH + SSM_WIDTH
N_MEM = 256
N_XHEADS = 4
XHEAD_DIM = D_MODEL // N_XHEADS
N_EXPERT_GROUPS = 4
EXPERTS_PER_GROUP = 8
N_EXPERTS = N_EXPERT_GROUPS * EXPERTS_PER_GROUP
EXPERT_FF = 128
PAST_LEN = 4096

LANES = 128
SUBLANES = 8
VMEM_LIMIT = 56 * 1024 * 1024


def _cparams(*sem):
    return pltpu.CompilerParams(dimension_semantics=sem, vmem_limit_bytes=VMEM_LIMIT)


def _rms(x, g):
    return x * lax.rsqrt(jnp.mean(x * x, axis=-1, keepdims=True) + EPS) * g


def _layer_spec(arr, l):
    shape = arr.shape[1:]
    zeros = (0,) * len(shape)
    return pl.BlockSpec((None,) + shape, lambda *_: (l,) + zeros, pipeline_mode=pl.Buffered(1))


def _rope_pairs(t, cos, sin_signed, first_half):
    swapped = jnp.where(first_half, pltpu.roll(t, LANES - HEAD_DIM // 2, 1), pltpu.roll(t, HEAD_DIM // 2, 1))
    return t * cos + swapped * sin_signed


def _two_source_specs(xp, xs, n_prompt, tm, width):
    npt = n_prompt // tm
    s_off = 0 if xs is xp else npt
    s_last = xs.shape[0] // tm - 1
    pspec = pl.BlockSpec((tm, width), lambda i, *_: (jnp.minimum(i, npt - 1), 0))
    sspec = pl.BlockSpec((tm, width), lambda i, *_: (jnp.clip(i - s_off, npt - s_off, s_last), 0))
    return pspec, sspec


def _in_proj_kernel(n_prompt_tiles, xp_ref, xs_ref, g_ref, w_ref, cos_ref, sin_ref, q_ref, k_ref, v_ref, u_ref):
    x = jnp.where(pl.program_id(0) < n_prompt_tiles, xp_ref[...], xs_ref[...])
    xn = _rms(x, g_ref[...])
    z = jnp.dot(xn.astype(BF16), w_ref[...], preferred_element_type=F32)
    cos = cos_ref[...]
    sin = sin_ref[...]
    lane = lax.broadcasted_iota(jnp.int32, cos.shape, 1)
    first_half = (lane % HEAD_DIM) < (HEAD_DIM // 2)
    scale = 1.0 / math.sqrt(HEAD_DIM)
    for j in range(ATTN_WIDTH // LANES):
        t = z[:, j * LANES:(j + 1) * LANES]
        q_ref[:, j * LANES:(j + 1) * LANES] = (_rope_pairs(t, cos, sin, first_half) * scale).astype(BF16)
    k_ref[...] = _rope_pairs(z[:, ATTN_WIDTH:ATTN_WIDTH + KV_WIDTH], cos, sin, first_half)
    v_ref[...] = z[:, ATTN_WIDTH + KV_WIDTH:ATTN_WIDTH + 2 * KV_WIDTH]
    u_ref[...] = z[:, ATTN_WIDTH + 2 * KV_WIDTH:]


def _in_proj(xp, xs, n, g, w_bf16, l, cos_tab, sin_tab, n_prompt, seq, tm):
    n_prompt_tiles = n_prompt // tm
    tiles_per_seq = seq // tm

    def tab_map(i):
        return (jnp.where(i < n_prompt_tiles, i % tiles_per_seq, tiles_per_seq), 0)

    row = lambda w: pl.BlockSpec((tm, w), lambda i: (i, 0))
    return pl.pallas_call(
        functools.partial(_in_proj_kernel, n_prompt_tiles),
        grid=(n // tm,),
        in_specs=[*_two_source_specs(xp, xs, n_prompt, tm, D_MODEL), _layer_spec(g, l), _layer_spec(w_bf16, l),
                  pl.BlockSpec((tm, LANES), tab_map),
                  pl.BlockSpec((tm, LANES), tab_map)],
        out_specs=[row(ATTN_WIDTH), row(KV_WIDTH), row(KV_WIDTH), row(SSM_WIDTH)],
        out_shape=[jax.ShapeDtypeStruct((n, ATTN_WIDTH), BF16),
                   jax.ShapeDtypeStruct((n, KV_WIDTH), F32),
                   jax.ShapeDtypeStruct((n, KV_WIDTH), F32),
                   jax.ShapeDtypeStruct((n, SSM_WIDTH), F32)],
        compiler_params=_cparams("parallel"),
        name="in_proj",
    )(xp, xs, g, w_bf16, cos_tab, sin_tab)


def _kv_pairs(keys, vals):
    lane = lax.broadcasted_iota(jnp.int32, keys.shape, 1)
    low = lane < HEAD_DIM
    k_sw = pltpu.roll(keys, HEAD_DIM, 1)
    v_sw = pltpu.roll(vals, HEAD_DIM, 1)
    kk = [jnp.where(low, keys, k_sw).astype(BF16), jnp.where(low, k_sw, keys).astype(BF16)]
    vv = [jnp.where(low, vals, v_sw).astype(BF16), jnp.where(low, v_sw, vals).astype(BF16)]
    return kk, vv


def _attend_pairs(q, kk, vv, key_rows, mask_add, sink_ref, o_ref, row0):
    tq = q.shape[0]
    qlane = lax.broadcasted_iota(jnp.int32, (tq, LANES), 1)
    qlow = qlane < HEAD_DIM
    row_top = lax.broadcasted_iota(jnp.int32, (2 * tq, 1), 0) < tq
    zero = jnp.zeros((), BF16)
    for pair in range(N_HEADS // 2):
        kv = pair // (N_HEADS // N_KV_HEADS // 2)
        qp = q[:, pair * LANES:(pair + 1) * LANES]
        qs = jnp.concatenate([jnp.where(qlow, qp, zero), jnp.where(qlow, zero, qp)], axis=0)
        s = lax.dot_general(qs, kk[kv][key_rows, :], (((1,), (1,)), ((), ())), preferred_element_type=F32)
        if mask_add is not None:
            s = s + mask_add
        sink = jnp.where(row_top, sink_ref[2 * pair], sink_ref[2 * pair + 1])
        m = jnp.maximum(jnp.max(s, axis=-1, keepdims=True), sink)
        e = jnp.exp(s - m)
        p = e / (jnp.sum(e, axis=-1, keepdims=True) + jnp.exp(sink - m))
        o = jnp.dot(p.astype(BF16), vv[kv][key_rows, :], preferred_element_type=F32)
        o_ref[row0:row0 + tq, pair * LANES:(pair + 1) * LANES] = jnp.where(qlow, o[:tq], o[tq:])


def _attend_blocks(blocks, sink_ref, o_ref):
    tq = blocks[0][0].shape[0]
    qlane = lax.broadcasted_iota(jnp.int32, (tq, LANES), 1)
    qlow = qlane < HEAD_DIM
    row_top = lax.broadcasted_iota(jnp.int32, (2 * tq, 1), 0) < tq
    zero = jnp.zeros((), BF16)
    scores, sinks = [], []
    for q, kk, vv, key_rows, mask_add, row0 in blocks:
        for pair in range(N_HEADS // 2):
            kv = pair // (N_HEADS // N_KV_HEADS // 2)
            qp = q[:, pair * LANES:(pair + 1) * LANES]
            qs = jnp.concatenate([jnp.where(qlow, qp, zero), jnp.where(qlow, zero, qp)], axis=0)
            s = lax.dot_general(qs, kk[kv][key_rows, :], (((1,), (1,)), ((), ())), preferred_element_type=F32)
            scores.append(s if mask_add is None else s + mask_add)
            sinks.append(jnp.where(row_top, sink_ref[2 * pair], sink_ref[2 * pair + 1]))
    s = jnp.concatenate(scores, axis=0)
    sink = jnp.concatenate(sinks, axis=0)
    m = jnp.maximum(jnp.max(s, axis=-1, keepdims=True), sink)
    e = jnp.exp(s - m)
    p = (e / (jnp.sum(e, axis=-1, keepdims=True) + jnp.exp(sink - m))).astype(BF16)
    piece = 0
    for q, kk, vv, key_rows, mask_add, row0 in blocks:
        for pair in range(N_HEADS // 2):
            kv = pair // (N_HEADS // N_KV_HEADS // 2)
            o = jnp.dot(p[piece * 2 * tq:(piece + 1) * 2 * tq, :], vv[kv][key_rows, :], preferred_element_type=F32)
            o_ref[row0:row0 + tq, pair * LANES:(pair + 1) * LANES] = jnp.where(qlow, o[:tq], o[tq:])
            piece += 1


ATTN_SUB = WINDOW
ATTN_TILE = 8 * ATTN_SUB


def _attn_prompt_kernel(sink_ref, ma_ref, mb_ref, q_ref, kp_ref, kc_ref, vp_ref, vc_ref, o_ref):
    kk, vv = _kv_pairs(jnp.concatenate([kp_ref[...], kc_ref[...]], axis=0),
                       jnp.concatenate([vp_ref[...], vc_ref[...]], axis=0))
    for s in range(ATTN_TILE // ATTN_SUB):
        m_ref = ma_ref if s == 0 else mb_ref
        rows = slice(s * ATTN_SUB, s * ATTN_SUB + 2 * WINDOW)
        _attend_pairs(q_ref[s * ATTN_SUB:(s + 1) * ATTN_SUB, :], kk, vv, rows, m_ref[...], sink_ref, o_ref, s * ATTN_SUB)


def _band_masks():
    r = (jnp.arange(2 * ATTN_SUB) % ATTN_SUB)[:, None] // CHUNK
    c = jnp.arange(2 * WINDOW)[None, :]
    band = (c // CHUNK >= r) & (c // CHUNK <= r + WINDOW // CHUNK)
    masks = jnp.stack([band, band & (c >= WINDOW)])
    return jnp.where(masks, 0.0, NEG_INF).astype(F32)


def _attn_prompt(q, k, v, sink, batch, seq):
    nt = seq // ATTN_TILE
    per_seq = seq // ATTN_SUB
    cur = lambda b, i: (b * nt + i, 0)
    prev = lambda b, i: (b * per_seq + jnp.maximum((ATTN_TILE // ATTN_SUB) * i - 1, 0), 0)
    masks = _band_masks()
    return pl.pallas_call(
        _attn_prompt_kernel,
        grid=(batch, nt),
        in_specs=[pl.BlockSpec(memory_space=pltpu.SMEM),
                  pl.BlockSpec((None, 2 * ATTN_SUB, 2 * WINDOW), lambda b, i: (jnp.where(i == 0, 1, 0), 0, 0)),
                  pl.BlockSpec((None, 2 * ATTN_SUB, 2 * WINDOW), lambda b, i: (0, 0, 0)),
                  pl.BlockSpec((ATTN_TILE, ATTN_WIDTH), cur),
                  pl.BlockSpec((ATTN_SUB, KV_WIDTH), prev),
                  pl.BlockSpec((ATTN_TILE, KV_WIDTH), cur),
                  pl.BlockSpec((ATTN_SUB, KV_WIDTH), prev),
                  pl.BlockSpec((ATTN_TILE, KV_WIDTH), cur)],
        out_specs=pl.BlockSpec((ATTN_TILE, ATTN_WIDTH), cur),
        out_shape=jax.ShapeDtypeStruct((batch * seq, ATTN_WIDTH), F32),
        compiler_params=_cparams("parallel", "parallel"),
        name="attn_prompt",
    )(sink, masks, masks, q, k, k, v, v)


def _attn_sample_kernel(sink_ref, q_ref, ck_ref, cv_ref, k_ref, v_ref, o_ref):
    nb = ck_ref.shape[0]
    t = q_ref.shape[0] // nb
    blocks = []
    for b in range(nb):
        rows = slice(b * t, (b + 1) * t)
        kk, vv = _kv_pairs(jnp.concatenate([ck_ref[b], k_ref[rows, :]], axis=0),
                           jnp.concatenate([cv_ref[b], v_ref[rows, :]], axis=0))
        blocks.append((q_ref[rows, :], kk, vv, slice(None), None, b * t))
    _attend_blocks(blocks, sink_ref, o_ref)


def _attn_sample(q, k, v, cache_k, cache_v, l, sink, n_prompt, dec_batch, t, nb):
    w = cache_k.shape[2]
    rows = nb * t
    base = n_prompt // rows
    tok = lambda width: pl.BlockSpec((rows, width), lambda i: (base + i, 0))
    cache = pl.BlockSpec((None, nb, w, KV_WIDTH), lambda i: (l, i, 0, 0))
    return pl.pallas_call(
        _attn_sample_kernel,
        grid=(dec_batch // nb,),
        in_specs=[pl.BlockSpec(memory_space=pltpu.SMEM),
                  tok(ATTN_WIDTH), cache, cache, tok(KV_WIDTH), tok(KV_WIDTH)],
        out_specs=pl.BlockSpec((rows, ATTN_WIDTH), lambda i: (i, 0)),
        out_shape=jax.ShapeDtypeStruct((dec_batch * t, ATTN_WIDTH), F32),
        compiler_params=_cparams("parallel"),
        name="attn_sample",
    )(sink, q, cache_k, cache_v, k, v)


SSM_BLOCK = SUBLANES


def _cmul(ar, ai, br, bi):
    return ar * br - ai * bi, ar * bi + ai * br


def _ssm_disc_kernel(lre_ref, lim_ref, dt_ref, bre_ref, bim_ref, cre_ref, cim_ref,
                     we_ref, tv_ref, coef_ref, vt_ref, wb_ref):
    we_ref[...] = jnp.zeros(we_ref.shape, we_ref.dtype)
    vt_ref[...] = jnp.zeros(vt_ref.shape, vt_ref.dtype)
    wb_ref[...] = jnp.zeros(wb_ref.shape, wb_ref.dtype)
    lane = lax.broadcasted_iota(jnp.int32, (SSM_GROUP, LANES), 1)
    half = [lane < SSM_STATE, lane >= SSM_STATE]
    row8 = lax.broadcasted_iota(jnp.int32, (SUBLANES, LANES), 0)
    for q in range(STATE_TILES):
        lre = lre_ref[q]
        lim = lim_ref[q]
        dt = dt_ref[q]
        mag = jnp.exp(lre * dt)
        ang = lim * dt
        lbr = mag * jnp.cos(ang)
        lbi = mag * jnp.sin(ang)
        nr, ni = lbr - 1.0, lbi
        den = lre * lre + lim * lim
        fr = (nr * lre + ni * lim) / den
        fi = (ni * lre - nr * lim) / den
        bbr, bbi = _cmul(fr, fi, bre_ref[q], bim_ref[q])
        cr, ci = cre_ref[q], cim_ref[q]
        pw = [(jnp.ones_like(lbr), jnp.zeros_like(lbr))]
        for _ in range(SSM_BLOCK):
            pw.append(_cmul(pw[-1][0], pw[-1][1], lbr, lbi))
        re_l = slice(q * 2 * LANES, q * 2 * LANES + LANES)
        im_l = slice(q * 2 * LANES + LANES, (q + 1) * 2 * LANES)
        for h in range(2):
            g = 2 * q + h
            grow = lambda blk: slice(blk * LANES + g * SSM_GROUP, blk * LANES + (g + 1) * SSM_GROUP)
            for j in range(SSM_BLOCK):
                wr, wi = _cmul(pw[SSM_BLOCK - 1 - j][0], pw[SSM_BLOCK - 1 - j][1], bbr, bbi)
                we_ref[grow(j), re_l] = jnp.where(half[h], wr, 0.0).astype(we_ref.dtype)
                we_ref[grow(j), im_l] = jnp.where(half[h], wi, 0.0).astype(we_ref.dtype)
            for d in range(SSM_BLOCK + 1):
                xr, xi = _cmul(cr, ci, pw[d][0], pw[d][1])
                vt_ref[grow(d), re_l] = jnp.where(half[h], xr, 0.0)
                vt_ref[grow(d), im_l] = jnp.where(half[h], -xi, 0.0)
            wb_ref[grow(0), re_l] = jnp.where(half[h], bbr, 0.0)
            wb_ref[grow(0), im_l] = jnp.where(half[h], bbi, 0.0)
        l8 = [pw[SSM_BLOCK]]
        for _ in range(SUBLANES - 1):
            l8.append(_cmul(l8[-1][0], l8[-1][1], pw[SSM_BLOCK][0], pw[SSM_BLOCK][1]))
        co = slice(q * LANES, (q + 1) * LANES)
        for kind, sh in enumerate((1, 2, 4)):
            for a in range(2):
                coef_ref[kind, a, :, co] = jnp.where(row8 >= sh, l8[sh - 1][a], 0.0)
        for a in range(2):
            tab = jnp.zeros((SUBLANES, LANES), F32)
            for k in range(SUBLANES):
                tab = jnp.where(row8 == k, l8[k][a], tab)
            coef_ref[3, a, :, co] = tab
    t0 = lax.dot_general(wb_ref[...], vt_ref[0:BLOCK_LANES, :], (((1,), (1,)), ((), ())),
                         preferred_element_type=F32, precision=lax.Precision.HIGHEST)
    for j in range(SSM_BLOCK):
        if j:
            tv_ref[j * LANES:(j + 1) * LANES, 0:j * LANES] = jnp.zeros((LANES, j * LANES), tv_ref.dtype)
        tv_ref[j * LANES:(j + 1) * LANES, j * LANES:] = t0[:, 0:BLOCK_LANES - j * LANES].astype(tv_ref.dtype)
    tv_ref[BLOCK_LANES:, :] = jnp.transpose(vt_ref[LANES:, :]).astype(tv_ref.dtype)


def _ssm_discretize(lam_re, lam_im, log_dt, b_re, b_im, c_re, c_im):
    g, p = lam_re.shape
    npair = g // 2
    pair = lambda a: a.reshape(npair, 1, 2 * p)
    rows = lambda a: a.reshape(npair, 2, SSM_GROUP, p).transpose(0, 2, 1, 3).reshape(npair, SSM_GROUP, 2 * p)
    dt = jnp.repeat(jnp.exp(log_dt), p).reshape(npair, 1, 2 * p)
    vec = pl.BlockSpec((STATE_TILES, 1, LANES), lambda m: (m, 0, 0))
    mat = pl.BlockSpec((STATE_TILES, SSM_GROUP, LANES), lambda m: (m, 0, 0))
    return pl.pallas_call(
        _ssm_disc_kernel,
        grid=(N_CH_BLOCKS,),
        in_specs=[vec, vec, vec, mat, mat, mat, mat],
        out_specs=[pl.BlockSpec((None, BLOCK_LANES, STATE_LANES), lambda m: (m, 0, 0)),
                   pl.BlockSpec((None, BLOCK_LANES + STATE_LANES, BLOCK_LANES), lambda m: (m, 0, 0)),
                   pl.BlockSpec((None, 4, 2, SUBLANES, STATE_LANES // 2), lambda m: (m, 0, 0, 0, 0))],
        out_shape=[jax.ShapeDtypeStruct((N_CH_BLOCKS, BLOCK_LANES, STATE_LANES), BF16),
                   jax.ShapeDtypeStruct((N_CH_BLOCKS, BLOCK_LANES + STATE_LANES, BLOCK_LANES), BF16),
                   jax.ShapeDtypeStruct((N_CH_BLOCKS, 4, 2, SUBLANES, STATE_LANES // 2), F32)],
        scratch_shapes=[pltpu.VMEM(((SSM_BLOCK + 1) * LANES, STATE_LANES), F32),
                        pltpu.VMEM((LANES, STATE_LANES), F32)],
        compiler_params=_cparams("parallel"),
        name="ssm_discretize",
    )(pair(lam_re), pair(lam_im), dt, rows(b_re.transpose(0, 2, 1)), rows(b_im.transpose(0, 2, 1)),
      rows(c_re), rows(c_im))


GROUPS_PER_CH_BLOCK = LANES // SSM_GROUP
N_CH_BLOCKS = SSM_WIDTH // LANES
STATE_LANES = 2 * GROUPS_PER_CH_BLOCK * SSM_STATE
STATE_TILES = STATE_LANES // (2 * LANES)
BLOCK_LANES = SSM_BLOCK * LANES


def _ssm_kernel(nb, u_ref, s0_ref, we_ref, tv_ref, d_ref, coef_ref, y_ref, fin_ref, st_ref, sprev_ref, ucat_ref):
    t_idx = pl.program_id(2)
    r = st_ref.shape[1] - SUBLANES
    rows = nb * r

    @pl.when(t_idx == 0)
    def _():
        for b in range(nb):
            st_ref[b, 0:SUBLANES, :] = jnp.broadcast_to(s0_ref[b], (SUBLANES, STATE_LANES))

    for j in range(SSM_BLOCK):
        ucat_ref[:, j * LANES:(j + 1) * LANES] = u_ref[pl.ds(j, rows, stride=SSM_BLOCK), :].astype(BF16)
    e = jnp.dot(ucat_ref[...], we_ref[...], preferred_element_type=F32)
    st_ref[:, SUBLANES:, :] = e.reshape(nb, r, STATE_LANES)

    first_row = lax.broadcasted_iota(jnp.int32, (SUBLANES, LANES), 0) == 0

    def group(rg, carry):
        r0 = pl.multiple_of(SUBLANES + rg * SUBLANES, SUBLANES)
        rp = pl.multiple_of(rg * SUBLANES, SUBLANES)
        for b in range(nb):
            for q in range(STATE_TILES):
                re_sl = pl.ds(q * 2 * LANES, LANES)
                im_sl = pl.ds(q * 2 * LANES + LANES, LANES)
                co = pl.ds(q * LANES, LANES)
                xr = st_ref[b, pl.ds(r0, SUBLANES), re_sl]
                xi = st_ref[b, pl.ds(r0, SUBLANES), im_sl]
                for step, sh in enumerate((1, 2, 4)):
                    ar, ai = _cmul(coef_ref[step, 0, :, co], coef_ref[step, 1, :, co],
                                   pltpu.roll(xr, sh, 0), pltpu.roll(xi, sh, 0))
                    xr = xr + ar
                    xi = xi + ai
                pr = jnp.broadcast_to(st_ref[b, pl.ds(rp, SUBLANES), re_sl][SUBLANES - 1:, :], (SUBLANES, LANES))
                pi = jnp.broadcast_to(st_ref[b, pl.ds(rp, SUBLANES), im_sl][SUBLANES - 1:, :], (SUBLANES, LANES))
                ar, ai = _cmul(coef_ref[3, 0, :, co], coef_ref[3, 1, :, co], pr, pi)
                xr = xr + ar
                xi = xi + ai
                st_ref[b, pl.ds(r0, SUBLANES), re_sl] = xr
                st_ref[b, pl.ds(r0, SUBLANES), im_sl] = xi
                out_rows = pl.ds(pl.multiple_of(b * r + rg * SUBLANES, SUBLANES), SUBLANES)
                sprev_ref[out_rows, re_sl] = jnp.where(first_row, pr, pltpu.roll(xr, 1, 0))
                sprev_ref[out_rows, im_sl] = jnp.where(first_row, pi, pltpu.roll(xi, 1, 0))
        return carry

    lax.fori_loop(0, r // SUBLANES, group, 0)

    ucat = ucat_ref[...]
    sprev = sprev_ref[...].astype(BF16)
    d = d_ref[...]
    for pair in range(SSM_BLOCK // 2):
        k = (pair + 1) * 2 * LANES
        cols = slice(pair * 2 * LANES, (pair + 1) * 2 * LANES)
        lhs = jnp.concatenate([ucat[:, :k], sprev], axis=1)
        rhs = jnp.concatenate([tv_ref[0:k, cols], tv_ref[BLOCK_LANES:, cols]], axis=0)
        y2 = jnp.dot(lhs, rhs, preferred_element_type=F32)
        for h in range(2):
            tok = pl.ds(2 * pair + h, rows, stride=SSM_BLOCK)
            y_ref[tok, :] = y2[:, h * LANES:(h + 1) * LANES] + d * u_ref[tok, :]

    for b in range(nb):
        tail = st_ref[b, r:r + SUBLANES, :]
        st_ref[b, 0:SUBLANES, :] = tail
        fin_ref[b] = tail[SUBLANES - 1:SUBLANES, :]


def _ssm(u, s0, we, tv, d, l, coef, row_base, n_seq, seq_len, nb, tt):
    rows = nb * tt
    r = tt // SSM_BLOCK
    nt = seq_len // tt
    base = row_base // rows
    return pl.pallas_call(
        functools.partial(_ssm_kernel, nb),
        grid=(N_CH_BLOCKS, n_seq // nb, nt),
        in_specs=[pl.BlockSpec((rows, LANES), lambda m, i, j: (base + i * nt + j, m)),
                  pl.BlockSpec((nb, 1, STATE_LANES), lambda m, i, j: (i, 0, m)),
                  pl.BlockSpec((None, BLOCK_LANES, STATE_LANES), lambda m, i, j: (m, 0, 0)),
                  pl.BlockSpec((None, BLOCK_LANES + STATE_LANES, BLOCK_LANES), lambda m, i, j: (m, 0, 0)),
                  pl.BlockSpec((None, 1, LANES), lambda m, i, j: (l, 0, m)),
                  pl.BlockSpec((None, 4, 2, SUBLANES, STATE_LANES // 2), lambda m, i, j: (m, 0, 0, 0, 0))],
        out_specs=[pl.BlockSpec((rows, LANES), lambda m, i, j: (i * nt + j, m)),
                   pl.BlockSpec((nb, 1, STATE_LANES), lambda m, i, j: (i, 0, m))],
        out_shape=[jax.ShapeDtypeStruct((n_seq * seq_len, SSM_WIDTH), F32),
                   jax.ShapeDtypeStruct((n_seq, 1, N_CH_BLOCKS * STATE_LANES), F32)],
        scratch_shapes=[pltpu.VMEM((nb, SUBLANES + r, STATE_LANES), F32),
                        pltpu.VMEM((nb * r, STATE_LANES), F32),
                        pltpu.VMEM((nb * r, BLOCK_LANES), BF16)],
        compiler_params=_cparams("parallel", "parallel", "arbitrary"),
        name="ssm_scan",
    )(u, s0, we, tv, d, coef)


def _state_to_tiles(s):
    lead = s.shape[:-3]
    t = s.reshape(lead + (N_CH_BLOCKS, STATE_TILES, 2, SSM_STATE, 2))
    t = jnp.moveaxis(t, -1, -3)
    return t.reshape(lead + (1, N_CH_BLOCKS * STATE_LANES))


def _tiles_to_state(f):
    b = f.shape[0]
    t = f.reshape(b, N_CH_BLOCKS, STATE_TILES, 2, 2, SSM_STATE)
    t = jnp.moveaxis(t, 3, -1)
    return t.reshape(b, N_SSM_GROUPS, SSM_STATE, 2)


def _merge_kernel(n_prompt_tiles, xp_ref, xs_ref, ap_ref, as_ref, yp_ref, ys_ref, wglu_ref, bglu_ref, ga_ref, gs_ref,
                  wout_ref, o_ref):
    is_prompt = pl.program_id(0) < n_prompt_tiles
    attn = jnp.where(is_prompt, ap_ref[...], as_ref[...])
    g = jax.nn.gelu(jnp.where(is_prompt, yp_ref[...], ys_ref[...]))
    glu = g * jax.nn.sigmoid(jnp.dot(g.astype(BF16), wglu_ref[...], preferred_element_type=F32) + bglu_ref[...])
    na = _rms(attn, ga_ref[...]).astype(BF16)
    ns = _rms(glu, gs_ref[...]).astype(BF16)
    o = jnp.dot(na, wout_ref[0:ATTN_WIDTH, :], preferred_element_type=F32)
    o = o + jnp.dot(ns, wout_ref[ATTN_WIDTH:, :], preferred_element_type=F32)
    o_ref[...] = jnp.where(is_prompt, xp_ref[...], xs_ref[...]) + o


def _merge(xp, xs, n, attn_p, attn_s, y_p, y_s, wglu, bglu, ga, gs, wout, l, tm):
    npt = attn_p.shape[0] // tm
    nst = attn_s.shape[0] // tm
    row = lambda w: pl.BlockSpec((tm, w), lambda i: (i, 0))
    prow = lambda w: pl.BlockSpec((tm, w), lambda i: (jnp.minimum(i, npt - 1), 0))
    srow = lambda w: pl.BlockSpec((tm, w), lambda i: (jnp.clip(i - npt, 0, nst - 1), 0))
    return pl.pallas_call(
        functools.partial(_merge_kernel, npt),
        grid=(n // tm,),
        in_specs=[*_two_source_specs(xp, xs, attn_p.shape[0], tm, D_MODEL),
                  prow(ATTN_WIDTH), srow(ATTN_WIDTH), prow(SSM_WIDTH), srow(SSM_WIDTH),
                  _layer_spec(wglu, l), _layer_spec(bglu, l), _layer_spec(ga, l), _layer_spec(gs, l),
                  _layer_spec(wout, l)],
        out_specs=row(D_MODEL),
        out_shape=jax.ShapeDtypeStruct((n, D_MODEL), F32),
        compiler_params=_cparams("parallel"),
        name="merge_heads",
    )(xp, xs, attn_p, attn_s, y_p, y_s, wglu, bglu, ga, gs, wout)


def _mem_kv_kernel(m_ref, g_ref, wk_ref, wv_ref, k_ref, v_ref):
    mn = _rms(m_ref[...], g_ref[...]).astype(BF16)
    k_ref[...] = jnp.dot(mn, wk_ref[...], preferred_element_type=F32)
    v_ref[...] = jnp.dot(mn, wv_ref[...], preferred_element_type=F32)


def _mem_kv(mem, g, wk, wv, l, tm):
    n = mem.shape[0]
    row = pl.BlockSpec((tm, D_MODEL), lambda i: (i, 0))
    return pl.pallas_call(
        _mem_kv_kernel,
        grid=(n // tm,),
        in_specs=[row, _layer_spec(g, l), _layer_spec(wk, l), _layer_spec(wv, l)],
        out_specs=[row, row],
        out_shape=[jax.ShapeDtypeStruct((n, D_MODEL), F32)] * 2,
        compiler_params=_cparams("parallel"),
        name="mem_kv",
    )(mem, g, wk, wv)


def _xattn_kernel(nb, x_ref, g_ref, wq_ref, wo_ref, mk_ref, mv_ref, o_ref, att_ref):
    t = x_ref.shape[0] // nb
    x = x_ref[...]
    hn = _rms(x, g_ref[...]).astype(BF16)
    q = jnp.dot(hn, wq_ref[...], preferred_element_type=F32) * (1.0 / math.sqrt(XHEAD_DIM))
    q = q.astype(BF16)
    heads = [slice(h * XHEAD_DIM, (h + 1) * XHEAD_DIM) for h in range(N_XHEADS)]
    scores = []
    for b in range(nb):
        mk = mk_ref[b * N_MEM:(b + 1) * N_MEM, :].astype(BF16)
        for sl in heads:
            scores.append(lax.dot_general(q[b * t:(b + 1) * t, sl], mk[:, sl], (((1,), (1,)), ((), ())),
                                          preferred_element_type=F32))
    s = jnp.concatenate(scores, axis=0)
    e = jnp.exp(s - jnp.max(s, axis=-1, keepdims=True))
    p = (e / jnp.sum(e, axis=-1, keepdims=True)).astype(BF16)
    for b in range(nb):
        mv = mv_ref[b * N_MEM:(b + 1) * N_MEM, :].astype(BF16)
        for h, sl in enumerate(heads):
            r0 = (b * N_XHEADS + h) * t
            att_ref[b * t:(b + 1) * t, sl] = jnp.dot(p[r0:r0 + t, :], mv[:, sl], preferred_element_type=F32)
    o = jnp.dot(att_ref[...].astype(BF16), wo_ref[...], preferred_element_type=F32)
    o_ref[...] = x + o


def _xattn(x, g, wq, wo, l, mk, mv, mem_spec, row_base, n_rows, nb, tm):
    base = row_base // tm
    xspec = pl.BlockSpec((tm, D_MODEL), lambda i: (base + i, 0))
    return pl.pallas_call(
        functools.partial(_xattn_kernel, nb),
        grid=(n_rows // tm,),
        in_specs=[xspec, _layer_spec(g, l), _layer_spec(wq, l), _layer_spec(wo, l), mem_spec, mem_spec],
        out_specs=xspec,
        out_shape=jax.ShapeDtypeStruct(x.shape, F32),
        scratch_shapes=[pltpu.VMEM((tm, D_MODEL), F32)],
        input_output_aliases={0: 0},
        compiler_params=_cparams("parallel"),
        name="cross_attn",
    )(x, g, wq, wo, mk, mv)


ROUTER_LANES = LANES
EXPERT_LANE0 = N_EXPERT_GROUPS
EXPERTS_PER_STEP = EXPERTS_PER_GROUP
MOE_SRC_TILE = 512
MOE_RUN_ALIGN = 16
MOE_SORTED_ROWS = 640
MOE_TILE = 1024


def _dot_f32_3pass(x, w):
    xh = x.astype(BF16)
    xl = (x - xh.astype(F32)).astype(BF16)
    wh = w.astype(BF16)
    wl = (w - wh.astype(F32)).astype(BF16)
    dot = lambda a, b: jnp.dot(a, b, preferred_element_type=F32)
    return dot(xh, wh) + (dot(xl, wh) + dot(xh, wl))


def _route(logits):
    lane_i = lax.broadcasted_iota(jnp.int32, logits.shape, 1)
    lane = lane_i.astype(F32)
    neg = jnp.float32(-jnp.inf)
    is_g = lane_i < N_EXPERT_GROUPS
    gl = jnp.where(is_g, logits, neg)
    gmax = jnp.max(gl, axis=-1, keepdims=True)
    gidx = jnp.min(jnp.where(gl == gmax, lane, float(ROUTER_LANES)), axis=-1, keepdims=True)
    g_w = 1.0 / jnp.sum(jnp.where(is_g, jnp.exp(gl - gmax), 0.0), axis=-1, keepdims=True)
    first = EXPERT_LANE0 + gidx * EXPERTS_PER_GROUP
    sel = (lane >= first) & (lane < first + EXPERTS_PER_GROUP)
    el = jnp.where(sel, logits, neg)
    m1 = jnp.max(el, axis=-1, keepdims=True)
    i1 = jnp.min(jnp.where(el == m1, lane, float(ROUTER_LANES)), axis=-1, keepdims=True)
    el2 = jnp.where(lane == i1, neg, el)
    m2 = jnp.max(el2, axis=-1, keepdims=True)
    i2 = jnp.min(jnp.where(el2 == m2, lane, float(ROUTER_LANES)), axis=-1, keepdims=True)
    r = jnp.exp(m2 - m1)
    w1 = g_w / (1.0 + r)
    w2 = w1 * r
    return jnp.where(lane == i1, w1, jnp.where(lane == i2, w2, 0.0)), gidx


def _moe_pre_kernel(x_ref, g_ref, wr_ref, br_ref, xn_ref, gate_ref, cnt_ref, tri_ref):
    tm = x_ref.shape[0]

    @pl.when(pl.program_id(0) == 0)
    def _():
        r = lax.broadcasted_iota(jnp.int32, (tm, tm), 0)
        c = lax.broadcasted_iota(jnp.int32, (tm, tm), 1)
        tri_ref[...] = (c <= r).astype(BF16)

    xn = _rms(x_ref[...], g_ref[...])
    xn_ref[...] = xn.astype(BF16)
    gates, gidx = _route(_dot_f32_3pass(xn, wr_ref[...]) + br_ref[...])
    lane = lax.broadcasted_iota(jnp.int32, gates.shape, 1).astype(F32)
    onehot = lane == gidx
    incl = jnp.dot(tri_ref[...], onehot.astype(BF16), preferred_element_type=F32)
    rank = jnp.sum(jnp.where(onehot, incl, 0.0), axis=-1, keepdims=True) - 1.0
    gate_ref[...] = jnp.where(lane == 0.0, gidx, jnp.where(lane == 1.0, rank, gates))
    cnt_ref[...] = incl[tm - 1:tm, :]


def _moe_pre(x, g, wr, br, l):
    n = x.shape[0]
    tm = MOE_SRC_TILE
    row = lambda w: pl.BlockSpec((tm, w), lambda i: (i, 0))
    return pl.pallas_call(
        _moe_pre_kernel,
        grid=(n // tm,),
        in_specs=[row(D_MODEL), _layer_spec(g, l), _layer_spec(wr, l), _layer_spec(br, l)],
        out_specs=[row(D_MODEL), row(ROUTER_LANES), pl.BlockSpec((None, 1, ROUTER_LANES), lambda i: (i, 0, 0))],
        out_shape=[jax.ShapeDtypeStruct((n, D_MODEL), BF16), jax.ShapeDtypeStruct((n, ROUTER_LANES), F32),
                   jax.ShapeDtypeStruct((n // tm, 1, ROUTER_LANES), F32)],
        scratch_shapes=[pltpu.VMEM((tm, tm), BF16)],
        compiler_params=_cparams("arbitrary"),
        name="moe_pre",
    )(x, g, wr, br)


def _sort_onehot(meta, start_ref, t):
    gid = meta[:, 0:1]
    pos = meta[:, 1:2]
    for grp in range(N_EXPERT_GROUPS):
        pos = pos + jnp.where(gid == float(grp), start_ref[t * N_EXPERT_GROUPS + grp].astype(F32), 0.0)
    col = lax.broadcasted_iota(jnp.int32, (meta.shape[0], MOE_SORTED_ROWS), 1).astype(F32)
    return (col == pos).astype(BF16)


def _run_copies(t, base, start_ref, off_ref, nblk_ref, pairs, sem, to_hbm):
    for grp in range(N_EXPERT_GROUPS):
        k = t * N_EXPERT_GROUPS + grp
        src0 = base + start_ref[k]
        dst0 = off_ref[k]

        def body(b, carry):
            lo = pl.multiple_of(src0 + b * MOE_RUN_ALIGN, MOE_RUN_ALIGN)
            hi = pl.multiple_of(dst0 + b * MOE_RUN_ALIGN, MOE_RUN_ALIGN)
            for buf, arr in pairs:
                a, h = buf.at[pl.ds(lo, MOE_RUN_ALIGN)], arr.at[pl.ds(hi, MOE_RUN_ALIGN)]
                (pltpu.make_async_copy(a, h, sem) if to_hbm else pltpu.make_async_copy(h, a, sem)).start()
            return carry

        lax.fori_loop(0, nblk_ref[k], body, 0)


def _run_wait(t, nblk_ref, pairs, sem, to_hbm):
    total = nblk_ref[t * N_EXPERT_GROUPS]
    for grp in range(1, N_EXPERT_GROUPS):
        total = total + nblk_ref[t * N_EXPERT_GROUPS + grp]
    rows = total * MOE_RUN_ALIGN

    @pl.when(rows > 0)
    def _():
        for buf, arr in pairs:
            a, h = buf.at[pl.ds(0, rows)], arr.at[pl.ds(0, rows)]
            (pltpu.make_async_copy(a, h, sem) if to_hbm else pltpu.make_async_copy(h, a, sem)).wait()


def _moe_pack_kernel(start_ref, off_ref, nblk_ref, end_ref, xn_ref, gate_ref, xs_hbm, gs_hbm,
                     xbuf, gbuf, zx, zg, sems):
    t = pl.program_id(0)
    last = pl.num_programs(0) - 1
    slot = t % 2
    base = pl.multiple_of(slot * MOE_SORTED_ROWS, MOE_SORTED_ROWS)
    pairs = [(xbuf, xs_hbm), (gbuf, gs_hbm)]

    @pl.when(t >= 2)
    def _():
        _run_wait(t - 2, nblk_ref, pairs, sems.at[slot], True)

    onehot = _sort_onehot(gate_ref[...], start_ref, t)
    tn = (((0,), (0,)), ((), ()))
    rows = pl.ds(base, MOE_SORTED_ROWS)
    xbuf[rows, :] = lax.dot_general(onehot, xn_ref[...], tn, preferred_element_type=F32).astype(BF16)
    gates = gate_ref[...]
    gh = gates.astype(BF16)
    gl = (gates - gh.astype(F32)).astype(BF16)
    gbuf[rows, :] = (lax.dot_general(onehot, gh, tn, preferred_element_type=F32)
                     + lax.dot_general(onehot, gl, tn, preferred_element_type=F32))
    _run_copies(t, base, start_ref, off_ref, nblk_ref, pairs, sems.at[slot], True)

    @pl.when(t == last)
    def _():
        @pl.when(t >= 1)
        def _():
            _run_wait(t - 1, nblk_ref, pairs, sems.at[1 - slot], True)

        _run_wait(t, nblk_ref, pairs, sems.at[slot], True)
        zx[...] = jnp.zeros(zx.shape, zx.dtype)
        zg[...] = jnp.zeros(zg.shape, zg.dtype)
        copies = []
        for grp in range(N_EXPERT_GROUPS):
            tail = pl.ds(pl.multiple_of(end_ref[grp], MOE_RUN_ALIGN), MOE_TILE)
            copies += [pltpu.make_async_copy(zx, xs_hbm.at[tail], sems.at[0]),
                       pltpu.make_async_copy(zg, gs_hbm.at[tail], sems.at[0])]
        for cp in copies:
            cp.start()
        for cp in copies:
            cp.wait()


def _moe_pack(xn, gates, tabs, n_rows):
    n = xn.shape[0]
    tm = MOE_SRC_TILE
    row = lambda w: pl.BlockSpec((tm, w), lambda i, *_: (i, 0))
    any_spec = pl.BlockSpec(memory_space=pl.ANY)
    grid_spec = pltpu.PrefetchScalarGridSpec(
        num_scalar_prefetch=4, grid=(n // tm,),
        in_specs=[row(D_MODEL), row(ROUTER_LANES)],
        out_specs=[any_spec, any_spec],
        scratch_shapes=[pltpu.VMEM((2 * MOE_SORTED_ROWS, D_MODEL), BF16),
                        pltpu.VMEM((2 * MOE_SORTED_ROWS, ROUTER_LANES), F32),
                        pltpu.VMEM((MOE_TILE, D_MODEL), BF16), pltpu.VMEM((MOE_TILE, ROUTER_LANES), F32),
                        pltpu.SemaphoreType.DMA((2,))])
    return pl.pallas_call(
        _moe_pack_kernel,
        grid_spec=grid_spec,
        out_shape=[jax.ShapeDtypeStruct((n_rows, D_MODEL), BF16), jax.ShapeDtypeStruct((n_rows, ROUTER_LANES), F32)],
        compiler_params=_cparams("arbitrary"),
        name="moe_pack",
    )(tabs["start"], tabs["off"], tabs["nblk"], tabs["end"], xn, gates)


def _moe_expert_kernel(blk_ref, grp_ref, valid_ref, x_ref, gate_ref, wg_ref, wu_ref, wd_ref, o_ref, wgu_s, wd_s):
    i = pl.program_id(0)
    group = grp_ref[i]

    @pl.when((i == 0) | (group != grp_ref[jnp.maximum(i - 1, 0)]))
    def _():
        for j in range(EXPERTS_PER_STEP):
            wgu_s[j] = jnp.concatenate([wg_ref[j].astype(BF16), wu_ref[j].astype(BF16)], axis=1)
        wd_s[...] = wd_ref[...].astype(BF16)

    @pl.when(valid_ref[i] > 0)
    def _():
        xn = x_ref[...]
        gates = gate_ref[...]
        lane = lax.broadcasted_iota(jnp.int32, gates.shape, 1)
        first = EXPERT_LANE0 + group * EXPERTS_PER_GROUP
        hids = []
        for j in range(EXPERTS_PER_STEP):
            h = jnp.dot(xn, wgu_s[j], preferred_element_type=F32)
            ge = jnp.sum(jnp.where(lane == first + j, gates, 0.0), axis=-1, keepdims=True)
            hids.append((jax.nn.silu(h[:, :EXPERT_FF]) * h[:, EXPERT_FF:] * ge).astype(BF16))
        o_ref[...] = jnp.dot(jnp.concatenate(hids, axis=1), wd_s[...], preferred_element_type=F32)


def _moe_experts(xs, gs, tabs, wg, wu, wd, l, n_tiles):
    es = EXPERTS_PER_STEP
    row = lambda w: pl.BlockSpec((MOE_TILE, w), lambda i, blk, grp, valid: (blk[i], 0))
    grid_spec = pltpu.PrefetchScalarGridSpec(
        num_scalar_prefetch=3, grid=(n_tiles,),
        in_specs=[row(D_MODEL), row(ROUTER_LANES),
                  pl.BlockSpec((None, es, D_MODEL, EXPERT_FF), lambda i, blk, grp, valid: (l, grp[i], 0, 0)),
                  pl.BlockSpec((None, es, D_MODEL, EXPERT_FF), lambda i, blk, grp, valid: (l, grp[i], 0, 0)),
                  pl.BlockSpec((None, es * EXPERT_FF, D_MODEL), lambda i, blk, grp, valid: (l, grp[i], 0))],
        out_specs=row(D_MODEL),
        scratch_shapes=[pltpu.VMEM((es, D_MODEL, 2 * EXPERT_FF), BF16), pltpu.VMEM((es * EXPERT_FF, D_MODEL), BF16)])
    return pl.pallas_call(
        _moe_expert_kernel,
        grid_spec=grid_spec,
        out_shape=jax.ShapeDtypeStruct((xs.shape[0], D_MODEL), F32),
        compiler_params=_cparams("arbitrary"),
        name="hier_moe",
    )(tabs["tile_blk"], tabs["tile_grp"], tabs["tile_valid"], xs, gs, wg, wu, wd)


def _moe_unpack_kernel(n_prompt_tiles, start_ref, off_ref, nblk_ref, x_ref, meta_ref, ys_hbm, *rest):
    ybuf, sems = rest[-2:]
    t = pl.program_id(0)
    slot = t % 2
    base = pl.multiple_of(slot * MOE_SORTED_ROWS, MOE_SORTED_ROWS)
    pairs = [(ybuf, ys_hbm)]

    @pl.when(t == 0)
    def _():
        ybuf[...] = jnp.zeros(ybuf.shape, ybuf.dtype)
        _run_copies(t, base, start_ref, off_ref, nblk_ref, pairs, sems.at[slot], False)

    @pl.when(t + 1 < pl.num_programs(0))
    def _():
        nxt = pl.multiple_of((1 - slot) * MOE_SORTED_ROWS, MOE_SORTED_ROWS)
        _run_copies(t + 1, nxt, start_ref, off_ref, nblk_ref, pairs, sems.at[1 - slot], False)

    onehot = _sort_onehot(meta_ref[...], start_ref, t)
    _run_wait(t, nblk_ref, pairs, sems.at[slot], False)
    y = ybuf[pl.ds(base, MOE_SORTED_ROWS), :]
    yh = y.astype(BF16)
    yl = (y - yh.astype(F32)).astype(BF16)
    res = x_ref[...] + (jnp.dot(onehot, yh, preferred_element_type=F32)
                        + jnp.dot(onehot, yl, preferred_element_type=F32))
    if len(rest) == 3:
        rest[0][...] = res
    else:
        gf_ref, yp_ref, ysm_ref = rest[:3]
        yn = _rms(res, gf_ref[...])

        @pl.when(t < n_prompt_tiles)
        def _():
            yp_ref[...] = yn

        @pl.when(t >= n_prompt_tiles)
        def _():
            ysm_ref[...] = yn


def _moe_unpack(x, meta, ys, tabs, g_final=None, n_prompt=0):
    n = x.shape[0]
    tm = MOE_SRC_TILE
    npt = n_prompt // tm
    row = lambda w: pl.BlockSpec((tm, w), lambda i, *_: (i, 0))
    in_specs = [row(D_MODEL), row(ROUTER_LANES), pl.BlockSpec(memory_space=pl.ANY)]
    operands = [x, meta, ys]
    if g_final is None:
        out_specs = row(D_MODEL)
        out_shape = jax.ShapeDtypeStruct((n, D_MODEL), F32)
    else:
        nst = (n - n_prompt) // tm
        in_specs.append(pl.BlockSpec((1, D_MODEL), lambda i, *_: (0, 0)))
        operands.append(g_final)
        out_specs = [pl.BlockSpec((tm, D_MODEL), lambda i, *_: (jnp.minimum(i, npt - 1), 0)),
                     pl.BlockSpec((tm, D_MODEL), lambda i, *_: (jnp.clip(i - npt, 0, nst - 1), 0))]
        out_shape = [jax.ShapeDtypeStruct((n_prompt, D_MODEL), F32), jax.ShapeDtypeStruct((n - n_prompt, D_MODEL), F32)]
    grid_spec = pltpu.PrefetchScalarGridSpec(
        num_scalar_prefetch=3, grid=(n // tm,),
        in_specs=in_specs, out_specs=out_specs,
        scratch_shapes=[pltpu.VMEM((2 * MOE_SORTED_ROWS, D_MODEL), F32), pltpu.SemaphoreType.DMA((2,))])
    return pl.pallas_call(
        functools.partial(_moe_unpack_kernel, npt),
        grid_spec=grid_spec,
        out_shape=out_shape,
        compiler_params=_cparams("arbitrary"),
        name="moe_unpack",
    )(tabs["start"], tabs["off"], tabs["nblk"], *operands)


def _moe_tables(cnt, n):
    n_src = cnt.shape[0]
    pad = (cnt + MOE_RUN_ALIGN - 1) // MOE_RUN_ALIGN * MOE_RUN_ALIGN
    worst = n + n_src * (MOE_RUN_ALIGN - 1)
    cap = (worst + 2 * MOE_TILE - 1) // MOE_TILE * MOE_TILE
    start = jnp.cumsum(pad, axis=1) - pad
    total = jnp.sum(pad, axis=0)
    base = jnp.arange(N_EXPERT_GROUPS, dtype=jnp.int32) * cap
    off = base[None, :] + jnp.cumsum(pad, axis=0) - pad
    tiles_g = (total + MOE_TILE - 1) // MOE_TILE
    tile_end = jnp.cumsum(tiles_g)
    n_tiles = n // MOE_TILE + N_EXPERT_GROUPS + (n_src * N_EXPERT_GROUPS * MOE_RUN_ALIGN + MOE_TILE - 1) // MOE_TILE
    i = jnp.minimum(jnp.arange(n_tiles, dtype=jnp.int32), jnp.maximum(tile_end[-1] - 1, 0))
    grp = jnp.minimum(jnp.sum((i[:, None] >= tile_end[None, :]).astype(jnp.int32), axis=1), N_EXPERT_GROUPS - 1)
    first_tile = (tile_end - tiles_g)
    blk = jnp.zeros_like(i)
    for g in range(N_EXPERT_GROUPS):
        blk = blk + jnp.where(grp == g, g * (cap // MOE_TILE) + i - first_tile[g], 0)
    i32 = lambda a: a.astype(jnp.int32)
    tabs = dict(start=i32(start.reshape(-1)), off=i32(off.reshape(-1)), nblk=i32((pad // MOE_RUN_ALIGN).reshape(-1)),
                end=i32(base + total), tile_blk=i32(blk), tile_grp=i32(grp),
                tile_valid=i32(jnp.arange(n_tiles) < tile_end[-1]))
    return tabs, N_EXPERT_GROUPS * cap, n_tiles


def _moe(x, g, wr, br, wg, wu, wd, l, g_final=None, n_prompt=0):
    n = x.shape[0]
    xn, gates, cnt = _moe_pre(x, g, wr, br, l)
    tabs, n_rows, n_tiles = _moe_tables(cnt[:, 0, :N_EXPERT_GROUPS].astype(jnp.int32), n)
    xs, gs = _moe_pack(xn, gates, tabs, n_rows)
    ys = _moe_experts(xs, gs, tabs, wg, wu, wd, l, n_tiles)
    return _moe_unpack(x, gates, ys, tabs, g_final, n_prompt)


def _rope_tables(seq, t_len, tm):
    half = HEAD_DIM // 2
    inv = ROPE_THETA ** (-jnp.arange(half, dtype=F32) / half)
    pos_s = PAST_LEN + jnp.arange(t_len)
    pos = jnp.concatenate([jnp.arange(seq), jnp.tile(pos_s, tm // t_len)]).astype(F32)
    ang = pos[:, None] * inv[None, :]
    cos = jnp.tile(jnp.cos(ang), (1, LANES // half))
    sign = jnp.where((jnp.arange(LANES) % HEAD_DIM) < half, -1.0, 1.0).astype(F32)
    sin = jnp.tile(jnp.sin(ang), (1, LANES // half)) * sign[None, :]
    return cos, sin


def kernel(x_prompt, x_sample, cache_win_k, cache_win_v, state_ssm, cache_mem_k, cache_mem_v, mem_prompt, w_in, attn_sink, lam_re, lam_im, log_dt, ssm_b_re, ssm_b_im, ssm_c_re, ssm_c_im, ssm_d, w_glu, b_glu, g_attn_out, g_ssm_out, w_out, g_mix, g_xattn, g_mem, wq_x, wk_x, wv_x, wo_x, g_ffn, w_group, b_group, w_router, b_router, w_gate, w_up, w_down, g_final):
    batch, seq, _ = x_prompt.shape
    dec_batch, t_len, _ = x_sample.shape
    depth = w_in.shape[0]
    win_rows = cache_win_k.shape[2]
    n_p = batch * seq
    n_s = dec_batch * t_len
    tm_wide = 1024 if (n_p + n_s) % 1024 == 0 else 512
    tm = tm_wide
    tm_x = 512
    sample_nb = tm_x // t_len

    n = n_p + n_s
    xp = x_prompt.reshape(n_p, D_MODEL)
    xs = x_sample.reshape(n_s, D_MODEL)
    cos_tab, sin_tab = _rope_tables(seq, t_len, tm)
    mem_flat = mem_prompt.reshape(batch * N_MEM, D_MODEL)
    zero_state = jnp.zeros((batch, 1, 2 * N_STATE), F32)
    vec = lambda a: a.reshape(depth, 1, a.shape[-1])

    w_in_b, w_glu_b, w_out_b = w_in.astype(BF16), w_glu.astype(BF16), w_out.astype(BF16)
    wq_b, wk_b, wv_b, wo_b = (w.astype(BF16) for w in (wq_x, wk_x, wv_x, wo_x))
    wd_r = w_down.reshape(depth, N_EXPERTS * EXPERT_FF, D_MODEL)
    wr = jnp.concatenate([w_group, w_router.transpose(0, 2, 1, 3).reshape(depth, D_MODEL, N_EXPERTS)], axis=-1)
    wr = jnp.pad(wr, ((0, 0), (0, 0), (0, ROUTER_LANES - wr.shape[-1])))
    br = jnp.concatenate([b_group, b_router.reshape(depth, N_EXPERTS)], axis=-1)
    br = jnp.pad(br, ((0, 0), (0, ROUTER_LANES - br.shape[-1]))).reshape(depth, 1, ROUTER_LANES)
    g_mix_r, g_xattn_r, g_mem_r, g_ffn_r = vec(g_mix), vec(g_xattn), vec(g_mem), vec(g_ffn)
    g_a_r, g_s_r, b_glu_r, ssm_d_r = vec(g_attn_out), vec(g_ssm_out), vec(b_glu), vec(ssm_d)
    cache_k = cache_win_k.reshape(depth, dec_batch, win_rows, KV_WIDTH)
    cache_v = cache_win_v.reshape(depth, dec_batch, win_rows, KV_WIDTH)
    cmem_k = cache_mem_k.reshape(depth, dec_batch * N_MEM, D_MODEL)
    cmem_v = cache_mem_v.reshape(depth, dec_batch * N_MEM, D_MODEL)
    state_in = _state_to_tiles(state_ssm)

    outs = {k: [] for k in ("wk_p", "wv_p", "ssm_p", "mk_p", "mv_p", "wk_s", "wv_s", "ssm_s")}
    for l in range(depth):
        q, k, v, u = _in_proj(xp, xs, n, g_mix_r, w_in_b, l, cos_tab, sin_tab, n_p, seq, tm)
        attn_p = _attn_prompt(q, k, v, attn_sink[l], batch, seq)
        attn_s = _attn_sample(q, k, v, cache_k, cache_v, l, attn_sink[l], n_p, dec_batch, t_len, 8)
        tail = lambda a: jnp.stack([a[(b + 1) * seq - WINDOW:(b + 1) * seq] for b in range(batch)])
        outs["wk_p"].append(tail(k).reshape(batch, WINDOW, N_KV_HEADS, HEAD_DIM))
        outs["wv_p"].append(tail(v).reshape(batch, WINDOW, N_KV_HEADS, HEAD_DIM))
        ks = k[n_p:].reshape(dec_batch, t_len, KV_WIDTH)
        vs = v[n_p:].reshape(dec_batch, t_len, KV_WIDTH)
        k_all = jnp.concatenate([cache_k[l], ks], axis=1)[:, -win_rows:]
        v_all = jnp.concatenate([cache_v[l], vs], axis=1)[:, -win_rows:]
        outs["wk_s"].append(k_all.reshape(dec_batch, win_rows, N_KV_HEADS, HEAD_DIM))
        outs["wv_s"].append(v_all.reshape(dec_batch, win_rows, N_KV_HEADS, HEAD_DIM))

        we, tv, coef = _ssm_discretize(lam_re[l], lam_im[l], log_dt[l], ssm_b_re[l], ssm_b_im[l],
                                       ssm_c_re[l], ssm_c_im[l])
        y_p, fin_p = _ssm(u, zero_state, we, tv, ssm_d_r, l, coef, 0, batch, seq, 1, seq)
        y_s, fin_s = _ssm(u, state_in[l], we, tv, ssm_d_r, l, coef, n_p, dec_batch, t_len, dec_batch, t_len)
        outs["ssm_p"].append(_tiles_to_state(fin_p))
        outs["ssm_s"].append(_tiles_to_state(fin_s))
        x = _merge(xp, xs, n, attn_p, attn_s, y_p, y_s, w_glu_b, b_glu_r, g_a_r, g_s_r, w_out_b, l, tm)

        mk_p, mv_p = _mem_kv(mem_flat, g_mem_r, wk_b, wv_b, l, 512)
        outs["mk_p"].append(mk_p.reshape(batch, N_MEM, N_XHEADS, XHEAD_DIM))
        outs["mv_p"].append(mv_p.reshape(batch, N_MEM, N_XHEADS, XHEAD_DIM))
        tiles_per_seq = seq // tm
        x = _xattn(x, g_xattn_r, wq_b, wo_b, l, mk_p, mv_p,
                   pl.BlockSpec((N_MEM, D_MODEL), lambda i: (i // tiles_per_seq, 0)), 0, n_p, 1, tm)
        x = _xattn(x, g_xattn_r, wq_b, wo_b, l, cmem_k, cmem_v,
                   pl.BlockSpec((None, sample_nb * N_MEM, D_MODEL), lambda i: (l, i, 0)), n_p, n_s, sample_nb, tm_x)

        if l + 1 < depth:
            x = _moe(x, g_ffn_r, wr, br, w_gate, w_up, wd_r, l)
            xp = xs = x
        else:
            y_p, y_s = _moe(x, g_ffn_r, wr, br, w_gate, w_up, wd_r, l, g_final.reshape(1, D_MODEL), n_p)

    st = lambda name: jnp.stack(outs[name], axis=0)
    return (y_p.reshape(batch, seq, D_MODEL), y_s.reshape(dec_batch, t_len, D_MODEL),
            st("wk_p"), st("wv_p"), st("ssm_p"), st("mk_p"), st("mv_p"), st("wk_s"), st("wv_s"), st("ssm_s"))
```
